```python
import math
import jax, jax.numpy as jnp
from jax import lax
import numpy as np

D_MODEL = 1024
BATCH = 32
SEQ = 256
DEPTH = 2
DEC_BATCH = 8
DEC_SEQ = 2048
PAST_LEN = 512

GRID_W = 64
N_MIXERS = 2
N_HYENA = (DEPTH + 1) // 2
N_ATTN = DEPTH // 2
HY_ORDER = 2
HY_EMB = 33
HY_FW = 64
HY_DECAY_TARGET = 1e-2
HY_FAST_PCT = 0.3
HY_SLOW_PCT = 1.5
SHORT_CONV = 3
N_HEADS = 8
HEAD_DIM = 64
V_DIM = 2 * HEAD_DIM
ROPE_BASE = 10000.0
Q_BLOCK = 128
D_FF = 2816
N_EXPERTS = 8
TOP_K = 2
D_FF_EXPERT = 3584
EPS = 1e-6

kernel_name = 'hybrid_hyena_diffattn_moe_dit_step'

F32 = jnp.float32


def rmsnorm(x, g):
    xf = x.astype(F32)
    y = xf * lax.rsqrt(jnp.mean(xf * xf, axis=-1, keepdims=True) + EPS)
    return (y * g.astype(F32)).astype(x.dtype)


def adaln_params(cond, w, b):
    m = jax.nn.silu(cond) @ w + b
    return jnp.split(m[:, None, :], 6, axis=-1)


def modulate(x, g, shift, scale):
    return rmsnorm(x, g) * (1 + scale) + shift


def short_conv(x, w, b):
    L = x.shape[1]
    pad = SHORT_CONV // 2
    xp = jnp.pad(x, ((0, 0), (pad, SHORT_CONV - 1 - pad), (0, 0)))
    return sum(xp[:, j:j + L] * w[j] for j in range(SHORT_CONV)) + b


def hyena_filters(L, w1, b1, w2, b2, freq, w3):
    t = jnp.linspace(0.0, 1.0, L, dtype=F32)[:, None]
    bands = (HY_EMB - 1) // 2
    w_ang = 2.0 * math.pi * jnp.arange(L, dtype=F32)[:, None] / L
    f = jnp.linspace(1e-4, bands - 1, bands, dtype=F32)[None, :]
    ang = f * w_ang
    feats = jnp.concatenate([t, jnp.cos(ang), -jnp.sin(ang)], axis=-1)
    fr = freq.astype(F32)
    h = jnp.sin(fr[0] * (feats @ w1.astype(F32) + b1.astype(F32)))
    h = jnp.sin(fr[1] * (h @ w2.astype(F32) + b2.astype(F32)))
    h = (h @ w3.astype(F32)).reshape(L, 2, HY_ORDER, D_MODEL)
    min_decay = math.log(HY_DECAY_TARGET) / HY_SLOW_PCT
    max_decay = math.log(HY_DECAY_TARGET) / HY_FAST_PCT
    deltas = jnp.linspace(min_decay, max_decay, D_MODEL, dtype=F32)
    decay = jnp.exp(-t * jnp.abs(deltas))
    h = h * decay[:, None, None, :]
    fwd = h[:, 0]
    bwd = h[1:, 1][::-1]
    k = jnp.concatenate([fwd, jnp.zeros((1, HY_ORDER, D_MODEL), F32), bwd], axis=0)
    return jnp.fft.rfft(k, axis=0)


def fftconv(u, kf, bias):
    L = u.shape[1]
    uf32 = u.astype(F32)
    uf = jnp.fft.rfft(uf32, n=2 * L, axis=1)
    y = jnp.fft.irfft(uf * kf[None], n=2 * L, axis=1)[:, :L]
    return (y + uf32 * bias.astype(F32)).astype(u.dtype)


def hyena_mixer(u, in_w, in_b, conv_w, conv_b, f_w1, f_b1, f_w2, f_b2, f_freq, f_w3, bias, out_w, out_b):
    L = u.shape[1]
    proj = short_conv(u @ in_w + in_b, conv_w, conv_b)
    v, x1, x2 = jnp.split(proj, 3, axis=-1)
    kf = hyena_filters(L, f_w1, f_b1, f_w2, f_b2, f_freq, f_w3)
    z = x1 * fftconv(v, kf[:, 0], bias[0])
    z = x2 * fftconv(z, kf[:, 1], bias[1])
    return z @ out_w + out_b


def rope_2d_tables(L):
    rows = L // GRID_W
    row = jnp.repeat(jnp.arange(rows, dtype=F32), GRID_W)
    col = jnp.tile(jnp.arange(GRID_W, dtype=F32), rows)
    quarter = HEAD_DIM // 4
    inv = ROPE_BASE ** (-jnp.arange(quarter, dtype=F32) / quarter)
    def axis_angles(pos):
        a = pos[:, None] * inv[None, :]
        return jnp.concatenate([a, a], axis=-1)
    ang = jnp.concatenate([axis_angles(row), axis_angles(col)], axis=-1)
    return jnp.cos(ang), jnp.sin(ang)


def rotate_half_2d(x):
    y = x.reshape(x.shape[:-1] + (2, 2, HEAD_DIM // 4))
    y = jnp.stack([-y[..., 1, :], y[..., 0, :]], axis=-2)
    return y.reshape(x.shape)


def apply_rope_2d(x, cos, sin):
    xf = x.astype(F32)
    cb = cos[None, :, None, None, :]
    sb = sin[None, :, None, None, :]
    return (xf * cb + rotate_half_2d(xf) * sb).astype(x.dtype)


def diff_qkv(h, qkv_w, q_g, k_g):
    B, L, _ = h.shape
    q, k, v = jnp.split(h @ qkv_w, 3, axis=-1)
    q = rmsnorm(q.reshape(B, L, N_HEADS, 2, HEAD_DIM), q_g)
    k = rmsnorm(k.reshape(B, L, N_HEADS, 2, HEAD_DIM), k_g)
    v = v.reshape(B, L, N_HEADS, V_DIM)
    return q, k, v


def diff_lambda(lam_vecs, lam_init):
    lv = lam_vecs.astype(F32)
    return jnp.exp(jnp.sum(lv[0] * lv[1])) - jnp.exp(jnp.sum(lv[2] * lv[3])) + lam_init


def diff_attend(q, k, v, lam):
    B, Lq = q.shape[:2]
    nb = Lq // Q_BLOCK
    qb = q.reshape(B, nb, Q_BLOCK, N_HEADS, 2, HEAD_DIM).swapaxes(0, 1)
    kf = k.astype(F32)
    vf = v.astype(F32)
    scale = HEAD_DIM ** -0.5
    def block(qi):
        s = jnp.einsum('bqhmd,bkhmd->bhmqk', qi.astype(F32), kf) * scale
        p = jax.nn.softmax(s, axis=-1)
        w = p[:, :, 0] - lam * p[:, :, 1]
        return jnp.einsum('bhqk,bkhe->bqhe', w, vf).astype(q.dtype)
    o = lax.map(block, qb)
    return o.swapaxes(0, 1).reshape(B, Lq, N_HEADS, V_DIM)


def diff_out(o, lam_init, subln_g, out_w):
    B, L = o.shape[:2]
    o = rmsnorm(o, subln_g) * (1.0 - lam_init)
    return o.reshape(B, L, N_HEADS * V_DIM) @ out_w


def swiglu(x, w_gu, w_down):
    g, u = jnp.split(x @ w_gu, 2, axis=-1)
    return (jax.nn.silu(g) * u) @ w_down


def moe_swiglu(h, router_w, router_b, w_gu, w_down):
    B, L, D = h.shape
    t = h.reshape(B * L, D)
    logits = t.astype(F32) @ router_w.astype(F32) + router_b.astype(F32)
    top_v, top_i = lax.top_k(logits, TOP_K)
    gates = jax.nn.softmax(top_v, axis=-1)
    gm = jnp.sum(jax.nn.one_hot(top_i, N_EXPERTS, dtype=F32) * gates[..., None], axis=1)
    out = jnp.zeros((B * L, D), F32)
    for e in range(N_EXPERTS):
        out = out + gm[:, e:e + 1] * swiglu(t, w_gu[e], w_down[e]).astype(F32)
    return out.astype(h.dtype).reshape(B, L, D)


def setup_inputs(seed: int = 0) -> dict:
    key = jax.random.key(seed)
    ks = iter(jax.random.split(key, 40))
    def nrm(shape, s):
        return s * jax.random.normal(next(ks), shape, F32)
    D = D_MODEL
    return {
        'x_prompt': nrm((BATCH, SEQ, D), 1.0),
        'x_sample': nrm((DEC_BATCH, DEC_SEQ, D), 1.0),
        'cache_k': nrm((DEC_BATCH, N_ATTN, PAST_LEN, N_HEADS, 2, HEAD_DIM), 1.0),
        'cache_v': nrm((DEC_BATCH, N_ATTN, PAST_LEN, N_HEADS, V_DIM), 1.0),
        'c': nrm((DEC_BATCH, D), 1.0),
        'c_ctx': nrm((D,), 1.0),
        'ada_w': nrm((DEPTH, D, 6 * D), 0.3 * D ** -0.5),
        'ada_b': nrm((DEPTH, 6 * D), 0.02),
        'norm_g': 1.0 + nrm((DEPTH, 2, D), 0.02),
        'hy_in_w': nrm((N_HYENA, D, 3 * D), D ** -0.5),
        'hy_in_b': nrm((N_HYENA, 3 * D), 0.02),
        'hy_conv_w': nrm((N_HYENA, SHORT_CONV, 3 * D), SHORT_CONV ** -0.5),
        'hy_conv_b': nrm((N_HYENA, 3 * D), 0.02),
        'hy_f_w1': nrm((N_HYENA, HY_EMB, HY_FW), HY_EMB ** -0.5),
        'hy_f_b1': nrm((N_HYENA, HY_FW), 0.02),
        'hy_f_w2': nrm((N_HYENA, HY_FW, HY_FW), HY_FW ** -0.5),
        'hy_f_b2': nrm((N_HYENA, HY_FW), 0.02),
        'hy_f_freq': 1.0 + nrm((N_HYENA, 2, HY_FW), 0.1),
        'hy_f_w3': nrm((N_HYENA, HY_FW, 2 * HY_ORDER * D), 0.02),
        'hy_bias': nrm((N_HYENA, HY_ORDER, D), 0.5),
        'hy_out_w': nrm((N_HYENA, D, D), D ** -0.5),
        'hy_out_b': nrm((N_HYENA, D), 0.02),
        'at_qkv_w': nrm((N_ATTN, D, 3 * N_HEADS * V_DIM), D ** -0.5),
        'at_q_g': 1.0 + nrm((N_ATTN, HEAD_DIM), 0.02),
        'at_k_g': 1.0 + nrm((N_ATTN, HEAD_DIM), 0.02),
        'at_lam': nrm((N_ATTN, 4, HEAD_DIM), 0.1),
        'at_subln_g': 1.0 + nrm((N_ATTN, V_DIM), 0.02),
        'at_out_w': nrm((N_ATTN, N_HEADS * V_DIM, D), (N_HEADS * V_DIM) ** -0.5),
        'dn_w_gu': nrm((N_HYENA, D, 2 * D_FF), D ** -0.5),
        'dn_w_down': nrm((N_HYENA, D_FF, D), D_FF ** -0.5),
        'mo_router_w': nrm((N_ATTN, D, N_EXPERTS), D ** -0.5),
        'mo_router_b': nrm((N_ATTN, N_EXPERTS), 0.01),
        'mo_w_gu': nrm((N_ATTN, N_EXPERTS, D, 2 * D_FF_EXPERT), D ** -0.5),
        'mo_w_down': nrm((N_ATTN, N_EXPERTS, D_FF_EXPERT, D), D_FF_EXPERT ** -0.5),
    }


def reference(x_prompt, x_sample, cache_k, cache_v, c, c_ctx, ada_w, ada_b, norm_g,
              hy_in_w, hy_in_b, hy_conv_w, hy_conv_b, hy_f_w1, hy_f_b1, hy_f_w2, hy_f_b2,
              hy_f_freq, hy_f_w3, hy_bias, hy_out_w, hy_out_b,
              at_qkv_w, at_q_g, at_k_g, at_lam, at_subln_g, at_out_w,
              dn_w_gu, dn_w_down, mo_router_w, mo_router_b, mo_w_gu, mo_w_down):
    xc = x_prompt
    xs = x_sample
    new_k = []
    new_v = []
    for i in range(DEPTH):
        j = i // N_MIXERS
        sh1c, sc1c, g1c, sh2c, sc2c, g2c = adaln_params(c_ctx[None, :], ada_w[i], ada_b[i])
        sh1s, sc1s, g1s, sh2s, sc2s, g2s = adaln_params(c, ada_w[i], ada_b[i])
        hc = modulate(xc, norm_g[i, 0], sh1c, sc1c)
        hs = modulate(xs, norm_g[i, 0], sh1s, sc1s)
        if i % N_MIXERS == 0:
            mc = hyena_mixer(hc, hy_in_w[j], hy_in_b[j], hy_conv_w[j], hy_conv_b[j], hy_f_w1[j], hy_f_b1[j],
                             hy_f_w2[j], hy_f_b2[j], hy_f_freq[j], hy_f_w3[j], hy_bias[j], hy_out_w[j], hy_out_b[j])
            ms = hyena_mixer(hs, hy_in_w[j], hy_in_b[j], hy_conv_w[j], hy_conv_b[j], hy_f_w1[j], hy_f_b1[j],
                             hy_f_w2[j], hy_f_b2[j], hy_f_freq[j], hy_f_w3[j], hy_bias[j], hy_out_w[j], hy_out_b[j])
        else:
            lam_init = 0.8 - 0.6 * math.exp(-0.3 * i)
            lam = diff_lambda(at_lam[j], lam_init)
            qc, kc, vc = diff_qkv(hc, at_qkv_w[j], at_q_g[j], at_k_g[j])
            new_k.append(kc)
            new_v.append(vc)
            mc = diff_out(diff_attend(qc, kc, vc, lam), lam_init, at_subln_g[j], at_out_w[j])
            qs, ks_, vs = diff_qkv(hs, at_qkv_w[j], at_q_g[j], at_k_g[j])
            cos, sin = rope_2d_tables(hs.shape[1])
            qs = apply_rope_2d(qs, cos, sin)
            ks_ = apply_rope_2d(ks_, cos, sin)
            keys = jnp.concatenate([cache_k[:, j].astype(ks_.dtype), ks_], axis=1)
            vals = jnp.concatenate([cache_v[:, j].astype(vs.dtype), vs], axis=1)
            ms = diff_out(diff_attend(qs, keys, vals, lam), lam_init, at_subln_g[j], at_out_w[j])
        xc = xc + g1c * mc
        xs = xs + g1s * ms
        hc = modulate(xc, norm_g[i, 1], sh2c, sc2c)
        hs = modulate(xs, norm_g[i, 1], sh2s, sc2s)
        if i % 2 == 0:
            fc = swiglu(hc, dn_w_gu[j], dn_w_down[j])
            fs = swiglu(hs, dn_w_gu[j], dn_w_down[j])
        else:
            fc = moe_swiglu(hc, mo_router_w[j], mo_router_b[j], mo_w_gu[j], mo_w_down[j])
            fs = moe_swiglu(hs, mo_router_w[j], mo_router_b[j], mo_w_gu[j], mo_w_down[j])
        xc = xc + g2c * fc
        xs = xs + g2s * fs
    new_cache_k = jnp.stack(new_k, axis=1)
    new_cache_v = jnp.stack(new_v, axis=1)
    return (xc, xs, new_cache_k, new_cache_v)
```

```python
import functools
import math

import numpy as np
import jax
import jax.numpy as jnp
from jax import lax
from jax.experimental import pallas as pl
from jax.experimental.pallas import tpu as pltpu

F32 = jnp.float32
BF16 = jnp.bfloat16
HIGHEST = lax.Precision.HIGHEST

D_MODEL = 1024
GRID_W = 64
HY_ORDER = 2
HY_EMB = 33
HY_FW = 64
HY_DECAY_TARGET = 1e-2
HY_FAST_PCT = 0.3
HY_SLOW_PCT = 1.5
N_HEADS = 8
HEAD_DIM = 64
V_DIM = 2 * HEAD_DIM
ROPE_BASE = 10000.0
D_FF = 2816
N_EXPERTS = 8
D_FF_EXPERT = 3584
EPS = 1e-6

V7X_LANES = 128
V7X_VMEM_LIMIT_BYTES = 56 * 1024 * 1024
COND_ROWS = 16
TOKEN_TILE = 512


def _params(*semantics):
    return pltpu.CompilerParams(dimension_semantics=semantics,
                                vmem_limit_bytes=V7X_VMEM_LIMIT_BYTES)


def _resident(shape):
    zeros = (0,) * len(shape)
    return pl.BlockSpec(shape, lambda *_: zeros, pipeline_mode=pl.Buffered(1))


def _dot(a, b):
    return jnp.dot(a, b, preferred_element_type=F32)


def _dot_f32(a, b):
    return jnp.dot(a, b, precision=HIGHEST, preferred_element_type=F32)


def _modulate(x, g, shift, scale):
    ms = jnp.mean(x * x, axis=-1, keepdims=True)
    return (x * lax.rsqrt(ms + EPS) * g) * (1.0 + scale) + shift


def _adaln_kernel(cond_ref, w_ref, b_ref, o_ref):
    c = cond_ref[...]
    o_ref[...] = _dot_f32(c * jax.nn.sigmoid(c), w_ref[...]) + b_ref[...]


def _adaln(cond, ada_w, ada_b):
    depth, d, n = ada_w.shape
    tn = 1536
    return pl.pallas_call(
        _adaln_kernel,
        grid=(depth, n // tn),
        in_specs=[
            pl.BlockSpec((COND_ROWS, d), lambda i, j: (0, 0)),
            pl.BlockSpec((None, d, tn), lambda i, j: (i, 0, j)),
            pl.BlockSpec((None, 1, tn), lambda i, j: (i, 0, j)),
        ],
        out_specs=pl.BlockSpec((None, COND_ROWS, tn), lambda i, j: (i, 0, j)),
        out_shape=jax.ShapeDtypeStruct((depth, COND_ROWS, n), F32),
        compiler_params=_params("parallel", "parallel"),
        name="adaln",
    )(cond, ada_w, ada_b.reshape(depth, 1, n))


class _Stream:
    def __init__(self, batch, seq, cond_row0, per_seq_cond):
        self.batch, self.seq = batch, seq
        self.rows = batch * seq
        self.tm = min(TOKEN_TILE, seq)
        self.tiles = self.rows // self.tm
        tiles_per_seq = seq // self.tm
        if per_seq_cond:
            self.cond_row = lambda i: cond_row0 + i // tiles_per_seq
        else:
            self.cond_row = lambda i: cond_row0

    def mod_spec(self, layer, chunk):
        return pl.BlockSpec((None, None, 1, D_MODEL),
                            lambda i, *_: (layer, self.cond_row(i), 0, chunk))

    def row_spec(self, width):
        return pl.BlockSpec((self.tm, width), lambda i, *_: (i, 0))


def _mod_matmul_kernel(n_out, has_bias, x_ref, g_ref, sh_ref, sc_ref, w_ref, *rest):
    if has_bias:
        b_ref, out_refs = rest[0], rest[1:]
    else:
        b_ref, out_refs = None, rest
    h = _modulate(x_ref[...], g_ref[...], sh_ref[...], sc_ref[...])
    y = _dot(h.astype(BF16), w_ref[...])
    if has_bias:
        y = y + b_ref[...]
    width = y.shape[1] // n_out
    for k, o_ref in enumerate(out_refs):
        o_ref[...] = y[:, k * width:(k + 1) * width].astype(o_ref.dtype)


def _mod_matmul(st, x, norm_g, mod, layer, w_bf16, bias, n_out, out_dtype):
    d, n = w_bf16.shape
    in_specs = [st.row_spec(d), _resident((1, d)), st.mod_spec(layer, 0), st.mod_spec(layer, 1),
                _resident((d, n))]
    args = [x, norm_g.reshape(1, d), mod, mod, w_bf16]
    if bias is not None:
        in_specs.append(_resident((1, n)))
        args.append(bias.reshape(1, n))
    width = n // n_out
    outs = pl.pallas_call(
        functools.partial(_mod_matmul_kernel, n_out, bias is not None),
        grid=(st.tiles,),
        in_specs=in_specs,
        out_specs=[st.row_spec(width)] * n_out,
        out_shape=[jax.ShapeDtypeStruct((st.rows, width), out_dtype)] * n_out,
        compiler_params=_params("parallel"),
        name="mod_matmul",
    )(*args)
    return outs


def _filter_time_kernel(feats_ref, w1_ref, b1_ref, w2_ref, b2_ref, fr_ref, w3_ref, dl_ref,
                        hs_ref, hd_ref):
    feats = feats_ref[...]
    fr = fr_ref[...]
    h = jnp.sin(fr[0:1] * (_dot_f32(feats, w1_ref[...]) + b1_ref[...]))
    h = jnp.sin(fr[1:2] * (_dot_f32(h, w2_ref[...]) + b2_ref[...]))
    h = _dot_f32(h, w3_ref[...])
    t = feats[:, 0:1]
    decay = jnp.exp(-t * jnp.abs(dl_ref[...]))
    half = HY_ORDER * D_MODEL
    decay2 = jnp.concatenate([decay] * HY_ORDER, axis=1)
    fwd = h[:, :half] * decay2
    bwd = jnp.where(t == 0.0, 0.0, h[:, half:] * decay2)
    hs_ref[...] = fwd + bwd
    hd_ref[...] = fwd - bwd


def _filter_time(seq, feats_pad, w1_pad, b1, w2, b2, freq, w3, deltas):
    tl = min(seq, 256)
    half = HY_ORDER * D_MODEL
    out = jax.ShapeDtypeStruct((seq, half), F32)
    return pl.pallas_call(
        _filter_time_kernel,
        grid=(seq // tl,),
        in_specs=[
            pl.BlockSpec((tl, V7X_LANES), lambda i: (i, 0)),
            _resident(w1_pad.shape), _resident((1, HY_FW)), _resident((HY_FW, HY_FW)),
            _resident((1, HY_FW)), _resident((2, HY_FW)), _resident(w3.shape),
            _resident((1, D_MODEL)),
        ],
        out_specs=[pl.BlockSpec((tl, half), lambda i: (i, 0))] * 2,
        out_shape=[out, out],
        compiler_params=_params("parallel"),
        name="hyena_filter_time",
    )(feats_pad, w1_pad, b1.reshape(1, HY_FW), w2, b2.reshape(1, HY_FW), freq, w3, deltas)


def _filter_spectrum_kernel(seq, hs_ref, hd_ref, c_ref, s_ref, kr_ref, ki_ref, kn_ref):
    hs = hs_ref[...]
    row = lax.broadcasted_iota(jnp.int32, (seq, 1), 0)
    wgt = jnp.where(row == 0, 1.0, 2.0) * (1.0 / (2 * seq))
    kr_ref[...] = wgt * _dot(c_ref[...], hs.astype(BF16))
    ki_ref[...] = -wgt * _dot(s_ref[...], hd_ref[...].astype(BF16))
    sign = (1 - 2 * (row & 1)).astype(F32)
    kn_ref[...] = jnp.sum(hs * sign, axis=0, keepdims=True) * (1.0 / (2 * seq))


def _filter_spectrum(seq, hs, hd, cmat, smat):
    half = hs.shape[1]
    tn = 256
    col = pl.BlockSpec((seq, tn), lambda j: (0, j))
    return pl.pallas_call(
        functools.partial(_filter_spectrum_kernel, seq),
        grid=(half // tn,),
        in_specs=[col, col, _resident((seq, seq)), _resident((seq, seq))],
        out_specs=[col, col, pl.BlockSpec((1, tn), lambda j: (0, j))],
        out_shape=[jax.ShapeDtypeStruct((seq, half), F32)] * 2
        + [jax.ShapeDtypeStruct((1, half), F32)],
        compiler_params=_params("parallel"),
        name="hyena_filter_spectrum",
    )(hs, hd, cmat, smat)


def _hyena_core_kernel(seq, chunk, pv_ref, p1_ref, p2_ref, cwv_ref, cw1_ref, cw2_ref, cbv_ref,
                       cb1_ref, cb2_ref, kr0_ref, ki0_ref, kn0_ref, kr1_ref, ki1_ref, kn1_ref,
                       bias0_ref, bias1_ref, c_ref, s_ref, z_ref,
                       u_ref, ub_ref, gate_ref, p_ref, q_ref):
    row = lax.broadcasted_iota(jnp.int32, (seq, 1), 0)
    sign = (1 - 2 * (row & 1)).astype(F32)
    chunks = [slice(r, r + chunk) for r in range(0, seq, chunk)]

    def short_conv(x_ref, w_ref, b_ref):
        x = x_ref[...].astype(F32)
        w = w_ref[...]
        prev = jnp.where(row == 0, 0.0, pltpu.roll(x, 1, 0))
        nxt = jnp.where(row == seq - 1, 0.0, pltpu.roll(x, seq - 1, 0))
        return prev * w[0:1] + x * w[1:2] + nxt * w[2:3] + b_ref[...]

    def gated_long_conv(kr_ref, ki_ref, kn_ref, bias_ref, write):
        u = u_ref[...]
        ub_ref[...] = u.astype(BF16)
        nyq = jnp.sum(u * sign, axis=0, keepdims=True) * kn_ref[...]
        for rows in chunks:
            a = _dot(c_ref[rows, :], ub_ref[...])
            b = _dot(s_ref[rows, :], ub_ref[...])
            kr, ki = kr_ref[rows, :], ki_ref[rows, :]
            p_ref[rows, :] = (a * kr + b * ki).astype(BF16)
            q_ref[rows, :] = (b * kr - a * ki).astype(BF16)
        for rows in chunks:
            y = _dot(c_ref[rows, :], p_ref[...]) + _dot(s_ref[rows, :], q_ref[...])
            y = y + sign[rows, :] * nyq + u_ref[rows, :] * bias_ref[...]
            write(rows, gate_ref[rows, :] * y)

    def to_u(rows, val):
        u_ref[rows, :] = val

    def to_z(rows, val):
        z_ref[rows, :] = val.astype(z_ref.dtype)

    u_ref[...] = short_conv(pv_ref, cwv_ref, cbv_ref)
    gate_ref[...] = short_conv(p1_ref, cw1_ref, cb1_ref)
    gated_long_conv(kr0_ref, ki0_ref, kn0_ref, bias0_ref, to_u)
    gate_ref[...] = short_conv(p2_ref, cw2_ref, cb2_ref)
    gated_long_conv(kr1_ref, ki1_ref, kn1_ref, bias1_ref, to_z)


def _hyena_core(st, proj, conv_w, conv_b, kr, ki, kn, bias, cmat, smat):
    seq, d = st.seq, D_MODEL
    tn = 256
    nj = d // tn
    proj3 = proj.reshape(st.batch, seq, 3 * d)

    def part(k):
        return pl.BlockSpec((None, seq, tn), lambda j, b: (b, 0, k * nj + j))

    def cols(rows, k, buffers=2):
        return pl.BlockSpec((rows, tn), lambda j, b: (0, k * nj + j),
                            pipeline_mode=pl.Buffered(buffers))

    in_specs = ([part(0), part(1), part(2)]
                + [cols(3, k) for k in range(3)] + [cols(1, k) for k in range(3)]
                + [cols(seq, 0, 1), cols(seq, 0, 1), cols(1, 0), cols(seq, 1, 1), cols(seq, 1, 1),
                   cols(1, 1)]
                + [cols(1, 0), cols(1, 0)]
                + [_resident((seq, seq)), _resident((seq, seq))])
    z = pl.pallas_call(
        functools.partial(_hyena_core_kernel, seq, min(seq, 512)),
        grid=(nj, st.batch),
        in_specs=in_specs,
        out_specs=pl.BlockSpec((None, seq, tn), lambda j, b: (b, 0, j)),
        out_shape=jax.ShapeDtypeStruct((st.batch, seq, d), BF16),
        scratch_shapes=[pltpu.VMEM((seq, tn), F32), pltpu.VMEM((seq, tn), BF16),
                        pltpu.VMEM((seq, tn), F32), pltpu.VMEM((seq, tn), BF16),
                        pltpu.VMEM((seq, tn), BF16)],
        compiler_params=_params("parallel", "parallel"),
        name="hyena_core",
    )(proj3, proj3, proj3, conv_w, conv_w, conv_w, conv_b, conv_b, conv_b,
      kr, ki, kn, kr, ki, kn, bias[0:1], bias[1:2], cmat, smat)
    return z.reshape(st.rows, d)


def _out_proj_kernel(has_bias, has_router, *refs):
    refs = list(refs)
    z_ref, w_ref = refs.pop(0), refs.pop(0)
    b_ref = refs.pop(0) if has_bias else None
    x_ref, g1_ref, ng_ref, sh_ref, sc_ref = (refs.pop(0) for _ in range(5))
    if has_router:
        rw_ref, rb_ref = refs.pop(0), refs.pop(0)
    x1_ref, h2_ref = refs.pop(0), refs.pop(0)
    m = _dot(z_ref[...], w_ref[...])
    if has_bias:
        m = m + b_ref[...]
    x1 = x_ref[...] + g1_ref[...] * m
    x1_ref[...] = x1
    h2 = _modulate(x1, ng_ref[...], sh_ref[...], sc_ref[...])
    h2_ref[...] = h2.astype(BF16)
    if has_router:
        gm_ref = refs.pop(0)
        logits = _dot_f32(h2, rw_ref[...]) + rb_ref[...]
        lane = lax.broadcasted_iota(jnp.int32, logits.shape, 1)
        neg = -jnp.inf
        logits = jnp.where(lane < N_EXPERTS, logits, neg)
        m1 = jnp.max(logits, axis=-1, keepdims=True)
        i1 = jnp.min(jnp.where(logits == m1, lane, V7X_LANES), axis=-1, keepdims=True)
        rest = jnp.where(lane == i1, neg, logits)
        m2 = jnp.max(rest, axis=-1, keepdims=True)
        i2 = jnp.min(jnp.where(rest == m2, lane, V7X_LANES), axis=-1, keepdims=True)
        e2 = jnp.exp(m2 - m1)
        den = 1.0 + e2
        gm_ref[...] = jnp.where(lane == i1, 1.0 / den, 0.0) + jnp.where(lane == i2, e2 / den, 0.0)


def _out_proj(st, z_bf16, w_bf16, bias, x, mod, layer, norm_g, router=None):
    d = D_MODEL
    in_specs = [st.row_spec(z_bf16.shape[1]), _resident(w_bf16.shape)]
    args = [z_bf16, w_bf16]
    if bias is not None:
        in_specs.append(_resident((1, d)))
        args.append(bias.reshape(1, d))
    in_specs += [st.row_spec(d), st.mod_spec(layer, 2), _resident((1, d)),
                 st.mod_spec(layer, 3), st.mod_spec(layer, 4)]
    args += [x, mod, norm_g.reshape(1, d), mod, mod]
    out_specs = [st.row_spec(d), st.row_spec(d)]
    out_shape = [jax.ShapeDtypeStruct((st.rows, d), F32), jax.ShapeDtypeStruct((st.rows, d), BF16)]
    if router is not None:
        rw_pad, rb_pad = router
        in_specs += [_resident(rw_pad.shape), _resident(rb_pad.shape)]
        args += [rw_pad, rb_pad]
        out_specs.append(st.row_spec(V7X_LANES))
        out_shape.append(jax.ShapeDtypeStruct((st.rows, V7X_LANES), F32))
    return pl.pallas_call(
        functools.partial(_out_proj_kernel, bias is not None, router is not None),
        grid=(st.tiles,),
        in_specs=in_specs,
        out_specs=out_specs,
        out_shape=out_shape,
        compiler_params=_params("parallel"),
        name="out_proj",
    )(*args)


def _swiglu_kernel(n_experts, h_ref, wg_ref, wu_ref, wd_ref, x_ref, g2_ref, *rest):
    if n_experts:
        gm_ref, o_ref, acc_ref = rest
    else:
        o_ref, acc_ref = rest
    e, j = pl.program_id(1), pl.program_id(2)

    @pl.when((e == 0) & (j == 0))
    def _():
        acc_ref[...] = jnp.zeros_like(acc_ref)

    h = h_ref[...]
    g = _dot(h, wg_ref[...])
    u = _dot(h, wu_ref[...])
    a = g * jax.nn.sigmoid(g) * u
    if n_experts:
        gm = gm_ref[...]
        lane = lax.broadcasted_iota(jnp.int32, gm.shape, 1)
        a = a * jnp.sum(jnp.where(lane == e, gm, 0.0), axis=-1, keepdims=True)
    acc_ref[...] += _dot(a.astype(BF16), wd_ref[...])

    @pl.when((e == pl.num_programs(1) - 1) & (j == pl.num_programs(2) - 1))
    def _():
        o_ref[...] = x_ref[...] + g2_ref[...] * acc_ref[...]


def _swiglu(st, h2, w_gu, w_down, x1, mod, layer, gm=None):
    n_e, d, two_f = w_gu.shape
    f = two_f // 2
    tf = 512 if f % 512 == 0 else 256
    nf = f // tf
    in_specs = [
        pl.BlockSpec((st.tm, d), lambda i, e, j: (i, 0)),
        pl.BlockSpec((None, d, tf), lambda i, e, j: (e, 0, j)),
        pl.BlockSpec((None, d, tf), lambda i, e, j: (e, 0, nf + j)),
        pl.BlockSpec((None, tf, d), lambda i, e, j: (e, j, 0)),
        pl.BlockSpec((st.tm, d), lambda i, e, j: (i, 0)),
        st.mod_spec(layer, 5),
    ]
    args = [h2, w_gu, w_gu, w_down, x1, mod]
    if gm is not None:
        in_specs.append(pl.BlockSpec((st.tm, V7X_LANES), lambda i, e, j: (i, 0)))
        args.append(gm)
    return pl.pallas_call(
        functools.partial(_swiglu_kernel, n_e if gm is not None else 0),
        grid=(st.tiles, n_e, nf),
        in_specs=in_specs,
        out_specs=pl.BlockSpec((st.tm, d), lambda i, e, j: (i, 0)),
        out_shape=jax.ShapeDtypeStruct((st.rows, d), F32),
        scratch_shapes=[pltpu.VMEM((st.tm, d), F32)],
        compiler_params=_params("parallel", "arbitrary", "arbitrary"),
        name="swiglu",
    )(*args)


def _head_rmsnorm(x, g2):
    lane = lax.broadcasted_iota(jnp.int32, x.shape, 1)
    lo = lane < HEAD_DIM
    sq = x * x
    s_lo = jnp.sum(jnp.where(lo, sq, 0.0), axis=-1, keepdims=True)
    s_hi = jnp.sum(jnp.where(lo, 0.0, sq), axis=-1, keepdims=True)
    ms = jnp.where(lo, s_lo, s_hi) * (1.0 / HEAD_DIM)
    return x * lax.rsqrt(ms + EPS) * g2


def _rope(x, cos, sin_signed):
    q4 = HEAD_DIM // 4
    lane = lax.broadcasted_iota(jnp.int32, x.shape, 1)
    first = (lane & q4) == 0
    width = x.shape[1]
    partner = jnp.where(first, pltpu.roll(x, width - q4, 1), pltpu.roll(x, q4, 1))
    return x * cos + partner * sin_signed


def _attention_kernel(past, use_rope, lam_init, *refs):
    refs = list(refs)
    q_ref, k_ref, v_ref = refs.pop(0), refs.pop(0), refs.pop(0)
    if past:
        ck_ref, cv_ref = refs.pop(0), refs.pop(0)
    if use_rope:
        cosq_ref, sinq_ref, cosk_ref, sink_ref = (refs.pop(0) for _ in range(4))
    qg_ref, kg_ref, lam_ref, sg_ref = (refs.pop(0) for _ in range(4))
    o_ref = refs.pop(0)
    nk_ref = None if past else refs.pop(0)
    kall_ref, vall_ref = refs

    qi = pl.program_id(2)

    @pl.when(qi == 0)
    def _():
        k = _head_rmsnorm(k_ref[...], kg_ref[...])
        if nk_ref is not None:
            nk_ref[...] = k
        if use_rope:
            k = _rope(k, cosk_ref[...], sink_ref[...])
        if past:
            kall_ref[0:past, :] = ck_ref[...].astype(BF16)
            vall_ref[0:past, :] = cv_ref[...].astype(BF16)
        kall_ref[past:, :] = k.astype(BF16)
        vall_ref[past:, :] = v_ref[...].astype(BF16)

    q = _head_rmsnorm(q_ref[...], qg_ref[...])
    if use_rope:
        q = _rope(q, cosq_ref[...], sinq_ref[...])
    q = q * (HEAD_DIM ** -0.5)
    lane = lax.broadcasted_iota(jnp.int32, q.shape, 1)
    lo = lane < HEAD_DIM
    kall = kall_ref[...]
    nt = (((1,), (1,)), ((), ()))

    def probs(qm):
        s = lax.dot_general(qm.astype(BF16), kall, nt, preferred_element_type=F32)
        p = jnp.exp(s - jnp.max(s, axis=-1, keepdims=True))
        return p, jnp.sum(p, axis=-1, keepdims=True)

    p0, l0 = probs(jnp.where(lo, q, 0.0))
    p1, l1 = probs(jnp.where(lo, 0.0, q))
    lv = lam_ref[...]
    lam = (jnp.exp(jnp.sum(lv[0:1] * lv[1:2], axis=-1, keepdims=True))
           - jnp.exp(jnp.sum(lv[2:3] * lv[3:4], axis=-1, keepdims=True)) + lam_init)
    w = p0 * (1.0 / l0) - p1 * (lam / l1)
    o = _dot(w.astype(BF16), vall_ref[...])
    ms = jnp.mean(o * o, axis=-1, keepdims=True)
    o = o * lax.rsqrt(ms + EPS) * sg_ref[...] * (1.0 - lam_init)
    o_ref[...] = o.astype(o_ref.dtype)


def _attention(st, q, k, v, q_g, k_g, lam_vecs, subln_g, lam_init, cache_k=None, cache_v=None,
               rope=None):
    seq, hd = st.seq, N_HEADS * V_DIM
    tq = min(seq, 256)
    nq = seq // tq
    past = 0 if cache_k is None else cache_k.shape[1]
    lk = past + seq
    q3, k3, v3 = (a.reshape(st.batch, seq, hd) for a in (q, k, v))
    qblk = pl.BlockSpec((None, tq, V_DIM), lambda b, h, i: (b, i, h))
    kblk = pl.BlockSpec((None, seq, V_DIM), lambda b, h, i: (b, 0, h))
    in_specs = [qblk, kblk, kblk]
    args = [q3, k3, v3]
    if past:
        cblk = pl.BlockSpec((None, past, V_DIM), lambda b, h, i: (b, 0, h))
        in_specs += [cblk, cblk]
        args += [cache_k.reshape(st.batch, past, hd), cache_v.reshape(st.batch, past, hd)]
    if rope is not None:
        cos2, sin2 = rope
        tq_tab = pl.BlockSpec((tq, V_DIM), lambda b, h, i: (i, 0))
        k_tab = pl.BlockSpec((seq, V_DIM), lambda b, h, i: (0, 0))
        in_specs += [tq_tab, tq_tab, k_tab, k_tab]
        args += [cos2, sin2, cos2, sin2]
    small = lambda shape: pl.BlockSpec(shape, lambda b, h, i: (0,) * len(shape))
    in_specs += [small((1, V_DIM)), small((1, V_DIM)), small((4, HEAD_DIM)), small((1, V_DIM))]
    args += [jnp.tile(q_g, 2).reshape(1, V_DIM), jnp.tile(k_g, 2).reshape(1, V_DIM), lam_vecs,
             subln_g.reshape(1, V_DIM)]
    out_specs = [qblk]
    out_shape = [jax.ShapeDtypeStruct((st.batch, seq, hd), BF16)]
    if not past:
        out_specs.append(kblk)
        out_shape.append(jax.ShapeDtypeStruct((st.batch, seq, hd), F32))
    outs = pl.pallas_call(
        functools.partial(_attention_kernel, past, rope is not None, lam_init),
        grid=(st.batch, N_HEADS, nq),
        in_specs=in_specs,
        out_specs=out_specs,
        out_shape=out_shape,
        scratch_shapes=[pltpu.VMEM((lk, V_DIM), BF16), pltpu.VMEM((lk, V_DIM), BF16)],
        compiler_params=_params("parallel", "parallel", "arbitrary"),
        name="diff_attention",
    )(*args)
    return [o.reshape(st.rows, hd) for o in outs]


def _dft_matrices(seq):
    idx = np.arange(seq, dtype=np.int64)
    ang = (np.outer(idx, idx) % (2 * seq)).astype(np.float64) * (math.pi / seq)
    return jnp.asarray(np.cos(ang), dtype=BF16), jnp.asarray(np.sin(ang), dtype=BF16)


def _filter_features(seq):
    t = jnp.linspace(0.0, 1.0, seq, dtype=F32)[:, None]
    bands = (HY_EMB - 1) // 2
    w_ang = 2.0 * math.pi * jnp.arange(seq, dtype=F32)[:, None] / seq
    f = jnp.linspace(1e-4, bands - 1, bands, dtype=F32)[None, :]
    ang = f * w_ang
    feats = jnp.concatenate([t, jnp.cos(ang), -jnp.sin(ang)], axis=-1)
    return jnp.pad(feats, ((0, 0), (0, V7X_LANES - HY_EMB)))


def _decay_rates():
    min_decay = math.log(HY_DECAY_TARGET) / HY_SLOW_PCT
    max_decay = math.log(HY_DECAY_TARGET) / HY_FAST_PCT
    return jnp.linspace(min_decay, max_decay, D_MODEL, dtype=F32)[None, :]


def _rope_tables(seq):
    rows = seq // GRID_W
    row = jnp.repeat(jnp.arange(rows, dtype=F32), GRID_W)
    col = jnp.tile(jnp.arange(GRID_W, dtype=F32), rows)
    quarter = HEAD_DIM // 4
    inv = ROPE_BASE ** (-jnp.arange(quarter, dtype=F32) / quarter)

    def axis_angles(pos):
        a = pos[:, None] * inv[None, :]
        return jnp.concatenate([a, a], axis=-1)

    ang = jnp.concatenate([axis_angles(row), axis_angles(col)], axis=-1)
    sign = jnp.where((jnp.arange(HEAD_DIM) & quarter) == 0, -1.0, 1.0).astype(F32)
    return jnp.tile(jnp.cos(ang), (1, 2)), jnp.tile(jnp.sin(ang) * sign[None, :], (1, 2))


def kernel(x_prompt, x_sample, cache_k, cache_v, c, c_ctx, ada_w, ada_b, norm_g, hy_in_w, hy_in_b, hy_conv_w, hy_conv_b, hy_f_w1, hy_f_b1, hy_f_w2, hy_f_b2, hy_f_freq, hy_f_w3, hy_bias, hy_out_w, hy_out_b, at_qkv_w, at_q_g, at_k_g, at_lam, at_subln_g, at_out_w, dn_w_gu, dn_w_down, mo_router_w, mo_router_b, mo_w_gu, mo_w_down):
    d = D_MODEL
    batch, seq = x_prompt.shape[:2]
    dec_batch, dec_seq = x_sample.shape[:2]
    past = cache_k.shape[2]
    streams = [
        (_Stream(batch, seq, 0, False), x_prompt.reshape(batch * seq, d), None),
        (_Stream(dec_batch, dec_seq, 1, True), x_sample.reshape(dec_batch * dec_seq, d),
         (cache_k[:, 0].reshape(dec_batch, past, N_HEADS * V_DIM), cache_v[:, 0])),
    ]

    cond = jnp.concatenate(
        [c_ctx[None, :], c, jnp.zeros((COND_ROWS - 1 - dec_batch, d), F32)], axis=0)
    mod = _adaln(cond, ada_w, ada_b)
    mod = mod.reshape(mod.shape[0], COND_ROWS, 1, 6 * d)

    in_w, out_w = hy_in_w[0].astype(BF16), hy_out_w[0].astype(BF16)
    qkv_w, at_out = at_qkv_w[0].astype(BF16), at_out_w[0].astype(BF16)
    dn_gu, dn_down = dn_w_gu.astype(BF16), dn_w_down.astype(BF16)
    mo_gu, mo_down = mo_w_gu[0].astype(BF16), mo_w_down[0].astype(BF16)
    w1_pad = jnp.pad(hy_f_w1[0], ((0, V7X_LANES - HY_EMB), (0, 0)))
    rw_pad = jnp.pad(mo_router_w[0], ((0, 0), (0, V7X_LANES - N_EXPERTS)))
    rb_pad = jnp.pad(mo_router_b[0], (0, V7X_LANES - N_EXPERTS)).reshape(1, V7X_LANES)
    deltas = _decay_rates()
    lam_init = 0.8 - 0.6 * math.exp(-0.3 * 1)

    results = []
    for st, x, cache in streams:
        cmat, smat = _dft_matrices(st.seq)
        hs, hd = _filter_time(st.seq, _filter_features(st.seq), w1_pad, hy_f_b1[0], hy_f_w2[0],
                              hy_f_b2[0], hy_f_freq[0], hy_f_w3[0], deltas)
        kr, ki, kn = _filter_spectrum(st.seq, hs, hd, cmat, smat)
        (proj,) = _mod_matmul(st, x, norm_g[0, 0], mod, 0, in_w, hy_in_b[0], 1, BF16)
        z = _hyena_core(st, proj, hy_conv_w[0], hy_conv_b[0].reshape(1, 3 * d), kr, ki, kn,
                        hy_bias[0], cmat, smat)
        x, h2 = _out_proj(st, z, out_w, hy_out_b[0], x, mod, 0, norm_g[0, 1])
        x = _swiglu(st, h2, dn_gu, dn_down, x, mod, 0)

        q, k, v = _mod_matmul(st, x, norm_g[1, 0], mod, 1, qkv_w, None, 3, F32)
        if cache is None:
            o, new_k = _attention(st, q, k, v, at_q_g[0], at_k_g[0], at_lam[0], at_subln_g[0],
                                  lam_init)
            new_kv = (new_k, v)
        else:
            (o,) = _attention(st, q, k, v, at_q_g[0], at_k_g[0], at_lam[0], at_subln_g[0],
                              lam_init, cache[0], cache[1], _rope_tables(st.seq))
        x, h2, gm = _out_proj(st, o, at_out, None, x, mod, 1, norm_g[1, 1], (rw_pad, rb_pad))
        x = _swiglu(st, h2, mo_gu, mo_down, x, mod, 1, gm)
        results.append(x.reshape(st.batch, st.seq, d))

    new_k, new_v = new_kv
    return (results[0], results[1],
            new_k.reshape(batch, 1, seq, N_HEADS, 2, HEAD_DIM),
            new_v.reshape(batch, 1, seq, N_HEADS, V_DIM))
```

```python
import functools
import math

import numpy as np
import jax
import jax.numpy as jnp
from jax import lax
from jax.experimental import pallas as pl
from jax.experimental.pallas import tpu as pltpu

F32 = jnp.float32
BF16 = jnp.bfloat16
HIGHEST = lax.Precision.HIGHEST

D_MODEL = 1024
GRID_W = 64
HY_ORDER = 2
HY_EMB = 33
HY_FW = 64
HY_DECAY_TARGET = 1e-2
HY_FAST_PCT = 0.3
HY_SLOW_PCT = 1.5
N_HEADS = 8
HEAD_DIM = 64
V_DIM = 2 * HEAD_DIM
ROPE_BASE = 10000.0
D_FF = 2816
N_EXPERTS = 8
D_FF_EXPERT = 3584
EPS = 1e-6

V7X_LANES = 128
V7X_VMEM_LIMIT_BYTES = 56 * 1024 * 1024
LANE_CHUNKS = D_MODEL // V7X_LANES
COND_ROWS = 16
TOKEN_TILE = 512
MOE_TILE = 512


def _params(*semantics):
    return pltpu.CompilerParams(dimension_semantics=semantics,
                                vmem_limit_bytes=V7X_VMEM_LIMIT_BYTES)


def _resident(shape):
    zeros = (0,) * len(shape)
    return pl.BlockSpec(shape, lambda *_: zeros, pipeline_mode=pl.Buffered(1))


def _dot(a, b):
    return jnp.dot(a, b, preferred_element_type=F32)


def _dot_f32(a, b):
    return jnp.dot(a, b, precision=HIGHEST, preferred_element_type=F32)


def _modulate(x, g, shift, scale):
    ms = jnp.mean(x * x, axis=-1, keepdims=True)
    return (x * lax.rsqrt(ms + EPS) * g) * (1.0 + scale) + shift


def _adaln_kernel(cond_ref, w_ref, b_ref, o_ref):
    c = cond_ref[...]
    o_ref[...] = _dot_f32(c * jax.nn.sigmoid(c), w_ref[...]) + b_ref[...]


def _adaln(cond, ada_w, ada_b):
    depth, d, n = ada_w.shape
    tn = 1536
    return pl.pallas_call(
        _adaln_kernel,
        grid=(depth, n // tn),
        in_specs=[
            pl.BlockSpec((COND_ROWS, d), lambda i, j: (0, 0)),
            pl.BlockSpec((None, d, tn), lambda i, j: (i, 0, j)),
            pl.BlockSpec((None, 1, tn), lambda i, j: (i, 0, j)),
        ],
        out_specs=pl.BlockSpec((None, COND_ROWS, tn), lambda i, j: (i, 0, j)),
        out_shape=jax.ShapeDtypeStruct((depth, COND_ROWS, n), F32),
        compiler_params=_params("parallel", "parallel"),
        name="adaln",
    )(cond, ada_w, ada_b.reshape(depth, 1, n))


class _Stream:
    def __init__(self, batch, seq, cond_row0, per_seq_cond, row0=0):
        self.batch, self.seq = batch, seq
        self.rows = batch * seq
        self.row0 = row0
        if per_seq_cond:
            self.tm = min(TOKEN_TILE, seq)
            tiles_per_seq = seq // self.tm
            self.cond_row = lambda i: cond_row0 + i // tiles_per_seq
        else:
            self.tm = min(TOKEN_TILE, self.rows)
            self.cond_row = lambda i: cond_row0
        self.tiles = self.rows // self.tm

    def mod_spec(self, layer, chunk):
        return pl.BlockSpec((None, None, 1, D_MODEL),
                            lambda i, *_: (layer, self.cond_row(i), 0, chunk))

    def row_spec(self, width):
        return pl.BlockSpec((self.tm, width), lambda i, *_: (i, 0))


def _mod_matmul_kernel(n_out, has_bias, x_ref, g_ref, sh_ref, sc_ref, w_ref, *rest):
    if has_bias:
        b_ref, out_refs = rest[0], rest[1:]
    else:
        b_ref, out_refs = None, rest
    h = _modulate(x_ref[...], g_ref[...], sh_ref[...], sc_ref[...])
    y = _dot(h.astype(BF16), w_ref[...])
    if has_bias:
        y = y + b_ref[...]
    width = y.shape[1] // n_out
    for k, o_ref in enumerate(out_refs):
        o_ref[...] = y[:, k * width:(k + 1) * width].astype(o_ref.dtype)


def _mod_matmul(st, x, norm_g, mod, layer, w_bf16, bias, n_out, out_dtype):
    d, n = w_bf16.shape
    in_specs = [st.row_spec(d), _resident((1, d)), st.mod_spec(layer, 0), st.mod_spec(layer, 1),
                _resident((d, n))]
    args = [x, norm_g.reshape(1, d), mod, mod, w_bf16]
    if bias is not None:
        in_specs.append(_resident((1, n)))
        args.append(bias.reshape(1, n))
    width = n // n_out
    outs = pl.pallas_call(
        functools.partial(_mod_matmul_kernel, n_out, bias is not None),
        grid=(st.tiles,),
        in_specs=in_specs,
        out_specs=[st.row_spec(width)] * n_out,
        out_shape=[jax.ShapeDtypeStruct((st.rows, width), out_dtype)] * n_out,
        compiler_params=_params("parallel"),
        name="mod_matmul",
    )(*args)
    return outs


def _filter_time_kernel(feats_ref, w1_ref, b1_ref, w2_ref, b2_ref, fr_ref, w3_ref, dl_ref,
                        hs_ref, hd_ref):
    feats = feats_ref[...]
    fr = fr_ref[...]
    h = jnp.sin(fr[0:1] * (_dot_f32(feats, w1_ref[...]) + b1_ref[...]))
    h = jnp.sin(fr[1:2] * (_dot_f32(h, w2_ref[...]) + b2_ref[...]))
    h = _dot_f32(h, w3_ref[...])
    t = feats[:, 0:1]
    decay = jnp.exp(-t * jnp.abs(dl_ref[...]))
    half = HY_ORDER * D_MODEL
    decay2 = jnp.concatenate([decay] * HY_ORDER, axis=1)
    fwd = h[:, :half] * decay2
    bwd = jnp.where(t == 0.0, 0.0, h[:, half:] * decay2)
    hs_ref[...] = fwd + bwd
    hd_ref[...] = fwd - bwd


def _filter_time(seq, feats_pad, w1_pad, b1, w2, b2, freq, w3, deltas):
    tl = min(seq, 256)
    half = HY_ORDER * D_MODEL
    out = jax.ShapeDtypeStruct((seq, half), F32)
    return pl.pallas_call(
        _filter_time_kernel,
        grid=(seq // tl,),
        in_specs=[
            pl.BlockSpec((tl, V7X_LANES), lambda i: (i, 0)),
            _resident(w1_pad.shape), _resident((1, HY_FW)), _resident((HY_FW, HY_FW)),
            _resident((1, HY_FW)), _resident((2, HY_FW)), _resident(w3.shape),
            _resident((1, D_MODEL)),
        ],
        out_specs=[pl.BlockSpec((tl, half), lambda i: (i, 0))] * 2,
        out_shape=[out, out],
        compiler_params=_params("parallel"),
        name="hyena_filter_time",
    )(feats_pad, w1_pad, b1.reshape(1, HY_FW), w2, b2.reshape(1, HY_FW), freq, w3, deltas)


def _filter_spectrum_kernel(seq, hs_ref, hd_ref, c_ref, s_ref, kr_ref, ki_ref, kn_ref):
    hs = hs_ref[...]
    row = lax.broadcasted_iota(jnp.int32, (seq, 1), 0)
    wgt = jnp.where(row == 0, 1.0, 2.0) * (1.0 / (2 * seq))
    kr_ref[...] = wgt * _dot(c_ref[...], hs.astype(BF16))
    ki_ref[...] = -wgt * _dot(s_ref[...], hd_ref[...].astype(BF16))
    sign = (1 - 2 * (row & 1)).astype(F32)
    kn_ref[...] = jnp.sum(hs * sign, axis=0, keepdims=True) * (1.0 / (2 * seq))


def _filter_spectrum(seq, hs, hd, cmat, smat):
    half = hs.shape[1]
    tn = 256
    col = pl.BlockSpec((seq, tn), lambda j: (0, j))
    return pl.pallas_call(
        functools.partial(_filter_spectrum_kernel, seq),
        grid=(half // tn,),
        in_specs=[col, col, _resident((seq, seq)), _resident((seq, seq))],
        out_specs=[col, col, pl.BlockSpec((1, tn), lambda j: (0, j))],
        out_shape=[jax.ShapeDtypeStruct((seq, half), F32)] * 2
        + [jax.ShapeDtypeStruct((1, half), F32)],
        compiler_params=_params("parallel"),
        name="hyena_filter_spectrum",
    )(hs, hd, cmat, smat)


def _hyena_core_kernel(seq, chunk, pv_ref, p1_ref, p2_ref, cwv_ref, cw1_ref, cw2_ref, cbv_ref,
                       cb1_ref, cb2_ref, kr0_ref, ki0_ref, kn0_ref, kr1_ref, ki1_ref, kn1_ref,
                       bias0_ref, bias1_ref, c_ref, s_ref, z_ref,
                       u_ref, ub_ref, gate_ref, p_ref, q_ref):
    row = lax.broadcasted_iota(jnp.int32, (seq, 1), 0)
    sign = (1 - 2 * (row & 1)).astype(F32)
    chunks = [slice(r, r + chunk) for r in range(0, seq, chunk)]

    def short_conv(x_ref, w_ref, b_ref):
        x = x_ref[...].astype(F32)
        w = w_ref[...]
        prev = jnp.where(row == 0, 0.0, pltpu.roll(x, 1, 0))
        nxt = jnp.where(row == seq - 1, 0.0, pltpu.roll(x, seq - 1, 0))
        return prev * w[0:1] + x * w[1:2] + nxt * w[2:3] + b_ref[...]

    def gated_long_conv(kr_ref, ki_ref, kn_ref, bias_ref, write):
        u = u_ref[...]
        ub_ref[...] = u.astype(BF16)
        nyq = jnp.sum(u * sign, axis=0, keepdims=True) * kn_ref[...]
        for rows in chunks:
            a = _dot(c_ref[rows, :], ub_ref[...])
            b = _dot(s_ref[rows, :], ub_ref[...])
            kr, ki = kr_ref[rows, :], ki_ref[rows, :]
            p_ref[rows, :] = (a * kr + b * ki).astype(BF16)
            q_ref[rows, :] = (b * kr - a * ki).astype(BF16)
        for rows in chunks:
            y = _dot(c_ref[rows, :], p_ref[...]) + _dot(s_ref[rows, :], q_ref[...])
            y = y + sign[rows, :] * nyq + u_ref[rows, :] * bias_ref[...]
            write(rows, gate_ref[rows, :] * y)

    def to_u(rows, val):
        u_ref[rows, :] = val

    def to_z(rows, val):
        z_ref[rows, :] = val.astype(z_ref.dtype)

    u_ref[...] = short_conv(pv_ref, cwv_ref, cbv_ref)
    gate_ref[...] = short_conv(p1_ref, cw1_ref, cb1_ref)
    gated_long_conv(kr0_ref, ki0_ref, kn0_ref, bias0_ref, to_u)
    gate_ref[...] = short_conv(p2_ref, cw2_ref, cb2_ref)
    gated_long_conv(kr1_ref, ki1_ref, kn1_ref, bias1_ref, to_z)


def _hyena_core(st, proj, conv_w, conv_b, kr, ki, kn, bias, cmat, smat):
    seq, d = st.seq, D_MODEL
    tn = 256
    nj = d // tn
    proj3 = proj.reshape(st.batch, seq, 3 * d)

    def part(k):
        return pl.BlockSpec((None, seq, tn), lambda j, b: (b, 0, k * nj + j))

    def cols(rows, k, buffers=2):
        return pl.BlockSpec((rows, tn), lambda j, b: (0, k * nj + j),
                            pipeline_mode=pl.Buffered(buffers))

    in_specs = ([part(0), part(1), part(2)]
                + [cols(3, k) for k in range(3)] + [cols(1, k) for k in range(3)]
                + [cols(seq, 0, 1), cols(seq, 0, 1), cols(1, 0), cols(seq, 1, 1), cols(seq, 1, 1),
                   cols(1, 1)]
                + [cols(1, 0), cols(1, 0)]
                + [_resident((seq, seq)), _resident((seq, seq))])
    z = pl.pallas_call(
        functools.partial(_hyena_core_kernel, seq, min(seq, 512)),
        grid=(nj, st.batch),
        in_specs=in_specs,
        out_specs=pl.BlockSpec((None, seq, tn), lambda j, b: (b, 0, j)),
        out_shape=jax.ShapeDtypeStruct((st.batch, seq, d), BF16),
        scratch_shapes=[pltpu.VMEM((seq, tn), F32), pltpu.VMEM((seq, tn), BF16),
                        pltpu.VMEM((seq, tn), F32), pltpu.VMEM((seq, tn), BF16),
                        pltpu.VMEM((seq, tn), BF16)],
        compiler_params=_params("parallel", "parallel"),
        name="hyena_core",
    )(proj3, proj3, proj3, conv_w, conv_w, conv_w, conv_b, conv_b, conv_b,
      kr, ki, kn, kr, ki, kn, bias[0:1], bias[1:2], cmat, smat)
    return z.reshape(st.rows, d)


def _out_proj_kernel(has_bias, has_router, *refs):
    refs = list(refs)
    z_ref, w_ref = refs.pop(0), refs.pop(0)
    b_ref = refs.pop(0) if has_bias else None
    x_ref, g1_ref, ng_ref, sh_ref, sc_ref = (refs.pop(0) for _ in range(5))
    if has_router:
        rw_ref, rb_ref = refs.pop(0), refs.pop(0)
        refs.pop(0)
    x1_ref, h2_ref = refs.pop(0), refs.pop(0)
    m = _dot(z_ref[...], w_ref[...])
    if has_bias:
        m = m + b_ref[...]
    x1 = x_ref[...] + g1_ref[...] * m
    x1_ref[...] = x1
    h2 = _modulate(x1, ng_ref[...], sh_ref[...], sc_ref[...])
    if not has_router:
        h2_ref[...] = h2.astype(BF16)
        return
    _store_token_tiles(h2_ref, h2)
    route_ref = refs.pop(0)
    logits = _dot_f32(h2, rw_ref[...]) + rb_ref[...]
    lane = lax.broadcasted_iota(jnp.int32, logits.shape, 1)
    neg = -jnp.inf
    logits = jnp.where(lane < N_EXPERTS, logits, neg)
    m1 = jnp.max(logits, axis=-1, keepdims=True)
    i1 = jnp.min(jnp.where(logits == m1, lane, V7X_LANES), axis=-1, keepdims=True)
    rest = jnp.where(lane == i1, neg, logits)
    m2 = jnp.max(rest, axis=-1, keepdims=True)
    i2 = jnp.min(jnp.where(rest == m2, lane, V7X_LANES), axis=-1, keepdims=True)
    e2 = jnp.exp(m2 - m1)
    den = 1.0 + e2
    route = jnp.where(lane == 0, i1.astype(F32), jnp.where(lane == 1, i2.astype(F32), 0.0))
    route_ref[...] = route + jnp.where(lane == 2, 1.0 / den, 0.0) + jnp.where(lane == 3, e2 / den, 0.0)


def _store_token_tiles(ref, val):
    rows = val.shape[0]
    for c in range(LANE_CHUNKS):
        ref[pl.ds(c, rows, stride=LANE_CHUNKS), :] = val[:, c * V7X_LANES:(c + 1) * V7X_LANES]


def _out_proj(st, z_bf16, w_bf16, bias, x, mod, layer, norm_g, router=None):
    d = D_MODEL
    in_specs = [st.row_spec(z_bf16.shape[1]), _resident(w_bf16.shape)]
    args = [z_bf16, w_bf16]
    if bias is not None:
        in_specs.append(_resident((1, d)))
        args.append(bias.reshape(1, d))
    in_specs += [st.row_spec(d), st.mod_spec(layer, 2), _resident((1, d)),
                 st.mod_spec(layer, 3), st.mod_spec(layer, 4)]
    args += [x, mod, norm_g.reshape(1, d), mod, mod]
    aliases = {}
    if router is None:
        out_specs = [st.row_spec(d), st.row_spec(d)]
        out_shape = [jax.ShapeDtypeStruct((st.rows, d), F32),
                     jax.ShapeDtypeStruct((st.rows, d), BF16)]
    else:
        rw_pad, rb_pad, pool = router
        in_specs += [_resident(rw_pad.shape), _resident(rb_pad.shape),
                     pl.BlockSpec(memory_space=pl.ANY)]
        args += [rw_pad, rb_pad, pool]
        aliases = {len(args) - 1: 1}
        tile0 = st.row0 // st.tm
        out_specs = [st.row_spec(d),
                     pl.BlockSpec((st.tm * LANE_CHUNKS, V7X_LANES), lambda i: (tile0 + i, 0)),
                     st.row_spec(V7X_LANES)]
        out_shape = [jax.ShapeDtypeStruct((st.rows, d), F32),
                     jax.ShapeDtypeStruct(pool.shape, F32),
                     jax.ShapeDtypeStruct((st.rows, V7X_LANES), F32)]
    return pl.pallas_call(
        functools.partial(_out_proj_kernel, bias is not None, router is not None),
        grid=(st.tiles,),
        in_specs=in_specs,
        out_specs=out_specs,
        out_shape=out_shape,
        input_output_aliases=aliases,
        compiler_params=_params("parallel"),
        name="out_proj",
    )(*args)


def _swiglu_part(x_bf16, wg_ref, wu_ref, wd_ref):
    g = _dot(x_bf16, wg_ref[...])
    u = _dot(x_bf16, wu_ref[...])
    return _dot((g * jax.nn.sigmoid(g) * u).astype(BF16), wd_ref[...])


def _swiglu_kernel(h_ref, wg_ref, wu_ref, wd_ref, x_ref, g2_ref, o_ref, acc_ref):
    j = pl.program_id(1)
    part = _swiglu_part(h_ref[...], wg_ref, wu_ref, wd_ref)

    @pl.when(j == 0)
    def _():
        acc_ref[...] = part

    @pl.when(j > 0)
    def _():
        acc_ref[...] += part

    @pl.when(j == pl.num_programs(1) - 1)
    def _():
        o_ref[...] = x_ref[...] + g2_ref[...] * acc_ref[...]


def _swiglu(st, h2, w_gu, w_down, x1, mod, layer):
    d, two_f = w_gu.shape
    f = two_f // 2
    tf = 256
    nf = f // tf
    return pl.pallas_call(
        _swiglu_kernel,
        grid=(st.tiles, nf),
        in_specs=[
            pl.BlockSpec((st.tm, d), lambda i, j: (i, 0)),
            pl.BlockSpec((d, tf), lambda i, j: (0, j)),
            pl.BlockSpec((d, tf), lambda i, j: (0, nf + j)),
            pl.BlockSpec((tf, d), lambda i, j: (j, 0)),
            pl.BlockSpec((st.tm, d), lambda i, j: (i, 0)),
            st.mod_spec(layer, 5),
        ],
        out_specs=pl.BlockSpec((st.tm, d), lambda i, j: (i, 0)),
        out_shape=jax.ShapeDtypeStruct((st.rows, d), F32),
        scratch_shapes=[pltpu.VMEM((st.tm, d), F32)],
        compiler_params=_params("parallel", "arbitrary"),
        name="swiglu",
    )(h2, w_gu, w_gu, w_down, x1, mod)


def _moe_plan(route, tm):
    tokens = route.shape[0]
    max_tiles = (2 * tokens) // tm + N_EXPERTS
    expert = route[:, :2].astype(jnp.int32).reshape(-1)
    onehot = (expert[:, None] == jnp.arange(N_EXPERTS, dtype=jnp.int32)[None, :]).astype(jnp.int32)
    csum = jnp.cumsum(onehot, axis=0)
    rank = jnp.sum(csum * onehot, axis=1) - 1
    counts = csum[-1]
    tiles_per_expert = (counts + tm - 1) // tm
    tiles_end = jnp.cumsum(tiles_per_expert)
    start = (tiles_end - tiles_per_expert) * tm
    pos = jnp.sum(start[None, :] * onehot, axis=1) + rank
    token = jnp.arange(2 * tokens, dtype=jnp.int32) // 2
    src = jnp.zeros((max_tiles * tm,), jnp.int32).at[pos].set(token, unique_indices=True)
    n_tiles = tiles_end[-1:]
    tile = jnp.minimum(jnp.arange(max_tiles, dtype=jnp.int32), n_tiles - 1)
    tile_expert = jnp.sum((tile[:, None] >= tiles_end[None, :]).astype(jnp.int32), axis=1)
    return src, pos.astype(jnp.int32), tile_expert.astype(jnp.int32), n_tiles.astype(jnp.int32)


def _token_tile_copy(src_hbm, row, dst, slot_row, sem):
    return pltpu.make_async_copy(
        src_hbm.at[pl.ds(pl.multiple_of(row * LANE_CHUNKS, LANE_CHUNKS), LANE_CHUNKS)],
        dst.at[pl.ds(pl.multiple_of(slot_row * LANE_CHUNKS, LANE_CHUNKS), LANE_CHUNKS)],
        sem)


def _moe_ffn_kernel(tm, src_ref, te_ref, nt_ref, x_hbm, wg_ref, wu_ref, wd_ref, y_ref,
                    xbuf, xd_ref, acc_ref, sem):
    t, j = pl.program_id(0), pl.program_id(1)
    last_j = pl.num_programs(1) - 1
    n_tiles = nt_ref[0]
    slot = t % 2
    rows = tm * LANE_CHUNKS

    def start_gather(tile, into):
        def body(r, carry):
            _token_tile_copy(x_hbm, src_ref[tile * tm + r], xbuf.at[into], r, sem.at[into]).start()
            return carry
        lax.fori_loop(0, tm, body, 0, unroll=8)

    @pl.when((t == 0) & (j == 0))
    def _():
        start_gather(0, 0)

    @pl.when((j == 0) & (t < n_tiles))
    def _():
        pltpu.make_async_copy(x_hbm.at[pl.ds(0, rows)], xbuf.at[slot], sem.at[slot]).wait()
        for c in range(LANE_CHUNKS):
            xd_ref[:, c * V7X_LANES:(c + 1) * V7X_LANES] = (
                xbuf[slot, pl.ds(c, tm, stride=LANE_CHUNKS), :].astype(BF16))

        @pl.when(t + 1 < n_tiles)
        def _():
            start_gather(t + 1, 1 - slot)

    @pl.when(t < n_tiles)
    def _():
        part = _swiglu_part(xd_ref[...], wg_ref, wu_ref, wd_ref)

        @pl.when(j == 0)
        def _():
            acc_ref[...] = part

        @pl.when(j > 0)
        def _():
            acc_ref[...] += part

        @pl.when(j == last_j)
        def _():
            _store_token_tiles(y_ref, acc_ref[...])

    @pl.when((t >= n_tiles) & (j == last_j))
    def _():
        y_ref[...] = jnp.zeros_like(y_ref)


def _moe_ffn(pool, src, tile_expert, n_tiles, w_gu, w_down, tm):
    n_e, d, two_f = w_gu.shape
    f = two_f // 2
    tf = 512
    nf = f // tf
    max_tiles = tile_expert.shape[0]
    rows = tm * LANE_CHUNKS
    grid_spec = pltpu.PrefetchScalarGridSpec(
        num_scalar_prefetch=3,
        grid=(max_tiles, nf),
        in_specs=[
            pl.BlockSpec(memory_space=pl.ANY),
            pl.BlockSpec((None, d, tf), lambda t, j, src, te, nt: (te[t], 0, j)),
            pl.BlockSpec((None, d, tf), lambda t, j, src, te, nt: (te[t], 0, nf + j)),
            pl.BlockSpec((None, tf, d), lambda t, j, src, te, nt: (te[t], j, 0)),
        ],
        out_specs=pl.BlockSpec((rows, V7X_LANES), lambda t, j, src, te, nt: (t, 0)),
        scratch_shapes=[pltpu.VMEM((2, rows, V7X_LANES), F32), pltpu.VMEM((tm, d), BF16),
                        pltpu.VMEM((tm, d), F32), pltpu.SemaphoreType.DMA((2,))],
    )
    return pl.pallas_call(
        functools.partial(_moe_ffn_kernel, tm),
        grid_spec=grid_spec,
        out_shape=jax.ShapeDtypeStruct((max_tiles * rows, V7X_LANES), F32),
        compiler_params=_params("arbitrary", "arbitrary"),
        name="moe_ffn",
    )(src, tile_expert, n_tiles, pool, w_gu, w_gu, w_down)


def _moe_combine_kernel(tm, token0, pos_ref, y_hbm, route_ref, x_ref, g2_ref, o_ref, ybuf, sem):
    i = pl.program_id(0)
    slot = i % 2
    rows = tm * LANE_CHUNKS

    def start_gather(tile, into):
        def body(r, carry):
            a = 2 * (token0 + tile * tm + r)
            for k in range(2):
                _token_tile_copy(y_hbm, pos_ref[a + k], ybuf.at[into], k * tm + r,
                                 sem.at[into]).start()
            return carry
        lax.fori_loop(0, tm, body, 0, unroll=4)

    @pl.when(i == 0)
    def _():
        start_gather(0, 0)

    pltpu.make_async_copy(y_hbm.at[pl.ds(0, 2 * rows)], ybuf.at[slot], sem.at[slot]).wait()

    @pl.when(i + 1 < pl.num_programs(0))
    def _():
        start_gather(i + 1, 1 - slot)

    route = route_ref[...]
    lane = lax.broadcasted_iota(jnp.int32, route.shape, 1)
    gate0 = jnp.sum(jnp.where(lane == 2, route, 0.0), axis=-1, keepdims=True)
    gate1 = jnp.sum(jnp.where(lane == 3, route, 0.0), axis=-1, keepdims=True)
    for c in range(LANE_CHUNKS):
        cols = slice(c * V7X_LANES, (c + 1) * V7X_LANES)
        y0 = ybuf[slot, pl.ds(c, tm, stride=LANE_CHUNKS), :]
        y1 = ybuf[slot, pl.ds(rows + c, tm, stride=LANE_CHUNKS), :]
        o_ref[:, cols] = x_ref[:, cols] + g2_ref[:, cols] * (gate0 * y0 + gate1 * y1)


def _moe_combine(st, y_slots, pos, route, x1, mod, layer):
    d = D_MODEL
    tm = st.tm
    rows = tm * LANE_CHUNKS
    grid_spec = pltpu.PrefetchScalarGridSpec(
        num_scalar_prefetch=1,
        grid=(st.tiles,),
        in_specs=[
            pl.BlockSpec(memory_space=pl.ANY),
            st.row_spec(V7X_LANES), st.row_spec(d), st.mod_spec(layer, 5),
        ],
        out_specs=st.row_spec(d),
        scratch_shapes=[pltpu.VMEM((2, 2 * rows, V7X_LANES), F32), pltpu.SemaphoreType.DMA((2,))],
    )
    return pl.pallas_call(
        functools.partial(_moe_combine_kernel, tm, st.row0),
        grid_spec=grid_spec,
        out_shape=jax.ShapeDtypeStruct((st.rows, d), F32),
        compiler_params=_params("arbitrary"),
        name="moe_combine",
    )(pos, y_slots, route, x1, mod)


def _head_rmsnorm(x, g2):
    lane = lax.broadcasted_iota(jnp.int32, x.shape, 1)
    lo = lane < HEAD_DIM
    sq = x * x
    s_lo = jnp.sum(jnp.where(lo, sq, 0.0), axis=-1, keepdims=True)
    s_hi = jnp.sum(jnp.where(lo, 0.0, sq), axis=-1, keepdims=True)
    ms = jnp.where(lo, s_lo, s_hi) * (1.0 / HEAD_DIM)
    return x * lax.rsqrt(ms + EPS) * g2


def _rope(x, cos, sin_signed):
    q4 = HEAD_DIM // 4
    lane = lax.broadcasted_iota(jnp.int32, x.shape, 1)
    first = (lane & q4) == 0
    width = x.shape[1]
    partner = jnp.where(first, pltpu.roll(x, width - q4, 1), pltpu.roll(x, q4, 1))
    return x * cos + partner * sin_signed


def _attention_kernel(past, use_rope, lam_init, *refs):
    refs = list(refs)
    q_ref, k_ref, v_ref = refs.pop(0), refs.pop(0), refs.pop(0)
    if past:
        ck_ref, cv_ref = refs.pop(0), refs.pop(0)
    if use_rope:
        cosq_ref, sinq_ref, cosk_ref, sink_ref = (refs.pop(0) for _ in range(4))
    qg_ref, kg_ref, lam_ref, sg_ref = (refs.pop(0) for _ in range(4))
    o_ref = refs.pop(0)
    nk_ref = None if past else refs.pop(0)
    kall_ref, vall_ref = refs

    qi = pl.program_id(2)

    @pl.when(qi == 0)
    def _():
        k = _head_rmsnorm(k_ref[...], kg_ref[...])
        if nk_ref is not None:
            nk_ref[...] = k
        if use_rope:
            k = _rope(k, cosk_ref[...], sink_ref[...])
        if past:
            kall_ref[0:past, :] = ck_ref[...].astype(BF16)
            vall_ref[0:past, :] = cv_ref[...].astype(BF16)
        kall_ref[past:, :] = k.astype(BF16)
        vall_ref[past:, :] = v_ref[...].astype(BF16)

    q = _head_rmsnorm(q_ref[...], qg_ref[...])
    if use_rope:
        q = _rope(q, cosq_ref[...], sinq_ref[...])
    q = q * (HEAD_DIM ** -0.5)
    lane = lax.broadcasted_iota(jnp.int32, q.shape, 1)
    lo = lane < HEAD_DIM
    kall = kall_ref[...]
    nt = (((1,), (1,)), ((), ()))

    def probs(qm):
        s = lax.dot_general(qm.astype(BF16), kall, nt, preferred_element_type=F32)
        p = jnp.exp(s - jnp.max(s, axis=-1, keepdims=True))
        return p, jnp.sum(p, axis=-1, keepdims=True)

    p0, l0 = probs(jnp.where(lo, q, 0.0))
    p1, l1 = probs(jnp.where(lo, 0.0, q))
    lv = lam_ref[...]
    lam = (jnp.exp(jnp.sum(lv[0:1] * lv[1:2], axis=-1, keepdims=True))
           - jnp.exp(jnp.sum(lv[2:3] * lv[3:4], axis=-1, keepdims=True)) + lam_init)
    w = p0 * (1.0 / l0) - p1 * (lam / l1)
    o = _dot(w.astype(BF16), vall_ref[...])
    ms = jnp.mean(o * o, axis=-1, keepdims=True)
    o = o * lax.rsqrt(ms + EPS) * sg_ref[...] * (1.0 - lam_init)
    o_ref[...] = o.astype(o_ref.dtype)


def _attention(st, q, k, v, q_g, k_g, lam_vecs, subln_g, lam_init, cache_k=None, cache_v=None,
               rope=None):
    seq, hd = st.seq, N_HEADS * V_DIM
    tq = min(seq, 256)
    nq = seq // tq
    past = 0 if cache_k is None else cache_k.shape[1]
    lk = past + seq
    q3, k3, v3 = (a.reshape(st.batch, seq, hd) for a in (q, k, v))
    qblk = pl.BlockSpec((None, tq, V_DIM), lambda b, h, i: (b, i, h))
    kblk = pl.BlockSpec((None, seq, V_DIM), lambda b, h, i: (b, 0, h))
    in_specs = [qblk, kblk, kblk]
    args = [q3, k3, v3]
    if past:
        cblk = pl.BlockSpec((None, past, V_DIM), lambda b, h, i: (b, 0, h))
        in_specs += [cblk, cblk]
        args += [cache_k.reshape(st.batch, past, hd), cache_v.reshape(st.batch, past, hd)]
    if rope is not None:
        cos2, sin2 = rope
        tq_tab = pl.BlockSpec((tq, V_DIM), lambda b, h, i: (i, 0))
        k_tab = pl.BlockSpec((seq, V_DIM), lambda b, h, i: (0, 0))
        in_specs += [tq_tab, tq_tab, k_tab, k_tab]
        args += [cos2, sin2, cos2, sin2]
    small = lambda shape: pl.BlockSpec(shape, lambda b, h, i: (0,) * len(shape))
    in_specs += [small((1, V_DIM)), small((1, V_DIM)), small((4, HEAD_DIM)), small((1, V_DIM))]
    args += [jnp.tile(q_g, 2).reshape(1, V_DIM), jnp.tile(k_g, 2).reshape(1, V_DIM), lam_vecs,
             subln_g.reshape(1, V_DIM)]
    out_specs = [qblk]
    out_shape = [jax.ShapeDtypeStruct((st.batch, seq, hd), BF16)]
    if not past:
        out_specs.append(kblk)
        out_shape.append(jax.ShapeDtypeStruct((st.batch, seq, hd), F32))
    outs = pl.pallas_call(
        functools.partial(_attention_kernel, past, rope is not None, lam_init),
        grid=(st.batch, N_HEADS, nq),
        in_specs=in_specs,
        out_specs=out_specs,
        out_shape=out_shape,
        scratch_shapes=[pltpu.VMEM((lk, V_DIM), BF16), pltpu.VMEM((lk, V_DIM), BF16)],
        compiler_params=_params("parallel", "parallel", "arbitrary"),
        name="diff_attention",
    )(*args)
    return [o.reshape(st.rows, hd) for o in outs]


def _dft_matrices(seq):
    idx = np.arange(seq, dtype=np.int64)
    ang = (np.outer(idx, idx) % (2 * seq)).astype(np.float64) * (math.pi / seq)
    return jnp.asarray(np.cos(ang), dtype=BF16), jnp.asarray(np.sin(ang), dtype=BF16)


def _filter_features(seq):
    t = jnp.linspace(0.0, 1.0, seq, dtype=F32)[:, None]
    bands = (HY_EMB - 1) // 2
    w_ang = 2.0 * math.pi * jnp.arange(seq, dtype=F32)[:, None] / seq
    f = jnp.linspace(1e-4, bands - 1, bands, dtype=F32)[None, :]
    ang = f * w_ang
    feats = jnp.concatenate([t, jnp.cos(ang), -jnp.sin(ang)], axis=-1)
    return jnp.pad(feats, ((0, 0), (0, V7X_LANES - HY_EMB)))


def _decay_rates():
    min_decay = math.log(HY_DECAY_TARGET) / HY_SLOW_PCT
    max_decay = math.log(HY_DECAY_TARGET) / HY_FAST_PCT
    return jnp.linspace(min_decay, max_decay, D_MODEL, dtype=F32)[None, :]


def _rope_tables(seq):
    rows = seq // GRID_W
    row = jnp.repeat(jnp.arange(rows, dtype=F32), GRID_W)
    col = jnp.tile(jnp.arange(GRID_W, dtype=F32), rows)
    quarter = HEAD_DIM // 4
    inv = ROPE_BASE ** (-jnp.arange(quarter, dtype=F32) / quarter)

    def axis_angles(pos):
        a = pos[:, None] * inv[None, :]
        return jnp.concatenate([a, a], axis=-1)

    ang = jnp.concatenate([axis_angles(row), axis_angles(col)], axis=-1)
    sign = jnp.where((jnp.arange(HEAD_DIM) & quarter) == 0, -1.0, 1.0).astype(F32)
    return jnp.tile(jnp.cos(ang), (1, 2)), jnp.tile(jnp.sin(ang) * sign[None, :], (1, 2))


def kernel(x_prompt, x_sample, cache_k, cache_v, c, c_ctx, ada_w, ada_b, norm_g, hy_in_w, hy_in_b, hy_conv_w, hy_conv_b, hy_f_w1, hy_f_b1, hy_f_w2, hy_f_b2, hy_f_freq, hy_f_w3, hy_bias, hy_out_w, hy_out_b, at_qkv_w, at_q_g, at_k_g, at_lam, at_subln_g, at_out_w, dn_w_gu, dn_w_down, mo_router_w, mo_router_b, mo_w_gu, mo_w_down):
    d = D_MODEL
    batch, seq = x_prompt.shape[:2]
    dec_batch, dec_seq = x_sample.shape[:2]
    past = cache_k.shape[2]
    streams = [
        (_Stream(batch, seq, 0, False), x_prompt.reshape(batch * seq, d), None),
        (_Stream(dec_batch, dec_seq, 1, True, row0=batch * seq),
         x_sample.reshape(dec_batch * dec_seq, d), (cache_k[:, 0], cache_v[:, 0])),
    ]
    tokens = batch * seq + dec_batch * dec_seq

    cond = jnp.concatenate(
        [c_ctx[None, :], c, jnp.zeros((COND_ROWS - 1 - dec_batch, d), F32)], axis=0)
    mod = _adaln(cond, ada_w, ada_b)
    mod = mod.reshape(mod.shape[0], COND_ROWS, 1, 6 * d)

    in_w, out_w = hy_in_w[0].astype(BF16), hy_out_w[0].astype(BF16)
    qkv_w, at_out = at_qkv_w[0].astype(BF16), at_out_w[0].astype(BF16)
    dn_gu, dn_down = dn_w_gu[0].astype(BF16), dn_w_down[0].astype(BF16)
    mo_gu, mo_down = mo_w_gu[0].astype(BF16), mo_w_down[0].astype(BF16)
    w1_pad = jnp.pad(hy_f_w1[0], ((0, V7X_LANES - HY_EMB), (0, 0)))
    rw_pad = jnp.pad(mo_router_w[0], ((0, 0), (0, V7X_LANES - N_EXPERTS)))
    rb_pad = jnp.pad(mo_router_b[0], (0, V7X_LANES - N_EXPERTS)).reshape(1, V7X_LANES)
    deltas = _decay_rates()
    lam_init = 0.8 - 0.6 * math.exp(-0.3 * 1)

    routed = []
    pool = jnp.zeros((tokens * LANE_CHUNKS, V7X_LANES), F32)
    for st, x, cache in streams:
        cmat, smat = _dft_matrices(st.seq)
        hs, hd = _filter_time(st.seq, _filter_features(st.seq), w1_pad, hy_f_b1[0], hy_f_w2[0],
                              hy_f_b2[0], hy_f_freq[0], hy_f_w3[0], deltas)
        kr, ki, kn = _filter_spectrum(st.seq, hs, hd, cmat, smat)
        (proj,) = _mod_matmul(st, x, norm_g[0, 0], mod, 0, in_w, hy_in_b[0], 1, BF16)
        z = _hyena_core(st, proj, hy_conv_w[0], hy_conv_b[0].reshape(1, 3 * d), kr, ki, kn,
                        hy_bias[0], cmat, smat)
        x, h2 = _out_proj(st, z, out_w, hy_out_b[0], x, mod, 0, norm_g[0, 1])
        x = _swiglu(st, h2, dn_gu, dn_down, x, mod, 0)

        q, k, v = _mod_matmul(st, x, norm_g[1, 0], mod, 1, qkv_w, None, 3, F32)
        if cache is None:
            o, new_k = _attention(st, q, k, v, at_q_g[0], at_k_g[0], at_lam[0], at_subln_g[0],
                                  lam_init)
            new_kv = (new_k, v)
        else:
            (o,) = _attention(st, q, k, v, at_q_g[0], at_k_g[0], at_lam[0], at_subln_g[0],
                              lam_init, cache[0], cache[1], _rope_tables(st.seq))
        x, pool, route = _out_proj(st, o, at_out, None, x, mod, 1, norm_g[1, 1],
                                   (rw_pad, rb_pad, pool))
        routed.append((st, x, route))

    src, pos, tile_expert, n_tiles = _moe_plan(
        jnp.concatenate([route[:, :4] for _, _, route in routed], axis=0), MOE_TILE)
    y_slots = _moe_ffn(pool, src, tile_expert, n_tiles, mo_gu, mo_down, MOE_TILE)
    results = [_moe_combine(st, y_slots, pos, route, x, mod, 1).reshape(st.batch, st.seq, d)
               for st, x, route in routed]

    new_k, new_v = new_kv
    return (results[0], results[1],
            new_k.reshape(batch, 1, seq, N_HEADS, 2, HEAD_DIM),
            new_v.reshape(batch, 1, seq, N_HEADS, V_DIM))
```

```python
import functools
import math

import numpy as np
import jax
import jax.numpy as jnp
from jax import lax
from jax.experimental import pallas as pl
from jax.experimental.pallas import tpu as pltpu

F32 = jnp.float32
BF16 = jnp.bfloat16
HIGHEST = lax.Precision.HIGHEST

D_MODEL = 1024
GRID_W = 64
HY_ORDER = 2
HY_EMB = 33
HY_FW = 64
HY_DECAY_TARGET = 1e-2
HY_FAST_PCT = 0.3
HY_SLOW_PCT = 1.5
N_HEADS = 8
HEAD_DIM = 64
V_DIM = 2 * HEAD_DIM
ROPE_BASE = 10000.0
D_FF = 2816
N_EXPERTS = 8
D_FF_EXPERT = 3584
EPS = 1e-6

V7X_LANES = 128
V7X_VMEM_LIMIT_BYTES = 56 * 1024 * 1024
LANE_CHUNKS = D_MODEL // V7X_LANES
COND_ROWS = 16
TOKEN_TILE = 1024
WIDE_OUT_TILE = 512
MOE_TILE = 1024


def _params(*semantics):
    return pltpu.CompilerParams(dimension_semantics=semantics,
                                vmem_limit_bytes=V7X_VMEM_LIMIT_BYTES)


def _resident(shape):
    zeros = (0,) * len(shape)
    return pl.BlockSpec(shape, lambda *_: zeros, pipeline_mode=pl.Buffered(1))


def _dot(a, b):
    return jnp.dot(a, b, preferred_element_type=F32)


def _dot_f32(a, b):
    return jnp.dot(a, b, precision=HIGHEST, preferred_element_type=F32)


def _modulate(x, g, shift, scale):
    ms = jnp.mean(x * x, axis=-1, keepdims=True)
    return (x * lax.rsqrt(ms + EPS) * g) * (1.0 + scale) + shift


def _adaln_kernel(cond_ref, w_ref, b_ref, o_ref):
    c = cond_ref[...]
    o_ref[...] = _dot_f32(c * jax.nn.sigmoid(c), w_ref[...]) + b_ref[...]


def _adaln(cond, ada_w, ada_b):
    depth, d, n = ada_w.shape
    tn = 1536
    return pl.pallas_call(
        _adaln_kernel,
        grid=(depth, n // tn),
        in_specs=[
            pl.BlockSpec((COND_ROWS, d), lambda i, j: (0, 0)),
            pl.BlockSpec((None, d, tn), lambda i, j: (i, 0, j)),
            pl.BlockSpec((None, 1, tn), lambda i, j: (i, 0, j)),
        ],
        out_specs=pl.BlockSpec((None, COND_ROWS, tn), lambda i, j: (i, 0, j)),
        out_shape=jax.ShapeDtypeStruct((depth, COND_ROWS, n), F32),
        compiler_params=_params("parallel", "parallel"),
        name="adaln",
    )(cond, ada_w, ada_b.reshape(depth, 1, n))


class _Stream:
    def __init__(self, batch, seq, cond_row0, per_seq_cond, row0=0, tile=TOKEN_TILE):
        self.batch, self.seq = batch, seq
        self.rows = batch * seq
        self.row0 = row0
        self._cond = (cond_row0, per_seq_cond)
        if per_seq_cond:
            self.tm = min(tile, seq)
            tiles_per_seq = seq // self.tm
            self.cond_row = lambda i: cond_row0 + i // tiles_per_seq
        else:
            self.tm = min(tile, self.rows)
            self.cond_row = lambda i: cond_row0
        self.tiles = self.rows // self.tm

    def retiled(self, tile):
        return _Stream(self.batch, self.seq, *self._cond, row0=self.row0, tile=tile)

    def mod_spec(self, layer, chunk):
        return pl.BlockSpec((None, None, 1, D_MODEL),
                            lambda i, *_: (layer, self.cond_row(i), 0, chunk))

    def row_spec(self, width):
        return pl.BlockSpec((self.tm, width), lambda i, *_: (i, 0))


def _mod_matmul_kernel(n_out, has_bias, x_ref, g_ref, sh_ref, sc_ref, w_ref, *rest):
    if has_bias:
        b_ref, out_refs = rest[0], rest[1:]
    else:
        b_ref, out_refs = None, rest
    h = _modulate(x_ref[...], g_ref[...], sh_ref[...], sc_ref[...])
    y = _dot(h.astype(BF16), w_ref[...])
    if has_bias:
        y = y + b_ref[...]
    width = y.shape[1] // n_out
    for k, o_ref in enumerate(out_refs):
        o_ref[...] = y[:, k * width:(k + 1) * width].astype(o_ref.dtype)


def _mod_matmul(st, x, norm_g, mod, layer, w_bf16, bias, n_out, out_dtype):
    st = st.retiled(WIDE_OUT_TILE)
    d, n = w_bf16.shape
    in_specs = [st.row_spec(d), _resident((1, d)), st.mod_spec(layer, 0), st.mod_spec(layer, 1),
                _resident((d, n))]
    args = [x, norm_g.reshape(1, d), mod, mod, w_bf16]
    if bias is not None:
        in_specs.append(_resident((1, n)))
        args.append(bias.reshape(1, n))
    width = n // n_out
    outs = pl.pallas_call(
        functools.partial(_mod_matmul_kernel, n_out, bias is not None),
        grid=(st.tiles,),
        in_specs=in_specs,
        out_specs=[st.row_spec(width)] * n_out,
        out_shape=[jax.ShapeDtypeStruct((st.rows, width), out_dtype)] * n_out,
        compiler_params=_params("parallel"),
        name="mod_matmul",
    )(*args)
    return outs


def _filter_time_kernel(feats_ref, w1_ref, b1_ref, w2_ref, b2_ref, fr_ref, w3_ref, dl_ref,
                        hs_ref, hd_ref):
    feats = feats_ref[...]
    fr = fr_ref[...]
    h = jnp.sin(fr[0:1] * (_dot_f32(feats, w1_ref[...]) + b1_ref[...]))
    h = jnp.sin(fr[1:2] * (_dot_f32(h, w2_ref[...]) + b2_ref[...]))
    h = _dot_f32(h, w3_ref[...])
    t = feats[:, 0:1]
    decay = jnp.exp(-t * jnp.abs(dl_ref[...]))
    half = HY_ORDER * D_MODEL
    decay2 = jnp.concatenate([decay] * HY_ORDER, axis=1)
    fwd = h[:, :half] * decay2
    bwd = jnp.where(t == 0.0, 0.0, h[:, half:] * decay2)
    hs_ref[...] = fwd + bwd
    hd_ref[...] = fwd - bwd


def _filter_time(seq, feats_pad, w1_pad, b1, w2, b2, freq, w3, deltas):
    tl = min(seq, 256)
    half = HY_ORDER * D_MODEL
    out = jax.ShapeDtypeStruct((seq, half), F32)
    return pl.pallas_call(
        _filter_time_kernel,
        grid=(seq // tl,),
        in_specs=[
            pl.BlockSpec((tl, V7X_LANES), lambda i: (i, 0)),
            _resident(w1_pad.shape), _resident((1, HY_FW)), _resident((HY_FW, HY_FW)),
            _resident((1, HY_FW)), _resident((2, HY_FW)), _resident(w3.shape),
            _resident((1, D_MODEL)),
        ],
        out_specs=[pl.BlockSpec((tl, half), lambda i: (i, 0))] * 2,
        out_shape=[out, out],
        compiler_params=_params("parallel"),
        name="hyena_filter_time",
    )(feats_pad, w1_pad, b1.reshape(1, HY_FW), w2, b2.reshape(1, HY_FW), freq, w3, deltas)


def _filter_spectrum_kernel(seq, hs_ref, hd_ref, c_ref, s_ref, kr_ref, ki_ref, kn_ref):
    hs = hs_ref[...]
    row = lax.broadcasted_iota(jnp.int32, (seq, 1), 0)
    wgt = jnp.where(row == 0, 1.0, 2.0) * (1.0 / (2 * seq))
    kr_ref[...] = wgt * _dot(c_ref[...], hs.astype(BF16))
    ki_ref[...] = -wgt * _dot(s_ref[...], hd_ref[...].astype(BF16))
    sign = (1 - 2 * (row & 1)).astype(F32)
    kn_ref[...] = jnp.sum(hs * sign, axis=0, keepdims=True) * (1.0 / (2 * seq))


def _filter_spectrum(seq, hs, hd, cmat, smat):
    half = hs.shape[1]
    tn = 256
    col = pl.BlockSpec((seq, tn), lambda j: (0, j))
    return pl.pallas_call(
        functools.partial(_filter_spectrum_kernel, seq),
        grid=(half // tn,),
        in_specs=[col, col, _resident((seq, seq)), _resident((seq, seq))],
        out_specs=[col, col, pl.BlockSpec((1, tn), lambda j: (0, j))],
        out_shape=[jax.ShapeDtypeStruct((seq, half), F32)] * 2
        + [jax.ShapeDtypeStruct((1, half), F32)],
        compiler_params=_params("parallel"),
        name="hyena_filter_spectrum",
    )(hs, hd, cmat, smat)


def _hyena_core_kernel(seq, chunk, pv_ref, p1_ref, p2_ref, cwv_ref, cw1_ref, cw2_ref, cbv_ref,
                       cb1_ref, cb2_ref, kr0_ref, ki0_ref, kn0_ref, kr1_ref, ki1_ref, kn1_ref,
                       bias0_ref, bias1_ref, c_ref, s_ref, z_ref,
                       u_ref, ub_ref, gate_ref, p_ref, q_ref):
    row = lax.broadcasted_iota(jnp.int32, (seq, 1), 0)
    sign = (1 - 2 * (row & 1)).astype(F32)
    chunks = [slice(r, r + chunk) for r in range(0, seq, chunk)]

    def short_conv(x_ref, w_ref, b_ref):
        x = x_ref[...].astype(F32)
        w = w_ref[...]
        prev = jnp.where(row == 0, 0.0, pltpu.roll(x, 1, 0))
        nxt = jnp.where(row == seq - 1, 0.0, pltpu.roll(x, seq - 1, 0))
        return prev * w[0:1] + x * w[1:2] + nxt * w[2:3] + b_ref[...]

    def gated_long_conv(kr_ref, ki_ref, kn_ref, bias_ref, write):
        u = u_ref[...]
        ub_ref[...] = u.astype(BF16)
        nyq = jnp.sum(u * sign, axis=0, keepdims=True) * kn_ref[...]
        for rows in chunks:
            a = _dot(c_ref[rows, :], ub_ref[...])
            b = _dot(s_ref[rows, :], ub_ref[...])
            kr, ki = kr_ref[rows, :], ki_ref[rows, :]
            p_ref[rows, :] = (a * kr + b * ki).astype(BF16)
            q_ref[rows, :] = (b * kr - a * ki).astype(BF16)
        for rows in chunks:
            y = _dot(c_ref[rows, :], p_ref[...]) + _dot(s_ref[rows, :], q_ref[...])
            y = y + sign[rows, :] * nyq + u_ref[rows, :] * bias_ref[...]
            write(rows, gate_ref[rows, :] * y)

    def to_u(rows, val):
        u_ref[rows, :] = val

    def to_z(rows, val):
        z_ref[rows, :] = val.astype(z_ref.dtype)

    u_ref[...] = short_conv(pv_ref, cwv_ref, cbv_ref)
    gate_ref[...] = short_conv(p1_ref, cw1_ref, cb1_ref)
    gated_long_conv(kr0_ref, ki0_ref, kn0_ref, bias0_ref, to_u)
    gate_ref[...] = short_conv(p2_ref, cw2_ref, cb2_ref)
    gated_long_conv(kr1_ref, ki1_ref, kn1_ref, bias1_ref, to_z)


def _hyena_core(st, proj, conv_w, conv_b, kr, ki, kn, bias, cmat, smat):
    seq, d = st.seq, D_MODEL
    tn = 256
    nj = d // tn
    proj3 = proj.reshape(st.batch, seq, 3 * d)

    def part(k):
        return pl.BlockSpec((None, seq, tn), lambda j, b: (b, 0, k * nj + j))

    def cols(rows, k, buffers=2):
        return pl.BlockSpec((rows, tn), lambda j, b: (0, k * nj + j),
                            pipeline_mode=pl.Buffered(buffers))

    in_specs = ([part(0), part(1), part(2)]
                + [cols(3, k) for k in range(3)] + [cols(1, k) for k in range(3)]
                + [cols(seq, 0, 1), cols(seq, 0, 1), cols(1, 0), cols(seq, 1, 1), cols(seq, 1, 1),
                   cols(1, 1)]
                + [cols(1, 0), cols(1, 0)]
                + [_resident((seq, seq)), _resident((seq, seq))])
    z = pl.pallas_call(
        functools.partial(_hyena_core_kernel, seq, min(seq, 512)),
        grid=(nj, st.batch),
        in_specs=in_specs,
        out_specs=pl.BlockSpec((None, seq, tn), lambda j, b: (b, 0, j)),
        out_shape=jax.ShapeDtypeStruct((st.batch, seq, d), BF16),
        scratch_shapes=[pltpu.VMEM((seq, tn), F32), pltpu.VMEM((seq, tn), BF16),
                        pltpu.VMEM((seq, tn), F32), pltpu.VMEM((seq, tn), BF16),
                        pltpu.VMEM((seq, tn), BF16)],
        compiler_params=_params("parallel", "parallel"),
        name="hyena_core",
    )(proj3, proj3, proj3, conv_w, conv_w, conv_w, conv_b, conv_b, conv_b,
      kr, ki, kn, kr, ki, kn, bias[0:1], bias[1:2], cmat, smat)
    return z.reshape(st.rows, d)


def _out_proj_kernel(has_bias, has_router, *refs):
    refs = list(refs)
    z_ref, w_ref = refs.pop(0), refs.pop(0)
    b_ref = refs.pop(0) if has_bias else None
    x_ref, g1_ref, ng_ref, sh_ref, sc_ref = (refs.pop(0) for _ in range(5))
    if has_router:
        rw_hi_ref, rw_lo_ref, rb_ref = refs.pop(0), refs.pop(0), refs.pop(0)
        refs.pop(0)
    x1_ref, h2_ref = refs.pop(0), refs.pop(0)
    m = _dot(z_ref[...], w_ref[...])
    if has_bias:
        m = m + b_ref[...]
    x1 = x_ref[...] + g1_ref[...] * m
    x1_ref[...] = x1
    h2 = _modulate(x1, ng_ref[...], sh_ref[...], sc_ref[...])
    if not has_router:
        h2_ref[...] = h2.astype(BF16)
        return
    _store_token_tiles(h2_ref, h2)
    route_ref = refs.pop(0)
    h_hi = h2.astype(BF16)
    h_lo = (h2 - h_hi.astype(F32)).astype(BF16)
    logits = (_dot(h_hi, rw_hi_ref[...])
              + (_dot(h_lo, rw_hi_ref[...]) + _dot(h_hi, rw_lo_ref[...])) + rb_ref[...])
    lane = lax.broadcasted_iota(jnp.int32, logits.shape, 1)
    neg = -jnp.inf
    logits = jnp.where(lane < N_EXPERTS, logits, neg)
    m1 = jnp.max(logits, axis=-1, keepdims=True)
    i1 = jnp.min(jnp.where(logits == m1, lane, V7X_LANES), axis=-1, keepdims=True)
    rest = jnp.where(lane == i1, neg, logits)
    m2 = jnp.max(rest, axis=-1, keepdims=True)
    i2 = jnp.min(jnp.where(rest == m2, lane, V7X_LANES), axis=-1, keepdims=True)
    e2 = jnp.exp(m2 - m1)
    den = 1.0 + e2
    route = jnp.where(lane == 0, i1.astype(F32), jnp.where(lane == 1, i2.astype(F32), 0.0))
    route_ref[...] = route + jnp.where(lane == 2, 1.0 / den, 0.0) + jnp.where(lane == 3, e2 / den, 0.0)


def _store_token_tiles(ref, val):
    rows = val.shape[0]
    for c in range(LANE_CHUNKS):
        ref[pl.ds(c, rows, stride=LANE_CHUNKS), :] = val[:, c * V7X_LANES:(c + 1) * V7X_LANES]


def _out_proj(st, z_bf16, w_bf16, bias, x, mod, layer, norm_g, router=None):
    d = D_MODEL
    in_specs = [st.row_spec(z_bf16.shape[1]), _resident(w_bf16.shape)]
    args = [z_bf16, w_bf16]
    if bias is not None:
        in_specs.append(_resident((1, d)))
        args.append(bias.reshape(1, d))
    in_specs += [st.row_spec(d), st.mod_spec(layer, 2), _resident((1, d)),
                 st.mod_spec(layer, 3), st.mod_spec(layer, 4)]
    args += [x, mod, norm_g.reshape(1, d), mod, mod]
    aliases = {}
    if router is None:
        out_specs = [st.row_spec(d), st.row_spec(d)]
        out_shape = [jax.ShapeDtypeStruct((st.rows, d), F32),
                     jax.ShapeDtypeStruct((st.rows, d), BF16)]
    else:
        rw_pad, rb_pad, pool = router
        rw_hi = rw_pad.astype(BF16)
        rw_lo = (rw_pad - rw_hi.astype(F32)).astype(BF16)
        in_specs += [_resident(rw_pad.shape), _resident(rw_pad.shape), _resident(rb_pad.shape),
                     pl.BlockSpec(memory_space=pl.ANY)]
        args += [rw_hi, rw_lo, rb_pad, pool]
        aliases = {len(args) - 1: 1}
        tile0 = st.row0 // st.tm
        out_specs = [st.row_spec(d),
                     pl.BlockSpec((st.tm * LANE_CHUNKS, V7X_LANES), lambda i: (tile0 + i, 0)),
                     st.row_spec(V7X_LANES)]
        out_shape = [jax.ShapeDtypeStruct((st.rows, d), F32),
                     jax.ShapeDtypeStruct(pool.shape, F32),
                     jax.ShapeDtypeStruct((st.rows, V7X_LANES), F32)]
    return pl.pallas_call(
        functools.partial(_out_proj_kernel, bias is not None, router is not None),
        grid=(st.tiles,),
        in_specs=in_specs,
        out_specs=out_specs,
        out_shape=out_shape,
        input_output_aliases=aliases,
        compiler_params=_params("parallel"),
        name="out_proj",
    )(*args)


def _swiglu_part(x_bf16, wg_ref, wu_ref, wd_ref):
    g = _dot(x_bf16, wg_ref[...])
    u = _dot(x_bf16, wu_ref[...])
    return _dot((g * jax.nn.sigmoid(g) * u).astype(BF16), wd_ref[...])


def _swiglu_kernel(h_ref, wg_ref, wu_ref, wd_ref, x_ref, g2_ref, o_ref, acc_ref):
    j = pl.program_id(1)
    part = _swiglu_part(h_ref[...], wg_ref, wu_ref, wd_ref)

    @pl.when(j == 0)
    def _():
        acc_ref[...] = part

    @pl.when(j > 0)
    def _():
        acc_ref[...] += part

    @pl.when(j == pl.num_programs(1) - 1)
    def _():
        o_ref[...] = x_ref[...] + g2_ref[...] * acc_ref[...]


def _swiglu(st, h2, w_gu, w_down, x1, mod, layer):
    d, two_f = w_gu.shape
    f = two_f // 2
    tf = 256
    nf = f // tf
    return pl.pallas_call(
        _swiglu_kernel,
        grid=(st.tiles, nf),
        in_specs=[
            pl.BlockSpec((st.tm, d), lambda i, j: (i, 0)),
            pl.BlockSpec((d, tf), lambda i, j: (0, j)),
            pl.BlockSpec((d, tf), lambda i, j: (0, nf + j)),
            pl.BlockSpec((tf, d), lambda i, j: (j, 0)),
            pl.BlockSpec((st.tm, d), lambda i, j: (i, 0)),
            st.mod_spec(layer, 5),
        ],
        out_specs=pl.BlockSpec((st.tm, d), lambda i, j: (i, 0)),
        out_shape=jax.ShapeDtypeStruct((st.rows, d), F32),
        scratch_shapes=[pltpu.VMEM((st.tm, d), F32)],
        compiler_params=_params("parallel", "arbitrary"),
        name="swiglu",
    )(h2, w_gu, w_gu, w_down, x1, mod)


def _moe_plan(route, tm):
    tokens = route.shape[0]
    max_tiles = (2 * tokens) // tm + N_EXPERTS
    expert = route[:, :2].astype(jnp.int32).reshape(-1)
    onehot = (expert[:, None] == jnp.arange(N_EXPERTS, dtype=jnp.int32)[None, :]).astype(jnp.int32)
    csum = jnp.cumsum(onehot, axis=0)
    rank = jnp.sum(csum * onehot, axis=1) - 1
    counts = csum[-1]
    tiles_per_expert = (counts + tm - 1) // tm
    tiles_end = jnp.cumsum(tiles_per_expert)
    start = (tiles_end - tiles_per_expert) * tm
    pos = jnp.sum(start[None, :] * onehot, axis=1) + rank
    token = jnp.arange(2 * tokens, dtype=jnp.int32) // 2
    src = jnp.zeros((max_tiles * tm,), jnp.int32).at[pos].set(token, unique_indices=True)
    n_tiles = tiles_end[-1:]
    tile = jnp.minimum(jnp.arange(max_tiles, dtype=jnp.int32), n_tiles - 1)
    tile_expert = jnp.sum((tile[:, None] >= tiles_end[None, :]).astype(jnp.int32), axis=1)
    return src, pos.astype(jnp.int32), tile_expert.astype(jnp.int32), n_tiles.astype(jnp.int32)


def _token_tile_copy(src_hbm, row, dst, slot_row, sem):
    return pltpu.make_async_copy(
        src_hbm.at[pl.ds(pl.multiple_of(row * LANE_CHUNKS, LANE_CHUNKS), LANE_CHUNKS)],
        dst.at[pl.ds(pl.multiple_of(slot_row * LANE_CHUNKS, LANE_CHUNKS), LANE_CHUNKS)],
        sem)


def _moe_ffn_kernel(tm, src_ref, te_ref, nt_ref, x_hbm, wg_ref, wu_ref, wd_ref, y_ref,
                    xbuf, xd_ref, acc_ref, sem):
    t, j = pl.program_id(0), pl.program_id(1)
    last_j = pl.num_programs(1) - 1
    n_tiles = nt_ref[0]
    slot = t % 2
    rows = tm * LANE_CHUNKS

    def start_gather(tile, into):
        def body(r, carry):
            _token_tile_copy(x_hbm, src_ref[tile * tm + r], xbuf.at[into], r, sem.at[into]).start()
            return carry
        lax.fori_loop(0, tm, body, 0, unroll=8)

    @pl.when((t == 0) & (j == 0))
    def _():
        start_gather(0, 0)

    @pl.when((j == 0) & (t < n_tiles))
    def _():
        pltpu.make_async_copy(x_hbm.at[pl.ds(0, rows)], xbuf.at[slot], sem.at[slot]).wait()
        for c in range(LANE_CHUNKS):
            xd_ref[:, c * V7X_LANES:(c + 1) * V7X_LANES] = (
                xbuf[slot, pl.ds(c, tm, stride=LANE_CHUNKS), :].astype(BF16))

        @pl.when(t + 1 < n_tiles)
        def _():
            start_gather(t + 1, 1 - slot)

    @pl.when(t < n_tiles)
    def _():
        part = _swiglu_part(xd_ref[...], wg_ref, wu_ref, wd_ref)

        @pl.when(j == 0)
        def _():
            acc_ref[...] = part

        @pl.when(j > 0)
        def _():
            acc_ref[...] += part

        @pl.when(j == last_j)
        def _():
            _store_token_tiles(y_ref, acc_ref[...])

    @pl.when((t >= n_tiles) & (j == last_j))
    def _():
        y_ref[...] = jnp.zeros_like(y_ref)


def _moe_ffn(pool, src, tile_expert, n_tiles, w_gu, w_down, tm):
    n_e, d, two_f = w_gu.shape
    f = two_f // 2
    tf = 512
    nf = f // tf
    max_tiles = tile_expert.shape[0]
    rows = tm * LANE_CHUNKS
    grid_spec = pltpu.PrefetchScalarGridSpec(
        num_scalar_prefetch=3,
        grid=(max_tiles, nf),
        in_specs=[
            pl.BlockSpec(memory_space=pl.ANY),
            pl.BlockSpec((None, d, tf), lambda t, j, src, te, nt: (te[t], 0, j)),
            pl.BlockSpec((None, d, tf), lambda t, j, src, te, nt: (te[t], 0, nf + j)),
            pl.BlockSpec((None, tf, d), lambda t, j, src, te, nt: (te[t], j, 0)),
        ],
        out_specs=pl.BlockSpec((rows, V7X_LANES), lambda t, j, src, te, nt: (t, 0)),
        scratch_shapes=[pltpu.VMEM((2, rows, V7X_LANES), F32), pltpu.VMEM((tm, d), BF16),
                        pltpu.VMEM((tm, d), F32), pltpu.SemaphoreType.DMA((2,))],
    )
    return pl.pallas_call(
        functools.partial(_moe_ffn_kernel, tm),
        grid_spec=grid_spec,
        out_shape=jax.ShapeDtypeStruct((max_tiles * rows, V7X_LANES), F32),
        compiler_params=_params("arbitrary", "arbitrary"),
        name="moe_ffn",
    )(src, tile_expert, n_tiles, pool, w_gu, w_gu, w_down)


def _moe_combine_kernel(tm, token0, pos_ref, y_hbm, route_ref, x_ref, g2_ref, o_ref, ybuf, sem):
    i = pl.program_id(0)
    slot = i % 2
    rows = tm * LANE_CHUNKS

    def start_gather(tile, into):
        def body(r, carry):
            a = 2 * (token0 + tile * tm + r)
            for k in range(2):
                _token_tile_copy(y_hbm, pos_ref[a + k], ybuf.at[into], k * tm + r,
                                 sem.at[into]).start()
            return carry
        lax.fori_loop(0, tm, body, 0, unroll=4)

    @pl.when(i == 0)
    def _():
        start_gather(0, 0)

    pltpu.make_async_copy(y_hbm.at[pl.ds(0, 2 * rows)], ybuf.at[slot], sem.at[slot]).wait()

    @pl.when(i + 1 < pl.num_programs(0))
    def _():
        start_gather(i + 1, 1 - slot)

    route = route_ref[...]
    lane = lax.broadcasted_iota(jnp.int32, route.shape, 1)
    gate0 = jnp.sum(jnp.where(lane == 2, route, 0.0), axis=-1, keepdims=True)
    gate1 = jnp.sum(jnp.where(lane == 3, route, 0.0), axis=-1, keepdims=True)
    for c in range(LANE_CHUNKS):
        cols = slice(c * V7X_LANES, (c + 1) * V7X_LANES)
        y0 = ybuf[slot, pl.ds(c, tm, stride=LANE_CHUNKS), :]
        y1 = ybuf[slot, pl.ds(rows + c, tm, stride=LANE_CHUNKS), :]
        o_ref[:, cols] = x_ref[:, cols] + g2_ref[:, cols] * (gate0 * y0 + gate1 * y1)


def _moe_combine(st, y_slots, pos, route, x1, mod, layer):
    d = D_MODEL
    tm = st.tm
    rows = tm * LANE_CHUNKS
    grid_spec = pltpu.PrefetchScalarGridSpec(
        num_scalar_prefetch=1,
        grid=(st.tiles,),
        in_specs=[
            pl.BlockSpec(memory_space=pl.ANY),
            st.row_spec(V7X_LANES), st.row_spec(d), st.mod_spec(layer, 5),
        ],
        out_specs=st.row_spec(d),
        scratch_shapes=[pltpu.VMEM((2, 2 * rows, V7X_LANES), F32), pltpu.SemaphoreType.DMA((2,))],
    )
    return pl.pallas_call(
        functools.partial(_moe_combine_kernel, tm, st.row0),
        grid_spec=grid_spec,
        out_shape=jax.ShapeDtypeStruct((st.rows, d), F32),
        compiler_params=_params("arbitrary"),
        name="moe_combine",
    )(pos, y_slots, route, x1, mod)


def _head_rmsnorm(x, g2):
    lane = lax.broadcasted_iota(jnp.int32, x.shape, 1)
    lo = lane < HEAD_DIM
    sq = x * x
    s_lo = jnp.sum(jnp.where(lo, sq, 0.0), axis=-1, keepdims=True)
    s_hi = jnp.sum(jnp.where(lo, 0.0, sq), axis=-1, keepdims=True)
    ms = jnp.where(lo, s_lo, s_hi) * (1.0 / HEAD_DIM)
    return x * lax.rsqrt(ms + EPS) * g2


def _rope(x, cos, sin_signed):
    q4 = HEAD_DIM // 4
    lane = lax.broadcasted_iota(jnp.int32, x.shape, 1)
    first = (lane & q4) == 0
    width = x.shape[1]
    partner = jnp.where(first, pltpu.roll(x, width - q4, 1), pltpu.roll(x, q4, 1))
    return x * cos + partner * sin_signed


def _attention_kernel(past, use_rope, lam_init, *refs):
    refs = list(refs)
    q_ref, k_ref, v_ref = refs.pop(0), refs.pop(0), refs.pop(0)
    if past:
        ck_ref, cv_ref = refs.pop(0), refs.pop(0)
    if use_rope:
        cosq_ref, sinq_ref, cosk_ref, sink_ref = (refs.pop(0) for _ in range(4))
    qg_ref, kg_ref, lam_ref, sg_ref = (refs.pop(0) for _ in range(4))
    o_ref = refs.pop(0)
    nk_ref = None if past else refs.pop(0)
    kall_ref, vall_ref = refs

    qi = pl.program_id(2)

    @pl.when(qi == 0)
    def _():
        k = _head_rmsnorm(k_ref[...], kg_ref[...])
        if nk_ref is not None:
            nk_ref[...] = k
        if use_rope:
            k = _rope(k, cosk_ref[...], sink_ref[...])
        if past:
            kall_ref[0:past, :] = ck_ref[...].astype(BF16)
            vall_ref[0:past, :] = cv_ref[...].astype(BF16)
        kall_ref[past:, :] = k.astype(BF16)
        vall_ref[past:, :] = v_ref[...].astype(BF16)

    q = _head_rmsnorm(q_ref[...], qg_ref[...])
    if use_rope:
        q = _rope(q, cosq_ref[...], sinq_ref[...])
    q = q * (HEAD_DIM ** -0.5 * math.log2(math.e))
    tq = q.shape[0]
    lane = lax.broadcasted_iota(jnp.int32, q.shape, 1)
    lo = lane < HEAD_DIM
    q2 = jnp.concatenate([jnp.where(lo, q, 0.0), jnp.where(lo, 0.0, q)], axis=0).astype(BF16)
    nt = (((1,), (1,)), ((), ()))
    s = lax.dot_general(q2, kall_ref[...], nt, preferred_element_type=F32)
    p = jnp.exp2(s - jnp.max(s, axis=-1, keepdims=True))
    norm = 1.0 / jnp.sum(p, axis=-1, keepdims=True)
    pv = _dot(p.astype(BF16), vall_ref[...])
    lv = lam_ref[...]
    lam = (jnp.exp(jnp.sum(lv[0:1] * lv[1:2], axis=-1, keepdims=True))
           - jnp.exp(jnp.sum(lv[2:3] * lv[3:4], axis=-1, keepdims=True)) + lam_init)
    o = pv[:tq] * norm[:tq] - pv[tq:] * (lam * norm[tq:])
    ms = jnp.mean(o * o, axis=-1, keepdims=True)
    o = o * lax.rsqrt(ms + EPS) * sg_ref[...] * (1.0 - lam_init)
    o_ref[...] = o.astype(o_ref.dtype)


def _attention(st, q, k, v, q_g, k_g, lam_vecs, subln_g, lam_init, cache_k=None, cache_v=None,
               rope=None):
    seq, hd = st.seq, N_HEADS * V_DIM
    tq = min(seq, 512)
    nq = seq // tq
    past = 0 if cache_k is None else cache_k.shape[1]
    lk = past + seq
    q3, k3, v3 = (a.reshape(st.batch, seq, hd) for a in (q, k, v))
    qblk = pl.BlockSpec((None, tq, V_DIM), lambda b, h, i: (b, i, h))
    kblk = pl.BlockSpec((None, seq, V_DIM), lambda b, h, i: (b, 0, h))
    in_specs = [qblk, kblk, kblk]
    args = [q3, k3, v3]
    if past:
        cblk = pl.BlockSpec((None, past, V_DIM), lambda b, h, i: (b, 0, h))
        in_specs += [cblk, cblk]
        args += [cache_k.reshape(st.batch, past, hd), cache_v.reshape(st.batch, past, hd)]
    if rope is not None:
        cos2, sin2 = rope
        tq_tab = pl.BlockSpec((tq, V_DIM), lambda b, h, i: (i, 0))
        k_tab = pl.BlockSpec((seq, V_DIM), lambda b, h, i: (0, 0))
        in_specs += [tq_tab, tq_tab, k_tab, k_tab]
        args += [cos2, sin2, cos2, sin2]
    small = lambda shape: pl.BlockSpec(shape, lambda b, h, i: (0,) * len(shape))
    in_specs += [small((1, V_DIM)), small((1, V_DIM)), small((4, HEAD_DIM)), small((1, V_DIM))]
    args += [jnp.tile(q_g, 2).reshape(1, V_DIM), jnp.tile(k_g, 2).reshape(1, V_DIM), lam_vecs,
             subln_g.reshape(1, V_DIM)]
    out_specs = [qblk]
    out_shape = [jax.ShapeDtypeStruct((st.batch, seq, hd), BF16)]
    if not past:
        out_specs.append(kblk)
        out_shape.append(jax.ShapeDtypeStruct((st.batch, seq, hd), F32))
    outs = pl.pallas_call(
        functools.partial(_attention_kernel, past, rope is not None, lam_init),
        grid=(st.batch, N_HEADS, nq),
        in_specs=in_specs,
        out_specs=out_specs,
        out_shape=out_shape,
        scratch_shapes=[pltpu.VMEM((lk, V_DIM), BF16), pltpu.VMEM((lk, V_DIM), BF16)],
        compiler_params=_params("parallel", "parallel", "arbitrary"),
        name="diff_attention",
    )(*args)
    return [o.reshape(st.rows, hd) for o in outs]


def _dft_matrices(seq):
    idx = np.arange(seq, dtype=np.int64)
    ang = (np.outer(idx, idx) % (2 * seq)).astype(np.float64) * (math.pi / seq)
    return jnp.asarray(np.cos(ang), dtype=BF16), jnp.asarray(np.sin(ang), dtype=BF16)


def _filter_features(seq):
    t = jnp.linspace(0.0, 1.0, seq, dtype=F32)[:, None]
    bands = (HY_EMB - 1) // 2
    w_ang = 2.0 * math.pi * jnp.arange(seq, dtype=F32)[:, None] / seq
    f = jnp.linspace(1e-4, bands - 1, bands, dtype=F32)[None, :]
    ang = f * w_ang
    feats = jnp.concatenate([t, jnp.cos(ang), -jnp.sin(ang)], axis=-1)
    return jnp.pad(feats, ((0, 0), (0, V7X_LANES - HY_EMB)))


def _decay_rates():
    min_decay = math.log(HY_DECAY_TARGET) / HY_SLOW_PCT
    max_decay = math.log(HY_DECAY_TARGET) / HY_FAST_PCT
    return jnp.linspace(min_decay, max_decay, D_MODEL, dtype=F32)[None, :]


def _rope_tables(seq):
    rows = seq // GRID_W
    row = jnp.repeat(jnp.arange(rows, dtype=F32), GRID_W)
    col = jnp.tile(jnp.arange(GRID_W, dtype=F32), rows)
    quarter = HEAD_DIM // 4
    inv = ROPE_BASE ** (-jnp.arange(quarter, dtype=F32) / quarter)

    def axis_angles(pos):
        a = pos[:, None] * inv[None, :]
        return jnp.concatenate([a, a], axis=-1)

    ang = jnp.concatenate([axis_angles(row), axis_angles(col)], axis=-1)
    sign = jnp.where((jnp.arange(HEAD_DIM) & quarter) == 0, -1.0, 1.0).astype(F32)
    return jnp.tile(jnp.cos(ang), (1, 2)), jnp.tile(jnp.sin(ang) * sign[None, :], (1, 2))


def kernel(x_prompt, x_sample, cache_k, cache_v, c, c_ctx, ada_w, ada_b, norm_g, hy_in_w, hy_in_b, hy_conv_w, hy_conv_b, hy_f_w1, hy_f_b1, hy_f_w2, hy_f_b2, hy_f_freq, hy_f_w3, hy_bias, hy_out_w, hy_out_b, at_qkv_w, at_q_g, at_k_g, at_lam, at_subln_g, at_out_w, dn_w_gu, dn_w_down, mo_router_w, mo_router_b, mo_w_gu, mo_w_down):
    d = D_MODEL
    batch, seq = x_prompt.shape[:2]
    dec_batch, dec_seq = x_sample.shape[:2]
    past = cache_k.shape[2]
    streams = [
        (_Stream(batch, seq, 0, False), x_prompt.reshape(batch * seq, d), None),
        (_Stream(dec_batch, dec_seq, 1, True, row0=batch * seq),
         x_sample.reshape(dec_batch * dec_seq, d), (cache_k[:, 0], cache_v[:, 0])),
    ]
    tokens = batch * seq + dec_batch * dec_seq

    cond = jnp.concatenate(
        [c_ctx[None, :], c, jnp.zeros((COND_ROWS - 1 - dec_batch, d), F32)], axis=0)
    mod = _adaln(cond, ada_w, ada_b)
    mod = mod.reshape(mod.shape[0], COND_ROWS, 1, 6 * d)

    in_w, out_w = hy_in_w[0].astype(BF16), hy_out_w[0].astype(BF16)
    qkv_w, at_out = at_qkv_w[0].astype(BF16), at_out_w[0].astype(BF16)
    dn_gu, dn_down = dn_w_gu[0].astype(BF16), dn_w_down[0].astype(BF16)
    mo_gu, mo_down = mo_w_gu[0].astype(BF16), mo_w_down[0].astype(BF16)
    w1_pad = jnp.pad(hy_f_w1[0], ((0, V7X_LANES - HY_EMB), (0, 0)))
    rw_pad = jnp.pad(mo_router_w[0], ((0, 0), (0, V7X_LANES - N_EXPERTS)))
    rb_pad = jnp.pad(mo_router_b[0], (0, V7X_LANES - N_EXPERTS)).reshape(1, V7X_LANES)
    deltas = _decay_rates()
    lam_init = 0.8 - 0.6 * math.exp(-0.3 * 1)

    routed = []
    pool = jnp.zeros((tokens * LANE_CHUNKS, V7X_LANES), F32)
    for st, x, cache in streams:
        cmat, smat = _dft_matrices(st.seq)
        hs, hd = _filter_time(st.seq, _filter_features(st.seq), w1_pad, hy_f_b1[0], hy_f_w2[0],
                              hy_f_b2[0], hy_f_freq[0], hy_f_w3[0], deltas)
        kr, ki, kn = _filter_spectrum(st.seq, hs, hd, cmat, smat)
        (proj,) = _mod_matmul(st, x, norm_g[0, 0], mod, 0, in_w, hy_in_b[0], 1, BF16)
        z = _hyena_core(st, proj, hy_conv_w[0], hy_conv_b[0].reshape(1, 3 * d), kr, ki, kn,
                        hy_bias[0], cmat, smat)
        x, h2 = _out_proj(st, z, out_w, hy_out_b[0], x, mod, 0, norm_g[0, 1])
        x = _swiglu(st, h2, dn_gu, dn_down, x, mod, 0)

        q, k, v = _mod_matmul(st, x, norm_g[1, 0], mod, 1, qkv_w, None, 3, F32)
        if cache is None:
            o, new_k = _attention(st, q, k, v, at_q_g[0], at_k_g[0], at_lam[0], at_subln_g[0],
                                  lam_init)
            new_kv = (new_k, v)
        else:
            (o,) = _attention(st, q, k, v, at_q_g[0], at_k_g[0], at_lam[0], at_subln_g[0],
                              lam_init, cache[0], cache[1], _rope_tables(st.seq))
        x, pool, route = _out_proj(st, o, at_out, None, x, mod, 1, norm_g[1, 1],
                                   (rw_pad, rb_pad, pool))
        routed.append((st, x, route))

    src, pos, tile_expert, n_tiles = _moe_plan(
        jnp.concatenate([route[:, :4] for _, _, route in routed], axis=0), MOE_TILE)
    y_slots = _moe_ffn(pool, src, tile_expert, n_tiles, mo_gu, mo_down, MOE_TILE)
    results = [_moe_combine(st, y_slots, pos, route, x, mod, 1).reshape(st.batch, st.seq, d)
               for st, x, route in routed]

    new_k, new_v = new_kv
    return (results[0], results[1],
            new_k.reshape(batch, 1, seq, N_HEADS, 2, HEAD_DIM),
            new_v.reshape(batch, 1, seq, N_HEADS, V_DIM))
```

```python
import functools
import math

import numpy as np
import jax
import jax.numpy as jnp
from jax import lax
from jax.experimental import pallas as pl
from jax.experimental.pallas import tpu as pltpu

F32 = jnp.float32
BF16 = jnp.bfloat16
HIGHEST = lax.Precision.HIGHEST

D_MODEL = 1024
GRID_W = 64
HY_ORDER = 2
HY_EMB = 33
HY_FW = 64
HY_DECAY_TARGET = 1e-2
HY_FAST_PCT = 0.3
HY_SLOW_PCT = 1.5
N_HEADS = 8
HEAD_DIM = 64
V_DIM = 2 * HEAD_DIM
ROPE_BASE = 10000.0
D_FF = 2816
N_EXPERTS = 8
D_FF_EXPERT = 3584
EPS = 1e-6

V7X_LANES = 128
V7X_VMEM_LIMIT_BYTES = 56 * 1024 * 1024
LANE_CHUNKS = D_MODEL // V7X_LANES
COND_ROWS = 16
TOKEN_TILE = 1024
WIDE_OUT_TILE = 512
MOE_TILE = 1024
MOE_FF_CHUNK = 512
DENSE_FF_CHUNK = D_FF // 2
ATTENTION_ROWS = 256


def _params(*semantics):
    return pltpu.CompilerParams(dimension_semantics=semantics,
                                vmem_limit_bytes=V7X_VMEM_LIMIT_BYTES)


def _resident(shape):
    zeros = (0,) * len(shape)
    return pl.BlockSpec(shape, lambda *_: zeros, pipeline_mode=pl.Buffered(1))


def _dot(a, b):
    return jnp.dot(a, b, preferred_element_type=F32)


def _dot_f32(a, b):
    return jnp.dot(a, b, precision=HIGHEST, preferred_element_type=F32)


def _modulate(x, g, shift, scale):
    ms = jnp.mean(x * x, axis=-1, keepdims=True)
    return (x * lax.rsqrt(ms + EPS) * g) * (1.0 + scale) + shift


def _adaln_kernel(cond_ref, w_ref, b_ref, o_ref):
    c = cond_ref[...]
    o_ref[...] = _dot_f32(c * jax.nn.sigmoid(c), w_ref[...]) + b_ref[...]


def _adaln(cond, ada_w, ada_b):
    depth, d, n = ada_w.shape
    tn = 1536
    return pl.pallas_call(
        _adaln_kernel,
        grid=(depth, n // tn),
        in_specs=[
            pl.BlockSpec((COND_ROWS, d), lambda i, j: (0, 0)),
            pl.BlockSpec((None, d, tn), lambda i, j: (i, 0, j)),
            pl.BlockSpec((None, 1, tn), lambda i, j: (i, 0, j)),
        ],
        out_specs=pl.BlockSpec((None, COND_ROWS, tn), lambda i, j: (i, 0, j)),
        out_shape=jax.ShapeDtypeStruct((depth, COND_ROWS, n), F32),
        compiler_params=_params("parallel", "parallel"),
        name="adaln",
    )(cond, ada_w, ada_b.reshape(depth, 1, n))


class _Stream:
    def __init__(self, batch, seq, cond_row0, per_seq_cond, row0=0, tile=TOKEN_TILE):
        self.batch, self.seq = batch, seq
        self.rows = batch * seq
        self.row0 = row0
        self._cond = (cond_row0, per_seq_cond)
        if per_seq_cond:
            self.tm = min(tile, seq)
            tiles_per_seq = seq // self.tm
            self.cond_row = lambda i: cond_row0 + i // tiles_per_seq
        else:
            self.tm = min(tile, self.rows)
            self.cond_row = lambda i: cond_row0
        self.tiles = self.rows // self.tm

    def retiled(self, tile):
        return _Stream(self.batch, self.seq, *self._cond, row0=self.row0, tile=tile)

    def mod_spec(self, layer, chunk):
        return pl.BlockSpec((None, None, 1, D_MODEL),
                            lambda i, *_: (layer, self.cond_row(i), 0, chunk))

    def row_spec(self, width):
        return pl.BlockSpec((self.tm, width), lambda i, *_: (i, 0))


def _mod_matmul_kernel(n_out, has_bias, x_ref, g_ref, sh_ref, sc_ref, w_ref, *rest):
    if has_bias:
        b_ref, out_refs = rest[0], rest[1:]
    else:
        b_ref, out_refs = None, rest
    h = _modulate(x_ref[...], g_ref[...], sh_ref[...], sc_ref[...])
    y = _dot(h.astype(BF16), w_ref[...])
    if has_bias:
        y = y + b_ref[...]
    width = y.shape[1] // n_out
    for k, o_ref in enumerate(out_refs):
        o_ref[...] = y[:, k * width:(k + 1) * width].astype(o_ref.dtype)


def _mod_matmul(st, x, norm_g, mod, layer, w_bf16, bias, n_out, out_dtype):
    st = st.retiled(WIDE_OUT_TILE)
    d, n = w_bf16.shape
    in_specs = [st.row_spec(d), _resident((1, d)), st.mod_spec(layer, 0), st.mod_spec(layer, 1),
                _resident((d, n))]
    args = [x, norm_g.reshape(1, d), mod, mod, w_bf16]
    if bias is not None:
        in_specs.append(_resident((1, n)))
        args.append(bias.reshape(1, n))
    width = n // n_out
    outs = pl.pallas_call(
        functools.partial(_mod_matmul_kernel, n_out, bias is not None),
        grid=(st.tiles,),
        in_specs=in_specs,
        out_specs=[st.row_spec(width)] * n_out,
        out_shape=[jax.ShapeDtypeStruct((st.rows, width), out_dtype)] * n_out,
        compiler_params=_params("parallel"),
        name="mod_matmul",
    )(*args)
    return outs


def _filter_time_kernel(feats_ref, w1_ref, b1_ref, w2_ref, b2_ref, fr_ref, w3_ref, dl_ref,
                        hs_ref, hd_ref):
    feats = feats_ref[...]
    fr = fr_ref[...]
    h = jnp.sin(fr[0:1] * (_dot_f32(feats, w1_ref[...]) + b1_ref[...]))
    h = jnp.sin(fr[1:2] * (_dot_f32(h, w2_ref[...]) + b2_ref[...]))
    h = _dot_f32(h, w3_ref[...])
    t = feats[:, 0:1]
    decay = jnp.exp(-t * jnp.abs(dl_ref[...]))
    half = HY_ORDER * D_MODEL
    decay2 = jnp.concatenate([decay] * HY_ORDER, axis=1)
    fwd = h[:, :half] * decay2
    bwd = jnp.where(t == 0.0, 0.0, h[:, half:] * decay2)
    hs_ref[...] = fwd + bwd
    hd_ref[...] = fwd - bwd


def _filter_time(seq, feats_pad, w1_pad, b1, w2, b2, freq, w3, deltas):
    tl = min(seq, 256)
    half = HY_ORDER * D_MODEL
    out = jax.ShapeDtypeStruct((seq, half), F32)
    return pl.pallas_call(
        _filter_time_kernel,
        grid=(seq // tl,),
        in_specs=[
            pl.BlockSpec((tl, V7X_LANES), lambda i: (i, 0)),
            _resident(w1_pad.shape), _resident((1, HY_FW)), _resident((HY_FW, HY_FW)),
            _resident((1, HY_FW)), _resident((2, HY_FW)), _resident(w3.shape),
            _resident((1, D_MODEL)),
        ],
        out_specs=[pl.BlockSpec((tl, half), lambda i: (i, 0))] * 2,
        out_shape=[out, out],
        compiler_params=_params("parallel"),
        name="hyena_filter_time",
    )(feats_pad, w1_pad, b1.reshape(1, HY_FW), w2, b2.reshape(1, HY_FW), freq, w3, deltas)


def _filter_spectrum_kernel(seq, hs_ref, hd_ref, c_ref, s_ref, kr_ref, ki_ref, kn_ref):
    hs = hs_ref[...]
    row = lax.broadcasted_iota(jnp.int32, (seq, 1), 0)
    wgt = jnp.where(row == 0, 1.0, 2.0) * (1.0 / (2 * seq))
    kr_ref[...] = wgt * _dot(c_ref[...], hs.astype(BF16))
    ki_ref[...] = -wgt * _dot(s_ref[...], hd_ref[...].astype(BF16))
    sign = (1 - 2 * (row & 1)).astype(F32)
    kn_ref[...] = jnp.sum(hs * sign, axis=0, keepdims=True) * (1.0 / (2 * seq))


def _filter_spectrum(seq, hs, hd, cmat, smat):
    half = hs.shape[1]
    tn = 256
    col = pl.BlockSpec((seq, tn), lambda j: (0, j))
    return pl.pallas_call(
        functools.partial(_filter_spectrum_kernel, seq),
        grid=(half // tn,),
        in_specs=[col, col, _resident((seq, seq)), _resident((seq, seq))],
        out_specs=[col, col, pl.BlockSpec((1, tn), lambda j: (0, j))],
        out_shape=[jax.ShapeDtypeStruct((seq, half), F32)] * 2
        + [jax.ShapeDtypeStruct((1, half), F32)],
        compiler_params=_params("parallel"),
        name="hyena_filter_spectrum",
    )(hs, hd, cmat, smat)


def _hyena_core_kernel(seq, chunk, pv_ref, p1_ref, p2_ref, cwv_ref, cw1_ref, cw2_ref, cbv_ref,
                       cb1_ref, cb2_ref, kr0_ref, ki0_ref, kn0_ref, kr1_ref, ki1_ref, kn1_ref,
                       bias0_ref, bias1_ref, c_ref, s_ref, z_ref,
                       u_ref, ub_ref, gate_ref, p_ref, q_ref):
    row = lax.broadcasted_iota(jnp.int32, (seq, 1), 0)
    sign = (1 - 2 * (row & 1)).astype(F32)
    chunks = [slice(r, r + chunk) for r in range(0, seq, chunk)]

    def short_conv(x_ref, w_ref, b_ref):
        x = x_ref[...].astype(F32)
        w = w_ref[...]
        prev = jnp.where(row == 0, 0.0, pltpu.roll(x, 1, 0))
        nxt = jnp.where(row == seq - 1, 0.0, pltpu.roll(x, seq - 1, 0))
        return prev * w[0:1] + x * w[1:2] + nxt * w[2:3] + b_ref[...]

    def gated_long_conv(kr_ref, ki_ref, kn_ref, bias_ref, write):
        u = u_ref[...]
        ub_ref[...] = u.astype(BF16)
        nyq = jnp.sum(u * sign, axis=0, keepdims=True) * kn_ref[...]
        for rows in chunks:
            a = _dot(c_ref[rows, :], ub_ref[...])
            b = _dot(s_ref[rows, :], ub_ref[...])
            kr, ki = kr_ref[rows, :], ki_ref[rows, :]
            p_ref[rows, :] = (a * kr + b * ki).astype(BF16)
            q_ref[rows, :] = (b * kr - a * ki).astype(BF16)
        for rows in chunks:
            y = _dot(c_ref[rows, :], p_ref[...]) + _dot(s_ref[rows, :], q_ref[...])
            y = y + sign[rows, :] * nyq + u_ref[rows, :] * bias_ref[...]
            write(rows, gate_ref[rows, :] * y)

    def to_u(rows, val):
        u_ref[rows, :] = val

    def to_z(rows, val):
        z_ref[rows, :] = val.astype(z_ref.dtype)

    u_ref[...] = short_conv(pv_ref, cwv_ref, cbv_ref)
    gate_ref[...] = short_conv(p1_ref, cw1_ref, cb1_ref)
    gated_long_conv(kr0_ref, ki0_ref, kn0_ref, bias0_ref, to_u)
    gate_ref[...] = short_conv(p2_ref, cw2_ref, cb2_ref)
    gated_long_conv(kr1_ref, ki1_ref, kn1_ref, bias1_ref, to_z)


def _hyena_core(st, proj, conv_w, conv_b, kr, ki, kn, bias, cmat, smat):
    seq, d = st.seq, D_MODEL
    tn = 256
    nj = d // tn
    proj3 = proj.reshape(st.batch, seq, 3 * d)

    def part(k):
        return pl.BlockSpec((None, seq, tn), lambda j, b: (b, 0, k * nj + j))

    def cols(rows, k, buffers=2):
        return pl.BlockSpec((rows, tn), lambda j, b: (0, k * nj + j),
                            pipeline_mode=pl.Buffered(buffers))

    in_specs = ([part(0), part(1), part(2)]
                + [cols(3, k) for k in range(3)] + [cols(1, k) for k in range(3)]
                + [cols(seq, 0, 1), cols(seq, 0, 1), cols(1, 0), cols(seq, 1, 1), cols(seq, 1, 1),
                   cols(1, 1)]
                + [cols(1, 0), cols(1, 0)]
                + [_resident((seq, seq)), _resident((seq, seq))])
    z = pl.pallas_call(
        functools.partial(_hyena_core_kernel, seq, min(seq, 512)),
        grid=(nj, st.batch),
        in_specs=in_specs,
        out_specs=pl.BlockSpec((None, seq, tn), lambda j, b: (b, 0, j)),
        out_shape=jax.ShapeDtypeStruct((st.batch, seq, d), BF16),
        scratch_shapes=[pltpu.VMEM((seq, tn), F32), pltpu.VMEM((seq, tn), BF16),
                        pltpu.VMEM((seq, tn), F32), pltpu.VMEM((seq, tn), BF16),
                        pltpu.VMEM((seq, tn), BF16)],
        compiler_params=_params("parallel", "parallel"),
        name="hyena_core",
    )(proj3, proj3, proj3, conv_w, conv_w, conv_w, conv_b, conv_b, conv_b,
      kr, ki, kn, kr, ki, kn, bias[0:1], bias[1:2], cmat, smat)
    return z.reshape(st.rows, d)


def _out_proj_kernel(has_bias, has_router, *refs):
    refs = list(refs)
    z_ref, w_ref = refs.pop(0), refs.pop(0)
    b_ref = refs.pop(0) if has_bias else None
    x_ref, g1_ref, ng_ref, sh_ref, sc_ref = (refs.pop(0) for _ in range(5))
    if has_router:
        rw_hi_ref, rw_lo_ref, rb_ref = refs.pop(0), refs.pop(0), refs.pop(0)
        refs.pop(0)
    x1_ref, h2_ref = refs.pop(0), refs.pop(0)
    m = _dot(z_ref[...], w_ref[...])
    if has_bias:
        m = m + b_ref[...]
    x1 = x_ref[...] + g1_ref[...] * m
    x1_ref[...] = x1
    h2 = _modulate(x1, ng_ref[...], sh_ref[...], sc_ref[...])
    if not has_router:
        h2_ref[...] = h2.astype(BF16)
        return
    _store_token_tiles(h2_ref, h2)
    route_ref = refs.pop(0)
    h_hi = h2.astype(BF16)
    h_lo = (h2 - h_hi.astype(F32)).astype(BF16)
    logits = (_dot(h_hi, rw_hi_ref[...])
              + (_dot(h_lo, rw_hi_ref[...]) + _dot(h_hi, rw_lo_ref[...])) + rb_ref[...])
    lane = lax.broadcasted_iota(jnp.int32, logits.shape, 1)
    neg = -jnp.inf
    logits = jnp.where(lane < N_EXPERTS, logits, neg)
    m1 = jnp.max(logits, axis=-1, keepdims=True)
    i1 = jnp.min(jnp.where(logits == m1, lane, V7X_LANES), axis=-1, keepdims=True)
    rest = jnp.where(lane == i1, neg, logits)
    m2 = jnp.max(rest, axis=-1, keepdims=True)
    i2 = jnp.min(jnp.where(rest == m2, lane, V7X_LANES), axis=-1, keepdims=True)
    e2 = jnp.exp(m2 - m1)
    den = 1.0 + e2
    route = jnp.where(lane == 0, i1.astype(F32), jnp.where(lane == 1, i2.astype(F32), 0.0))
    route_ref[...] = route + jnp.where(lane == 2, 1.0 / den, 0.0) + jnp.where(lane == 3, e2 / den, 0.0)


def _store_token_tiles(ref, val):
    rows = val.shape[0]
    for c in range(LANE_CHUNKS):
        ref[pl.ds(c, rows, stride=LANE_CHUNKS), :] = val[:, c * V7X_LANES:(c + 1) * V7X_LANES]


def _out_proj(st, z_bf16, w_bf16, bias, x, mod, layer, norm_g, router=None):
    d = D_MODEL
    in_specs = [st.row_spec(z_bf16.shape[1]), _resident(w_bf16.shape)]
    args = [z_bf16, w_bf16]
    if bias is not None:
        in_specs.append(_resident((1, d)))
        args.append(bias.reshape(1, d))
    in_specs += [st.row_spec(d), st.mod_spec(layer, 2), _resident((1, d)),
                 st.mod_spec(layer, 3), st.mod_spec(layer, 4)]
    args += [x, mod, norm_g.reshape(1, d), mod, mod]
    aliases = {}
    if router is None:
        out_specs = [st.row_spec(d), st.row_spec(d)]
        out_shape = [jax.ShapeDtypeStruct((st.rows, d), F32),
                     jax.ShapeDtypeStruct((st.rows, d), BF16)]
    else:
        rw_pad, rb_pad, pool = router
        rw_hi = rw_pad.astype(BF16)
        rw_lo = (rw_pad - rw_hi.astype(F32)).astype(BF16)
        in_specs += [_resident(rw_pad.shape), _resident(rw_pad.shape), _resident(rb_pad.shape),
                     pl.BlockSpec(memory_space=pl.ANY)]
        args += [rw_hi, rw_lo, rb_pad, pool]
        aliases = {len(args) - 1: 1}
        tile0 = st.row0 // st.tm
        out_specs = [st.row_spec(d),
                     pl.BlockSpec((st.tm * LANE_CHUNKS, V7X_LANES), lambda i: (tile0 + i, 0)),
                     st.row_spec(V7X_LANES)]
        out_shape = [jax.ShapeDtypeStruct((st.rows, d), F32),
                     jax.ShapeDtypeStruct(pool.shape, F32),
                     jax.ShapeDtypeStruct((st.rows, V7X_LANES), F32)]
    return pl.pallas_call(
        functools.partial(_out_proj_kernel, bias is not None, router is not None),
        grid=(st.tiles,),
        in_specs=in_specs,
        out_specs=out_specs,
        out_shape=out_shape,
        input_output_aliases=aliases,
        compiler_params=_params("parallel"),
        name="out_proj",
    )(*args)


def _swiglu_part(x_bf16, wg_ref, wu_ref, wd_ref):
    g = _dot(x_bf16, wg_ref[...])
    u = _dot(x_bf16, wu_ref[...])
    return _dot((g * jax.nn.sigmoid(g) * u).astype(BF16), wd_ref[...])


def _swiglu_kernel(h_ref, wg_ref, wu_ref, wd_ref, x_ref, g2_ref, o_ref, acc_ref):
    j = pl.program_id(1)
    part = _swiglu_part(h_ref[...], wg_ref, wu_ref, wd_ref)

    @pl.when(j == 0)
    def _():
        acc_ref[...] = part

    @pl.when(j > 0)
    def _():
        acc_ref[...] += part

    @pl.when(j == pl.num_programs(1) - 1)
    def _():
        o_ref[...] = x_ref[...] + g2_ref[...] * acc_ref[...]


def _swiglu(st, h2, w_gu, w_down, x1, mod, layer):
    st = st.retiled(WIDE_OUT_TILE)
    d, two_f = w_gu.shape
    f = two_f // 2
    tf = DENSE_FF_CHUNK
    nf = f // tf
    return pl.pallas_call(
        _swiglu_kernel,
        grid=(st.tiles, nf),
        in_specs=[
            pl.BlockSpec((st.tm, d), lambda i, j: (i, 0)),
            pl.BlockSpec((d, tf), lambda i, j: (0, j)),
            pl.BlockSpec((d, tf), lambda i, j: (0, nf + j)),
            pl.BlockSpec((tf, d), lambda i, j: (j, 0)),
            pl.BlockSpec((st.tm, d), lambda i, j: (i, 0)),
            st.mod_spec(layer, 5),
        ],
        out_specs=pl.BlockSpec((st.tm, d), lambda i, j: (i, 0)),
        out_shape=jax.ShapeDtypeStruct((st.rows, d), F32),
        scratch_shapes=[pltpu.VMEM((st.tm, d), F32)],
        compiler_params=_params("parallel", "arbitrary"),
        name="swiglu",
    )(h2, w_gu, w_gu, w_down, x1, mod)


def _moe_plan(route, tm):
    tokens = route.shape[0]
    max_tiles = (2 * tokens) // tm + N_EXPERTS
    expert = route[:, :2].astype(jnp.int32).reshape(-1)
    onehot = (expert[:, None] == jnp.arange(N_EXPERTS, dtype=jnp.int32)[None, :]).astype(jnp.int32)
    csum = jnp.cumsum(onehot, axis=0)
    rank = jnp.sum(csum * onehot, axis=1) - 1
    counts = csum[-1]
    tiles_per_expert = (counts + tm - 1) // tm
    tiles_end = jnp.cumsum(tiles_per_expert)
    start = (tiles_end - tiles_per_expert) * tm
    pos = jnp.sum(start[None, :] * onehot, axis=1) + rank
    token = jnp.arange(2 * tokens, dtype=jnp.int32) // 2
    src = jnp.zeros((max_tiles * tm,), jnp.int32).at[pos].set(token, unique_indices=True)
    n_tiles = tiles_end[-1:]
    tile = jnp.minimum(jnp.arange(max_tiles, dtype=jnp.int32), n_tiles - 1)
    tile_expert = jnp.sum((tile[:, None] >= tiles_end[None, :]).astype(jnp.int32), axis=1)
    return src, pos.astype(jnp.int32), tile_expert.astype(jnp.int32), n_tiles.astype(jnp.int32)


def _token_tile_copy(src_hbm, row, dst, slot_row, sem):
    return pltpu.make_async_copy(
        src_hbm.at[pl.ds(pl.multiple_of(row * LANE_CHUNKS, LANE_CHUNKS), LANE_CHUNKS)],
        dst.at[pl.ds(pl.multiple_of(slot_row * LANE_CHUNKS, LANE_CHUNKS), LANE_CHUNKS)],
        sem)


def _moe_ffn_kernel(tm, src_ref, te_ref, nt_ref, x_hbm, wg_ref, wu_ref, wd_ref, y_ref,
                    xbuf, xd_ref, acc_ref, sem):
    t, j = pl.program_id(0), pl.program_id(1)
    last_j = pl.num_programs(1) - 1
    n_tiles = nt_ref[0]
    slot = t % 2
    rows = tm * LANE_CHUNKS

    def start_gather(tile, into):
        def body(r, carry):
            _token_tile_copy(x_hbm, src_ref[tile * tm + r], xbuf.at[into], r, sem.at[into]).start()
            return carry
        lax.fori_loop(0, tm, body, 0, unroll=8)

    @pl.when((t == 0) & (j == 0))
    def _():
        start_gather(0, 0)

    @pl.when((j == 0) & (t < n_tiles))
    def _():
        pltpu.make_async_copy(x_hbm.at[pl.ds(0, rows)], xbuf.at[slot], sem.at[slot]).wait()
        for c in range(LANE_CHUNKS):
            xd_ref[:, c * V7X_LANES:(c + 1) * V7X_LANES] = (
                xbuf[slot, pl.ds(c, tm, stride=LANE_CHUNKS), :].astype(BF16))

        @pl.when(t + 1 < n_tiles)
        def _():
            start_gather(t + 1, 1 - slot)

    @pl.when(t < n_tiles)
    def _():
        part = _swiglu_part(xd_ref[...], wg_ref, wu_ref, wd_ref)

        @pl.when(j == 0)
        def _():
            acc_ref[...] = part

        @pl.when(j > 0)
        def _():
            acc_ref[...] += part

        @pl.when(j == last_j)
        def _():
            _store_token_tiles(y_ref, acc_ref[...])

    @pl.when((t >= n_tiles) & (j == last_j))
    def _():
        y_ref[...] = jnp.zeros_like(y_ref)


def _moe_ffn(pool, src, tile_expert, n_tiles, w_gu, w_down, tm):
    n_e, d, two_f = w_gu.shape
    f = two_f // 2
    tf = MOE_FF_CHUNK
    nf = f // tf
    max_tiles = tile_expert.shape[0]
    rows = tm * LANE_CHUNKS
    grid_spec = pltpu.PrefetchScalarGridSpec(
        num_scalar_prefetch=3,
        grid=(max_tiles, nf),
        in_specs=[
            pl.BlockSpec(memory_space=pl.ANY),
            pl.BlockSpec((None, d, tf), lambda t, j, src, te, nt: (te[t], 0, j)),
            pl.BlockSpec((None, d, tf), lambda t, j, src, te, nt: (te[t], 0, nf + j)),
            pl.BlockSpec((None, tf, d), lambda t, j, src, te, nt: (te[t], j, 0)),
        ],
        out_specs=pl.BlockSpec((rows, V7X_LANES), lambda t, j, src, te, nt: (t, 0)),
        scratch_shapes=[pltpu.VMEM((2, rows, V7X_LANES), F32), pltpu.VMEM((tm, d), BF16),
                        pltpu.VMEM((tm, d), F32), pltpu.SemaphoreType.DMA((2,))],
    )
    return pl.pallas_call(
        functools.partial(_moe_ffn_kernel, tm),
        grid_spec=grid_spec,
        out_shape=jax.ShapeDtypeStruct((max_tiles * rows, V7X_LANES), F32),
        compiler_params=_params("arbitrary", "arbitrary"),
        name="moe_ffn",
    )(src, tile_expert, n_tiles, pool, w_gu, w_gu, w_down)


def _moe_combine_kernel(tm, token0, pos_ref, y_hbm, route_ref, x_ref, g2_ref, o_ref, ybuf, sem):
    i = pl.program_id(0)
    slot = i % 2
    rows = tm * LANE_CHUNKS

    def start_gather(tile, into):
        def body(r, carry):
            a = 2 * (token0 + tile * tm + r)
            for k in range(2):
                _token_tile_copy(y_hbm, pos_ref[a + k], ybuf.at[into], k * tm + r,
                                 sem.at[into]).start()
            return carry
        lax.fori_loop(0, tm, body, 0, unroll=4)

    @pl.when(i == 0)
    def _():
        start_gather(0, 0)

    pltpu.make_async_copy(y_hbm.at[pl.ds(0, 2 * rows)], ybuf.at[slot], sem.at[slot]).wait()

    @pl.when(i + 1 < pl.num_programs(0))
    def _():
        start_gather(i + 1, 1 - slot)

    route = route_ref[...]
    lane = lax.broadcasted_iota(jnp.int32, route.shape, 1)
    gate0 = jnp.sum(jnp.where(lane == 2, route, 0.0), axis=-1, keepdims=True)
    gate1 = jnp.sum(jnp.where(lane == 3, route, 0.0), axis=-1, keepdims=True)
    for c in range(LANE_CHUNKS):
        cols = slice(c * V7X_LANES, (c + 1) * V7X_LANES)
        y0 = ybuf[slot, pl.ds(c, tm, stride=LANE_CHUNKS), :]
        y1 = ybuf[slot, pl.ds(rows + c, tm, stride=LANE_CHUNKS), :]
        o_ref[:, cols] = x_ref[:, cols] + g2_ref[:, cols] * (gate0 * y0 + gate1 * y1)


def _moe_combine(st, y_slots, pos, route, x1, mod, layer):
    d = D_MODEL
    tm = st.tm
    rows = tm * LANE_CHUNKS
    grid_spec = pltpu.PrefetchScalarGridSpec(
        num_scalar_prefetch=1,
        grid=(st.tiles,),
        in_specs=[
            pl.BlockSpec(memory_space=pl.ANY),
            st.row_spec(V7X_LANES), st.row_spec(d), st.mod_spec(layer, 5),
        ],
        out_specs=st.row_spec(d),
        scratch_shapes=[pltpu.VMEM((2, 2 * rows, V7X_LANES), F32), pltpu.SemaphoreType.DMA((2,))],
    )
    return pl.pallas_call(
        functools.partial(_moe_combine_kernel, tm, st.row0),
        grid_spec=grid_spec,
        out_shape=jax.ShapeDtypeStruct((st.rows, d), F32),
        compiler_params=_params("arbitrary"),
        name="moe_combine",
    )(pos, y_slots, route, x1, mod)


def _head_rmsnorm(x, g2):
    lane = lax.broadcasted_iota(jnp.int32, x.shape, 1)
    lo = lane < HEAD_DIM
    sq = x * x
    s_lo = jnp.sum(jnp.where(lo, sq, 0.0), axis=-1, keepdims=True)
    s_hi = jnp.sum(jnp.where(lo, 0.0, sq), axis=-1, keepdims=True)
    ms = jnp.where(lo, s_lo, s_hi) * (1.0 / HEAD_DIM)
    return x * lax.rsqrt(ms + EPS) * g2


def _rope(x, cos, sin_signed):
    q4 = HEAD_DIM // 4
    lane = lax.broadcasted_iota(jnp.int32, x.shape, 1)
    first = (lane & q4) == 0
    width = x.shape[1]
    partner = jnp.where(first, pltpu.roll(x, width - q4, 1), pltpu.roll(x, q4, 1))
    return x * cos + partner * sin_signed


def _attention_kernel(past, use_rope, lam_init, *refs):
    refs = list(refs)
    q_ref, k_ref, v_ref = refs.pop(0), refs.pop(0), refs.pop(0)
    if past:
        ck_ref, cv_ref = refs.pop(0), refs.pop(0)
    if use_rope:
        cosq_ref, sinq_ref, cosk_ref, sink_ref = (refs.pop(0) for _ in range(4))
    qg_ref, kg_ref, lam_ref, sg_ref = (refs.pop(0) for _ in range(4))
    o_ref = refs.pop(0)
    nk_ref = None if past else refs.pop(0)
    kall_ref, vall_ref = refs

    qi = pl.program_id(2)

    @pl.when(qi == 0)
    def _():
        k = _head_rmsnorm(k_ref[...], kg_ref[...])
        if nk_ref is not None:
            nk_ref[...] = k
        if use_rope:
            k = _rope(k, cosk_ref[...], sink_ref[...])
        if past:
            kall_ref[0:past, :] = ck_ref[...].astype(BF16)
            vall_ref[0:past, :] = cv_ref[...].astype(BF16)
        kall_ref[past:, :] = k.astype(BF16)
        vall_ref[past:, :] = v_ref[...].astype(BF16)

    q = _head_rmsnorm(q_ref[...], qg_ref[...])
    if use_rope:
        q = _rope(q, cosq_ref[...], sinq_ref[...])
    q = q * (HEAD_DIM ** -0.5 * math.log2(math.e))
    tq = q.shape[0]
    lane = lax.broadcasted_iota(jnp.int32, q.shape, 1)
    lo = lane < HEAD_DIM
    nt = (((1,), (1,)), ((), ()))

    def attend(qm):
        s = lax.dot_general(qm.astype(BF16), kall_ref[...], nt, preferred_element_type=F32)
        p = jnp.exp2(s - jnp.max(s, axis=-1, keepdims=True))
        norm = 1.0 / jnp.sum(p, axis=-1, keepdims=True)
        return _dot(p.astype(BF16), vall_ref[...]) * norm

    row_chunks = [slice(r, r + ATTENTION_ROWS) for r in range(0, tq, ATTENTION_ROWS)]
    first = jnp.where(lo, q, 0.0)
    second = jnp.where(lo, 0.0, q)
    a0 = jnp.concatenate([attend(first[rows]) for rows in row_chunks], axis=0)
    a1 = jnp.concatenate([attend(second[rows]) for rows in row_chunks], axis=0)
    lv = lam_ref[...]
    lam = (jnp.exp(jnp.sum(lv[0:1] * lv[1:2], axis=-1, keepdims=True))
           - jnp.exp(jnp.sum(lv[2:3] * lv[3:4], axis=-1, keepdims=True)) + lam_init)
    o = a0 - lam * a1
    ms = jnp.mean(o * o, axis=-1, keepdims=True)
    o = o * lax.rsqrt(ms + EPS) * sg_ref[...] * (1.0 - lam_init)
    o_ref[...] = o.astype(o_ref.dtype)


def _attention(st, q, k, v, q_g, k_g, lam_vecs, subln_g, lam_init, cache_k=None, cache_v=None,
               rope=None):
    seq, hd = st.seq, N_HEADS * V_DIM
    tq = min(seq, 512)
    nq = seq // tq
    past = 0 if cache_k is None else cache_k.shape[1]
    lk = past + seq
    q3, k3, v3 = (a.reshape(st.batch, seq, hd) for a in (q, k, v))
    qblk = pl.BlockSpec((None, tq, V_DIM), lambda b, h, i: (b, i, h))
    kblk = pl.BlockSpec((None, seq, V_DIM), lambda b, h, i: (b, 0, h))
    in_specs = [qblk, kblk, kblk]
    args = [q3, k3, v3]
    if past:
        cblk = pl.BlockSpec((None, past, V_DIM), lambda b, h, i: (b, 0, h))
        in_specs += [cblk, cblk]
        args += [cache_k.reshape(st.batch, past, hd), cache_v.reshape(st.batch, past, hd)]
    if rope is not None:
        cos2, sin2 = rope
        tq_tab = pl.BlockSpec((tq, V_DIM), lambda b, h, i: (i, 0))
        k_tab = pl.BlockSpec((seq, V_DIM), lambda b, h, i: (0, 0))
        in_specs += [tq_tab, tq_tab, k_tab, k_tab]
        args += [cos2, sin2, cos2, sin2]
    small = lambda shape: pl.BlockSpec(shape, lambda b, h, i: (0,) * len(shape))
    in_specs += [small((1, V_DIM)), small((1, V_DIM)), small((4, HEAD_DIM)), small((1, V_DIM))]
    args += [jnp.tile(q_g, 2).reshape(1, V_DIM), jnp.tile(k_g, 2).reshape(1, V_DIM), lam_vecs,
             subln_g.reshape(1, V_DIM)]
    out_specs = [qblk]
    out_shape = [jax.ShapeDtypeStruct((st.batch, seq, hd), BF16)]
    if not past:
        out_specs.append(kblk)
        out_shape.append(jax.ShapeDtypeStruct((st.batch, seq, hd), F32))
    outs = pl.pallas_call(
        functools.partial(_attention_kernel, past, rope is not None, lam_init),
        grid=(st.batch, N_HEADS, nq),
        in_specs=in_specs,
        out_specs=out_specs,
        out_shape=out_shape,
        scratch_shapes=[pltpu.VMEM((lk, V_DIM), BF16), pltpu.VMEM((lk, V_DIM), BF16)],
        compiler_params=_params("parallel", "parallel", "arbitrary"),
        name="diff_attention",
    )(*args)
    return [o.reshape(st.rows, hd) for o in outs]


def _dft_matrices(seq):
    idx = np.arange(seq, dtype=np.int64)
    ang = (np.outer(idx, idx) % (2 * seq)).astype(np.float64) * (math.pi / seq)
    return jnp.asarray(np.cos(ang), dtype=BF16), jnp.asarray(np.sin(ang), dtype=BF16)


def _filter_features(seq):
    t = jnp.linspace(0.0, 1.0, seq, dtype=F32)[:, None]
    bands = (HY_EMB - 1) // 2
    w_ang = 2.0 * math.pi * jnp.arange(seq, dtype=F32)[:, None] / seq
    f = jnp.linspace(1e-4, bands - 1, bands, dtype=F32)[None, :]
    ang = f * w_ang
    feats = jnp.concatenate([t, jnp.cos(ang), -jnp.sin(ang)], axis=-1)
    return jnp.pad(feats, ((0, 0), (0, V7X_LANES - HY_EMB)))


def _decay_rates():
    min_decay = math.log(HY_DECAY_TARGET) / HY_SLOW_PCT
    max_decay = math.log(HY_DECAY_TARGET) / HY_FAST_PCT
    return jnp.linspace(min_decay, max_decay, D_MODEL, dtype=F32)[None, :]


def _rope_tables(seq):
    rows = seq // GRID_W
    row = jnp.repeat(jnp.arange(rows, dtype=F32), GRID_W)
    col = jnp.tile(jnp.arange(GRID_W, dtype=F32), rows)
    quarter = HEAD_DIM // 4
    inv = ROPE_BASE ** (-jnp.arange(quarter, dtype=F32) / quarter)

    def axis_angles(pos):
        a = pos[:, None] * inv[None, :]
        return jnp.concatenate([a, a], axis=-1)

    ang = jnp.concatenate([axis_angles(row), axis_angles(col)], axis=-1)
    sign = jnp.where((jnp.arange(HEAD_DIM) & quarter) == 0, -1.0, 1.0).astype(F32)
    return jnp.tile(jnp.cos(ang), (1, 2)), jnp.tile(jnp.sin(ang) * sign[None, :], (1, 2))


def kernel(x_prompt, x_sample, cache_k, cache_v, c, c_ctx, ada_w, ada_b, norm_g, hy_in_w, hy_in_b, hy_conv_w, hy_conv_b, hy_f_w1, hy_f_b1, hy_f_w2, hy_f_b2, hy_f_freq, hy_f_w3, hy_bias, hy_out_w, hy_out_b, at_qkv_w, at_q_g, at_k_g, at_lam, at_subln_g, at_out_w, dn_w_gu, dn_w_down, mo_router_w, mo_router_b, mo_w_gu, mo_w_down):
    d = D_MODEL
    batch, seq = x_prompt.shape[:2]
    dec_batch, dec_seq = x_sample.shape[:2]
    past = cache_k.shape[2]
    streams = [
        (_Stream(batch, seq, 0, False), x_prompt.reshape(batch * seq, d), None),
        (_Stream(dec_batch, dec_seq, 1, True, row0=batch * seq),
         x_sample.reshape(dec_batch * dec_seq, d), (cache_k[:, 0], cache_v[:, 0])),
    ]
    tokens = batch * seq + dec_batch * dec_seq

    cond = jnp.concatenate(
        [c_ctx[None, :], c, jnp.zeros((COND_ROWS - 1 - dec_batch, d), F32)], axis=0)
    mod = _adaln(cond, ada_w, ada_b)
    mod = mod.reshape(mod.shape[0], COND_ROWS, 1, 6 * d)

    in_w, out_w = hy_in_w[0].astype(BF16), hy_out_w[0].astype(BF16)
    qkv_w, at_out = at_qkv_w[0].astype(BF16), at_out_w[0].astype(BF16)
    dn_gu, dn_down = dn_w_gu[0].astype(BF16), dn_w_down[0].astype(BF16)
    mo_gu, mo_down = mo_w_gu[0].astype(BF16), mo_w_down[0].astype(BF16)
    w1_pad = jnp.pad(hy_f_w1[0], ((0, V7X_LANES - HY_EMB), (0, 0)))
    rw_pad = jnp.pad(mo_router_w[0], ((0, 0), (0, V7X_LANES - N_EXPERTS)))
    rb_pad = jnp.pad(mo_router_b[0], (0, V7X_LANES - N_EXPERTS)).reshape(1, V7X_LANES)
    deltas = _decay_rates()
    lam_init = 0.8 - 0.6 * math.exp(-0.3 * 1)

    routed = []
    pool = jnp.zeros((tokens * LANE_CHUNKS, V7X_LANES), F32)
    for st, x, cache in streams:
        cmat, smat = _dft_matrices(st.seq)
        hs, hd = _filter_time(st.seq, _filter_features(st.seq), w1_pad, hy_f_b1[0], hy_f_w2[0],
                              hy_f_b2[0], hy_f_freq[0], hy_f_w3[0], deltas)
        kr, ki, kn = _filter_spectrum(st.seq, hs, hd, cmat, smat)
        (proj,) = _mod_matmul(st, x, norm_g[0, 0], mod, 0, in_w, hy_in_b[0], 1, BF16)
        z = _hyena_core(st, proj, hy_conv_w[0], hy_conv_b[0].reshape(1, 3 * d), kr, ki, kn,
                        hy_bias[0], cmat, smat)
        x, h2 = _out_proj(st, z, out_w, hy_out_b[0], x, mod, 0, norm_g[0, 1])
        x = _swiglu(st, h2, dn_gu, dn_down, x, mod, 0)

        q, k, v = _mod_matmul(st, x, norm_g[1, 0], mod, 1, qkv_w, None, 3, F32)
        if cache is None:
            o, new_k = _attention(st, q, k, v, at_q_g[0], at_k_g[0], at_lam[0], at_subln_g[0],
                                  lam_init)
            new_kv = (new_k, v)
        else:
            (o,) = _attention(st, q, k, v, at_q_g[0], at_k_g[0], at_lam[0], at_subln_g[0],
                              lam_init, cache[0], cache[1], _rope_tables(st.seq))
        x, pool, route = _out_proj(st, o, at_out, None, x, mod, 1, norm_g[1, 1],
                                   (rw_pad, rb_pad, pool))
        routed.append((st, x, route))

    src, pos, tile_expert, n_tiles = _moe_plan(
        jnp.concatenate([route[:, :4] for _, _, route in routed], axis=0), MOE_TILE)
    y_slots = _moe_ffn(pool, src, tile_expert, n_tiles, mo_gu, mo_down, MOE_TILE)
    results = [_moe_combine(st, y_slots, pos, route, x, mod, 1).reshape(st.batch, st.seq, d)
               for st, x, route in routed]

    new_k, new_v = new_kv
    return (results[0], results[1],
            new_k.reshape(batch, 1, seq, N_HEADS, 2, HEAD_DIM),
            new_v.reshape(batch, 1, seq, N_HEADS, V_DIM))
```

```python
import functools
import math

import numpy as np
import jax
import jax.numpy as jnp
from jax import lax
from jax.experimental import pallas as pl
from jax.experimental.pallas import tpu as pltpu

F32 = jnp.float32
BF16 = jnp.bfloat16
HIGHEST = lax.Precision.HIGHEST

D_MODEL = 1024
GRID_W = 64
HY_ORDER = 2
HY_EMB = 33
HY_FW = 64
HY_DECAY_TARGET = 1e-2
HY_FAST_PCT = 0.3
HY_SLOW_PCT = 1.5
N_HEADS = 8
HEAD_DIM = 64
V_DIM = 2 * HEAD_DIM
ROPE_BASE = 10000.0
D_FF = 2816
N_EXPERTS = 8
D_FF_EXPERT = 3584
EPS = 1e-6

V7X_LANES = 128
V7X_VMEM_LIMIT_BYTES = 56 * 1024 * 1024
LANE_CHUNKS = D_MODEL // V7X_LANES
COND_ROWS = 16
TOKEN_TILE = 1024
WIDE_OUT_TILE = 512
MOE_TILE = 1024
MOE_FF_CHUNK = 512
DENSE_FF_CHUNK = D_FF // 2
HYENA_BLOCK = 512
NYQUIST_ROWS = 8
ATTENTION_QUERIES = 512
ATTENTION_GROUP_QUERIES = 1024
ATTENTION_ROWS = 256


def _params(*semantics):
    return pltpu.CompilerParams(dimension_semantics=semantics,
                                vmem_limit_bytes=V7X_VMEM_LIMIT_BYTES)


def _resident(shape):
    zeros = (0,) * len(shape)
    return pl.BlockSpec(shape, lambda *_: zeros, pipeline_mode=pl.Buffered(1))


def _dot(a, b):
    return jnp.dot(a, b, preferred_element_type=F32)


def _dot_f32(a, b):
    return jnp.dot(a, b, precision=HIGHEST, preferred_element_type=F32)


def _modulate(x, g, shift, scale):
    ms = jnp.mean(x * x, axis=-1, keepdims=True)
    return (x * lax.rsqrt(ms + EPS) * g) * (1.0 + scale) + shift


def _adaln_kernel(cond_ref, w_ref, b_ref, o_ref):
    c = cond_ref[...]
    o_ref[...] = _dot_f32(c * jax.nn.sigmoid(c), w_ref[...]) + b_ref[...]


def _adaln(cond, ada_w, ada_b):
    depth, d, n = ada_w.shape
    tn = 1536
    return pl.pallas_call(
        _adaln_kernel,
        grid=(depth, n // tn),
        in_specs=[
            pl.BlockSpec((COND_ROWS, d), lambda i, j: (0, 0)),
            pl.BlockSpec((None, d, tn), lambda i, j: (i, 0, j)),
            pl.BlockSpec((None, 1, tn), lambda i, j: (i, 0, j)),
        ],
        out_specs=pl.BlockSpec((None, COND_ROWS, tn), lambda i, j: (i, 0, j)),
        out_shape=jax.ShapeDtypeStruct((depth, COND_ROWS, n), F32),
        compiler_params=_params("parallel", "parallel"),
        name="adaln",
    )(cond, ada_w, ada_b.reshape(depth, 1, n))


class _Stream:
    def __init__(self, batch, seq, cond_row0, per_seq_cond, row0=0, tile=TOKEN_TILE):
        self.batch, self.seq = batch, seq
        self.rows = batch * seq
        self.row0 = row0
        self._cond = (cond_row0, per_seq_cond)
        if per_seq_cond:
            self.tm = min(tile, seq)
            tiles_per_seq = seq // self.tm
            self.cond_row = lambda i: cond_row0 + i // tiles_per_seq
        else:
            self.tm = min(tile, self.rows)
            self.cond_row = lambda i: cond_row0
        self.tiles = self.rows // self.tm

    def retiled(self, tile):
        return _Stream(self.batch, self.seq, *self._cond, row0=self.row0, tile=tile)

    def mod_spec(self, layer, chunk):
        return pl.BlockSpec((None, None, 1, D_MODEL),
                            lambda i, *_: (layer, self.cond_row(i), 0, chunk))

    def row_spec(self, width):
        return pl.BlockSpec((self.tm, width), lambda i, *_: (i, 0))


def _mod_matmul_kernel(n_out, has_bias, x_ref, g_ref, sh_ref, sc_ref, w_ref, *rest):
    if has_bias:
        b_ref, out_refs = rest[0], rest[1:]
    else:
        b_ref, out_refs = None, rest
    h = _modulate(x_ref[...], g_ref[...], sh_ref[...], sc_ref[...])
    y = _dot(h.astype(BF16), w_ref[...])
    if has_bias:
        y = y + b_ref[...]
    width = y.shape[1] // n_out
    for k, o_ref in enumerate(out_refs):
        o_ref[...] = y[:, k * width:(k + 1) * width].astype(o_ref.dtype)


def _mod_matmul(st, x, norm_g, mod, layer, w_bf16, bias, n_out, out_dtype):
    st = st.retiled(WIDE_OUT_TILE)
    d, n = w_bf16.shape
    in_specs = [st.row_spec(d), _resident((1, d)), st.mod_spec(layer, 0), st.mod_spec(layer, 1),
                _resident((d, n))]
    args = [x, norm_g.reshape(1, d), mod, mod, w_bf16]
    if bias is not None:
        in_specs.append(_resident((1, n)))
        args.append(bias.reshape(1, n))
    width = n // n_out
    outs = pl.pallas_call(
        functools.partial(_mod_matmul_kernel, n_out, bias is not None),
        grid=(st.tiles,),
        in_specs=in_specs,
        out_specs=[st.row_spec(width)] * n_out,
        out_shape=[jax.ShapeDtypeStruct((st.rows, width), out_dtype)] * n_out,
        compiler_params=_params("parallel"),
        name="mod_matmul",
    )(*args)
    return outs


def _filter_time_kernel(feats_ref, w1_ref, b1_ref, w2_ref, b2_ref, fr_ref, w3_ref, dl_ref,
                        hf_ref, hb_ref):
    feats = feats_ref[...]
    fr = fr_ref[...]
    h = jnp.sin(fr[0:1] * (_dot_f32(feats, w1_ref[...]) + b1_ref[...]))
    h = jnp.sin(fr[1:2] * (_dot_f32(h, w2_ref[...]) + b2_ref[...]))
    h = _dot_f32(h, w3_ref[...])
    t = feats[:, 0:1]
    decay = jnp.exp(-t * jnp.abs(dl_ref[...]))
    half = HY_ORDER * D_MODEL
    decay2 = jnp.concatenate([decay] * HY_ORDER, axis=1)
    hf_ref[...] = h[:, :half] * decay2
    hb_ref[...] = jnp.where(t == 0.0, 0.0, h[:, half:] * decay2)


def _filter_time(seq, feats_pad, w1_pad, b1, w2, b2, freq, w3, deltas):
    tl = min(seq, 256)
    half = HY_ORDER * D_MODEL
    out = jax.ShapeDtypeStruct((seq, half), F32)
    return pl.pallas_call(
        _filter_time_kernel,
        grid=(seq // tl,),
        in_specs=[
            pl.BlockSpec((tl, V7X_LANES), lambda i: (i, 0)),
            _resident(w1_pad.shape), _resident((1, HY_FW)), _resident((HY_FW, HY_FW)),
            _resident((1, HY_FW)), _resident((2, HY_FW)), _resident(w3.shape),
            _resident((1, D_MODEL)),
        ],
        out_specs=[pl.BlockSpec((tl, half), lambda i: (i, 0))] * 2,
        out_shape=[out, out],
        compiler_params=_params("parallel"),
        name="hyena_filter_time",
    )(feats_pad, w1_pad, b1.reshape(1, HY_FW), w2, b2.reshape(1, HY_FW), freq, w3, deltas)


def _filter_spectrum_kernel(blk, nb, hf_ref, hb_ref, c_ref, s_ref, kr_ref, ki_ref, kn_ref):
    row = lax.broadcasted_iota(jnp.int32, (blk, 1), 0)
    sg = (1 - 2 * (row & 1)).astype(F32)
    wgt = jnp.where(row == 0, 1.0, 2.0) * (1.0 / (2 * blk))
    fwd, bwd = [], []
    for j in range(nb):
        rows = slice(j * blk, (j + 1) * blk)
        for ref, out in ((hf_ref, fwd), (hb_ref, bwd)):
            x = ref[rows, :]
            xb = x.astype(BF16)
            out.append((_dot(c_ref[...], xb), _dot(s_ref[...], xb),
                        jnp.sum(x * sg, axis=0, keepdims=True), x[0:1, :],
                        xb[0:1, :].astype(F32)))
    kn_ref[...] = jnp.zeros_like(kn_ref)
    for d in range(-(nb - 1), nb):
        if d == 0:
            (fc, fs, fn, _, _), (bc, bs, bn, _, _) = fwd[0], bwd[0]
            kr, ki, kn = fc + bc, bs - fs, fn + bn
        else:
            parts, im_sign = (fwd, -1.0) if d > 0 else (bwd, 1.0)
            c1, s1, n1, _, _ = parts[abs(d)]
            c0, s0, n0, x0, x0_seen = parts[abs(d) - 1]
            kr, ki, kn = c1 + sg * (c0 - x0_seen), im_sign * (s1 + sg * s0), n1 + n0 - x0
        slot = d + nb - 1
        kr_ref[slot * blk:(slot + 1) * blk, :] = wgt * kr
        ki_ref[slot * blk:(slot + 1) * blk, :] = wgt * ki
        kn_ref[slot:slot + 1, :] = kn * (1.0 / (2 * blk))


def _filter_spectrum(seq, blk, hf, hb, cmat, smat):
    nb = seq // blk
    half = hf.shape[1]
    tn = 256
    col = pl.BlockSpec((seq, tn), lambda j: (0, j))
    spec_rows = (2 * nb - 1) * blk
    out_col = pl.BlockSpec((spec_rows, tn), lambda j: (0, j))
    return pl.pallas_call(
        functools.partial(_filter_spectrum_kernel, blk, nb),
        grid=(half // tn,),
        in_specs=[col, col, _resident((blk, blk)), _resident((blk, blk))],
        out_specs=[out_col, out_col, pl.BlockSpec((NYQUIST_ROWS, tn), lambda j: (0, j))],
        out_shape=[jax.ShapeDtypeStruct((spec_rows, half), F32)] * 2
        + [jax.ShapeDtypeStruct((NYQUIST_ROWS, half), F32)],
        compiler_params=_params("parallel"),
        name="hyena_filter_spectrum",
    )(hf, hb, cmat, smat)


def _hyena_core_kernel(seq, blk, pv_ref, p1_ref, p2_ref, cwv_ref, cw1_ref, cw2_ref, cbv_ref,
                       cb1_ref, cb2_ref, kr0_ref, ki0_ref, kn0_ref, kr1_ref, ki1_ref, kn1_ref,
                       bias0_ref, bias1_ref, c_ref, s_ref, z_ref,
                       u_ref, ub_ref, gate_ref, a_ref, b_ref):
    row = lax.broadcasted_iota(jnp.int32, (seq, 1), 0)
    sign = (1 - 2 * (row[:blk] & 1)).astype(F32)
    nb = seq // blk
    blocks = [slice(j * blk, (j + 1) * blk) for j in range(nb)]

    def short_conv(x_ref, w_ref, b_ref):
        x = x_ref[...].astype(F32)
        w = w_ref[...]
        prev = jnp.where(row == 0, 0.0, pltpu.roll(x, 1, 0))
        nxt = jnp.where(row == seq - 1, 0.0, pltpu.roll(x, seq - 1, 0))
        return prev * w[0:1] + x * w[1:2] + nxt * w[2:3] + b_ref[...]

    def gated_long_conv(kr_ref, ki_ref, kn_ref, bias_ref, write):
        ub_ref[...] = u_ref[...].astype(BF16)
        nyq_in = []
        for rows in blocks:
            a_ref[rows, :] = _dot(c_ref[...], ub_ref[rows, :])
            b_ref[rows, :] = _dot(s_ref[...], ub_ref[rows, :])
            nyq_in.append(jnp.sum(u_ref[rows, :] * sign, axis=0, keepdims=True))
        for i, rows in enumerate(blocks):
            p = q = nyq = None
            for j, src in enumerate(blocks):
                slot = i - j + nb - 1
                kr = kr_ref[slot * blk:(slot + 1) * blk, :]
                ki = ki_ref[slot * blk:(slot + 1) * blk, :]
                a, b = a_ref[src, :], b_ref[src, :]
                pj, qj = a * kr + b * ki, b * kr - a * ki
                nj = nyq_in[j] * kn_ref[slot:slot + 1, :]
                p, q, nyq = (pj, qj, nj) if j == 0 else (p + pj, q + qj, nyq + nj)
            y = _dot(c_ref[...], p.astype(BF16)) + _dot(s_ref[...], q.astype(BF16))
            y = y + sign * nyq + u_ref[rows, :] * bias_ref[...]
            write(rows, gate_ref[rows, :] * y)

    def to_u(rows, val):
        u_ref[rows, :] = val

    def to_z(rows, val):
        z_ref[rows, :] = val.astype(z_ref.dtype)

    u_ref[...] = short_conv(pv_ref, cwv_ref, cbv_ref)
    gate_ref[...] = short_conv(p1_ref, cw1_ref, cb1_ref)
    gated_long_conv(kr0_ref, ki0_ref, kn0_ref, bias0_ref, to_u)
    gate_ref[...] = short_conv(p2_ref, cw2_ref, cb2_ref)
    gated_long_conv(kr1_ref, ki1_ref, kn1_ref, bias1_ref, to_z)


def _hyena_core(st, blk, proj, conv_w, conv_b, kr, ki, kn, bias, cmat, smat):
    seq, d = st.seq, D_MODEL
    tn = 256
    nj = d // tn
    spec_rows = kr.shape[0]
    proj3 = proj.reshape(st.batch, seq, 3 * d)

    def part(k):
        return pl.BlockSpec((None, seq, tn), lambda j, b: (b, 0, k * nj + j))

    def cols(rows, k, buffers=2):
        return pl.BlockSpec((rows, tn), lambda j, b: (0, k * nj + j),
                            pipeline_mode=pl.Buffered(buffers))

    in_specs = ([part(0), part(1), part(2)]
                + [cols(3, k) for k in range(3)] + [cols(1, k) for k in range(3)]
                + [cols(spec_rows, 0, 1), cols(spec_rows, 0, 1), cols(NYQUIST_ROWS, 0),
                   cols(spec_rows, 1, 1), cols(spec_rows, 1, 1), cols(NYQUIST_ROWS, 1)]
                + [cols(1, 0), cols(1, 0)]
                + [_resident((blk, blk)), _resident((blk, blk))])
    z = pl.pallas_call(
        functools.partial(_hyena_core_kernel, seq, blk),
        grid=(nj, st.batch),
        in_specs=in_specs,
        out_specs=pl.BlockSpec((None, seq, tn), lambda j, b: (b, 0, j)),
        out_shape=jax.ShapeDtypeStruct((st.batch, seq, d), BF16),
        scratch_shapes=[pltpu.VMEM((seq, tn), F32), pltpu.VMEM((seq, tn), BF16),
                        pltpu.VMEM((seq, tn), F32), pltpu.VMEM((seq, tn), F32),
                        pltpu.VMEM((seq, tn), F32)],
        compiler_params=_params("parallel", "parallel"),
        name="hyena_core",
    )(proj3, proj3, proj3, conv_w, conv_w, conv_w, conv_b, conv_b, conv_b,
      kr, ki, kn, kr, ki, kn, bias[0:1], bias[1:2], cmat, smat)
    return z.reshape(st.rows, d)


def _out_proj_kernel(has_bias, has_router, *refs):
    refs = list(refs)
    z_ref, w_ref = refs.pop(0), refs.pop(0)
    b_ref = refs.pop(0) if has_bias else None
    x_ref, g1_ref, ng_ref, sh_ref, sc_ref = (refs.pop(0) for _ in range(5))
    if has_router:
        rw_hi_ref, rw_lo_ref, rb_ref = refs.pop(0), refs.pop(0), refs.pop(0)
        refs.pop(0)
    x1_ref, h2_ref = refs.pop(0), refs.pop(0)
    m = _dot(z_ref[...], w_ref[...])
    if has_bias:
        m = m + b_ref[...]
    x1 = x_ref[...] + g1_ref[...] * m
    x1_ref[...] = x1
    h2 = _modulate(x1, ng_ref[...], sh_ref[...], sc_ref[...])
    if not has_router:
        h2_ref[...] = h2.astype(BF16)
        return
    _store_token_tiles(h2_ref, h2)
    route_ref = refs.pop(0)
    h_hi = h2.astype(BF16)
    h_lo = (h2 - h_hi.astype(F32)).astype(BF16)
    logits = (_dot(h_hi, rw_hi_ref[...])
              + (_dot(h_lo, rw_hi_ref[...]) + _dot(h_hi, rw_lo_ref[...])) + rb_ref[...])
    lane = lax.broadcasted_iota(jnp.int32, logits.shape, 1)
    neg = -jnp.inf
    logits = jnp.where(lane < N_EXPERTS, logits, neg)
    m1 = jnp.max(logits, axis=-1, keepdims=True)
    i1 = jnp.min(jnp.where(logits == m1, lane, V7X_LANES), axis=-1, keepdims=True)
    rest = jnp.where(lane == i1, neg, logits)
    m2 = jnp.max(rest, axis=-1, keepdims=True)
    i2 = jnp.min(jnp.where(rest == m2, lane, V7X_LANES), axis=-1, keepdims=True)
    e2 = jnp.exp(m2 - m1)
    den = 1.0 + e2
    route = jnp.where(lane == 0, i1.astype(F32), jnp.where(lane == 1, i2.astype(F32), 0.0))
    route_ref[...] = route + jnp.where(lane == 2, 1.0 / den, 0.0) + jnp.where(lane == 3, e2 / den, 0.0)


def _store_token_tiles(ref, val):
    rows = val.shape[0]
    for c in range(LANE_CHUNKS):
        ref[pl.ds(c, rows, stride=LANE_CHUNKS), :] = val[:, c * V7X_LANES:(c + 1) * V7X_LANES]


def _out_proj(st, z_bf16, w_bf16, bias, x, mod, layer, norm_g, router=None):
    d = D_MODEL
    in_specs = [st.row_spec(z_bf16.shape[1]), _resident(w_bf16.shape)]
    args = [z_bf16, w_bf16]
    if bias is not None:
        in_specs.append(_resident((1, d)))
        args.append(bias.reshape(1, d))
    in_specs += [st.row_spec(d), st.mod_spec(layer, 2), _resident((1, d)),
                 st.mod_spec(layer, 3), st.mod_spec(layer, 4)]
    args += [x, mod, norm_g.reshape(1, d), mod, mod]
    aliases = {}
    if router is None:
        out_specs = [st.row_spec(d), st.row_spec(d)]
        out_shape = [jax.ShapeDtypeStruct((st.rows, d), F32),
                     jax.ShapeDtypeStruct((st.rows, d), BF16)]
    else:
        rw_pad, rb_pad, pool = router
        rw_hi = rw_pad.astype(BF16)
        rw_lo = (rw_pad - rw_hi.astype(F32)).astype(BF16)
        in_specs += [_resident(rw_pad.shape), _resident(rw_pad.shape), _resident(rb_pad.shape),
                     pl.BlockSpec(memory_space=pl.ANY)]
        args += [rw_hi, rw_lo, rb_pad, pool]
        aliases = {len(args) - 1: 1}
        tile0 = st.row0 // st.tm
        out_specs = [st.row_spec(d),
                     pl.BlockSpec((st.tm * LANE_CHUNKS, V7X_LANES), lambda i: (tile0 + i, 0)),
                     st.row_spec(V7X_LANES)]
        out_shape = [jax.ShapeDtypeStruct((st.rows, d), F32),
                     jax.ShapeDtypeStruct(pool.shape, F32),
                     jax.ShapeDtypeStruct((st.rows, V7X_LANES), F32)]
    return pl.pallas_call(
        functools.partial(_out_proj_kernel, bias is not None, router is not None),
        grid=(st.tiles,),
        in_specs=in_specs,
        out_specs=out_specs,
        out_shape=out_shape,
        input_output_aliases=aliases,
        compiler_params=_params("parallel"),
        name="out_proj",
    )(*args)


def _swiglu_part(x_bf16, wg_ref, wu_ref, wd_ref):
    g = _dot(x_bf16, wg_ref[...])
    u = _dot(x_bf16, wu_ref[...])
    return _dot((g * jax.nn.sigmoid(g) * u).astype(BF16), wd_ref[...])


def _swiglu_kernel(h_ref, wg_ref, wu_ref, wd_ref, x_ref, g2_ref, o_ref, acc_ref):
    j = pl.program_id(1)
    part = _swiglu_part(h_ref[...], wg_ref, wu_ref, wd_ref)

    @pl.when(j == 0)
    def _():
        acc_ref[...] = part

    @pl.when(j > 0)
    def _():
        acc_ref[...] += part

    @pl.when(j == pl.num_programs(1) - 1)
    def _():
        o_ref[...] = x_ref[...] + g2_ref[...] * acc_ref[...]


def _swiglu(st, h2, w_gu, w_down, x1, mod, layer):
    st = st.retiled(WIDE_OUT_TILE)
    d, two_f = w_gu.shape
    f = two_f // 2
    tf = DENSE_FF_CHUNK
    nf = f // tf
    return pl.pallas_call(
        _swiglu_kernel,
        grid=(st.tiles, nf),
        in_specs=[
            pl.BlockSpec((st.tm, d), lambda i, j: (i, 0)),
            pl.BlockSpec((d, tf), lambda i, j: (0, j)),
            pl.BlockSpec((d, tf), lambda i, j: (0, nf + j)),
            pl.BlockSpec((tf, d), lambda i, j: (j, 0)),
            pl.BlockSpec((st.tm, d), lambda i, j: (i, 0)),
            st.mod_spec(layer, 5),
        ],
        out_specs=pl.BlockSpec((st.tm, d), lambda i, j: (i, 0)),
        out_shape=jax.ShapeDtypeStruct((st.rows, d), F32),
        scratch_shapes=[pltpu.VMEM((st.tm, d), F32)],
        compiler_params=_params("parallel", "arbitrary"),
        name="swiglu",
    )(h2, w_gu, w_gu, w_down, x1, mod)


def _moe_plan(route, tm):
    tokens = route.shape[0]
    max_tiles = (2 * tokens) // tm + N_EXPERTS
    expert = route[:, :2].astype(jnp.int32).reshape(-1)
    onehot = (expert[:, None] == jnp.arange(N_EXPERTS, dtype=jnp.int32)[None, :]).astype(jnp.int32)
    csum = jnp.cumsum(onehot, axis=0)
    rank = jnp.sum(csum * onehot, axis=1) - 1
    counts = csum[-1]
    tiles_per_expert = (counts + tm - 1) // tm
    tiles_end = jnp.cumsum(tiles_per_expert)
    start = (tiles_end - tiles_per_expert) * tm
    pos = jnp.sum(start[None, :] * onehot, axis=1) + rank
    token = jnp.arange(2 * tokens, dtype=jnp.int32) // 2
    src = jnp.zeros((max_tiles * tm,), jnp.int32).at[pos].set(token, unique_indices=True)
    n_tiles = tiles_end[-1:]
    tile = jnp.minimum(jnp.arange(max_tiles, dtype=jnp.int32), n_tiles - 1)
    tile_expert = jnp.sum((tile[:, None] >= tiles_end[None, :]).astype(jnp.int32), axis=1)
    return src, pos.astype(jnp.int32), tile_expert.astype(jnp.int32), n_tiles.astype(jnp.int32)


def _token_tile_copy(src_hbm, row, dst, slot_row, sem):
    return pltpu.make_async_copy(
        src_hbm.at[pl.ds(pl.multiple_of(row * LANE_CHUNKS, LANE_CHUNKS), LANE_CHUNKS)],
        dst.at[pl.ds(pl.multiple_of(slot_row * LANE_CHUNKS, LANE_CHUNKS), LANE_CHUNKS)],
        sem)


def _moe_ffn_kernel(tm, src_ref, te_ref, nt_ref, x_hbm, wg_ref, wu_ref, wd_ref, y_ref,
                    xbuf, xd_ref, acc_ref, sem):
    t, j = pl.program_id(0), pl.program_id(1)
    last_j = pl.num_programs(1) - 1
    n_tiles = nt_ref[0]
    slot = t % 2
    rows = tm * LANE_CHUNKS

    def start_gather(tile, into):
        def body(r, carry):
            _token_tile_copy(x_hbm, src_ref[tile * tm + r], xbuf.at[into], r, sem.at[into]).start()
            return carry
        lax.fori_loop(0, tm, body, 0, unroll=8)

    @pl.when((t == 0) & (j == 0))
    def _():
        start_gather(0, 0)

    @pl.when((j == 0) & (t < n_tiles))
    def _():
        pltpu.make_async_copy(x_hbm.at[pl.ds(0, rows)], xbuf.at[slot], sem.at[slot]).wait()
        for c in range(LANE_CHUNKS):
            xd_ref[:, c * V7X_LANES:(c + 1) * V7X_LANES] = (
                xbuf[slot, pl.ds(c, tm, stride=LANE_CHUNKS), :].astype(BF16))

        @pl.when(t + 1 < n_tiles)
        def _():
            start_gather(t + 1, 1 - slot)

    @pl.when(t < n_tiles)
    def _():
        part = _swiglu_part(xd_ref[...], wg_ref, wu_ref, wd_ref)

        @pl.when(j == 0)
        def _():
            acc_ref[...] = part

        @pl.when(j > 0)
        def _():
            acc_ref[...] += part

        @pl.when(j == last_j)
        def _():
            _store_token_tiles(y_ref, acc_ref[...])

    @pl.when((t >= n_tiles) & (j == last_j))
    def _():
        y_ref[...] = jnp.zeros_like(y_ref)


def _moe_ffn(pool, src, tile_expert, n_tiles, w_gu, w_down, tm):
    n_e, d, two_f = w_gu.shape
    f = two_f // 2
    tf = MOE_FF_CHUNK
    nf = f // tf
    max_tiles = tile_expert.shape[0]
    rows = tm * LANE_CHUNKS
    grid_spec = pltpu.PrefetchScalarGridSpec(
        num_scalar_prefetch=3,
        grid=(max_tiles, nf),
        in_specs=[
            pl.BlockSpec(memory_space=pl.ANY),
            pl.BlockSpec((None, d, tf), lambda t, j, src, te, nt: (te[t], 0, j)),
            pl.BlockSpec((None, d, tf), lambda t, j, src, te, nt: (te[t], 0, nf + j)),
            pl.BlockSpec((None, tf, d), lambda t, j, src, te, nt: (te[t], j, 0)),
        ],
        out_specs=pl.BlockSpec((rows, V7X_LANES), lambda t, j, src, te, nt: (t, 0)),
        scratch_shapes=[pltpu.VMEM((2, rows, V7X_LANES), F32), pltpu.VMEM((tm, d), BF16),
                        pltpu.VMEM((tm, d), F32), pltpu.SemaphoreType.DMA((2,))],
    )
    return pl.pallas_call(
        functools.partial(_moe_ffn_kernel, tm),
        grid_spec=grid_spec,
        out_shape=jax.ShapeDtypeStruct((max_tiles * rows, V7X_LANES), F32),
        compiler_params=_params("arbitrary", "arbitrary"),
        name="moe_ffn",
    )(src, tile_expert, n_tiles, pool, w_gu, w_gu, w_down)


def _moe_combine_kernel(tm, token0, pos_ref, y_hbm, route_ref, x_ref, g2_ref, o_ref, ybuf, sem):
    i = pl.program_id(0)
    slot = i % 2
    rows = tm * LANE_CHUNKS

    def start_gather(tile, into):
        def body(r, carry):
            a = 2 * (token0 + tile * tm + r)
            for k in range(2):
                _token_tile_copy(y_hbm, pos_ref[a + k], ybuf.at[into], k * tm + r,
                                 sem.at[into]).start()
            return carry
        lax.fori_loop(0, tm, body, 0, unroll=4)

    @pl.when(i == 0)
    def _():
        start_gather(0, 0)

    pltpu.make_async_copy(y_hbm.at[pl.ds(0, 2 * rows)], ybuf.at[slot], sem.at[slot]).wait()

    @pl.when(i + 1 < pl.num_programs(0))
    def _():
        start_gather(i + 1, 1 - slot)

    route = route_ref[...]
    lane = lax.broadcasted_iota(jnp.int32, route.shape, 1)
    gate0 = jnp.sum(jnp.where(lane == 2, route, 0.0), axis=-1, keepdims=True)
    gate1 = jnp.sum(jnp.where(lane == 3, route, 0.0), axis=-1, keepdims=True)
    for c in range(LANE_CHUNKS):
        cols = slice(c * V7X_LANES, (c + 1) * V7X_LANES)
        y0 = ybuf[slot, pl.ds(c, tm, stride=LANE_CHUNKS), :]
        y1 = ybuf[slot, pl.ds(rows + c, tm, stride=LANE_CHUNKS), :]
        o_ref[:, cols] = x_ref[:, cols] + g2_ref[:, cols] * (gate0 * y0 + gate1 * y1)


def _moe_combine(st, y_slots, pos, route, x1, mod, layer):
    d = D_MODEL
    tm = st.tm
    rows = tm * LANE_CHUNKS
    grid_spec = pltpu.PrefetchScalarGridSpec(
        num_scalar_prefetch=1,
        grid=(st.tiles,),
        in_specs=[
            pl.BlockSpec(memory_space=pl.ANY),
            st.row_spec(V7X_LANES), st.row_spec(d), st.mod_spec(layer, 5),
        ],
        out_specs=st.row_spec(d),
        scratch_shapes=[pltpu.VMEM((2, 2 * rows, V7X_LANES), F32), pltpu.SemaphoreType.DMA((2,))],
    )
    return pl.pallas_call(
        functools.partial(_moe_combine_kernel, tm, st.row0),
        grid_spec=grid_spec,
        out_shape=jax.ShapeDtypeStruct((st.rows, d), F32),
        compiler_params=_params("arbitrary"),
        name="moe_combine",
    )(pos, y_slots, route, x1, mod)


def _head_rmsnorm(x, g2):
    lane = lax.broadcasted_iota(jnp.int32, x.shape, 1)
    lo = lane < HEAD_DIM
    sq = x * x
    s_lo = jnp.sum(jnp.where(lo, sq, 0.0), axis=-1, keepdims=True)
    s_hi = jnp.sum(jnp.where(lo, 0.0, sq), axis=-1, keepdims=True)
    ms = jnp.where(lo, s_lo, s_hi) * (1.0 / HEAD_DIM)
    return x * lax.rsqrt(ms + EPS) * g2


def _rope(x, cos, sin_signed):
    q4 = HEAD_DIM // 4
    lane = lax.broadcasted_iota(jnp.int32, x.shape, 1)
    first = (lane & q4) == 0
    width = x.shape[1]
    partner = jnp.where(first, pltpu.roll(x, width - q4, 1), pltpu.roll(x, q4, 1))
    return x * cos + partner * sin_signed


def _attention_kernel(past, use_rope, lam_init, *refs):
    refs = list(refs)
    n_seq = refs[0].shape[0]
    n_in = 3 + (2 if past else 0)
    seq_refs, refs = refs[:n_in], refs[n_in:]
    if use_rope:
        rope_refs, refs = refs[:4], refs[4:]
    else:
        rope_refs = []
    param_refs, refs = refs[:4], refs[4:]
    n_out = 1 if past else 2
    out_refs, (kall_ref, vall_ref) = refs[:n_out], refs[n_out:]
    for s in range(n_seq):
        _attention_one_sequence(past, use_rope, lam_init, *[r.at[s] for r in seq_refs],
                                *rope_refs, *param_refs, *[r.at[s] for r in out_refs],
                                kall_ref.at[s], vall_ref.at[s])


def _attention_one_sequence(past, use_rope, lam_init, *refs):
    refs = list(refs)
    q_ref, k_ref, v_ref = refs.pop(0), refs.pop(0), refs.pop(0)
    if past:
        ck_ref, cv_ref = refs.pop(0), refs.pop(0)
    if use_rope:
        cosq_ref, sinq_ref, cosk_ref, sink_ref = (refs.pop(0) for _ in range(4))
    qg_ref, kg_ref, lam_ref, sg_ref = (refs.pop(0) for _ in range(4))
    o_ref = refs.pop(0)
    nk_ref = None if past else refs.pop(0)
    kall_ref, vall_ref = refs

    qi = pl.program_id(2)

    @pl.when(qi == 0)
    def _():
        k = _head_rmsnorm(k_ref[...], kg_ref[...])
        if nk_ref is not None:
            nk_ref[...] = k
        if use_rope:
            k = _rope(k, cosk_ref[...], sink_ref[...])
        if past:
            kall_ref[0:past, :] = ck_ref[...].astype(BF16)
            vall_ref[0:past, :] = cv_ref[...].astype(BF16)
        kall_ref[past:, :] = k.astype(BF16)
        vall_ref[past:, :] = v_ref[...].astype(BF16)

    q = _head_rmsnorm(q_ref[...], qg_ref[...])
    if use_rope:
        q = _rope(q, cosq_ref[...], sinq_ref[...])
    q = q * (HEAD_DIM ** -0.5 * math.log2(math.e))
    tq = q.shape[0]
    lane = lax.broadcasted_iota(jnp.int32, q.shape, 1)
    lo = lane < HEAD_DIM
    nt = (((1,), (1,)), ((), ()))

    def attend(qm):
        s = lax.dot_general(qm.astype(BF16), kall_ref[...], nt, preferred_element_type=F32)
        p = jnp.exp2(s - jnp.max(s, axis=-1, keepdims=True))
        norm = 1.0 / jnp.sum(p, axis=-1, keepdims=True)
        return _dot(p.astype(BF16), vall_ref[...]) * norm

    row_chunks = [slice(r, r + ATTENTION_ROWS) for r in range(0, tq, ATTENTION_ROWS)]
    first = jnp.where(lo, q, 0.0)
    second = jnp.where(lo, 0.0, q)
    a0 = jnp.concatenate([attend(first[rows]) for rows in row_chunks], axis=0)
    a1 = jnp.concatenate([attend(second[rows]) for rows in row_chunks], axis=0)
    lv = lam_ref[...]
    lam = (jnp.exp(jnp.sum(lv[0:1] * lv[1:2], axis=-1, keepdims=True))
           - jnp.exp(jnp.sum(lv[2:3] * lv[3:4], axis=-1, keepdims=True)) + lam_init)
    o = a0 - lam * a1
    ms = jnp.mean(o * o, axis=-1, keepdims=True)
    o = o * lax.rsqrt(ms + EPS) * sg_ref[...] * (1.0 - lam_init)
    o_ref[...] = o.astype(o_ref.dtype)


def _attention(st, q, k, v, q_g, k_g, lam_vecs, subln_g, lam_init, cache_k=None, cache_v=None,
               rope=None):
    seq, hd = st.seq, N_HEADS * V_DIM
    tq = min(seq, ATTENTION_QUERIES)
    nq = seq // tq
    group = max(1, min(ATTENTION_GROUP_QUERIES // seq, st.batch))
    assert st.batch % group == 0
    past = 0 if cache_k is None else cache_k.shape[1]
    lk = past + seq
    q3, k3, v3 = (a.reshape(st.batch, seq, hd) for a in (q, k, v))
    qblk = pl.BlockSpec((group, tq, V_DIM), lambda b, h, i: (b, i, h))
    kblk = pl.BlockSpec((group, seq, V_DIM), lambda b, h, i: (b, 0, h))
    in_specs = [qblk, kblk, kblk]
    args = [q3, k3, v3]
    if past:
        cblk = pl.BlockSpec((group, past, V_DIM), lambda b, h, i: (b, 0, h))
        in_specs += [cblk, cblk]
        args += [cache_k.reshape(st.batch, past, hd), cache_v.reshape(st.batch, past, hd)]
    if rope is not None:
        cos2, sin2 = rope
        tq_tab = pl.BlockSpec((tq, V_DIM), lambda b, h, i: (i, 0))
        k_tab = pl.BlockSpec((seq, V_DIM), lambda b, h, i: (0, 0))
        in_specs += [tq_tab, tq_tab, k_tab, k_tab]
        args += [cos2, sin2, cos2, sin2]
    small = lambda shape: pl.BlockSpec(shape, lambda b, h, i: (0,) * len(shape))
    in_specs += [small((1, V_DIM)), small((1, V_DIM)), small((4, HEAD_DIM)), small((1, V_DIM))]
    args += [jnp.tile(q_g, 2).reshape(1, V_DIM), jnp.tile(k_g, 2).reshape(1, V_DIM), lam_vecs,
             subln_g.reshape(1, V_DIM)]
    out_specs = [qblk]
    out_shape = [jax.ShapeDtypeStruct((st.batch, seq, hd), BF16)]
    if not past:
        out_specs.append(kblk)
        out_shape.append(jax.ShapeDtypeStruct((st.batch, seq, hd), F32))
    outs = pl.pallas_call(
        functools.partial(_attention_kernel, past, rope is not None, lam_init),
        grid=(st.batch // group, N_HEADS, nq),
        in_specs=in_specs,
        out_specs=out_specs,
        out_shape=out_shape,
        scratch_shapes=[pltpu.VMEM((group, lk, V_DIM), BF16),
                        pltpu.VMEM((group, lk, V_DIM), BF16)],
        compiler_params=_params("parallel", "parallel", "arbitrary"),
        name="diff_attention",
    )(*args)
    return [o.reshape(st.rows, hd) for o in outs]


def _dft_matrices(seq):
    idx = np.arange(seq, dtype=np.int64)
    ang = (np.outer(idx, idx) % (2 * seq)).astype(np.float64) * (math.pi / seq)
    return jnp.asarray(np.cos(ang), dtype=BF16), jnp.asarray(np.sin(ang), dtype=BF16)


def _filter_features(seq):
    t = jnp.linspace(0.0, 1.0, seq, dtype=F32)[:, None]
    bands = (HY_EMB - 1) // 2
    w_ang = 2.0 * math.pi * jnp.arange(seq, dtype=F32)[:, None] / seq
    f = jnp.linspace(1e-4, bands - 1, bands, dtype=F32)[None, :]
    ang = f * w_ang
    feats = jnp.concatenate([t, jnp.cos(ang), -jnp.sin(ang)], axis=-1)
    return jnp.pad(feats, ((0, 0), (0, V7X_LANES - HY_EMB)))


def _decay_rates():
    min_decay = math.log(HY_DECAY_TARGET) / HY_SLOW_PCT
    max_decay = math.log(HY_DECAY_TARGET) / HY_FAST_PCT
    return jnp.linspace(min_decay, max_decay, D_MODEL, dtype=F32)[None, :]


def _rope_tables(seq):
    rows = seq // GRID_W
    row = jnp.repeat(jnp.arange(rows, dtype=F32), GRID_W)
    col = jnp.tile(jnp.arange(GRID_W, dtype=F32), rows)
    quarter = HEAD_DIM // 4
    inv = ROPE_BASE ** (-jnp.arange(quarter, dtype=F32) / quarter)

    def axis_angles(pos):
        a = pos[:, None] * inv[None, :]
        return jnp.concatenate([a, a], axis=-1)

    ang = jnp.concatenate([axis_angles(row), axis_angles(col)], axis=-1)
    sign = jnp.where((jnp.arange(HEAD_DIM) & quarter) == 0, -1.0, 1.0).astype(F32)
    return jnp.tile(jnp.cos(ang), (1, 2)), jnp.tile(jnp.sin(ang) * sign[None, :], (1, 2))


def kernel(x_prompt, x_sample, cache_k, cache_v, c, c_ctx, ada_w, ada_b, norm_g, hy_in_w, hy_in_b, hy_conv_w, hy_conv_b, hy_f_w1, hy_f_b1, hy_f_w2, hy_f_b2, hy_f_freq, hy_f_w3, hy_bias, hy_out_w, hy_out_b, at_qkv_w, at_q_g, at_k_g, at_lam, at_subln_g, at_out_w, dn_w_gu, dn_w_down, mo_router_w, mo_router_b, mo_w_gu, mo_w_down):
    d = D_MODEL
    batch, seq = x_prompt.shape[:2]
    dec_batch, dec_seq = x_sample.shape[:2]
    past = cache_k.shape[2]
    streams = [
        (_Stream(batch, seq, 0, False), x_prompt.reshape(batch * seq, d), None),
        (_Stream(dec_batch, dec_seq, 1, True, row0=batch * seq),
         x_sample.reshape(dec_batch * dec_seq, d), (cache_k[:, 0], cache_v[:, 0])),
    ]
    tokens = batch * seq + dec_batch * dec_seq

    cond = jnp.concatenate(
        [c_ctx[None, :], c, jnp.zeros((COND_ROWS - 1 - dec_batch, d), F32)], axis=0)
    mod = _adaln(cond, ada_w, ada_b)
    mod = mod.reshape(mod.shape[0], COND_ROWS, 1, 6 * d)

    in_w, out_w = hy_in_w[0].astype(BF16), hy_out_w[0].astype(BF16)
    qkv_w, at_out = at_qkv_w[0].astype(BF16), at_out_w[0].astype(BF16)
    dn_gu, dn_down = dn_w_gu[0].astype(BF16), dn_w_down[0].astype(BF16)
    mo_gu, mo_down = mo_w_gu[0].astype(BF16), mo_w_down[0].astype(BF16)
    w1_pad = jnp.pad(hy_f_w1[0], ((0, V7X_LANES - HY_EMB), (0, 0)))
    rw_pad = jnp.pad(mo_router_w[0], ((0, 0), (0, V7X_LANES - N_EXPERTS)))
    rb_pad = jnp.pad(mo_router_b[0], (0, V7X_LANES - N_EXPERTS)).reshape(1, V7X_LANES)
    deltas = _decay_rates()
    lam_init = 0.8 - 0.6 * math.exp(-0.3 * 1)

    routed = []
    pool = jnp.zeros((tokens * LANE_CHUNKS, V7X_LANES), F32)
    for st, x, cache in streams:
        blk = min(st.seq, HYENA_BLOCK)
        cmat, smat = _dft_matrices(blk)
        hf, hb = _filter_time(st.seq, _filter_features(st.seq), w1_pad, hy_f_b1[0], hy_f_w2[0],
                              hy_f_b2[0], hy_f_freq[0], hy_f_w3[0], deltas)
        kr, ki, kn = _filter_spectrum(st.seq, blk, hf, hb, cmat, smat)
        (proj,) = _mod_matmul(st, x, norm_g[0, 0], mod, 0, in_w, hy_in_b[0], 1, BF16)
        z = _hyena_core(st, blk, proj, hy_conv_w[0], hy_conv_b[0].reshape(1, 3 * d), kr, ki, kn,
                        hy_bias[0], cmat, smat)
        x, h2 = _out_proj(st, z, out_w, hy_out_b[0], x, mod, 0, norm_g[0, 1])
        x = _swiglu(st, h2, dn_gu, dn_down, x, mod, 0)

        q, k, v = _mod_matmul(st, x, norm_g[1, 0], mod, 1, qkv_w, None, 3, F32)
        if cache is None:
            o, new_k = _attention(st, q, k, v, at_q_g[0], at_k_g[0], at_lam[0], at_subln_g[0],
                                  lam_init)
            new_kv = (new_k, v)
        else:
            (o,) = _attention(st, q, k, v, at_q_g[0], at_k_g[0], at_lam[0], at_subln_g[0],
                              lam_init, cache[0], cache[1], _rope_tables(st.seq))
        x, pool, route = _out_proj(st, o, at_out, None, x, mod, 1, norm_g[1, 1],
                                   (rw_pad, rb_pad, pool))
        routed.append((st, x, route))

    src, pos, tile_expert, n_tiles = _moe_plan(
        jnp.concatenate([route[:, :4] for _, _, route in routed], axis=0), MOE_TILE)
    y_slots = _moe_ffn(pool, src, tile_expert, n_tiles, mo_gu, mo_down, MOE_TILE)
    results = [_moe_combine(st, y_slots, pos, route, x, mod, 1).reshape(st.batch, st.seq, d)
               for st, x, route in routed]

    new_k, new_v = new_kv
    return (results[0], results[1],
            new_k.reshape(batch, 1, seq, N_HEADS, 2, HEAD_DIM),
            new_v.reshape(batch, 1, seq, N_HEADS, V_DIM))
```

```python
import functools
import math

import numpy as np
import jax
import jax.numpy as jnp
from jax import lax
from jax.experimental import pallas as pl
from jax.experimental.pallas import tpu as pltpu

F32 = jnp.float32
BF16 = jnp.bfloat16
HIGHEST = lax.Precision.HIGHEST

D_MODEL = 1024
GRID_W = 64
HY_ORDER = 2
HY_EMB = 33
HY_FW = 64
HY_DECAY_TARGET = 1e-2
HY_FAST_PCT = 0.3
HY_SLOW_PCT = 1.5
N_HEADS = 8
HEAD_DIM = 64
V_DIM = 2 * HEAD_DIM
ROPE_BASE = 10000.0
D_FF = 2816
N_EXPERTS = 8
D_FF_EXPERT = 3584
EPS = 1e-6

V7X_LANES = 128
V7X_VMEM_LIMIT_BYTES = 56 * 1024 * 1024
LANE_CHUNKS = D_MODEL // V7X_LANES
COND_ROWS = 16
TOKEN_TILE = 1024
WIDE_OUT_TILE = 512
MOE_TILE = 1024
MOE_FF_CHUNK = 512
DENSE_FF_CHUNK = D_FF // 2
HYENA_BLOCK = 512
NYQUIST_ROWS = 8
ATTENTION_QUERIES = 1024
ATTENTION_GROUP_QUERIES = 1024
ATTENTION_ROWS = 256


def _params(*semantics):
    return pltpu.CompilerParams(dimension_semantics=semantics,
                                vmem_limit_bytes=V7X_VMEM_LIMIT_BYTES)


def _resident(shape):
    zeros = (0,) * len(shape)
    return pl.BlockSpec(shape, lambda *_: zeros, pipeline_mode=pl.Buffered(1))


def _dot(a, b):
    return jnp.dot(a, b, preferred_element_type=F32)


def _dot_f32(a, b):
    return jnp.dot(a, b, precision=HIGHEST, preferred_element_type=F32)


def _modulate(x, g, shift, scale):
    ms = jnp.mean(x * x, axis=-1, keepdims=True)
    return (x * lax.rsqrt(ms + EPS) * g) * (1.0 + scale) + shift


def _adaln_kernel(cond_ref, w_ref, b_ref, o_ref):
    c = cond_ref[...]
    o_ref[...] = _dot_f32(c * jax.nn.sigmoid(c), w_ref[...]) + b_ref[...]


def _adaln(cond, ada_w, ada_b):
    depth, d, n = ada_w.shape
    tn = 1536
    return pl.pallas_call(
        _adaln_kernel,
        grid=(depth, n // tn),
        in_specs=[
            pl.BlockSpec((COND_ROWS, d), lambda i, j: (0, 0)),
            pl.BlockSpec((None, d, tn), lambda i, j: (i, 0, j)),
            pl.BlockSpec((None, 1, tn), lambda i, j: (i, 0, j)),
        ],
        out_specs=pl.BlockSpec((None, COND_ROWS, tn), lambda i, j: (i, 0, j)),
        out_shape=jax.ShapeDtypeStruct((depth, COND_ROWS, n), F32),
        compiler_params=_params("parallel", "parallel"),
        name="adaln",
    )(cond, ada_w, ada_b.reshape(depth, 1, n))


class _Stream:
    def __init__(self, batch, seq, cond_row0, per_seq_cond, row0=0, tile=TOKEN_TILE):
        self.batch, self.seq = batch, seq
        self.rows = batch * seq
        self.row0 = row0
        self._cond = (cond_row0, per_seq_cond)
        if per_seq_cond:
            self.tm = min(tile, seq)
            tiles_per_seq = seq // self.tm
            self.cond_row = lambda i: cond_row0 + i // tiles_per_seq
        else:
            self.tm = min(tile, self.rows)
            self.cond_row = lambda i: cond_row0
        self.tiles = self.rows // self.tm

    def retiled(self, tile):
        return _Stream(self.batch, self.seq, *self._cond, row0=self.row0, tile=tile)

    def mod_spec(self, layer, chunk):
        return pl.BlockSpec((None, None, 1, D_MODEL),
                            lambda i, *_: (layer, self.cond_row(i), 0, chunk))

    def row_spec(self, width):
        return pl.BlockSpec((self.tm, width), lambda i, *_: (i, 0))


def _mod_matmul_kernel(n_out, has_bias, x_ref, g_ref, sh_ref, sc_ref, w_ref, *rest):
    if has_bias:
        b_ref, out_refs = rest[0], rest[1:]
    else:
        b_ref, out_refs = None, rest
    h = _modulate(x_ref[...], g_ref[...], sh_ref[...], sc_ref[...])
    y = _dot(h.astype(BF16), w_ref[...])
    if has_bias:
        y = y + b_ref[...]
    width = y.shape[1] // n_out
    for k, o_ref in enumerate(out_refs):
        o_ref[...] = y[:, k * width:(k + 1) * width].astype(o_ref.dtype)


def _mod_matmul(st, x, norm_g, mod, layer, w_bf16, bias, n_out, out_dtype):
    st = st.retiled(WIDE_OUT_TILE)
    d, n = w_bf16.shape
    in_specs = [st.row_spec(d), _resident((1, d)), st.mod_spec(layer, 0), st.mod_spec(layer, 1),
                _resident((d, n))]
    args = [x, norm_g.reshape(1, d), mod, mod, w_bf16]
    if bias is not None:
        in_specs.append(_resident((1, n)))
        args.append(bias.reshape(1, n))
    width = n // n_out
    outs = pl.pallas_call(
        functools.partial(_mod_matmul_kernel, n_out, bias is not None),
        grid=(st.tiles,),
        in_specs=in_specs,
        out_specs=[st.row_spec(width)] * n_out,
        out_shape=[jax.ShapeDtypeStruct((st.rows, width), out_dtype)] * n_out,
        compiler_params=_params("parallel"),
        name="mod_matmul",
    )(*args)
    return outs


def _filter_time_kernel(feats_ref, w1_ref, b1_ref, w2_ref, b2_ref, fr_ref, w3_ref, dl_ref,
                        hf_ref, hb_ref):
    feats = feats_ref[...]
    fr = fr_ref[...]
    h = jnp.sin(fr[0:1] * (_dot_f32(feats, w1_ref[...]) + b1_ref[...]))
    h = jnp.sin(fr[1:2] * (_dot_f32(h, w2_ref[...]) + b2_ref[...]))
    h = _dot_f32(h, w3_ref[...])
    t = feats[:, 0:1]
    decay = jnp.exp(-t * jnp.abs(dl_ref[...]))
    half = HY_ORDER * D_MODEL
    decay2 = jnp.concatenate([decay] * HY_ORDER, axis=1)
    hf_ref[...] = h[:, :half] * decay2
    hb_ref[...] = jnp.where(t == 0.0, 0.0, h[:, half:] * decay2)


def _filter_time(seq, feats_pad, w1_pad, b1, w2, b2, freq, w3, deltas):
    tl = min(seq, 256)
    half = HY_ORDER * D_MODEL
    out = jax.ShapeDtypeStruct((seq, half), F32)
    return pl.pallas_call(
        _filter_time_kernel,
        grid=(seq // tl,),
        in_specs=[
            pl.BlockSpec((tl, V7X_LANES), lambda i: (i, 0)),
            _resident(w1_pad.shape), _resident((1, HY_FW)), _resident((HY_FW, HY_FW)),
            _resident((1, HY_FW)), _resident((2, HY_FW)), _resident(w3.shape),
            _resident((1, D_MODEL)),
        ],
        out_specs=[pl.BlockSpec((tl, half), lambda i: (i, 0))] * 2,
        out_shape=[out, out],
        compiler_params=_params("parallel"),
        name="hyena_filter_time",
    )(feats_pad, w1_pad, b1.reshape(1, HY_FW), w2, b2.reshape(1, HY_FW), freq, w3, deltas)


def _filter_spectrum_kernel(blk, nb, hf_ref, hb_ref, c_ref, s_ref, kr_ref, ki_ref, kn_ref):
    row = lax.broadcasted_iota(jnp.int32, (blk, 1), 0)
    sg = (1 - 2 * (row & 1)).astype(F32)
    wgt = jnp.where(row == 0, 1.0, 2.0) * (1.0 / (2 * blk))
    fwd, bwd = [], []
    for j in range(nb):
        rows = slice(j * blk, (j + 1) * blk)
        for ref, out in ((hf_ref, fwd), (hb_ref, bwd)):
            x = ref[rows, :]
            xb = x.astype(BF16)
            out.append((_dot(c_ref[...], xb), _dot(s_ref[...], xb),
                        jnp.sum(x * sg, axis=0, keepdims=True), x[0:1, :],
                        xb[0:1, :].astype(F32)))
    kn_ref[...] = jnp.zeros_like(kn_ref)
    for d in range(-(nb - 1), nb):
        if d == 0:
            (fc, fs, fn, _, _), (bc, bs, bn, _, _) = fwd[0], bwd[0]
            kr, ki, kn = fc + bc, bs - fs, fn + bn
        else:
            parts, im_sign = (fwd, -1.0) if d > 0 else (bwd, 1.0)
            c1, s1, n1, _, _ = parts[abs(d)]
            c0, s0, n0, x0, x0_seen = parts[abs(d) - 1]
            kr, ki, kn = c1 + sg * (c0 - x0_seen), im_sign * (s1 + sg * s0), n1 + n0 - x0
        slot = d + nb - 1
        kr_ref[slot * blk:(slot + 1) * blk, :] = wgt * kr
        ki_ref[slot * blk:(slot + 1) * blk, :] = wgt * ki
        kn_ref[slot:slot + 1, :] = kn * (1.0 / (2 * blk))


def _filter_spectrum(seq, blk, hf, hb, cmat, smat):
    nb = seq // blk
    half = hf.shape[1]
    tn = 256
    col = pl.BlockSpec((seq, tn), lambda j: (0, j))
    spec_rows = (2 * nb - 1) * blk
    out_col = pl.BlockSpec((spec_rows, tn), lambda j: (0, j))
    return pl.pallas_call(
        functools.partial(_filter_spectrum_kernel, blk, nb),
        grid=(half // tn,),
        in_specs=[col, col, _resident((blk, blk)), _resident((blk, blk))],
        out_specs=[out_col, out_col, pl.BlockSpec((NYQUIST_ROWS, tn), lambda j: (0, j))],
        out_shape=[jax.ShapeDtypeStruct((spec_rows, half), F32)] * 2
        + [jax.ShapeDtypeStruct((NYQUIST_ROWS, half), F32)],
        compiler_params=_params("parallel"),
        name="hyena_filter_spectrum",
    )(hf, hb, cmat, smat)


def _hyena_core_kernel(seq, blk, pv_ref, p1_ref, p2_ref, cwv_ref, cw1_ref, cw2_ref, cbv_ref,
                       cb1_ref, cb2_ref, kr0_ref, ki0_ref, kn0_ref, kr1_ref, ki1_ref, kn1_ref,
                       bias0_ref, bias1_ref, c_ref, s_ref, z_ref,
                       u_ref, ub_ref, gate_ref, a_ref, b_ref):
    row = lax.broadcasted_iota(jnp.int32, (seq, 1), 0)
    sign = (1 - 2 * (row[:blk] & 1)).astype(F32)
    nb = seq // blk
    blocks = [slice(j * blk, (j + 1) * blk) for j in range(nb)]

    def short_conv(x_ref, w_ref, b_ref):
        x = x_ref[...].astype(F32)
        w = w_ref[...]
        prev = jnp.where(row == 0, 0.0, pltpu.roll(x, 1, 0))
        nxt = jnp.where(row == seq - 1, 0.0, pltpu.roll(x, seq - 1, 0))
        return prev * w[0:1] + x * w[1:2] + nxt * w[2:3] + b_ref[...]

    def gated_long_conv(kr_ref, ki_ref, kn_ref, bias_ref, write):
        ub_ref[...] = u_ref[...].astype(BF16)
        nyq_in = []
        for rows in blocks:
            a_ref[rows, :] = _dot(c_ref[...], ub_ref[rows, :])
            b_ref[rows, :] = _dot(s_ref[...], ub_ref[rows, :])
            nyq_in.append(jnp.sum(u_ref[rows, :] * sign, axis=0, keepdims=True))
        for i, rows in enumerate(blocks):
            p = q = nyq = None
            for j, src in enumerate(blocks):
                slot = i - j + nb - 1
                kr = kr_ref[slot * blk:(slot + 1) * blk, :]
                ki = ki_ref[slot * blk:(slot + 1) * blk, :]
                a, b = a_ref[src, :], b_ref[src, :]
                pj, qj = a * kr + b * ki, b * kr - a * ki
                nj = nyq_in[j] * kn_ref[slot:slot + 1, :]
                p, q, nyq = (pj, qj, nj) if j == 0 else (p + pj, q + qj, nyq + nj)
            y = _dot(c_ref[...], p.astype(BF16)) + _dot(s_ref[...], q.astype(BF16))
            y = y + sign * nyq + u_ref[rows, :] * bias_ref[...]
            write(rows, gate_ref[rows, :] * y)

    def to_u(rows, val):
        u_ref[rows, :] = val

    def to_z(rows, val):
        z_ref[rows, :] = val.astype(z_ref.dtype)

    u_ref[...] = short_conv(pv_ref, cwv_ref, cbv_ref)
    gate_ref[...] = short_conv(p1_ref, cw1_ref, cb1_ref)
    gated_long_conv(kr0_ref, ki0_ref, kn0_ref, bias0_ref, to_u)
    gate_ref[...] = short_conv(p2_ref, cw2_ref, cb2_ref)
    gated_long_conv(kr1_ref, ki1_ref, kn1_ref, bias1_ref, to_z)


def _hyena_core(st, blk, proj, conv_w, conv_b, kr, ki, kn, bias, cmat, smat):
    seq, d = st.seq, D_MODEL
    tn = 256
    nj = d // tn
    spec_rows = kr.shape[0]
    proj3 = proj.reshape(st.batch, seq, 3 * d)

    def part(k):
        return pl.BlockSpec((None, seq, tn), lambda j, b: (b, 0, k * nj + j))

    def cols(rows, k, buffers=2):
        return pl.BlockSpec((rows, tn), lambda j, b: (0, k * nj + j),
                            pipeline_mode=pl.Buffered(buffers))

    in_specs = ([part(0), part(1), part(2)]
                + [cols(3, k) for k in range(3)] + [cols(1, k) for k in range(3)]
                + [cols(spec_rows, 0, 1), cols(spec_rows, 0, 1), cols(NYQUIST_ROWS, 0),
                   cols(spec_rows, 1, 1), cols(spec_rows, 1, 1), cols(NYQUIST_ROWS, 1)]
                + [cols(1, 0), cols(1, 0)]
                + [_resident((blk, blk)), _resident((blk, blk))])
    z = pl.pallas_call(
        functools.partial(_hyena_core_kernel, seq, blk),
        grid=(nj, st.batch),
        in_specs=in_specs,
        out_specs=pl.BlockSpec((None, seq, tn), lambda j, b: (b, 0, j)),
        out_shape=jax.ShapeDtypeStruct((st.batch, seq, d), BF16),
        scratch_shapes=[pltpu.VMEM((seq, tn), F32), pltpu.VMEM((seq, tn), BF16),
                        pltpu.VMEM((seq, tn), F32), pltpu.VMEM((seq, tn), F32),
                        pltpu.VMEM((seq, tn), F32)],
        compiler_params=_params("parallel", "parallel"),
        name="hyena_core",
    )(proj3, proj3, proj3, conv_w, conv_w, conv_w, conv_b, conv_b, conv_b,
      kr, ki, kn, kr, ki, kn, bias[0:1], bias[1:2], cmat, smat)
    return z.reshape(st.rows, d)


def _out_proj_kernel(has_bias, has_router, *refs):
    refs = list(refs)
    z_ref, w_ref = refs.pop(0), refs.pop(0)
    b_ref = refs.pop(0) if has_bias else None
    x_ref, g1_ref, ng_ref, sh_ref, sc_ref = (refs.pop(0) for _ in range(5))
    if has_router:
        rw_hi_ref, rw_lo_ref, rb_ref = refs.pop(0), refs.pop(0), refs.pop(0)
        refs.pop(0)
    x1_ref, h2_ref = refs.pop(0), refs.pop(0)
    m = _dot(z_ref[...], w_ref[...])
    if has_bias:
        m = m + b_ref[...]
    x1 = x_ref[...] + g1_ref[...] * m
    x1_ref[...] = x1
    h2 = _modulate(x1, ng_ref[...], sh_ref[...], sc_ref[...])
    if not has_router:
        h2_ref[...] = h2.astype(BF16)
        return
    _store_token_tiles(h2_ref, h2)
    route_ref = refs.pop(0)
    h_hi = h2.astype(BF16)
    h_lo = (h2 - h_hi.astype(F32)).astype(BF16)
    logits = (_dot(h_hi, rw_hi_ref[...])
              + (_dot(h_lo, rw_hi_ref[...]) + _dot(h_hi, rw_lo_ref[...])) + rb_ref[...])
    lane = lax.broadcasted_iota(jnp.int32, logits.shape, 1)
    neg = -jnp.inf
    logits = jnp.where(lane < N_EXPERTS, logits, neg)
    m1 = jnp.max(logits, axis=-1, keepdims=True)
    i1 = jnp.min(jnp.where(logits == m1, lane, V7X_LANES), axis=-1, keepdims=True)
    rest = jnp.where(lane == i1, neg, logits)
    m2 = jnp.max(rest, axis=-1, keepdims=True)
    i2 = jnp.min(jnp.where(rest == m2, lane, V7X_LANES), axis=-1, keepdims=True)
    e2 = jnp.exp(m2 - m1)
    den = 1.0 + e2
    route = jnp.where(lane == 0, i1.astype(F32), jnp.where(lane == 1, i2.astype(F32), 0.0))
    route_ref[...] = route + jnp.where(lane == 2, 1.0 / den, 0.0) + jnp.where(lane == 3, e2 / den, 0.0)


def _store_token_tiles(ref, val):
    rows = val.shape[0]
    for c in range(LANE_CHUNKS):
        ref[pl.ds(c, rows, stride=LANE_CHUNKS), :] = val[:, c * V7X_LANES:(c + 1) * V7X_LANES]


def _out_proj(st, z_bf16, w_bf16, bias, x, mod, layer, norm_g, router=None):
    d = D_MODEL
    in_specs = [st.row_spec(z_bf16.shape[1]), _resident(w_bf16.shape)]
    args = [z_bf16, w_bf16]
    if bias is not None:
        in_specs.append(_resident((1, d)))
        args.append(bias.reshape(1, d))
    in_specs += [st.row_spec(d), st.mod_spec(layer, 2), _resident((1, d)),
                 st.mod_spec(layer, 3), st.mod_spec(layer, 4)]
    args += [x, mod, norm_g.reshape(1, d), mod, mod]
    aliases = {}
    if router is None:
        out_specs = [st.row_spec(d), st.row_spec(d)]
        out_shape = [jax.ShapeDtypeStruct((st.rows, d), F32),
                     jax.ShapeDtypeStruct((st.rows, d), BF16)]
    else:
        rw_pad, rb_pad, pool = router
        rw_hi = rw_pad.astype(BF16)
        rw_lo = (rw_pad - rw_hi.astype(F32)).astype(BF16)
        in_specs += [_resident(rw_pad.shape), _resident(rw_pad.shape), _resident(rb_pad.shape),
                     pl.BlockSpec(memory_space=pl.ANY)]
        args += [rw_hi, rw_lo, rb_pad, pool]
        aliases = {len(args) - 1: 1}
        tile0 = st.row0 // st.tm
        out_specs = [st.row_spec(d),
                     pl.BlockSpec((st.tm * LANE_CHUNKS, V7X_LANES), lambda i: (tile0 + i, 0)),
                     st.row_spec(V7X_LANES)]
        out_shape = [jax.ShapeDtypeStruct((st.rows, d), F32),
                     jax.ShapeDtypeStruct(pool.shape, F32),
                     jax.ShapeDtypeStruct((st.rows, V7X_LANES), F32)]
    return pl.pallas_call(
        functools.partial(_out_proj_kernel, bias is not None, router is not None),
        grid=(st.tiles,),
        in_specs=in_specs,
        out_specs=out_specs,
        out_shape=out_shape,
        input_output_aliases=aliases,
        compiler_params=_params("parallel"),
        name="out_proj",
    )(*args)


def _swiglu_part(x_bf16, wg_ref, wu_ref, wd_ref):
    g = _dot(x_bf16, wg_ref[...].astype(BF16))
    u = _dot(x_bf16, wu_ref[...].astype(BF16))
    return _dot((g * jax.nn.sigmoid(g) * u).astype(BF16), wd_ref[...].astype(BF16))


def _swiglu_kernel(h_ref, wg_ref, wu_ref, wd_ref, x_ref, g2_ref, o_ref, acc_ref):
    j = pl.program_id(1)
    part = _swiglu_part(h_ref[...], wg_ref, wu_ref, wd_ref)

    @pl.when(j == 0)
    def _():
        acc_ref[...] = part

    @pl.when(j > 0)
    def _():
        acc_ref[...] += part

    @pl.when(j == pl.num_programs(1) - 1)
    def _():
        o_ref[...] = x_ref[...] + g2_ref[...] * acc_ref[...]


def _swiglu(st, h2, w_gu, w_down, x1, mod, layer):
    st = st.retiled(WIDE_OUT_TILE)
    d, two_f = w_gu.shape
    f = two_f // 2
    tf = DENSE_FF_CHUNK
    nf = f // tf
    return pl.pallas_call(
        _swiglu_kernel,
        grid=(st.tiles, nf),
        in_specs=[
            pl.BlockSpec((st.tm, d), lambda i, j: (i, 0)),
            pl.BlockSpec((d, tf), lambda i, j: (0, j)),
            pl.BlockSpec((d, tf), lambda i, j: (0, nf + j)),
            pl.BlockSpec((tf, d), lambda i, j: (j, 0)),
            pl.BlockSpec((st.tm, d), lambda i, j: (i, 0)),
            st.mod_spec(layer, 5),
        ],
        out_specs=pl.BlockSpec((st.tm, d), lambda i, j: (i, 0)),
        out_shape=jax.ShapeDtypeStruct((st.rows, d), F32),
        scratch_shapes=[pltpu.VMEM((st.tm, d), F32)],
        compiler_params=_params("parallel", "arbitrary"),
        name="swiglu",
    )(h2, w_gu, w_gu, w_down, x1, mod)


def _moe_plan(route, tm):
    tokens = route.shape[0]
    max_tiles = (2 * tokens) // tm + N_EXPERTS
    expert = route[:, :2].astype(jnp.int32).reshape(-1)
    experts = jnp.arange(N_EXPERTS, dtype=jnp.int32)
    onehot = (expert[:, None] == experts[None, :]).astype(jnp.int32)
    csum = jnp.cumsum(onehot, axis=0)
    rank = jnp.sum(csum * onehot, axis=1) - 1
    counts = csum[-1]
    tiles_per_expert = (counts + tm - 1) // tm
    tiles_end = jnp.cumsum(tiles_per_expert)
    first_tile = tiles_end - tiles_per_expert
    first_sorted = jnp.cumsum(counts) - counts
    pos = jnp.sum((first_tile * tm)[None, :] * onehot, axis=1) + rank
    token = jnp.arange(2 * tokens, dtype=jnp.int32) // 2
    _, order = lax.sort_key_val(pos, token)
    n_tiles = tiles_end[-1:]
    tile = jnp.minimum(jnp.arange(max_tiles, dtype=jnp.int32), n_tiles - 1)
    tile_expert = jnp.sum((tile[:, None] >= tiles_end[None, :]).astype(jnp.int32), axis=1)
    mine = (tile_expert[:, None] == experts[None, :]).astype(jnp.int32)
    done = (tile - jnp.sum(first_tile[None, :] * mine, axis=1)) * tm
    tile_base = jnp.sum(first_sorted[None, :] * mine, axis=1) + done
    tile_valid = jnp.clip(jnp.sum(counts[None, :] * mine, axis=1) - done, 1, tm)
    as_i32 = lambda a: a.astype(jnp.int32)
    return (as_i32(order), as_i32(pos), as_i32(tile_expert), as_i32(tile_base),
            as_i32(tile_valid), as_i32(n_tiles))


def _token_tile_copy(src_hbm, row, dst, slot_row, sem):
    return pltpu.make_async_copy(
        src_hbm.at[pl.ds(pl.multiple_of(row * LANE_CHUNKS, LANE_CHUNKS), LANE_CHUNKS)],
        dst.at[pl.ds(pl.multiple_of(slot_row * LANE_CHUNKS, LANE_CHUNKS), LANE_CHUNKS)],
        sem)


def _moe_ffn_kernel(tm, order_ref, te_ref, base_ref, valid_ref, nt_ref, x_hbm, wg_ref, wu_ref,
                    wd_ref, y_ref, xbuf, xd_ref, acc_ref, sem):
    t, j = pl.program_id(0), pl.program_id(1)
    last_j = pl.num_programs(1) - 1
    n_tiles = nt_ref[0]
    slot = t % 2
    rows = tm * LANE_CHUNKS

    def start_gather(tile, into):
        base, last = base_ref[tile], valid_ref[tile] - 1

        def body(r, carry):
            token = order_ref[base + jnp.minimum(r, last)]
            _token_tile_copy(x_hbm, token, xbuf.at[into], r, sem.at[into]).start()
            return carry
        lax.fori_loop(0, tm, body, 0, unroll=8)

    @pl.when((t == 0) & (j == 0))
    def _():
        start_gather(0, 0)

    @pl.when((j == 0) & (t < n_tiles))
    def _():
        pltpu.make_async_copy(x_hbm.at[pl.ds(0, rows)], xbuf.at[slot], sem.at[slot]).wait()
        for c in range(LANE_CHUNKS):
            xd_ref[:, c * V7X_LANES:(c + 1) * V7X_LANES] = (
                xbuf[slot, pl.ds(c, tm, stride=LANE_CHUNKS), :].astype(BF16))

        @pl.when(t + 1 < n_tiles)
        def _():
            start_gather(t + 1, 1 - slot)

    @pl.when(t < n_tiles)
    def _():
        part = _swiglu_part(xd_ref[...], wg_ref, wu_ref, wd_ref)

        @pl.when(j == 0)
        def _():
            acc_ref[...] = part

        @pl.when(j > 0)
        def _():
            acc_ref[...] += part

        @pl.when(j == last_j)
        def _():
            _store_token_tiles(y_ref, acc_ref[...])

    @pl.when((t >= n_tiles) & (j == last_j))
    def _():
        y_ref[...] = jnp.zeros_like(y_ref)


def _moe_ffn(pool, order, tile_expert, tile_base, tile_valid, n_tiles, w_gu, w_down, tm):
    n_e, d, two_f = w_gu.shape
    f = two_f // 2
    tf = MOE_FF_CHUNK
    nf = f // tf
    max_tiles = tile_expert.shape[0]
    rows = tm * LANE_CHUNKS
    grid_spec = pltpu.PrefetchScalarGridSpec(
        num_scalar_prefetch=5,
        grid=(max_tiles, nf),
        in_specs=[
            pl.BlockSpec(memory_space=pl.ANY),
            pl.BlockSpec((None, d, tf), lambda t, j, order, te, *_: (te[t], 0, j)),
            pl.BlockSpec((None, d, tf), lambda t, j, order, te, *_: (te[t], 0, nf + j)),
            pl.BlockSpec((None, tf, d), lambda t, j, order, te, *_: (te[t], j, 0)),
        ],
        out_specs=pl.BlockSpec((rows, V7X_LANES), lambda t, j, *_: (t, 0)),
        scratch_shapes=[pltpu.VMEM((2, rows, V7X_LANES), F32), pltpu.VMEM((tm, d), BF16),
                        pltpu.VMEM((tm, d), F32), pltpu.SemaphoreType.DMA((2,))],
    )
    return pl.pallas_call(
        functools.partial(_moe_ffn_kernel, tm),
        grid_spec=grid_spec,
        out_shape=jax.ShapeDtypeStruct((max_tiles * rows, V7X_LANES), F32),
        compiler_params=_params("arbitrary", "arbitrary"),
        name="moe_ffn",
    )(order, tile_expert, tile_base, tile_valid, n_tiles, pool, w_gu, w_gu, w_down)


def _moe_combine_kernel(tm, token0, pos_ref, y_hbm, route_ref, x_ref, g2_ref, o_ref, ybuf, sem):
    i = pl.program_id(0)
    slot = i % 2
    rows = tm * LANE_CHUNKS

    def start_gather(tile, into):
        def body(r, carry):
            a = 2 * (token0 + tile * tm + r)
            for k in range(2):
                _token_tile_copy(y_hbm, pos_ref[a + k], ybuf.at[into], k * tm + r,
                                 sem.at[into]).start()
            return carry
        lax.fori_loop(0, tm, body, 0, unroll=4)

    @pl.when(i == 0)
    def _():
        start_gather(0, 0)

    pltpu.make_async_copy(y_hbm.at[pl.ds(0, 2 * rows)], ybuf.at[slot], sem.at[slot]).wait()

    @pl.when(i + 1 < pl.num_programs(0))
    def _():
        start_gather(i + 1, 1 - slot)

    route = route_ref[...]
    lane = lax.broadcasted_iota(jnp.int32, route.shape, 1)
    gate0 = jnp.sum(jnp.where(lane == 2, route, 0.0), axis=-1, keepdims=True)
    gate1 = jnp.sum(jnp.where(lane == 3, route, 0.0), axis=-1, keepdims=True)
    for c in range(LANE_CHUNKS):
        cols = slice(c * V7X_LANES, (c + 1) * V7X_LANES)
        y0 = ybuf[slot, pl.ds(c, tm, stride=LANE_CHUNKS), :]
        y1 = ybuf[slot, pl.ds(rows + c, tm, stride=LANE_CHUNKS), :]
        o_ref[:, cols] = x_ref[:, cols] + g2_ref[:, cols] * (gate0 * y0 + gate1 * y1)


def _moe_combine(st, y_slots, pos, route, x1, mod, layer):
    d = D_MODEL
    tm = st.tm
    rows = tm * LANE_CHUNKS
    grid_spec = pltpu.PrefetchScalarGridSpec(
        num_scalar_prefetch=1,
        grid=(st.tiles,),
        in_specs=[
            pl.BlockSpec(memory_space=pl.ANY),
            st.row_spec(V7X_LANES), st.row_spec(d), st.mod_spec(layer, 5),
        ],
        out_specs=st.row_spec(d),
        scratch_shapes=[pltpu.VMEM((2, 2 * rows, V7X_LANES), F32), pltpu.SemaphoreType.DMA((2,))],
    )
    return pl.pallas_call(
        functools.partial(_moe_combine_kernel, tm, st.row0),
        grid_spec=grid_spec,
        out_shape=jax.ShapeDtypeStruct((st.rows, d), F32),
        compiler_params=_params("arbitrary"),
        name="moe_combine",
    )(pos, y_slots, route, x1, mod)


def _head_rmsnorm(x, g2):
    lane = lax.broadcasted_iota(jnp.int32, x.shape, 1)
    lo = lane < HEAD_DIM
    sq = x * x
    s_lo = jnp.sum(jnp.where(lo, sq, 0.0), axis=-1, keepdims=True)
    s_hi = jnp.sum(jnp.where(lo, 0.0, sq), axis=-1, keepdims=True)
    ms = jnp.where(lo, s_lo, s_hi) * (1.0 / HEAD_DIM)
    return x * lax.rsqrt(ms + EPS) * g2


def _rope(x, cos, sin_signed):
    q4 = HEAD_DIM // 4
    lane = lax.broadcasted_iota(jnp.int32, x.shape, 1)
    first = (lane & q4) == 0
    width = x.shape[1]
    partner = jnp.where(first, pltpu.roll(x, width - q4, 1), pltpu.roll(x, q4, 1))
    return x * cos + partner * sin_signed


def _attention_kernel(past, use_rope, lam_init, *refs):
    refs = list(refs)
    n_seq = refs[0].shape[0]
    n_in = 3 + (2 if past else 0)
    seq_refs, refs = refs[:n_in], refs[n_in:]
    if use_rope:
        rope_refs, refs = refs[:4], refs[4:]
    else:
        rope_refs = []
    param_refs, refs = refs[:4], refs[4:]
    n_out = 1 if past else 2
    out_refs, (kall_ref, vall_ref) = refs[:n_out], refs[n_out:]
    for s in range(n_seq):
        _attention_one_sequence(past, use_rope, lam_init, *[r.at[s] for r in seq_refs],
                                *rope_refs, *param_refs, *[r.at[s] for r in out_refs],
                                kall_ref.at[s], vall_ref.at[s])


def _attention_one_sequence(past, use_rope, lam_init, *refs):
    refs = list(refs)
    q_ref, k_ref, v_ref = refs.pop(0), refs.pop(0), refs.pop(0)
    if past:
        ck_ref, cv_ref = refs.pop(0), refs.pop(0)
    if use_rope:
        cosq_ref, sinq_ref, cosk_ref, sink_ref = (refs.pop(0) for _ in range(4))
    qg_ref, kg_ref, lam_ref, sg_ref = (refs.pop(0) for _ in range(4))
    o_ref = refs.pop(0)
    nk_ref = None if past else refs.pop(0)
    kall_ref, vall_ref = refs

    qi = pl.program_id(2)

    @pl.when(qi == 0)
    def _():
        k = _head_rmsnorm(k_ref[...], kg_ref[...])
        if nk_ref is not None:
            nk_ref[...] = k
        if use_rope:
            k = _rope(k, cosk_ref[...], sink_ref[...])
        if past:
            kall_ref[0:past, :] = ck_ref[...].astype(BF16)
            vall_ref[0:past, :] = cv_ref[...].astype(BF16)
        kall_ref[past:, :] = k.astype(BF16)
        vall_ref[past:, :] = v_ref[...].astype(BF16)

    q = _head_rmsnorm(q_ref[...], qg_ref[...])
    if use_rope:
        q = _rope(q, cosq_ref[...], sinq_ref[...])
    q = q * (HEAD_DIM ** -0.5 * math.log2(math.e))
    tq = q.shape[0]
    lane = lax.broadcasted_iota(jnp.int32, q.shape, 1)
    lo = lane < HEAD_DIM
    nt = (((1,), (1,)), ((), ()))

    def attend(qm):
        s = lax.dot_general(qm.astype(BF16), kall_ref[...], nt, preferred_element_type=F32)
        p = jnp.exp2(s - jnp.max(s, axis=-1, keepdims=True))
        norm = 1.0 / jnp.sum(p, axis=-1, keepdims=True)
        return _dot(p.astype(BF16), vall_ref[...]) * norm

    row_chunks = [slice(r, r + ATTENTION_ROWS) for r in range(0, tq, ATTENTION_ROWS)]
    first = jnp.where(lo, q, 0.0)
    second = jnp.where(lo, 0.0, q)
    a0 = jnp.concatenate([attend(first[rows]) for rows in row_chunks], axis=0)
    a1 = jnp.concatenate([attend(second[rows]) for rows in row_chunks], axis=0)
    lv = lam_ref[...]
    lam = (jnp.exp(jnp.sum(lv[0:1] * lv[1:2], axis=-1, keepdims=True))
           - jnp.exp(jnp.sum(lv[2:3] * lv[3:4], axis=-1, keepdims=True)) + lam_init)
    o = a0 - lam * a1
    ms = jnp.mean(o * o, axis=-1, keepdims=True)
    o = o * lax.rsqrt(ms + EPS) * sg_ref[...] * (1.0 - lam_init)
    o_ref[...] = o.astype(o_ref.dtype)


def _attention(st, q, k, v, q_g, k_g, lam_vecs, subln_g, lam_init, cache_k=None, cache_v=None,
               rope=None):
    seq, hd = st.seq, N_HEADS * V_DIM
    tq = min(seq, ATTENTION_QUERIES)
    nq = seq // tq
    group = max(1, min(ATTENTION_GROUP_QUERIES // seq, st.batch))
    assert st.batch % group == 0
    past = 0 if cache_k is None else cache_k.shape[1]
    lk = past + seq
    q3, k3, v3 = (a.reshape(st.batch, seq, hd) for a in (q, k, v))
    qblk = pl.BlockSpec((group, tq, V_DIM), lambda b, h, i: (b, i, h))
    kblk = pl.BlockSpec((group, seq, V_DIM), lambda b, h, i: (b, 0, h))
    in_specs = [qblk, kblk, kblk]
    args = [q3, k3, v3]
    if past:
        cblk = pl.BlockSpec((group, past, V_DIM), lambda b, h, i: (b, 0, h))
        in_specs += [cblk, cblk]
        args += [cache_k.reshape(st.batch, past, hd), cache_v.reshape(st.batch, past, hd)]
    if rope is not None:
        cos2, sin2 = rope
        tq_tab = pl.BlockSpec((tq, V_DIM), lambda b, h, i: (i, 0))
        k_tab = pl.BlockSpec((seq, V_DIM), lambda b, h, i: (0, 0))
        in_specs += [tq_tab, tq_tab, k_tab, k_tab]
        args += [cos2, sin2, cos2, sin2]
    small = lambda shape: pl.BlockSpec(shape, lambda b, h, i: (0,) * len(shape))
    in_specs += [small((1, V_DIM)), small((1, V_DIM)), small((4, HEAD_DIM)), small((1, V_DIM))]
    args += [jnp.tile(q_g, 2).reshape(1, V_DIM), jnp.tile(k_g, 2).reshape(1, V_DIM), lam_vecs,
             subln_g.reshape(1, V_DIM)]
    out_specs = [qblk]
    out_shape = [jax.ShapeDtypeStruct((st.batch, seq, hd), BF16)]
    if not past:
        out_specs.append(kblk)
        out_shape.append(jax.ShapeDtypeStruct((st.batch, seq, hd), F32))
    outs = pl.pallas_call(
        functools.partial(_attention_kernel, past, rope is not None, lam_init),
        grid=(st.batch // group, N_HEADS, nq),
        in_specs=in_specs,
        out_specs=out_specs,
        out_shape=out_shape,
        scratch_shapes=[pltpu.VMEM((group, lk, V_DIM), BF16),
                        pltpu.VMEM((group, lk, V_DIM), BF16)],
        compiler_params=_params("parallel", "parallel", "arbitrary"),
        name="diff_attention",
    )(*args)
    return [o.reshape(st.rows, hd) for o in outs]


def _dft_matrices(seq):
    idx = np.arange(seq, dtype=np.int64)
    ang = (np.outer(idx, idx) % (2 * seq)).astype(np.float64) * (math.pi / seq)
    return jnp.asarray(np.cos(ang), dtype=BF16), jnp.asarray(np.sin(ang), dtype=BF16)


def _filter_features(seq):
    t = jnp.linspace(0.0, 1.0, seq, dtype=F32)[:, None]
    bands = (HY_EMB - 1) // 2
    w_ang = 2.0 * math.pi * jnp.arange(seq, dtype=F32)[:, None] / seq
    f = jnp.linspace(1e-4, bands - 1, bands, dtype=F32)[None, :]
    ang = f * w_ang
    feats = jnp.concatenate([t, jnp.cos(ang), -jnp.sin(ang)], axis=-1)
    return jnp.pad(feats, ((0, 0), (0, V7X_LANES - HY_EMB)))


def _decay_rates():
    min_decay = math.log(HY_DECAY_TARGET) / HY_SLOW_PCT
    max_decay = math.log(HY_DECAY_TARGET) / HY_FAST_PCT
    return jnp.linspace(min_decay, max_decay, D_MODEL, dtype=F32)[None, :]


def _rope_tables(seq):
    rows = seq // GRID_W
    row = jnp.repeat(jnp.arange(rows, dtype=F32), GRID_W)
    col = jnp.tile(jnp.arange(GRID_W, dtype=F32), rows)
    quarter = HEAD_DIM // 4
    inv = ROPE_BASE ** (-jnp.arange(quarter, dtype=F32) / quarter)

    def axis_angles(pos):
        a = pos[:, None] * inv[None, :]
        return jnp.concatenate([a, a], axis=-1)

    ang = jnp.concatenate([axis_angles(row), axis_angles(col)], axis=-1)
    sign = jnp.where((jnp.arange(HEAD_DIM) & quarter) == 0, -1.0, 1.0).astype(F32)
    return jnp.tile(jnp.cos(ang), (1, 2)), jnp.tile(jnp.sin(ang) * sign[None, :], (1, 2))


def kernel(x_prompt, x_sample, cache_k, cache_v, c, c_ctx, ada_w, ada_b, norm_g, hy_in_w, hy_in_b, hy_conv_w, hy_conv_b, hy_f_w1, hy_f_b1, hy_f_w2, hy_f_b2, hy_f_freq, hy_f_w3, hy_bias, hy_out_w, hy_out_b, at_qkv_w, at_q_g, at_k_g, at_lam, at_subln_g, at_out_w, dn_w_gu, dn_w_down, mo_router_w, mo_router_b, mo_w_gu, mo_w_down):
    d = D_MODEL
    batch, seq = x_prompt.shape[:2]
    dec_batch, dec_seq = x_sample.shape[:2]
    past = cache_k.shape[2]
    streams = [
        (_Stream(batch, seq, 0, False), x_prompt.reshape(batch * seq, d), None),
        (_Stream(dec_batch, dec_seq, 1, True, row0=batch * seq),
         x_sample.reshape(dec_batch * dec_seq, d), (cache_k[:, 0], cache_v[:, 0])),
    ]
    tokens = batch * seq + dec_batch * dec_seq

    cond = jnp.concatenate(
        [c_ctx[None, :], c, jnp.zeros((COND_ROWS - 1 - dec_batch, d), F32)], axis=0)
    mod = _adaln(cond, ada_w, ada_b)
    mod = mod.reshape(mod.shape[0], COND_ROWS, 1, 6 * d)

    in_w, out_w = hy_in_w[0].astype(BF16), hy_out_w[0].astype(BF16)
    qkv_w, at_out = at_qkv_w[0].astype(BF16), at_out_w[0].astype(BF16)
    dn_gu, dn_down = dn_w_gu[0].astype(BF16), dn_w_down[0].astype(BF16)
    mo_gu, mo_down = mo_w_gu[0], mo_w_down[0]
    w1_pad = jnp.pad(hy_f_w1[0], ((0, V7X_LANES - HY_EMB), (0, 0)))
    rw_pad = jnp.pad(mo_router_w[0], ((0, 0), (0, V7X_LANES - N_EXPERTS)))
    rb_pad = jnp.pad(mo_router_b[0], (0, V7X_LANES - N_EXPERTS)).reshape(1, V7X_LANES)
    deltas = _decay_rates()
    lam_init = 0.8 - 0.6 * math.exp(-0.3 * 1)

    routed = []
    pool = jnp.zeros((tokens * LANE_CHUNKS, V7X_LANES), F32)
    for st, x, cache in streams:
        blk = min(st.seq, HYENA_BLOCK)
        cmat, smat = _dft_matrices(blk)
        hf, hb = _filter_time(st.seq, _filter_features(st.seq), w1_pad, hy_f_b1[0], hy_f_w2[0],
                              hy_f_b2[0], hy_f_freq[0], hy_f_w3[0], deltas)
        kr, ki, kn = _filter_spectrum(st.seq, blk, hf, hb, cmat, smat)
        (proj,) = _mod_matmul(st, x, norm_g[0, 0], mod, 0, in_w, hy_in_b[0], 1, BF16)
        z = _hyena_core(st, blk, proj, hy_conv_w[0], hy_conv_b[0].reshape(1, 3 * d), kr, ki, kn,
                        hy_bias[0], cmat, smat)
        x, h2 = _out_proj(st, z, out_w, hy_out_b[0], x, mod, 0, norm_g[0, 1])
        x = _swiglu(st, h2, dn_gu, dn_down, x, mod, 0)

        q, k, v = _mod_matmul(st, x, norm_g[1, 0], mod, 1, qkv_w, None, 3, F32)
        if cache is None:
            o, new_k = _attention(st, q, k, v, at_q_g[0], at_k_g[0], at_lam[0], at_subln_g[0],
                                  lam_init)
            new_kv = (new_k, v)
        else:
            (o,) = _attention(st, q, k, v, at_q_g[0], at_k_g[0], at_lam[0], at_subln_g[0],
                              lam_init, cache[0], cache[1], _rope_tables(st.seq))
        x, pool, route = _out_proj(st, o, at_out, None, x, mod, 1, norm_g[1, 1],
                                   (rw_pad, rb_pad, pool))
        routed.append((st, x, route))

    order, pos, *tile_table = _moe_plan(
        jnp.concatenate([route[:, :4] for _, _, route in routed], axis=0), MOE_TILE)
    y_slots = _moe_ffn(pool, order, *tile_table, mo_gu, mo_down, MOE_TILE)
    results = [_moe_combine(st, y_slots, pos, route, x, mod, 1).reshape(st.batch, st.seq, d)
               for st, x, route in routed]

    new_k, new_v = new_kv
    return (results[0], results[1],
            new_k.reshape(batch, 1, seq, N_HEADS, 2, HEAD_DIM),
            new_v.reshape(batch, 1, seq, N_HEADS, V_DIM))
```

```python
import functools
import math

import numpy as np
import jax
import jax.numpy as jnp
from jax import lax
from jax.experimental import pallas as pl
from jax.experimental.pallas import tpu as pltpu

F32 = jnp.float32
BF16 = jnp.bfloat16
HIGHEST = lax.Precision.HIGHEST

D_MODEL = 1024
GRID_W = 64
HY_ORDER = 2
HY_EMB = 33
HY_FW = 64
HY_DECAY_TARGET = 1e-2
HY_FAST_PCT = 0.3
HY_SLOW_PCT = 1.5
N_HEADS = 8
HEAD_DIM = 64
V_DIM = 2 * HEAD_DIM
ROPE_BASE = 10000.0
D_FF = 2816
N_EXPERTS = 8
D_FF_EXPERT = 3584
EPS = 1e-6

V7X_LANES = 128
V7X_VMEM_LIMIT_BYTES = 56 * 1024 * 1024
LANE_CHUNKS = D_MODEL // V7X_LANES
COND_ROWS = 16
TOKEN_TILE = 1024
WIDE_OUT_TILE = 512
MOE_TILE = 1024
MOE_FF_CHUNK = 512
DENSE_FF_CHUNK = D_FF // 2
HYENA_BLOCK = 512
NYQUIST_ROWS = 8
ATTENTION_QUERIES = 1024
ATTENTION_GROUP_QUERIES = 1024
ATTENTION_ROWS = 256


def _params(*semantics):
    return pltpu.CompilerParams(dimension_semantics=semantics,
                                vmem_limit_bytes=V7X_VMEM_LIMIT_BYTES)


def _resident(shape):
    zeros = (0,) * len(shape)
    return pl.BlockSpec(shape, lambda *_: zeros, pipeline_mode=pl.Buffered(1))


def _dot(a, b):
    return jnp.dot(a, b, preferred_element_type=F32)


def _dot_f32(a, b):
    return jnp.dot(a, b, precision=HIGHEST, preferred_element_type=F32)


def _modulate(x, g, shift, scale):
    ms = jnp.mean(x * x, axis=-1, keepdims=True)
    return (x * lax.rsqrt(ms + EPS) * g) * (1.0 + scale) + shift


def _adaln_kernel(cond_ref, w_ref, b_ref, o_ref):
    c = cond_ref[...]
    o_ref[...] = _dot_f32(c * jax.nn.sigmoid(c), w_ref[...]) + b_ref[...]


def _adaln(cond, ada_w, ada_b):
    depth, d, n = ada_w.shape
    tn = 1536
    return pl.pallas_call(
        _adaln_kernel,
        grid=(depth, n // tn),
        in_specs=[
            pl.BlockSpec((COND_ROWS, d), lambda i, j: (0, 0)),
            pl.BlockSpec((None, d, tn), lambda i, j: (i, 0, j)),
            pl.BlockSpec((None, 1, tn), lambda i, j: (i, 0, j)),
        ],
        out_specs=pl.BlockSpec((None, COND_ROWS, tn), lambda i, j: (i, 0, j)),
        out_shape=jax.ShapeDtypeStruct((depth, COND_ROWS, n), F32),
        compiler_params=_params("parallel", "parallel"),
        name="adaln",
    )(cond, ada_w, ada_b.reshape(depth, 1, n))


class _Stream:
    def __init__(self, batch, seq, cond_row0, per_seq_cond, row0=0, tile=TOKEN_TILE):
        self.batch, self.seq = batch, seq
        self.rows = batch * seq
        self.row0 = row0
        self._cond = (cond_row0, per_seq_cond)
        if per_seq_cond:
            self.tm = min(tile, seq)
            tiles_per_seq = seq // self.tm
            self.cond_row = lambda i: cond_row0 + i // tiles_per_seq
        else:
            self.tm = min(tile, self.rows)
            self.cond_row = lambda i: cond_row0
        self.tiles = self.rows // self.tm

    def retiled(self, tile):
        return _Stream(self.batch, self.seq, *self._cond, row0=self.row0, tile=tile)

    def mod_spec(self, layer, chunk):
        return pl.BlockSpec((None, None, 1, D_MODEL),
                            lambda i, *_: (layer, self.cond_row(i), 0, chunk))

    def row_spec(self, width):
        return pl.BlockSpec((self.tm, width), lambda i, *_: (i, 0))


def _mod_matmul_kernel(n_out, has_bias, x_ref, g_ref, sh_ref, sc_ref, w_ref, *rest):
    if has_bias:
        b_ref, out_refs = rest[0], rest[1:]
    else:
        b_ref, out_refs = None, rest
    h = _modulate(x_ref[...], g_ref[...], sh_ref[...], sc_ref[...])
    y = _dot(h.astype(BF16), w_ref[...])
    if has_bias:
        y = y + b_ref[...]
    width = y.shape[1] // n_out
    for k, o_ref in enumerate(out_refs):
        o_ref[...] = y[:, k * width:(k + 1) * width].astype(o_ref.dtype)


def _mod_matmul(st, x, norm_g, mod, layer, w_bf16, bias, n_out, out_dtype):
    st = st.retiled(WIDE_OUT_TILE)
    d, n = w_bf16.shape
    in_specs = [st.row_spec(d), _resident((1, d)), st.mod_spec(layer, 0), st.mod_spec(layer, 1),
                _resident((d, n))]
    args = [x, norm_g.reshape(1, d), mod, mod, w_bf16]
    if bias is not None:
        in_specs.append(_resident((1, n)))
        args.append(bias.reshape(1, n))
    width = n // n_out
    outs = pl.pallas_call(
        functools.partial(_mod_matmul_kernel, n_out, bias is not None),
        grid=(st.tiles,),
        in_specs=in_specs,
        out_specs=[st.row_spec(width)] * n_out,
        out_shape=[jax.ShapeDtypeStruct((st.rows, width), out_dtype)] * n_out,
        compiler_params=_params("parallel"),
        name="mod_matmul",
    )(*args)
    return outs


def _filter_time_kernel(feats_ref, w1_ref, b1_ref, w2_ref, b2_ref, fr_ref, w3_ref, dl_ref,
                        hf_ref, hb_ref):
    feats = feats_ref[...]
    fr = fr_ref[...]
    h = jnp.sin(fr[0:1] * (_dot_f32(feats, w1_ref[...]) + b1_ref[...]))
    h = jnp.sin(fr[1:2] * (_dot_f32(h, w2_ref[...]) + b2_ref[...]))
    h = _dot_f32(h, w3_ref[...])
    t = feats[:, 0:1]
    decay = jnp.exp(-t * jnp.abs(dl_ref[...]))
    half = HY_ORDER * D_MODEL
    decay2 = jnp.concatenate([decay] * HY_ORDER, axis=1)
    hf_ref[...] = h[:, :half] * decay2
    hb_ref[...] = jnp.where(t == 0.0, 0.0, h[:, half:] * decay2)


def _filter_time(seq, feats_pad, w1_pad, b1, w2, b2, freq, w3, deltas):
    tl = min(seq, 256)
    half = HY_ORDER * D_MODEL
    out = jax.ShapeDtypeStruct((seq, half), F32)
    return pl.pallas_call(
        _filter_time_kernel,
        grid=(seq // tl,),
        in_specs=[
            pl.BlockSpec((tl, V7X_LANES), lambda i: (i, 0)),
            _resident(w1_pad.shape), _resident((1, HY_FW)), _resident((HY_FW, HY_FW)),
            _resident((1, HY_FW)), _resident((2, HY_FW)), _resident(w3.shape),
            _resident((1, D_MODEL)),
        ],
        out_specs=[pl.BlockSpec((tl, half), lambda i: (i, 0))] * 2,
        out_shape=[out, out],
        compiler_params=_params("parallel"),
        name="hyena_filter_time",
    )(feats_pad, w1_pad, b1.reshape(1, HY_FW), w2, b2.reshape(1, HY_FW), freq, w3, deltas)


def _filter_spectrum_kernel(blk, nb, hf_ref, hb_ref, c_ref, s_ref, kr_ref, ki_ref, kn_ref):
    row = lax.broadcasted_iota(jnp.int32, (blk, 1), 0)
    sg = (1 - 2 * (row & 1)).astype(F32)
    wgt = jnp.where(row == 0, 1.0, 2.0) * (1.0 / (2 * blk))
    fwd, bwd = [], []
    for j in range(nb):
        rows = slice(j * blk, (j + 1) * blk)
        for ref, out in ((hf_ref, fwd), (hb_ref, bwd)):
            x = ref[rows, :]
            xb = x.astype(BF16)
            out.append((_dot(c_ref[...], xb), _dot(s_ref[...], xb),
                        jnp.sum(x * sg, axis=0, keepdims=True), x[0:1, :],
                        xb[0:1, :].astype(F32)))
    kn_ref[...] = jnp.zeros_like(kn_ref)
    for d in range(-(nb - 1), nb):
        if d == 0:
            (fc, fs, fn, _, _), (bc, bs, bn, _, _) = fwd[0], bwd[0]
            kr, ki, kn = fc + bc, bs - fs, fn + bn
        else:
            parts, im_sign = (fwd, -1.0) if d > 0 else (bwd, 1.0)
            c1, s1, n1, _, _ = parts[abs(d)]
            c0, s0, n0, x0, x0_seen = parts[abs(d) - 1]
            kr, ki, kn = c1 + sg * (c0 - x0_seen), im_sign * (s1 + sg * s0), n1 + n0 - x0
        slot = d + nb - 1
        kr_ref[slot * blk:(slot + 1) * blk, :] = wgt * kr
        ki_ref[slot * blk:(slot + 1) * blk, :] = wgt * ki
        kn_ref[slot:slot + 1, :] = kn * (1.0 / (2 * blk))


def _filter_spectrum(seq, blk, hf, hb, cmat, smat):
    nb = seq // blk
    half = hf.shape[1]
    tn = 256
    col = pl.BlockSpec((seq, tn), lambda j: (0, j))
    spec_rows = (2 * nb - 1) * blk
    out_col = pl.BlockSpec((spec_rows, tn), lambda j: (0, j))
    return pl.pallas_call(
        functools.partial(_filter_spectrum_kernel, blk, nb),
        grid=(half // tn,),
        in_specs=[col, col, _resident((blk, blk)), _resident((blk, blk))],
        out_specs=[out_col, out_col, pl.BlockSpec((NYQUIST_ROWS, tn), lambda j: (0, j))],
        out_shape=[jax.ShapeDtypeStruct((spec_rows, half), F32)] * 2
        + [jax.ShapeDtypeStruct((NYQUIST_ROWS, half), F32)],
        compiler_params=_params("parallel"),
        name="hyena_filter_spectrum",
    )(hf, hb, cmat, smat)


def _hyena_core_kernel(seq, blk, pv_ref, p1_ref, p2_ref, cwv_ref, cw1_ref, cw2_ref, cbv_ref,
                       cb1_ref, cb2_ref, kr0_ref, ki0_ref, kn0_ref, kr1_ref, ki1_ref, kn1_ref,
                       bias0_ref, bias1_ref, c_ref, s_ref, z_ref,
                       u_ref, ub_ref, gate_ref, a_ref, b_ref):
    row = lax.broadcasted_iota(jnp.int32, (blk, 1), 0)
    sign = (1 - 2 * (row & 1)).astype(F32)
    nb = seq // blk
    blocks = [slice(j * blk, (j + 1) * blk) for j in range(nb)]

    def short_conv(dst_ref, x_ref, w_ref, b_ref):
        x = x_ref[...].astype(F32)
        w, b = w_ref[...], b_ref[...]
        dst_ref[...] = (pltpu.roll(x, 1, 0) * w[0:1] + x * w[1:2]
                        + pltpu.roll(x, seq - 1, 0) * w[2:3] + b)
        edge = 16
        head = x_ref[0:edge, :].astype(F32)
        tail = x_ref[seq - edge:seq, :].astype(F32)
        dst_ref[0:1, :] = head[0:1] * w[1:2] + head[1:2] * w[2:3] + b
        dst_ref[seq - 1:seq, :] = tail[edge - 2:edge - 1] * w[0:1] + tail[edge - 1:edge] * w[1:2] + b

    def gated_long_conv(kr_ref, ki_ref, kn_ref, bias_ref, write):
        ub_ref[...] = u_ref[...].astype(BF16)
        nyq_in = []
        for rows in blocks:
            a_ref[rows, :] = _dot(c_ref[...], ub_ref[rows, :])
            b_ref[rows, :] = _dot(s_ref[...], ub_ref[rows, :])
            nyq_in.append(jnp.sum(u_ref[rows, :] * sign, axis=0, keepdims=True))
        for i, rows in enumerate(blocks):
            p = q = nyq = None
            for j, src in enumerate(blocks):
                slot = i - j + nb - 1
                kr = kr_ref[slot * blk:(slot + 1) * blk, :]
                ki = ki_ref[slot * blk:(slot + 1) * blk, :]
                a, b = a_ref[src, :], b_ref[src, :]
                pj, qj = a * kr + b * ki, b * kr - a * ki
                nj = nyq_in[j] * kn_ref[slot:slot + 1, :]
                p, q, nyq = (pj, qj, nj) if j == 0 else (p + pj, q + qj, nyq + nj)
            y = _dot(c_ref[...], p.astype(BF16)) + _dot(s_ref[...], q.astype(BF16))
            y = y + sign * nyq + u_ref[rows, :] * bias_ref[...]
            write(rows, gate_ref[rows, :] * y)

    def to_u(rows, val):
        u_ref[rows, :] = val

    def to_z(rows, val):
        z_ref[rows, :] = val.astype(z_ref.dtype)

    short_conv(u_ref, pv_ref, cwv_ref, cbv_ref)
    short_conv(gate_ref, p1_ref, cw1_ref, cb1_ref)
    gated_long_conv(kr0_ref, ki0_ref, kn0_ref, bias0_ref, to_u)
    short_conv(gate_ref, p2_ref, cw2_ref, cb2_ref)
    gated_long_conv(kr1_ref, ki1_ref, kn1_ref, bias1_ref, to_z)


def _hyena_core(st, blk, proj, conv_w, conv_b, kr, ki, kn, bias, cmat, smat):
    seq, d = st.seq, D_MODEL
    tn = 256
    nj = d // tn
    spec_rows = kr.shape[0]
    proj3 = proj.reshape(st.batch, seq, 3 * d)

    def part(k):
        return pl.BlockSpec((None, seq, tn), lambda j, b: (b, 0, k * nj + j))

    def cols(rows, k, buffers=2):
        return pl.BlockSpec((rows, tn), lambda j, b: (0, k * nj + j),
                            pipeline_mode=pl.Buffered(buffers))

    in_specs = ([part(0), part(1), part(2)]
                + [cols(3, k) for k in range(3)] + [cols(1, k) for k in range(3)]
                + [cols(spec_rows, 0, 1), cols(spec_rows, 0, 1), cols(NYQUIST_ROWS, 0),
                   cols(spec_rows, 1, 1), cols(spec_rows, 1, 1), cols(NYQUIST_ROWS, 1)]
                + [cols(1, 0), cols(1, 0)]
                + [_resident((blk, blk)), _resident((blk, blk))])
    z = pl.pallas_call(
        functools.partial(_hyena_core_kernel, seq, blk),
        grid=(nj, st.batch),
        in_specs=in_specs,
        out_specs=pl.BlockSpec((None, seq, tn), lambda j, b: (b, 0, j)),
        out_shape=jax.ShapeDtypeStruct((st.batch, seq, d), BF16),
        scratch_shapes=[pltpu.VMEM((seq, tn), F32), pltpu.VMEM((seq, tn), BF16),
                        pltpu.VMEM((seq, tn), F32), pltpu.VMEM((seq, tn), F32),
                        pltpu.VMEM((seq, tn), F32)],
        compiler_params=_params("parallel", "parallel"),
        name="hyena_core",
    )(proj3, proj3, proj3, conv_w, conv_w, conv_w, conv_b, conv_b, conv_b,
      kr, ki, kn, kr, ki, kn, bias[0:1], bias[1:2], cmat, smat)
    return z.reshape(st.rows, d)


def _out_proj_kernel(has_bias, has_router, *refs):
    refs = list(refs)
    z_ref, w_ref = refs.pop(0), refs.pop(0)
    b_ref = refs.pop(0) if has_bias else None
    x_ref, g1_ref, ng_ref, sh_ref, sc_ref = (refs.pop(0) for _ in range(5))
    if has_router:
        rw_hi_ref, rw_lo_ref, rb_ref = refs.pop(0), refs.pop(0), refs.pop(0)
        refs.pop(0)
    x1_ref, h2_ref = refs.pop(0), refs.pop(0)
    m = _dot(z_ref[...], w_ref[...])
    if has_bias:
        m = m + b_ref[...]
    x1 = x_ref[...] + g1_ref[...] * m
    x1_ref[...] = x1
    h2 = _modulate(x1, ng_ref[...], sh_ref[...], sc_ref[...])
    if not has_router:
        h2_ref[...] = h2.astype(BF16)
        return
    _store_token_tiles(h2_ref, h2)
    route_ref = refs.pop(0)
    h_hi = h2.astype(BF16)
    h_lo = (h2 - h_hi.astype(F32)).astype(BF16)
    logits = (_dot(h_hi, rw_hi_ref[...])
              + (_dot(h_lo, rw_hi_ref[...]) + _dot(h_hi, rw_lo_ref[...])) + rb_ref[...])
    lane = lax.broadcasted_iota(jnp.int32, logits.shape, 1)
    neg = -jnp.inf
    logits = jnp.where(lane < N_EXPERTS, logits, neg)
    m1 = jnp.max(logits, axis=-1, keepdims=True)
    i1 = jnp.min(jnp.where(logits == m1, lane, V7X_LANES), axis=-1, keepdims=True)
    rest = jnp.where(lane == i1, neg, logits)
    m2 = jnp.max(rest, axis=-1, keepdims=True)
    i2 = jnp.min(jnp.where(rest == m2, lane, V7X_LANES), axis=-1, keepdims=True)
    e2 = jnp.exp(m2 - m1)
    den = 1.0 + e2
    route = jnp.where(lane == 0, i1.astype(F32), jnp.where(lane == 1, i2.astype(F32), 0.0))
    route_ref[...] = route + jnp.where(lane == 2, 1.0 / den, 0.0) + jnp.where(lane == 3, e2 / den, 0.0)


def _store_token_tiles(ref, val):
    rows = val.shape[0]
    for c in range(LANE_CHUNKS):
        ref[pl.ds(c, rows, stride=LANE_CHUNKS), :] = val[:, c * V7X_LANES:(c + 1) * V7X_LANES]


def _out_proj(st, z_bf16, w_bf16, bias, x, mod, layer, norm_g, router=None):
    d = D_MODEL
    in_specs = [st.row_spec(z_bf16.shape[1]), _resident(w_bf16.shape)]
    args = [z_bf16, w_bf16]
    if bias is not None:
        in_specs.append(_resident((1, d)))
        args.append(bias.reshape(1, d))
    in_specs += [st.row_spec(d), st.mod_spec(layer, 2), _resident((1, d)),
                 st.mod_spec(layer, 3), st.mod_spec(layer, 4)]
    args += [x, mod, norm_g.reshape(1, d), mod, mod]
    aliases = {}
    if router is None:
        out_specs = [st.row_spec(d), st.row_spec(d)]
        out_shape = [jax.ShapeDtypeStruct((st.rows, d), F32),
                     jax.ShapeDtypeStruct((st.rows, d), BF16)]
    else:
        rw_pad, rb_pad, pool = router
        rw_hi = rw_pad.astype(BF16)
        rw_lo = (rw_pad - rw_hi.astype(F32)).astype(BF16)
        in_specs += [_resident(rw_pad.shape), _resident(rw_pad.shape), _resident(rb_pad.shape),
                     pl.BlockSpec(memory_space=pl.ANY)]
        args += [rw_hi, rw_lo, rb_pad, pool]
        aliases = {len(args) - 1: 1}
        tile0 = st.row0 // st.tm
        out_specs = [st.row_spec(d),
                     pl.BlockSpec((st.tm * LANE_CHUNKS, V7X_LANES), lambda i: (tile0 + i, 0)),
                     st.row_spec(V7X_LANES)]
        out_shape = [jax.ShapeDtypeStruct((st.rows, d), F32),
                     jax.ShapeDtypeStruct(pool.shape, F32),
                     jax.ShapeDtypeStruct((st.rows, V7X_LANES), F32)]
    return pl.pallas_call(
        functools.partial(_out_proj_kernel, bias is not None, router is not None),
        grid=(st.tiles,),
        in_specs=in_specs,
        out_specs=out_specs,
        out_shape=out_shape,
        input_output_aliases=aliases,
        compiler_params=_params("parallel"),
        name="out_proj",
    )(*args)


def _swiglu_part(x_bf16, wg_ref, wu_ref, wd_ref):
    g = _dot(x_bf16, wg_ref[...].astype(BF16))
    u = _dot(x_bf16, wu_ref[...].astype(BF16))
    return _dot((g * jax.nn.sigmoid(g) * u).astype(BF16), wd_ref[...].astype(BF16))


def _swiglu_kernel(h_ref, wg_ref, wu_ref, wd_ref, x_ref, g2_ref, o_ref, acc_ref):
    j = pl.program_id(1)
    part = _swiglu_part(h_ref[...], wg_ref, wu_ref, wd_ref)

    @pl.when(j == 0)
    def _():
        acc_ref[...] = part

    @pl.when(j > 0)
    def _():
        acc_ref[...] += part

    @pl.when(j == pl.num_programs(1) - 1)
    def _():
        o_ref[...] = x_ref[...] + g2_ref[...] * acc_ref[...]


def _swiglu(st, h2, w_gu, w_down, x1, mod, layer):
    st = st.retiled(WIDE_OUT_TILE)
    d, two_f = w_gu.shape
    f = two_f // 2
    tf = DENSE_FF_CHUNK
    nf = f // tf
    return pl.pallas_call(
        _swiglu_kernel,
        grid=(st.tiles, nf),
        in_specs=[
            pl.BlockSpec((st.tm, d), lambda i, j: (i, 0)),
            pl.BlockSpec((d, tf), lambda i, j: (0, j)),
            pl.BlockSpec((d, tf), lambda i, j: (0, nf + j)),
            pl.BlockSpec((tf, d), lambda i, j: (j, 0)),
            pl.BlockSpec((st.tm, d), lambda i, j: (i, 0)),
            st.mod_spec(layer, 5),
        ],
        out_specs=pl.BlockSpec((st.tm, d), lambda i, j: (i, 0)),
        out_shape=jax.ShapeDtypeStruct((st.rows, d), F32),
        scratch_shapes=[pltpu.VMEM((st.tm, d), F32)],
        compiler_params=_params("parallel", "arbitrary"),
        name="swiglu",
    )(h2, w_gu, w_gu, w_down, x1, mod)


def _moe_plan(route, tm):
    tokens = route.shape[0]
    max_tiles = (2 * tokens) // tm + N_EXPERTS
    expert = route[:, :2].astype(jnp.int32).reshape(-1)
    experts = jnp.arange(N_EXPERTS, dtype=jnp.int32)
    onehot = (expert[:, None] == experts[None, :]).astype(jnp.int32)
    csum = jnp.cumsum(onehot, axis=0)
    rank = jnp.sum(csum * onehot, axis=1) - 1
    counts = csum[-1]
    tiles_per_expert = (counts + tm - 1) // tm
    tiles_end = jnp.cumsum(tiles_per_expert)
    first_tile = tiles_end - tiles_per_expert
    first_sorted = jnp.cumsum(counts) - counts
    pos = jnp.sum((first_tile * tm)[None, :] * onehot, axis=1) + rank
    token = jnp.arange(2 * tokens, dtype=jnp.int32) // 2
    _, order = lax.sort_key_val(pos, token)
    n_tiles = tiles_end[-1:]
    tile = jnp.minimum(jnp.arange(max_tiles, dtype=jnp.int32), n_tiles - 1)
    tile_expert = jnp.sum((tile[:, None] >= tiles_end[None, :]).astype(jnp.int32), axis=1)
    mine = (tile_expert[:, None] == experts[None, :]).astype(jnp.int32)
    done = (tile - jnp.sum(first_tile[None, :] * mine, axis=1)) * tm
    tile_base = jnp.sum(first_sorted[None, :] * mine, axis=1) + done
    tile_valid = jnp.clip(jnp.sum(counts[None, :] * mine, axis=1) - done, 1, tm)
    as_i32 = lambda a: a.astype(jnp.int32)
    return (as_i32(order), as_i32(pos), as_i32(tile_expert), as_i32(tile_base),
            as_i32(tile_valid), as_i32(n_tiles))


def _token_tile_copy(src_hbm, row, dst, slot_row, sem):
    return pltpu.make_async_copy(
        src_hbm.at[pl.ds(pl.multiple_of(row * LANE_CHUNKS, LANE_CHUNKS), LANE_CHUNKS)],
        dst.at[pl.ds(pl.multiple_of(slot_row * LANE_CHUNKS, LANE_CHUNKS), LANE_CHUNKS)],
        sem)


def _moe_ffn_kernel(tm, order_ref, te_ref, base_ref, valid_ref, nt_ref, x_hbm, wg_ref, wu_ref,
                    wd_ref, y_ref, xbuf, xd_ref, acc_ref, sem):
    t, j = pl.program_id(0), pl.program_id(1)
    last_j = pl.num_programs(1) - 1
    n_tiles = nt_ref[0]
    slot = t % 2
    rows = tm * LANE_CHUNKS

    def start_gather(tile, into):
        base, last = base_ref[tile], valid_ref[tile] - 1

        def body(r, carry):
            token = order_ref[base + jnp.minimum(r, last)]
            _token_tile_copy(x_hbm, token, xbuf.at[into], r, sem.at[into]).start()
            return carry
        lax.fori_loop(0, tm, body, 0, unroll=8)

    @pl.when((t == 0) & (j == 0))
    def _():
        start_gather(0, 0)

    @pl.when((j == 0) & (t < n_tiles))
    def _():
        pltpu.make_async_copy(x_hbm.at[pl.ds(0, rows)], xbuf.at[slot], sem.at[slot]).wait()
        for c in range(LANE_CHUNKS):
            xd_ref[:, c * V7X_LANES:(c + 1) * V7X_LANES] = (
                xbuf[slot, pl.ds(c, tm, stride=LANE_CHUNKS), :].astype(BF16))

        @pl.when(t + 1 < n_tiles)
        def _():
            start_gather(t + 1, 1 - slot)

    @pl.when(t < n_tiles)
    def _():
        part = _swiglu_part(xd_ref[...], wg_ref, wu_ref, wd_ref)

        @pl.when(j == 0)
        def _():
            acc_ref[...] = part

        @pl.when(j > 0)
        def _():
            acc_ref[...] += part

        @pl.when(j == last_j)
        def _():
            _store_token_tiles(y_ref, acc_ref[...])

    @pl.when((t >= n_tiles) & (j == last_j))
    def _():
        y_ref[...] = jnp.zeros_like(y_ref)


def _moe_ffn(pool, order, tile_expert, tile_base, tile_valid, n_tiles, w_gu, w_down, tm):
    n_e, d, two_f = w_gu.shape
    f = two_f // 2
    tf = MOE_FF_CHUNK
    nf = f // tf
    max_tiles = tile_expert.shape[0]
    rows = tm * LANE_CHUNKS
    grid_spec = pltpu.PrefetchScalarGridSpec(
        num_scalar_prefetch=5,
        grid=(max_tiles, nf),
        in_specs=[
            pl.BlockSpec(memory_space=pl.ANY),
            pl.BlockSpec((None, d, tf), lambda t, j, order, te, *_: (te[t], 0, j)),
            pl.BlockSpec((None, d, tf), lambda t, j, order, te, *_: (te[t], 0, nf + j)),
            pl.BlockSpec((None, tf, d), lambda t, j, order, te, *_: (te[t], j, 0)),
        ],
        out_specs=pl.BlockSpec((rows, V7X_LANES), lambda t, j, *_: (t, 0)),
        scratch_shapes=[pltpu.VMEM((2, rows, V7X_LANES), F32), pltpu.VMEM((tm, d), BF16),
                        pltpu.VMEM((tm, d), F32), pltpu.SemaphoreType.DMA((2,))],
    )
    return pl.pallas_call(
        functools.partial(_moe_ffn_kernel, tm),
        grid_spec=grid_spec,
        out_shape=jax.ShapeDtypeStruct((max_tiles * rows, V7X_LANES), F32),
        compiler_params=_params("arbitrary", "arbitrary"),
        name="moe_ffn",
    )(order, tile_expert, tile_base, tile_valid, n_tiles, pool, w_gu, w_gu, w_down)


def _moe_combine_kernel(tm, token0, pos_ref, y_hbm, route_ref, x_ref, g2_ref, o_ref, ybuf, sem):
    i = pl.program_id(0)
    slot = i % 2
    rows = tm * LANE_CHUNKS

    def start_gather(tile, into):
        def body(r, carry):
            a = 2 * (token0 + tile * tm + r)
            for k in range(2):
                _token_tile_copy(y_hbm, pos_ref[a + k], ybuf.at[into], k * tm + r,
                                 sem.at[into]).start(priority=k)
            return carry
        lax.fori_loop(0, tm, body, 0, unroll=4)

    @pl.when(i == 0)
    def _():
        start_gather(0, 0)

    pltpu.make_async_copy(y_hbm.at[pl.ds(0, 2 * rows)], ybuf.at[slot], sem.at[slot]).wait()

    @pl.when(i + 1 < pl.num_programs(0))
    def _():
        start_gather(i + 1, 1 - slot)

    route = route_ref[...]
    lane = lax.broadcasted_iota(jnp.int32, route.shape, 1)
    gate0 = jnp.sum(jnp.where(lane == 2, route, 0.0), axis=-1, keepdims=True)
    gate1 = jnp.sum(jnp.where(lane == 3, route, 0.0), axis=-1, keepdims=True)
    for c in range(LANE_CHUNKS):
        cols = slice(c * V7X_LANES, (c + 1) * V7X_LANES)
        y0 = ybuf[slot, pl.ds(c, tm, stride=LANE_CHUNKS), :]
        y1 = ybuf[slot, pl.ds(rows + c, tm, stride=LANE_CHUNKS), :]
        o_ref[:, cols] = x_ref[:, cols] + g2_ref[:, cols] * (gate0 * y0 + gate1 * y1)


def _moe_combine(st, y_slots, pos, route, x1, mod, layer):
    d = D_MODEL
    tm = st.tm
    rows = tm * LANE_CHUNKS
    grid_spec = pltpu.PrefetchScalarGridSpec(
        num_scalar_prefetch=1,
        grid=(st.tiles,),
        in_specs=[
            pl.BlockSpec(memory_space=pl.ANY),
            st.row_spec(V7X_LANES), st.row_spec(d), st.mod_spec(layer, 5),
        ],
        out_specs=st.row_spec(d),
        scratch_shapes=[pltpu.VMEM((2, 2 * rows, V7X_LANES), F32), pltpu.SemaphoreType.DMA((2,))],
    )
    return pl.pallas_call(
        functools.partial(_moe_combine_kernel, tm, st.row0),
        grid_spec=grid_spec,
        out_shape=jax.ShapeDtypeStruct((st.rows, d), F32),
        compiler_params=_params("arbitrary"),
        name="moe_combine",
    )(pos, y_slots, route, x1, mod)


def _head_rmsnorm(x, g2):
    lane = lax.broadcasted_iota(jnp.int32, x.shape, 1)
    lo = lane < HEAD_DIM
    sq = x * x
    s_lo = jnp.sum(jnp.where(lo, sq, 0.0), axis=-1, keepdims=True)
    s_hi = jnp.sum(jnp.where(lo, 0.0, sq), axis=-1, keepdims=True)
    ms = jnp.where(lo, s_lo, s_hi) * (1.0 / HEAD_DIM)
    return x * lax.rsqrt(ms + EPS) * g2


def _rope(x, cos, sin_signed):
    q4 = HEAD_DIM // 4
    lane = lax.broadcasted_iota(jnp.int32, x.shape, 1)
    first = (lane & q4) == 0
    width = x.shape[1]
    partner = jnp.where(first, pltpu.roll(x, width - q4, 1), pltpu.roll(x, q4, 1))
    return x * cos + partner * sin_signed


def _attention_kernel(past, use_rope, lam_init, *refs):
    refs = list(refs)
    n_seq = refs[0].shape[0]
    n_in = 3 + (2 if past else 0)
    seq_refs, refs = refs[:n_in], refs[n_in:]
    if use_rope:
        rope_refs, refs = refs[:4], refs[4:]
    else:
        rope_refs = []
    param_refs, refs = refs[:4], refs[4:]
    n_out = 1 if past else 2
    out_refs, (kall_ref, vall_ref) = refs[:n_out], refs[n_out:]
    for s in range(n_seq):
        _attention_one_sequence(past, use_rope, lam_init, *[r.at[s] for r in seq_refs],
                                *rope_refs, *param_refs, *[r.at[s] for r in out_refs],
                                kall_ref.at[s], vall_ref.at[s])


def _attention_one_sequence(past, use_rope, lam_init, *refs):
    refs = list(refs)
    q_ref, k_ref, v_ref = refs.pop(0), refs.pop(0), refs.pop(0)
    if past:
        ck_ref, cv_ref = refs.pop(0), refs.pop(0)
    if use_rope:
        cosq_ref, sinq_ref, cosk_ref, sink_ref = (refs.pop(0) for _ in range(4))
    qg_ref, kg_ref, lam_ref, sg_ref = (refs.pop(0) for _ in range(4))
    o_ref = refs.pop(0)
    nk_ref = None if past else refs.pop(0)
    kall_ref, vall_ref = refs

    qi = pl.program_id(2)

    @pl.when(qi == 0)
    def _():
        k = _head_rmsnorm(k_ref[...], kg_ref[...])
        if nk_ref is not None:
            nk_ref[...] = k
        if use_rope:
            k = _rope(k, cosk_ref[...], sink_ref[...])
        if past:
            kall_ref[0:past, :] = ck_ref[...].astype(BF16)
            vall_ref[0:past, :] = cv_ref[...].astype(BF16)
        kall_ref[past:, :] = k.astype(BF16)
        vall_ref[past:, :] = v_ref[...].astype(BF16)

    q = _head_rmsnorm(q_ref[...], qg_ref[...])
    if use_rope:
        q = _rope(q, cosq_ref[...], sinq_ref[...])
    q = q * (HEAD_DIM ** -0.5 * math.log2(math.e))
    tq = q.shape[0]
    nt = (((1,), (1,)), ((), ()))
    lv = lam_ref[...]
    lam = (jnp.exp(jnp.sum(lv[0:1] * lv[1:2], axis=-1, keepdims=True))
           - jnp.exp(jnp.sum(lv[2:3] * lv[3:4], axis=-1, keepdims=True)) + lam_init)

    def softmax(qm):
        s = lax.dot_general(qm.astype(BF16), kall_ref[...], nt, preferred_element_type=F32)
        p = jnp.exp2(s - jnp.max(s, axis=-1, keepdims=True))
        return p, 1.0 / jnp.sum(p, axis=-1, keepdims=True)

    def attend(rows):
        qc = q[rows]
        first = lax.broadcasted_iota(jnp.int32, qc.shape, 1) < HEAD_DIM
        p0, n0 = softmax(jnp.where(first, qc, 0.0))
        p1, n1 = softmax(jnp.where(first, 0.0, qc))
        return _dot((p0 * n0 - p1 * (lam * n1)).astype(BF16), vall_ref[...])

    row_chunks = [slice(r, r + ATTENTION_ROWS) for r in range(0, tq, ATTENTION_ROWS)]
    o = jnp.concatenate([attend(rows) for rows in row_chunks], axis=0)
    ms = jnp.mean(o * o, axis=-1, keepdims=True)
    o = o * lax.rsqrt(ms + EPS) * sg_ref[...] * (1.0 - lam_init)
    o_ref[...] = o.astype(o_ref.dtype)


def _attention(st, q, k, v, q_g, k_g, lam_vecs, subln_g, lam_init, cache_k=None, cache_v=None,
               rope=None, cache_layer=0):
    seq, hd = st.seq, N_HEADS * V_DIM
    tq = min(seq, ATTENTION_QUERIES)
    nq = seq // tq
    group = max(1, min(ATTENTION_GROUP_QUERIES // seq, st.batch))
    assert st.batch % group == 0
    past = 0 if cache_k is None else cache_k.shape[2]
    lk = past + seq
    q3, k3, v3 = (a.reshape(st.batch, seq, hd) for a in (q, k, v))
    qblk = pl.BlockSpec((group, tq, V_DIM), lambda b, h, i: (b, i, h))
    kblk = pl.BlockSpec((group, seq, V_DIM), lambda b, h, i: (b, 0, h))
    in_specs = [qblk, kblk, kblk]
    args = [q3, k3, v3]
    if past:
        layers = cache_k.shape[1]
        cblk = pl.BlockSpec((group, None, past, V_DIM), lambda b, h, i: (b, cache_layer, 0, h))
        in_specs += [cblk, cblk]
        args += [cache_k.reshape(st.batch, layers, past, hd),
                 cache_v.reshape(st.batch, layers, past, hd)]
    if rope is not None:
        cos2, sin2 = rope
        tq_tab = pl.BlockSpec((tq, V_DIM), lambda b, h, i: (i, 0))
        k_tab = pl.BlockSpec((seq, V_DIM), lambda b, h, i: (0, 0))
        in_specs += [tq_tab, tq_tab, k_tab, k_tab]
        args += [cos2, sin2, cos2, sin2]
    small = lambda shape: pl.BlockSpec(shape, lambda b, h, i: (0,) * len(shape))
    in_specs += [small((1, V_DIM)), small((1, V_DIM)), small((4, HEAD_DIM)), small((1, V_DIM))]
    args += [jnp.tile(q_g, 2).reshape(1, V_DIM), jnp.tile(k_g, 2).reshape(1, V_DIM), lam_vecs,
             subln_g.reshape(1, V_DIM)]
    out_specs = [qblk]
    out_shape = [jax.ShapeDtypeStruct((st.batch, seq, hd), BF16)]
    if not past:
        out_specs.append(kblk)
        out_shape.append(jax.ShapeDtypeStruct((st.batch, seq, hd), F32))
    outs = pl.pallas_call(
        functools.partial(_attention_kernel, past, rope is not None, lam_init),
        grid=(st.batch // group, N_HEADS, nq),
        in_specs=in_specs,
        out_specs=out_specs,
        out_shape=out_shape,
        scratch_shapes=[pltpu.VMEM((group, lk, V_DIM), BF16),
                        pltpu.VMEM((group, lk, V_DIM), BF16)],
        compiler_params=_params("parallel", "parallel", "arbitrary"),
        name="diff_attention",
    )(*args)
    return [o.reshape(st.rows, hd) for o in outs]


def _dft_matrices(seq):
    idx = np.arange(seq, dtype=np.int64)
    ang = (np.outer(idx, idx) % (2 * seq)).astype(np.float64) * (math.pi / seq)
    return jnp.asarray(np.cos(ang), dtype=BF16), jnp.asarray(np.sin(ang), dtype=BF16)


def _filter_features(seq):
    t = jnp.linspace(0.0, 1.0, seq, dtype=F32)[:, None]
    bands = (HY_EMB - 1) // 2
    w_ang = 2.0 * math.pi * jnp.arange(seq, dtype=F32)[:, None] / seq
    f = jnp.linspace(1e-4, bands - 1, bands, dtype=F32)[None, :]
    ang = f * w_ang
    feats = jnp.concatenate([t, jnp.cos(ang), -jnp.sin(ang)], axis=-1)
    return jnp.pad(feats, ((0, 0), (0, V7X_LANES - HY_EMB)))


def _decay_rates():
    min_decay = math.log(HY_DECAY_TARGET) / HY_SLOW_PCT
    max_decay = math.log(HY_DECAY_TARGET) / HY_FAST_PCT
    return jnp.linspace(min_decay, max_decay, D_MODEL, dtype=F32)[None, :]


def _rope_tables(seq):
    rows = seq // GRID_W
    row = jnp.repeat(jnp.arange(rows, dtype=F32), GRID_W)
    col = jnp.tile(jnp.arange(GRID_W, dtype=F32), rows)
    quarter = HEAD_DIM // 4
    inv = ROPE_BASE ** (-jnp.arange(quarter, dtype=F32) / quarter)

    def axis_angles(pos):
        a = pos[:, None] * inv[None, :]
        return jnp.concatenate([a, a], axis=-1)

    ang = jnp.concatenate([axis_angles(row), axis_angles(col)], axis=-1)
    sign = jnp.where((jnp.arange(HEAD_DIM) & quarter) == 0, -1.0, 1.0).astype(F32)
    return jnp.tile(jnp.cos(ang), (1, 2)), jnp.tile(jnp.sin(ang) * sign[None, :], (1, 2))


def kernel(x_prompt, x_sample, cache_k, cache_v, c, c_ctx, ada_w, ada_b, norm_g, hy_in_w, hy_in_b, hy_conv_w, hy_conv_b, hy_f_w1, hy_f_b1, hy_f_w2, hy_f_b2, hy_f_freq, hy_f_w3, hy_bias, hy_out_w, hy_out_b, at_qkv_w, at_q_g, at_k_g, at_lam, at_subln_g, at_out_w, dn_w_gu, dn_w_down, mo_router_w, mo_router_b, mo_w_gu, mo_w_down):
    d = D_MODEL
    batch, seq = x_prompt.shape[:2]
    dec_batch, dec_seq = x_sample.shape[:2]
    streams = [
        (_Stream(batch, seq, 0, False), x_prompt.reshape(batch * seq, d), None),
        (_Stream(dec_batch, dec_seq, 1, True, row0=batch * seq),
         x_sample.reshape(dec_batch * dec_seq, d), (cache_k, cache_v)),
    ]
    tokens = batch * seq + dec_batch * dec_seq

    cond = jnp.concatenate(
        [c_ctx[None, :], c, jnp.zeros((COND_ROWS - 1 - dec_batch, d), F32)], axis=0)
    mod = _adaln(cond, ada_w, ada_b)
    mod = mod.reshape(mod.shape[0], COND_ROWS, 1, 6 * d)

    in_w, out_w = hy_in_w[0].astype(BF16), hy_out_w[0].astype(BF16)
    qkv_w, at_out = at_qkv_w[0].astype(BF16), at_out_w[0].astype(BF16)
    dn_gu, dn_down = dn_w_gu[0].astype(BF16), dn_w_down[0].astype(BF16)
    mo_gu, mo_down = mo_w_gu[0], mo_w_down[0]
    w1_pad = jnp.pad(hy_f_w1[0], ((0, V7X_LANES - HY_EMB), (0, 0)))
    rw_pad = jnp.pad(mo_router_w[0], ((0, 0), (0, V7X_LANES - N_EXPERTS)))
    rb_pad = jnp.pad(mo_router_b[0], (0, V7X_LANES - N_EXPERTS)).reshape(1, V7X_LANES)
    deltas = _decay_rates()
    lam_init = 0.8 - 0.6 * math.exp(-0.3 * 1)

    routed = []
    pool = jnp.zeros((tokens * LANE_CHUNKS, V7X_LANES), F32)
    for st, x, cache in streams:
        blk = min(st.seq, HYENA_BLOCK)
        cmat, smat = _dft_matrices(blk)
        hf, hb = _filter_time(st.seq, _filter_features(st.seq), w1_pad, hy_f_b1[0], hy_f_w2[0],
                              hy_f_b2[0], hy_f_freq[0], hy_f_w3[0], deltas)
        kr, ki, kn = _filter_spectrum(st.seq, blk, hf, hb, cmat, smat)
        (proj,) = _mod_matmul(st, x, norm_g[0, 0], mod, 0, in_w, hy_in_b[0], 1, BF16)
        z = _hyena_core(st, blk, proj, hy_conv_w[0], hy_conv_b[0].reshape(1, 3 * d), kr, ki, kn,
                        hy_bias[0], cmat, smat)
        x, h2 = _out_proj(st, z, out_w, hy_out_b[0], x, mod, 0, norm_g[0, 1])
        x = _swiglu(st, h2, dn_gu, dn_down, x, mod, 0)

        q, k, v = _mod_matmul(st, x, norm_g[1, 0], mod, 1, qkv_w, None, 3, F32)
        if cache is None:
            o, new_k = _attention(st, q, k, v, at_q_g[0], at_k_g[0], at_lam[0], at_subln_g[0],
                                  lam_init)
            new_kv = (new_k, v)
        else:
            (o,) = _attention(st, q, k, v, at_q_g[0], at_k_g[0], at_lam[0], at_subln_g[0],
                              lam_init, cache[0], cache[1], _rope_tables(st.seq))
        x, pool, route = _out_proj(st, o, at_out, None, x, mod, 1, norm_g[1, 1],
                                   (rw_pad, rb_pad, pool))
        routed.append((st, x, route))

    order, pos, *tile_table = _moe_plan(
        jnp.concatenate([route[:, :4] for _, _, route in routed], axis=0), MOE_TILE)
    y_slots = _moe_ffn(pool, order, *tile_table, mo_gu, mo_down, MOE_TILE)
    results = [_moe_combine(st, y_slots, pos, route, x, mod, 1).reshape(st.batch, st.seq, d)
               for st, x, route in routed]

    new_k, new_v = new_kv
    return (results[0], results[1],
            new_k.reshape(batch, 1, seq, N_HEADS, 2, HEAD_DIM),
            new_v.reshape(batch, 1, seq, N_HEADS, V_DIM))
```

```python
import functools
import math

import numpy as np
import jax
import jax.numpy as jnp
from jax import lax
from jax.experimental import pallas as pl
from jax.experimental.pallas import tpu as pltpu

F32 = jnp.float32
BF16 = jnp.bfloat16
HIGHEST = lax.Precision.HIGHEST

D_MODEL = 1024
GRID_W = 64
HY_ORDER = 2
HY_EMB = 33
HY_FW = 64
HY_DECAY_TARGET = 1e-2
HY_FAST_PCT = 0.3
HY_SLOW_PCT = 1.5
N_HEADS = 8
HEAD_DIM = 64
V_DIM = 2 * HEAD_DIM
ROPE_BASE = 10000.0
D_FF = 2816
N_EXPERTS = 8
D_FF_EXPERT = 3584
EPS = 1e-6

V7X_LANES = 128
V7X_VMEM_LIMIT_BYTES = 56 * 1024 * 1024
LANE_CHUNKS = D_MODEL // V7X_LANES
COND_ROWS = 16
TOKEN_TILE = 1024
WIDE_OUT_TILE = 512
MOE_TILE = 1024
MOE_FF_CHUNK = 512
DENSE_FF_CHUNK = D_FF // 2
HYENA_BLOCK = 512
HYENA_GROUP_STEPS = 1024
NYQUIST_ROWS = 8
ATTENTION_QUERIES = 1024
ATTENTION_GROUP_QUERIES = 1024
ATTENTION_ROWS = 256


def _params(*semantics):
    return pltpu.CompilerParams(dimension_semantics=semantics,
                                vmem_limit_bytes=V7X_VMEM_LIMIT_BYTES)


def _resident(shape):
    zeros = (0,) * len(shape)
    return pl.BlockSpec(shape, lambda *_: zeros, pipeline_mode=pl.Buffered(1))


def _dot(a, b):
    return jnp.dot(a, b, preferred_element_type=F32)


def _dot_f32(a, b):
    return jnp.dot(a, b, precision=HIGHEST, preferred_element_type=F32)


def _modulate(x, g, shift, scale):
    ms = jnp.mean(x * x, axis=-1, keepdims=True)
    return (x * lax.rsqrt(ms + EPS) * g) * (1.0 + scale) + shift


def _adaln_kernel(cond_ref, w_ref, b_ref, o_ref):
    c = cond_ref[...]
    o_ref[...] = _dot_f32(c * jax.nn.sigmoid(c), w_ref[...]) + b_ref[...]


def _adaln(cond, ada_w, ada_b):
    depth, d, n = ada_w.shape
    tn = 1536
    return pl.pallas_call(
        _adaln_kernel,
        grid=(depth, n // tn),
        in_specs=[
            pl.BlockSpec((COND_ROWS, d), lambda i, j: (0, 0)),
            pl.BlockSpec((None, d, tn), lambda i, j: (i, 0, j)),
            pl.BlockSpec((None, 1, tn), lambda i, j: (i, 0, j)),
        ],
        out_specs=pl.BlockSpec((None, COND_ROWS, tn), lambda i, j: (i, 0, j)),
        out_shape=jax.ShapeDtypeStruct((depth, COND_ROWS, n), F32),
        compiler_params=_params("parallel", "parallel"),
        name="adaln",
    )(cond, ada_w, ada_b.reshape(depth, 1, n))


class _Stream:
    def __init__(self, batch, seq, cond_row0, per_seq_cond, row0=0, tile=TOKEN_TILE):
        self.batch, self.seq = batch, seq
        self.rows = batch * seq
        self.row0 = row0
        self._cond = (cond_row0, per_seq_cond)
        if per_seq_cond:
            self.tm = min(tile, seq)
            tiles_per_seq = seq // self.tm
            self.cond_row = lambda i: cond_row0 + i // tiles_per_seq
        else:
            self.tm = min(tile, self.rows)
            self.cond_row = lambda i: cond_row0
        self.tiles = self.rows // self.tm

    def retiled(self, tile):
        return _Stream(self.batch, self.seq, *self._cond, row0=self.row0, tile=tile)

    def mod_spec(self, layer, chunk):
        return pl.BlockSpec((None, None, 1, D_MODEL),
                            lambda i, *_: (layer, self.cond_row(i), 0, chunk))

    def row_spec(self, width):
        return pl.BlockSpec((self.tm, width), lambda i, *_: (i, 0))


def _mod_matmul_kernel(n_out, has_bias, x_ref, g_ref, sh_ref, sc_ref, w_ref, *rest):
    if has_bias:
        b_ref, out_refs = rest[0], rest[1:]
    else:
        b_ref, out_refs = None, rest
    h = _modulate(x_ref[...], g_ref[...], sh_ref[...], sc_ref[...])
    y = _dot(h.astype(BF16), w_ref[...])
    if has_bias:
        y = y + b_ref[...]
    width = y.shape[1] // n_out
    for k, o_ref in enumerate(out_refs):
        o_ref[...] = y[:, k * width:(k + 1) * width].astype(o_ref.dtype)


def _mod_matmul(st, x, norm_g, mod, layer, w_bf16, bias, n_out, out_dtype):
    st = st.retiled(WIDE_OUT_TILE)
    d, n = w_bf16.shape
    in_specs = [st.row_spec(d), _resident((1, d)), st.mod_spec(layer, 0), st.mod_spec(layer, 1),
                _resident((d, n))]
    args = [x, norm_g.reshape(1, d), mod, mod, w_bf16]
    if bias is not None:
        in_specs.append(_resident((1, n)))
        args.append(bias.reshape(1, n))
    width = n // n_out
    outs = pl.pallas_call(
        functools.partial(_mod_matmul_kernel, n_out, bias is not None),
        grid=(st.tiles,),
        in_specs=in_specs,
        out_specs=[st.row_spec(width)] * n_out,
        out_shape=[jax.ShapeDtypeStruct((st.rows, width), out_dtype)] * n_out,
        compiler_params=_params("parallel"),
        name="mod_matmul",
    )(*args)
    return outs


def _filter_time_kernel(feats_ref, w1_ref, b1_ref, w2_ref, b2_ref, fr_ref, w3_ref, dl_ref,
                        hf_ref, hb_ref):
    feats = feats_ref[...]
    fr = fr_ref[...]
    h = jnp.sin(fr[0:1] * (_dot_f32(feats, w1_ref[...]) + b1_ref[...]))
    h = jnp.sin(fr[1:2] * (_dot_f32(h, w2_ref[...]) + b2_ref[...]))
    h = _dot_f32(h, w3_ref[...])
    t = feats[:, 0:1]
    decay = jnp.exp(-t * jnp.abs(dl_ref[...]))
    half = HY_ORDER * D_MODEL
    decay2 = jnp.concatenate([decay] * HY_ORDER, axis=1)
    hf_ref[...] = h[:, :half] * decay2
    hb_ref[...] = jnp.where(t == 0.0, 0.0, h[:, half:] * decay2)


def _filter_time(seq, feats_pad, w1_pad, b1, w2, b2, freq, w3, deltas):
    tl = min(seq, 256)
    half = HY_ORDER * D_MODEL
    out = jax.ShapeDtypeStruct((seq, half), F32)
    return pl.pallas_call(
        _filter_time_kernel,
        grid=(seq // tl,),
        in_specs=[
            pl.BlockSpec((tl, V7X_LANES), lambda i: (i, 0)),
            _resident(w1_pad.shape), _resident((1, HY_FW)), _resident((HY_FW, HY_FW)),
            _resident((1, HY_FW)), _resident((2, HY_FW)), _resident(w3.shape),
            _resident((1, D_MODEL)),
        ],
        out_specs=[pl.BlockSpec((tl, half), lambda i: (i, 0))] * 2,
        out_shape=[out, out],
        compiler_params=_params("parallel"),
        name="hyena_filter_time",
    )(feats_pad, w1_pad, b1.reshape(1, HY_FW), w2, b2.reshape(1, HY_FW), freq, w3, deltas)


def _filter_spectrum_kernel(blk, nb, hf_ref, hb_ref, c_ref, s_ref, kr_ref, ki_ref, kn_ref):
    row = lax.broadcasted_iota(jnp.int32, (blk, 1), 0)
    sg = (1 - 2 * (row & 1)).astype(F32)
    wgt = jnp.where(row == 0, 1.0, 2.0) * (1.0 / (2 * blk))
    fwd, bwd = [], []
    for j in range(nb):
        rows = slice(j * blk, (j + 1) * blk)
        for ref, out in ((hf_ref, fwd), (hb_ref, bwd)):
            x = ref[rows, :]
            xb = x.astype(BF16)
            out.append((_dot(c_ref[...], xb), _dot(s_ref[...], xb),
                        jnp.sum(x * sg, axis=0, keepdims=True), x[0:1, :],
                        xb[0:1, :].astype(F32)))
    kn_ref[...] = jnp.zeros_like(kn_ref)
    for d in range(-(nb - 1), nb):
        if d == 0:
            (fc, fs, fn, _, _), (bc, bs, bn, _, _) = fwd[0], bwd[0]
            kr, ki, kn = fc + bc, bs - fs, fn + bn
        else:
            parts, im_sign = (fwd, -1.0) if d > 0 else (bwd, 1.0)
            c1, s1, n1, _, _ = parts[abs(d)]
            c0, s0, n0, x0, x0_seen = parts[abs(d) - 1]
            kr, ki, kn = c1 + sg * (c0 - x0_seen), im_sign * (s1 + sg * s0), n1 + n0 - x0
        slot = d + nb - 1
        kr_ref[slot * blk:(slot + 1) * blk, :] = wgt * kr
        ki_ref[slot * blk:(slot + 1) * blk, :] = wgt * ki
        kn_ref[slot:slot + 1, :] = kn * (1.0 / (2 * blk))


def _filter_spectrum(seq, blk, hf, hb, cmat, smat):
    nb = seq // blk
    half = hf.shape[1]
    tn = 256
    col = pl.BlockSpec((seq, tn), lambda j: (0, j))
    spec_rows = (2 * nb - 1) * blk
    out_col = pl.BlockSpec((spec_rows, tn), lambda j: (0, j))
    return pl.pallas_call(
        functools.partial(_filter_spectrum_kernel, blk, nb),
        grid=(half // tn,),
        in_specs=[col, col, _resident((blk, blk)), _resident((blk, blk))],
        out_specs=[out_col, out_col, pl.BlockSpec((NYQUIST_ROWS, tn), lambda j: (0, j))],
        out_shape=[jax.ShapeDtypeStruct((spec_rows, half), F32)] * 2
        + [jax.ShapeDtypeStruct((NYQUIST_ROWS, half), F32)],
        compiler_params=_params("parallel"),
        name="hyena_filter_spectrum",
    )(hf, hb, cmat, smat)


def _hyena_core_kernel(seq, blk, *refs):
    projections, shared, per_seq = refs[:3], refs[3:19], refs[19:]
    for s in range(projections[0].shape[0]):
        _hyena_core_one_sequence(seq, blk, *[r.at[s] for r in projections], *shared,
                                 *[r.at[s] for r in per_seq])


def _hyena_core_one_sequence(seq, blk, pv_ref, p1_ref, p2_ref, cwv_ref, cw1_ref, cw2_ref, cbv_ref,
                             cb1_ref, cb2_ref, kr0_ref, ki0_ref, kn0_ref, kr1_ref, ki1_ref,
                             kn1_ref, bias0_ref, bias1_ref, c_ref, s_ref, z_ref,
                             u_ref, ub_ref, gate_ref, a_ref, b_ref):
    row = lax.broadcasted_iota(jnp.int32, (blk, 1), 0)
    sign = (1 - 2 * (row & 1)).astype(F32)
    nb = seq // blk
    blocks = [slice(j * blk, (j + 1) * blk) for j in range(nb)]

    def short_conv(dst_ref, x_ref, w_ref, b_ref):
        x = x_ref[...].astype(F32)
        w = w_ref[...]
        time = lax.broadcasted_iota(jnp.int32, (seq, 1), 0)
        prev = jnp.where(time == 0, 0.0, pltpu.roll(x, 1, 0))
        nxt = jnp.where(time == seq - 1, 0.0, pltpu.roll(x, seq - 1, 0))
        dst_ref[...] = prev * w[0:1] + x * w[1:2] + nxt * w[2:3] + b_ref[...]

    def gated_long_conv(kr_ref, ki_ref, kn_ref, bias_ref, write):
        ub_ref[...] = u_ref[...].astype(BF16)
        nyq_in = []
        for rows in blocks:
            a_ref[rows, :] = _dot(c_ref[...], ub_ref[rows, :])
            b_ref[rows, :] = _dot(s_ref[...], ub_ref[rows, :])
            nyq_in.append(jnp.sum(u_ref[rows, :] * sign, axis=0, keepdims=True))
        for i, rows in enumerate(blocks):
            p = q = nyq = None
            for j, src in enumerate(blocks):
                slot = i - j + nb - 1
                kr = kr_ref[slot * blk:(slot + 1) * blk, :]
                ki = ki_ref[slot * blk:(slot + 1) * blk, :]
                a, b = a_ref[src, :], b_ref[src, :]
                pj, qj = a * kr + b * ki, b * kr - a * ki
                nj = nyq_in[j] * kn_ref[slot:slot + 1, :]
                p, q, nyq = (pj, qj, nj) if j == 0 else (p + pj, q + qj, nyq + nj)
            y = _dot(c_ref[...], p.astype(BF16)) + _dot(s_ref[...], q.astype(BF16))
            y = y + sign * nyq + u_ref[rows, :] * bias_ref[...]
            write(rows, gate_ref[rows, :] * y)

    def to_u(rows, val):
        u_ref[rows, :] = val

    def to_z(rows, val):
        z_ref[rows, :] = val.astype(z_ref.dtype)

    short_conv(u_ref, pv_ref, cwv_ref, cbv_ref)
    short_conv(gate_ref, p1_ref, cw1_ref, cb1_ref)
    gated_long_conv(kr0_ref, ki0_ref, kn0_ref, bias0_ref, to_u)
    short_conv(gate_ref, p2_ref, cw2_ref, cb2_ref)
    gated_long_conv(kr1_ref, ki1_ref, kn1_ref, bias1_ref, to_z)


def _hyena_core(st, blk, proj, conv_w, conv_b, kr, ki, kn, bias, cmat, smat):
    seq, d = st.seq, D_MODEL
    tn = 256
    nj = d // tn
    spec_rows = kr.shape[0]
    proj3 = proj.reshape(st.batch, seq, 3 * d)
    group = max(1, min(HYENA_GROUP_STEPS // seq, st.batch))
    assert st.batch % group == 0

    def part(k):
        return pl.BlockSpec((group, seq, tn), lambda j, b: (b, 0, k * nj + j))

    def cols(rows, k, buffers=2):
        return pl.BlockSpec((rows, tn), lambda j, b: (0, k * nj + j),
                            pipeline_mode=pl.Buffered(buffers))

    in_specs = ([part(0), part(1), part(2)]
                + [cols(3, k) for k in range(3)] + [cols(1, k) for k in range(3)]
                + [cols(spec_rows, 0, 1), cols(spec_rows, 0, 1), cols(NYQUIST_ROWS, 0),
                   cols(spec_rows, 1, 1), cols(spec_rows, 1, 1), cols(NYQUIST_ROWS, 1)]
                + [cols(1, 0), cols(1, 0)]
                + [_resident((blk, blk)), _resident((blk, blk))])
    z = pl.pallas_call(
        functools.partial(_hyena_core_kernel, seq, blk),
        grid=(nj, st.batch // group),
        in_specs=in_specs,
        out_specs=pl.BlockSpec((group, seq, tn), lambda j, b: (b, 0, j)),
        out_shape=jax.ShapeDtypeStruct((st.batch, seq, d), BF16),
        scratch_shapes=[pltpu.VMEM((group, seq, tn), F32), pltpu.VMEM((group, seq, tn), BF16),
                        pltpu.VMEM((group, seq, tn), F32), pltpu.VMEM((group, seq, tn), F32),
                        pltpu.VMEM((group, seq, tn), F32)],
        compiler_params=_params("parallel", "parallel"),
        name="hyena_core",
    )(proj3, proj3, proj3, conv_w, conv_w, conv_w, conv_b, conv_b, conv_b,
      kr, ki, kn, kr, ki, kn, bias[0:1], bias[1:2], cmat, smat)
    return z.reshape(st.rows, d)


def _route_kernel(z_ref, w_ref, x_ref, g1_ref, ng_ref, sh_ref, sc_ref, rw_hi_ref, rw_lo_ref,
                  rb_ref, pool_in_ref, x1_ref, h2_ref, route_ref):
    del pool_in_ref
    x1 = x_ref[...] + g1_ref[...] * _dot(z_ref[...], w_ref[...])
    x1_ref[...] = x1
    h2 = _modulate(x1, ng_ref[...], sh_ref[...], sc_ref[...])
    _store_token_tiles(h2_ref, h2)
    h_hi = h2.astype(BF16)
    h_lo = (h2 - h_hi.astype(F32)).astype(BF16)
    logits = (_dot(h_hi, rw_hi_ref[...])
              + (_dot(h_lo, rw_hi_ref[...]) + _dot(h_hi, rw_lo_ref[...])) + rb_ref[...])
    lane = lax.broadcasted_iota(jnp.int32, logits.shape, 1)
    neg = -jnp.inf
    logits = jnp.where(lane < N_EXPERTS, logits, neg)
    m1 = jnp.max(logits, axis=-1, keepdims=True)
    i1 = jnp.min(jnp.where(logits == m1, lane, V7X_LANES), axis=-1, keepdims=True)
    rest = jnp.where(lane == i1, neg, logits)
    m2 = jnp.max(rest, axis=-1, keepdims=True)
    i2 = jnp.min(jnp.where(rest == m2, lane, V7X_LANES), axis=-1, keepdims=True)
    e2 = jnp.exp(m2 - m1)
    den = 1.0 + e2
    route = jnp.where(lane == 0, i1.astype(F32), jnp.where(lane == 1, i2.astype(F32), 0.0))
    route_ref[...] = route + jnp.where(lane == 2, 1.0 / den, 0.0) + jnp.where(lane == 3, e2 / den, 0.0)


def _store_token_tiles(ref, val):
    rows = val.shape[0]
    for c in range(LANE_CHUNKS):
        ref[pl.ds(c, rows, stride=LANE_CHUNKS), :] = val[:, c * V7X_LANES:(c + 1) * V7X_LANES]


def _out_proj_route(st, z_bf16, w_bf16, x, mod, layer, norm_g, rw_pad, rb_pad, pool):
    d = D_MODEL
    rw_hi = rw_pad.astype(BF16)
    rw_lo = (rw_pad - rw_hi.astype(F32)).astype(BF16)
    tile0 = st.row0 // st.tm
    pool_index = 10
    return pl.pallas_call(
        _route_kernel,
        grid=(st.tiles,),
        in_specs=[st.row_spec(z_bf16.shape[1]), _resident(w_bf16.shape), st.row_spec(d),
                  st.mod_spec(layer, 2), _resident((1, d)), st.mod_spec(layer, 3),
                  st.mod_spec(layer, 4), _resident(rw_pad.shape), _resident(rw_pad.shape),
                  _resident(rb_pad.shape), pl.BlockSpec(memory_space=pl.ANY)],
        out_specs=[st.row_spec(d),
                   pl.BlockSpec((st.tm * LANE_CHUNKS, V7X_LANES), lambda i: (tile0 + i, 0)),
                   st.row_spec(V7X_LANES)],
        out_shape=[jax.ShapeDtypeStruct((st.rows, d), F32),
                   jax.ShapeDtypeStruct(pool.shape, F32),
                   jax.ShapeDtypeStruct((st.rows, V7X_LANES), F32)],
        input_output_aliases={pool_index: 1},
        compiler_params=_params("parallel"),
        name="out_proj_route",
    )(z_bf16, w_bf16, x, mod, norm_g.reshape(1, d), mod, mod, rw_hi, rw_lo, rb_pad, pool)


def _swiglu_part(x_bf16, wg_ref, wu_ref, wd_ref):
    g = _dot(x_bf16, wg_ref[...].astype(BF16))
    u = _dot(x_bf16, wu_ref[...].astype(BF16))
    return _dot((g * jax.nn.sigmoid(g) * u).astype(BF16), wd_ref[...].astype(BF16))


def _out_proj_swiglu_kernel(z_ref, w_ref, b_ref, x_ref, g1_ref, ng_ref, sh_ref, sc_ref, wg_ref,
                            wu_ref, wd_ref, g2_ref, o_ref, x1_ref, h2_ref, acc_ref):
    j = pl.program_id(1)

    @pl.when(j == 0)
    def _():
        x1 = x_ref[...] + g1_ref[...] * (_dot(z_ref[...], w_ref[...]) + b_ref[...])
        x1_ref[...] = x1
        h2_ref[...] = _modulate(x1, ng_ref[...], sh_ref[...], sc_ref[...]).astype(BF16)

    part = _swiglu_part(h2_ref[...], wg_ref, wu_ref, wd_ref)

    @pl.when(j == 0)
    def _():
        acc_ref[...] = part

    @pl.when(j > 0)
    def _():
        acc_ref[...] += part

    @pl.when(j == pl.num_programs(1) - 1)
    def _():
        o_ref[...] = x1_ref[...] + g2_ref[...] * acc_ref[...]


def _out_proj_swiglu(st, z_bf16, w_bf16, bias, x, mod, layer, norm_g, w_gu, w_down):
    st = st.retiled(WIDE_OUT_TILE)
    d, two_f = w_gu.shape
    f = two_f // 2
    tf = DENSE_FF_CHUNK
    nf = f // tf
    rows = pl.BlockSpec((st.tm, d), lambda i, j: (i, 0))
    return pl.pallas_call(
        _out_proj_swiglu_kernel,
        grid=(st.tiles, nf),
        in_specs=[
            rows, _resident(w_bf16.shape), _resident((1, d)), rows, st.mod_spec(layer, 2),
            _resident((1, d)), st.mod_spec(layer, 3), st.mod_spec(layer, 4),
            pl.BlockSpec((d, tf), lambda i, j: (0, j)),
            pl.BlockSpec((d, tf), lambda i, j: (0, nf + j)),
            pl.BlockSpec((tf, d), lambda i, j: (j, 0)),
            st.mod_spec(layer, 5),
        ],
        out_specs=rows,
        out_shape=jax.ShapeDtypeStruct((st.rows, d), F32),
        scratch_shapes=[pltpu.VMEM((st.tm, d), F32), pltpu.VMEM((st.tm, d), BF16),
                        pltpu.VMEM((st.tm, d), F32)],
        compiler_params=_params("parallel", "arbitrary"),
        name="out_proj_swiglu",
    )(z_bf16, w_bf16, bias.reshape(1, d), x, mod, norm_g.reshape(1, d), mod, mod,
      w_gu, w_gu, w_down, mod)


def _moe_plan(route, tm):
    tokens = route.shape[0]
    max_tiles = (2 * tokens) // tm + N_EXPERTS
    expert = route[:, :2].astype(jnp.int32).reshape(-1)
    experts = jnp.arange(N_EXPERTS, dtype=jnp.int32)
    onehot = (expert[:, None] == experts[None, :]).astype(jnp.int32)
    csum = jnp.cumsum(onehot, axis=0)
    rank = jnp.sum(csum * onehot, axis=1) - 1
    counts = csum[-1]
    tiles_per_expert = (counts + tm - 1) // tm
    tiles_end = jnp.cumsum(tiles_per_expert)
    first_tile = tiles_end - tiles_per_expert
    first_sorted = jnp.cumsum(counts) - counts
    pos = jnp.sum((first_tile * tm)[None, :] * onehot, axis=1) + rank
    token = jnp.arange(2 * tokens, dtype=jnp.int32) // 2
    _, order = lax.sort_key_val(pos, token)
    n_tiles = tiles_end[-1:]
    tile = jnp.minimum(jnp.arange(max_tiles, dtype=jnp.int32), n_tiles - 1)
    tile_expert = jnp.sum((tile[:, None] >= tiles_end[None, :]).astype(jnp.int32), axis=1)
    mine = (tile_expert[:, None] == experts[None, :]).astype(jnp.int32)
    done = (tile - jnp.sum(first_tile[None, :] * mine, axis=1)) * tm
    tile_base = jnp.sum(first_sorted[None, :] * mine, axis=1) + done
    tile_valid = jnp.clip(jnp.sum(counts[None, :] * mine, axis=1) - done, 1, tm)
    as_i32 = lambda a: a.astype(jnp.int32)
    return (as_i32(order), as_i32(pos), as_i32(tile_expert), as_i32(tile_base),
            as_i32(tile_valid), as_i32(n_tiles))


def _token_tile_copy(src_hbm, row, dst, slot_row, sem):
    return pltpu.make_async_copy(
        src_hbm.at[pl.ds(pl.multiple_of(row * LANE_CHUNKS, LANE_CHUNKS), LANE_CHUNKS)],
        dst.at[pl.ds(pl.multiple_of(slot_row * LANE_CHUNKS, LANE_CHUNKS), LANE_CHUNKS)],
        sem)


def _moe_ffn_kernel(tm, order_ref, te_ref, base_ref, valid_ref, nt_ref, x_hbm, wg_ref, wu_ref,
                    wd_ref, y_ref, xbuf, xd_ref, acc_ref, sem):
    t, j = pl.program_id(0), pl.program_id(1)
    last_j = pl.num_programs(1) - 1
    n_tiles = nt_ref[0]
    slot = t % 2
    rows = tm * LANE_CHUNKS

    def start_gather(tile, into):
        base, last = base_ref[tile], valid_ref[tile] - 1

        def body(r, carry):
            token = order_ref[base + jnp.minimum(r, last)]
            _token_tile_copy(x_hbm, token, xbuf.at[into], r, sem.at[into]).start()
            return carry
        lax.fori_loop(0, tm, body, 0, unroll=8)

    @pl.when((t == 0) & (j == 0))
    def _():
        start_gather(0, 0)

    @pl.when((j == 0) & (t < n_tiles))
    def _():
        pltpu.make_async_copy(x_hbm.at[pl.ds(0, rows)], xbuf.at[slot], sem.at[slot]).wait()
        for c in range(LANE_CHUNKS):
            xd_ref[:, c * V7X_LANES:(c + 1) * V7X_LANES] = (
                xbuf[slot, pl.ds(c, tm, stride=LANE_CHUNKS), :].astype(BF16))

        @pl.when(t + 1 < n_tiles)
        def _():
            start_gather(t + 1, 1 - slot)

    @pl.when(t < n_tiles)
    def _():
        part = _swiglu_part(xd_ref[...], wg_ref, wu_ref, wd_ref)

        @pl.when(j == 0)
        def _():
            acc_ref[...] = part

        @pl.when(j > 0)
        def _():
            acc_ref[...] += part

        @pl.when(j == last_j)
        def _():
            _store_token_tiles(y_ref, acc_ref[...])

    @pl.when((t >= n_tiles) & (j == last_j))
    def _():
        y_ref[...] = jnp.zeros_like(y_ref)


def _moe_ffn(pool, order, tile_expert, tile_base, tile_valid, n_tiles, w_gu, w_down, tm):
    n_e, d, two_f = w_gu.shape
    f = two_f // 2
    tf = MOE_FF_CHUNK
    nf = f // tf
    max_tiles = tile_expert.shape[0]
    rows = tm * LANE_CHUNKS
    grid_spec = pltpu.PrefetchScalarGridSpec(
        num_scalar_prefetch=5,
        grid=(max_tiles, nf),
        in_specs=[
            pl.BlockSpec(memory_space=pl.ANY),
            pl.BlockSpec((None, d, tf), lambda t, j, order, te, *_: (te[t], 0, j)),
            pl.BlockSpec((None, d, tf), lambda t, j, order, te, *_: (te[t], 0, nf + j)),
            pl.BlockSpec((None, tf, d), lambda t, j, order, te, *_: (te[t], j, 0)),
        ],
        out_specs=pl.BlockSpec((rows, V7X_LANES), lambda t, j, *_: (t, 0)),
        scratch_shapes=[pltpu.VMEM((2, rows, V7X_LANES), F32), pltpu.VMEM((tm, d), BF16),
                        pltpu.VMEM((tm, d), F32), pltpu.SemaphoreType.DMA((2,))],
    )
    return pl.pallas_call(
        functools.partial(_moe_ffn_kernel, tm),
        grid_spec=grid_spec,
        out_shape=jax.ShapeDtypeStruct((max_tiles * rows, V7X_LANES), F32),
        compiler_params=_params("arbitrary", "arbitrary"),
        name="moe_ffn",
    )(order, tile_expert, tile_base, tile_valid, n_tiles, pool, w_gu, w_gu, w_down)


def _moe_combine_kernel(tm, token0, pos_ref, y_hbm, route_ref, x_ref, g2_ref, o_ref, ybuf, sem):
    i = pl.program_id(0)
    slot = i % 2
    rows = tm * LANE_CHUNKS

    def start_gather(tile, into):
        def body(r, carry):
            a = 2 * (token0 + tile * tm + r)
            for k in range(2):
                _token_tile_copy(y_hbm, pos_ref[a + k], ybuf.at[into], k * tm + r,
                                 sem.at[into]).start(priority=k)
            return carry
        lax.fori_loop(0, tm, body, 0, unroll=4)

    @pl.when(i == 0)
    def _():
        start_gather(0, 0)

    pltpu.make_async_copy(y_hbm.at[pl.ds(0, 2 * rows)], ybuf.at[slot], sem.at[slot]).wait()

    @pl.when(i + 1 < pl.num_programs(0))
    def _():
        start_gather(i + 1, 1 - slot)

    route = route_ref[...]
    lane = lax.broadcasted_iota(jnp.int32, route.shape, 1)
    gate0 = jnp.sum(jnp.where(lane == 2, route, 0.0), axis=-1, keepdims=True)
    gate1 = jnp.sum(jnp.where(lane == 3, route, 0.0), axis=-1, keepdims=True)
    for c in range(LANE_CHUNKS):
        cols = slice(c * V7X_LANES, (c + 1) * V7X_LANES)
        y0 = ybuf[slot, pl.ds(c, tm, stride=LANE_CHUNKS), :]
        y1 = ybuf[slot, pl.ds(rows + c, tm, stride=LANE_CHUNKS), :]
        o_ref[:, cols] = x_ref[:, cols] + g2_ref[:, cols] * (gate0 * y0 + gate1 * y1)


def _moe_combine(st, y_slots, pos, route, x1, mod, layer):
    d = D_MODEL
    tm = st.tm
    rows = tm * LANE_CHUNKS
    grid_spec = pltpu.PrefetchScalarGridSpec(
        num_scalar_prefetch=1,
        grid=(st.tiles,),
        in_specs=[
            pl.BlockSpec(memory_space=pl.ANY),
            st.row_spec(V7X_LANES), st.row_spec(d), st.mod_spec(layer, 5),
        ],
        out_specs=st.row_spec(d),
        scratch_shapes=[pltpu.VMEM((2, 2 * rows, V7X_LANES), F32), pltpu.SemaphoreType.DMA((2,))],
    )
    return pl.pallas_call(
        functools.partial(_moe_combine_kernel, tm, st.row0),
        grid_spec=grid_spec,
        out_shape=jax.ShapeDtypeStruct((st.rows, d), F32),
        compiler_params=_params("arbitrary"),
        name="moe_combine",
    )(pos, y_slots, route, x1, mod)


def _head_rmsnorm(x, g2):
    lane = lax.broadcasted_iota(jnp.int32, x.shape, 1)
    lo = lane < HEAD_DIM
    sq = x * x
    s_lo = jnp.sum(jnp.where(lo, sq, 0.0), axis=-1, keepdims=True)
    s_hi = jnp.sum(jnp.where(lo, 0.0, sq), axis=-1, keepdims=True)
    ms = jnp.where(lo, s_lo, s_hi) * (1.0 / HEAD_DIM)
    return x * lax.rsqrt(ms + EPS) * g2


def _rope(x, cos, sin_signed):
    q4 = HEAD_DIM // 4
    lane = lax.broadcasted_iota(jnp.int32, x.shape, 1)
    first = (lane & q4) == 0
    width = x.shape[1]
    partner = jnp.where(first, pltpu.roll(x, width - q4, 1), pltpu.roll(x, q4, 1))
    return x * cos + partner * sin_signed


def _attention_kernel(past, use_rope, lam_init, *refs):
    refs = list(refs)
    n_seq = refs[0].shape[0]
    n_in = 3 + (2 if past else 0)
    seq_refs, refs = refs[:n_in], refs[n_in:]
    if use_rope:
        rope_refs, refs = refs[:4], refs[4:]
    else:
        rope_refs = []
    param_refs, refs = refs[:4], refs[4:]
    n_out = 1 if past else 2
    out_refs, (kall_ref, vall_ref) = refs[:n_out], refs[n_out:]
    for s in range(n_seq):
        _attention_one_sequence(past, use_rope, lam_init, *[r.at[s] for r in seq_refs],
                                *rope_refs, *param_refs, *[r.at[s] for r in out_refs],
                                kall_ref.at[s], vall_ref.at[s])


def _attention_one_sequence(past, use_rope, lam_init, *refs):
    refs = list(refs)
    q_ref, k_ref, v_ref = refs.pop(0), refs.pop(0), refs.pop(0)
    if past:
        ck_ref, cv_ref = refs.pop(0), refs.pop(0)
    if use_rope:
        cosq_ref, sinq_ref, cosk_ref, sink_ref = (refs.pop(0) for _ in range(4))
    qg_ref, kg_ref, lam_ref, sg_ref = (refs.pop(0) for _ in range(4))
    o_ref = refs.pop(0)
    nk_ref = None if past else refs.pop(0)
    kall_ref, vall_ref = refs

    qi = pl.program_id(2)

    @pl.when(qi == 0)
    def _():
        k = _head_rmsnorm(k_ref[...], kg_ref[...])
        if nk_ref is not None:
            nk_ref[...] = k
        if use_rope:
            k = _rope(k, cosk_ref[...], sink_ref[...])
        if past:
            kall_ref[0:past, :] = ck_ref[...].astype(BF16)
            vall_ref[0:past, :] = cv_ref[...].astype(BF16)
        kall_ref[past:, :] = k.astype(BF16)
        vall_ref[past:, :] = v_ref[...].astype(BF16)

    q = _head_rmsnorm(q_ref[...], qg_ref[...])
    if use_rope:
        q = _rope(q, cosq_ref[...], sinq_ref[...])
    q = q * (HEAD_DIM ** -0.5 * math.log2(math.e))
    tq = q.shape[0]
    nt = (((1,), (1,)), ((), ()))
    lv = lam_ref[...]
    lam = (jnp.exp(jnp.sum(lv[0:1] * lv[1:2], axis=-1, keepdims=True))
           - jnp.exp(jnp.sum(lv[2:3] * lv[3:4], axis=-1, keepdims=True)) + lam_init)

    def attend(qm):
        s = lax.dot_general(qm.astype(BF16), kall_ref[...], nt, preferred_element_type=F32)
        p = jnp.exp2(s - jnp.max(s, axis=-1, keepdims=True))
        norm = 1.0 / jnp.sum(p, axis=-1, keepdims=True)
        return _dot(p.astype(BF16), vall_ref[...]) * norm

    row_chunks = [slice(r, r + ATTENTION_ROWS) for r in range(0, tq, ATTENTION_ROWS)]
    lo = lax.broadcasted_iota(jnp.int32, q.shape, 1) < HEAD_DIM
    first = jnp.where(lo, q, 0.0)
    second = jnp.where(lo, 0.0, q)
    a0 = jnp.concatenate([attend(first[rows]) for rows in row_chunks], axis=0)
    a1 = jnp.concatenate([attend(second[rows]) for rows in row_chunks], axis=0)
    o = a0 - lam * a1
    ms = jnp.mean(o * o, axis=-1, keepdims=True)
    o = o * lax.rsqrt(ms + EPS) * sg_ref[...] * (1.0 - lam_init)
    o_ref[...] = o.astype(o_ref.dtype)


def _attention(st, q, k, v, q_g, k_g, lam_vecs, subln_g, lam_init, cache_k=None, cache_v=None,
               rope=None, cache_layer=0):
    seq, hd = st.seq, N_HEADS * V_DIM
    tq = min(seq, ATTENTION_QUERIES)
    nq = seq // tq
    group = max(1, min(ATTENTION_GROUP_QUERIES // seq, st.batch))
    assert st.batch % group == 0
    past = 0 if cache_k is None else cache_k.shape[2]
    lk = past + seq
    q3, k3, v3 = (a.reshape(st.batch, seq, hd) for a in (q, k, v))
    qblk = pl.BlockSpec((group, tq, V_DIM), lambda b, h, i: (b, i, h))
    kblk = pl.BlockSpec((group, seq, V_DIM), lambda b, h, i: (b, 0, h))
    in_specs = [qblk, kblk, kblk]
    args = [q3, k3, v3]
    if past:
        layers = cache_k.shape[1]
        cblk = pl.BlockSpec((group, None, past, V_DIM), lambda b, h, i: (b, cache_layer, 0, h))
        in_specs += [cblk, cblk]
        args += [cache_k.reshape(st.batch, layers, past, hd),
                 cache_v.reshape(st.batch, layers, past, hd)]
    if rope is not None:
        cos2, sin2 = rope
        tq_tab = pl.BlockSpec((tq, V_DIM), lambda b, h, i: (i, 0))
        k_tab = pl.BlockSpec((seq, V_DIM), lambda b, h, i: (0, 0))
        in_specs += [tq_tab, tq_tab, k_tab, k_tab]
        args += [cos2, sin2, cos2, sin2]
    small = lambda shape: pl.BlockSpec(shape, lambda b, h, i: (0,) * len(shape))
    in_specs += [small((1, V_DIM)), small((1, V_DIM)), small((4, HEAD_DIM)), small((1, V_DIM))]
    args += [jnp.tile(q_g, 2).reshape(1, V_DIM), jnp.tile(k_g, 2).reshape(1, V_DIM), lam_vecs,
             subln_g.reshape(1, V_DIM)]
    out_specs = [qblk]
    out_shape = [jax.ShapeDtypeStruct((st.batch, seq, hd), BF16)]
    if not past:
        out_specs.append(kblk)
        out_shape.append(jax.ShapeDtypeStruct((st.batch, seq, hd), F32))
    outs = pl.pallas_call(
        functools.partial(_attention_kernel, past, rope is not None, lam_init),
        grid=(st.batch // group, N_HEADS, nq),
        in_specs=in_specs,
        out_specs=out_specs,
        out_shape=out_shape,
        scratch_shapes=[pltpu.VMEM((group, lk, V_DIM), BF16),
                        pltpu.VMEM((group, lk, V_DIM), BF16)],
        compiler_params=_params("parallel", "parallel", "arbitrary"),
        name="diff_attention",
    )(*args)
    return [o.reshape(st.rows, hd) for o in outs]


def _dft_matrices(seq):
    idx = np.arange(seq, dtype=np.int64)
    ang = (np.outer(idx, idx) % (2 * seq)).astype(np.float64) * (math.pi / seq)
    return jnp.asarray(np.cos(ang), dtype=BF16), jnp.asarray(np.sin(ang), dtype=BF16)


def _filter_features(seq):
    t = jnp.linspace(0.0, 1.0, seq, dtype=F32)[:, None]
    bands = (HY_EMB - 1) // 2
    w_ang = 2.0 * math.pi * jnp.arange(seq, dtype=F32)[:, None] / seq
    f = jnp.linspace(1e-4, bands - 1, bands, dtype=F32)[None, :]
    ang = f * w_ang
    feats = jnp.concatenate([t, jnp.cos(ang), -jnp.sin(ang)], axis=-1)
    return jnp.pad(feats, ((0, 0), (0, V7X_LANES - HY_EMB)))


def _decay_rates():
    min_decay = math.log(HY_DECAY_TARGET) / HY_SLOW_PCT
    max_decay = math.log(HY_DECAY_TARGET) / HY_FAST_PCT
    return jnp.linspace(min_decay, max_decay, D_MODEL, dtype=F32)[None, :]


def _rope_tables(seq):
    rows = seq // GRID_W
    row = jnp.repeat(jnp.arange(rows, dtype=F32), GRID_W)
    col = jnp.tile(jnp.arange(GRID_W, dtype=F32), rows)
    quarter = HEAD_DIM // 4
    inv = ROPE_BASE ** (-jnp.arange(quarter, dtype=F32) / quarter)

    def axis_angles(pos):
        a = pos[:, None] * inv[None, :]
        return jnp.concatenate([a, a], axis=-1)

    ang = jnp.concatenate([axis_angles(row), axis_angles(col)], axis=-1)
    sign = jnp.where((jnp.arange(HEAD_DIM) & quarter) == 0, -1.0, 1.0).astype(F32)
    return jnp.tile(jnp.cos(ang), (1, 2)), jnp.tile(jnp.sin(ang) * sign[None, :], (1, 2))


def kernel(x_prompt, x_sample, cache_k, cache_v, c, c_ctx, ada_w, ada_b, norm_g, hy_in_w, hy_in_b, hy_conv_w, hy_conv_b, hy_f_w1, hy_f_b1, hy_f_w2, hy_f_b2, hy_f_freq, hy_f_w3, hy_bias, hy_out_w, hy_out_b, at_qkv_w, at_q_g, at_k_g, at_lam, at_subln_g, at_out_w, dn_w_gu, dn_w_down, mo_router_w, mo_router_b, mo_w_gu, mo_w_down):
    d = D_MODEL
    batch, seq = x_prompt.shape[:2]
    dec_batch, dec_seq = x_sample.shape[:2]
    streams = [
        (_Stream(batch, seq, 0, False), x_prompt.reshape(batch * seq, d), None),
        (_Stream(dec_batch, dec_seq, 1, True, row0=batch * seq),
         x_sample.reshape(dec_batch * dec_seq, d), (cache_k, cache_v)),
    ]
    tokens = batch * seq + dec_batch * dec_seq

    cond = jnp.concatenate(
        [c_ctx[None, :], c, jnp.zeros((COND_ROWS - 1 - dec_batch, d), F32)], axis=0)
    mod = _adaln(cond, ada_w, ada_b)
    mod = mod.reshape(mod.shape[0], COND_ROWS, 1, 6 * d)

    in_w, out_w = hy_in_w[0].astype(BF16), hy_out_w[0].astype(BF16)
    qkv_w, at_out = at_qkv_w[0].astype(BF16), at_out_w[0].astype(BF16)
    dn_gu, dn_down = dn_w_gu[0].astype(BF16), dn_w_down[0].astype(BF16)
    mo_gu, mo_down = mo_w_gu[0], mo_w_down[0]
    w1_pad = jnp.pad(hy_f_w1[0], ((0, V7X_LANES - HY_EMB), (0, 0)))
    rw_pad = jnp.pad(mo_router_w[0], ((0, 0), (0, V7X_LANES - N_EXPERTS)))
    rb_pad = jnp.pad(mo_router_b[0], (0, V7X_LANES - N_EXPERTS)).reshape(1, V7X_LANES)
    deltas = _decay_rates()
    lam_init = 0.8 - 0.6 * math.exp(-0.3 * 1)

    routed = []
    pool = jnp.zeros((tokens * LANE_CHUNKS, V7X_LANES), F32)
    for st, x, cache in streams:
        blk = min(st.seq, HYENA_BLOCK)
        cmat, smat = _dft_matrices(blk)
        hf, hb = _filter_time(st.seq, _filter_features(st.seq), w1_pad, hy_f_b1[0], hy_f_w2[0],
                              hy_f_b2[0], hy_f_freq[0], hy_f_w3[0], deltas)
        kr, ki, kn = _filter_spectrum(st.seq, blk, hf, hb, cmat, smat)
        (proj,) = _mod_matmul(st, x, norm_g[0, 0], mod, 0, in_w, hy_in_b[0], 1, BF16)
        z = _hyena_core(st, blk, proj, hy_conv_w[0], hy_conv_b[0].reshape(1, 3 * d), kr, ki, kn,
                        hy_bias[0], cmat, smat)
        x = _out_proj_swiglu(st, z, out_w, hy_out_b[0], x, mod, 0, norm_g[0, 1], dn_gu, dn_down)

        q, k, v = _mod_matmul(st, x, norm_g[1, 0], mod, 1, qkv_w, None, 3, F32)
        if cache is None:
            o, new_k = _attention(st, q, k, v, at_q_g[0], at_k_g[0], at_lam[0], at_subln_g[0],
                                  lam_init)
            new_kv = (new_k, v)
        else:
            (o,) = _attention(st, q, k, v, at_q_g[0], at_k_g[0], at_lam[0], at_subln_g[0],
                              lam_init, cache[0], cache[1], _rope_tables(st.seq))
        x, pool, route = _out_proj_route(st, o, at_out, x, mod, 1, norm_g[1, 1], rw_pad, rb_pad,
                                         pool)
        routed.append((st, x, route))

    order, pos, *tile_table = _moe_plan(
        jnp.concatenate([route[:, :4] for _, _, route in routed], axis=0), MOE_TILE)
    y_slots = _moe_ffn(pool, order, *tile_table, mo_gu, mo_down, MOE_TILE)
    results = [_moe_combine(st, y_slots, pos, route, x, mod, 1).reshape(st.batch, st.seq, d)
               for st, x, route in routed]

    new_k, new_v = new_kv
    return (results[0], results[1],
            new_k.reshape(batch, 1, seq, N_HEADS, 2, HEAD_DIM),
            new_v.reshape(batch, 1, seq, N_HEADS, V_DIM))
```

```python
import functools
import math

import numpy as np
import jax
import jax.numpy as jnp
from jax import lax
from jax.experimental import pallas as pl
from jax.experimental.pallas import tpu as pltpu

F32 = jnp.float32
BF16 = jnp.bfloat16
HIGHEST = lax.Precision.HIGHEST

D_MODEL = 1024
GRID_W = 64
HY_ORDER = 2
HY_EMB = 33
HY_FW = 64
HY_DECAY_TARGET = 1e-2
HY_FAST_PCT = 0.3
HY_SLOW_PCT = 1.5
N_HEADS = 8
HEAD_DIM = 64
V_DIM = 2 * HEAD_DIM
ROPE_BASE = 10000.0
D_FF = 2816
N_EXPERTS = 8
D_FF_EXPERT = 3584
EPS = 1e-6

V7X_LANES = 128
V7X_VMEM_LIMIT_BYTES = 56 * 1024 * 1024
LANE_CHUNKS = D_MODEL // V7X_LANES
COND_ROWS = 16
TOKEN_TILE = 1024
WIDE_OUT_TILE = 512
MOE_TILE = 1024
MOE_FF_CHUNK = 512
DENSE_FF_CHUNK = D_FF // 2
HYENA_BLOCK = 512
HYENA_GROUP_STEPS = 1024
NYQUIST_ROWS = 8
ATTENTION_QUERIES = 1024
ATTENTION_GROUP_QUERIES = 1024
ATTENTION_ROWS = 256


def _params(*semantics):
    return pltpu.CompilerParams(dimension_semantics=semantics,
                                vmem_limit_bytes=V7X_VMEM_LIMIT_BYTES)


def _resident(shape):
    zeros = (0,) * len(shape)
    return pl.BlockSpec(shape, lambda *_: zeros, pipeline_mode=pl.Buffered(1))


def _dot(a, b):
    return jnp.dot(a, b, preferred_element_type=F32)


def _dot_f32(a, b):
    return jnp.dot(a, b, precision=HIGHEST, preferred_element_type=F32)


def _modulate(x, g, shift, scale):
    ms = jnp.mean(x * x, axis=-1, keepdims=True)
    return (x * lax.rsqrt(ms + EPS) * g) * (1.0 + scale) + shift


def _adaln_kernel(cond_ref, w_ref, b_ref, o_ref):
    c = cond_ref[...]
    o_ref[...] = _dot_f32(c * jax.nn.sigmoid(c), w_ref[...]) + b_ref[...]


def _adaln(cond, ada_w, ada_b):
    depth, d, n = ada_w.shape
    tn = 1536
    return pl.pallas_call(
        _adaln_kernel,
        grid=(depth, n // tn),
        in_specs=[
            pl.BlockSpec((COND_ROWS, d), lambda i, j: (0, 0)),
            pl.BlockSpec((None, d, tn), lambda i, j: (i, 0, j)),
            pl.BlockSpec((None, 1, tn), lambda i, j: (i, 0, j)),
        ],
        out_specs=pl.BlockSpec((None, COND_ROWS, tn), lambda i, j: (i, 0, j)),
        out_shape=jax.ShapeDtypeStruct((depth, COND_ROWS, n), F32),
        compiler_params=_params("parallel", "parallel"),
        name="adaln",
    )(cond, ada_w, ada_b.reshape(depth, 1, n))


class _Stream:
    def __init__(self, batch, seq, cond_row0, per_seq_cond, row0=0, tile=TOKEN_TILE):
        self.batch, self.seq = batch, seq
        self.rows = batch * seq
        self.row0 = row0
        self._cond = (cond_row0, per_seq_cond)
        if per_seq_cond:
            self.tm = min(tile, seq)
            tiles_per_seq = seq // self.tm
            self.cond_row = lambda i: cond_row0 + i // tiles_per_seq
        else:
            self.tm = min(tile, self.rows)
            self.cond_row = lambda i: cond_row0
        self.tiles = self.rows // self.tm

    def retiled(self, tile):
        return _Stream(self.batch, self.seq, *self._cond, row0=self.row0, tile=tile)

    def mod_spec(self, layer, chunk):
        return pl.BlockSpec((None, None, 1, D_MODEL),
                            lambda i, *_: (layer, self.cond_row(i), 0, chunk))

    def row_spec(self, width):
        return pl.BlockSpec((self.tm, width), lambda i, *_: (i, 0))


def _mod_matmul_kernel(n_out, has_bias, x_ref, g_ref, sh_ref, sc_ref, w_ref, *rest):
    if has_bias:
        b_ref, out_refs = rest[0], rest[1:]
    else:
        b_ref, out_refs = None, rest
    h = _modulate(x_ref[...], g_ref[...], sh_ref[...], sc_ref[...])
    y = _dot(h.astype(BF16), w_ref[...])
    if has_bias:
        y = y + b_ref[...]
    width = y.shape[1] // n_out
    for k, o_ref in enumerate(out_refs):
        o_ref[...] = y[:, k * width:(k + 1) * width].astype(o_ref.dtype)


def _mod_matmul(st, x, norm_g, mod, layer, w_bf16, bias, n_out, out_dtype):
    st = st.retiled(WIDE_OUT_TILE)
    d, n = w_bf16.shape
    in_specs = [st.row_spec(d), _resident((1, d)), st.mod_spec(layer, 0), st.mod_spec(layer, 1),
                _resident((d, n))]
    args = [x, norm_g.reshape(1, d), mod, mod, w_bf16]
    if bias is not None:
        in_specs.append(_resident((1, n)))
        args.append(bias.reshape(1, n))
    width = n // n_out
    outs = pl.pallas_call(
        functools.partial(_mod_matmul_kernel, n_out, bias is not None),
        grid=(st.tiles,),
        in_specs=in_specs,
        out_specs=[st.row_spec(width)] * n_out,
        out_shape=[jax.ShapeDtypeStruct((st.rows, width), out_dtype)] * n_out,
        compiler_params=_params("parallel"),
        name="mod_matmul",
    )(*args)
    return outs


def _filter_time_kernel(feats_ref, w1_ref, b1_ref, w2_ref, b2_ref, fr_ref, w3_ref, dl_ref,
                        hf_ref, hb_ref):
    feats = feats_ref[...]
    fr = fr_ref[...]
    h = jnp.sin(fr[0:1] * (_dot_f32(feats, w1_ref[...]) + b1_ref[...]))
    h = jnp.sin(fr[1:2] * (_dot_f32(h, w2_ref[...]) + b2_ref[...]))
    h = _dot_f32(h, w3_ref[...])
    t = feats[:, 0:1]
    decay = jnp.exp(-t * jnp.abs(dl_ref[...]))
    half = HY_ORDER * D_MODEL
    decay2 = jnp.concatenate([decay] * HY_ORDER, axis=1)
    hf_ref[...] = h[:, :half] * decay2
    hb_ref[...] = jnp.where(t == 0.0, 0.0, h[:, half:] * decay2)


def _filter_time(seq, feats_pad, w1_pad, b1, w2, b2, freq, w3, deltas):
    tl = min(seq, 256)
    half = HY_ORDER * D_MODEL
    out = jax.ShapeDtypeStruct((seq, half), F32)
    return pl.pallas_call(
        _filter_time_kernel,
        grid=(seq // tl,),
        in_specs=[
            pl.BlockSpec((tl, V7X_LANES), lambda i: (i, 0)),
            _resident(w1_pad.shape), _resident((1, HY_FW)), _resident((HY_FW, HY_FW)),
            _resident((1, HY_FW)), _resident((2, HY_FW)), _resident(w3.shape),
            _resident((1, D_MODEL)),
        ],
        out_specs=[pl.BlockSpec((tl, half), lambda i: (i, 0))] * 2,
        out_shape=[out, out],
        compiler_params=_params("parallel"),
        name="hyena_filter_time",
    )(feats_pad, w1_pad, b1.reshape(1, HY_FW), w2, b2.reshape(1, HY_FW), freq, w3, deltas)


def _filter_spectrum_kernel(blk, nb, hf_ref, hb_ref, c_ref, s_ref, kr_ref, ki_ref, kn_ref):
    row = lax.broadcasted_iota(jnp.int32, (blk, 1), 0)
    sg = (1 - 2 * (row & 1)).astype(F32)
    wgt = jnp.where(row == 0, 1.0, 2.0) * (1.0 / (2 * blk))
    fwd, bwd = [], []
    for j in range(nb):
        rows = slice(j * blk, (j + 1) * blk)
        for ref, out in ((hf_ref, fwd), (hb_ref, bwd)):
            x = ref[rows, :]
            xb = x.astype(BF16)
            out.append((_dot(c_ref[...], xb), _dot(s_ref[...], xb),
                        jnp.sum(x * sg, axis=0, keepdims=True), x[0:1, :],
                        xb[0:1, :].astype(F32)))
    kn_ref[...] = jnp.zeros_like(kn_ref)
    for d in range(-(nb - 1), nb):
        if d == 0:
            (fc, fs, fn, _, _), (bc, bs, bn, _, _) = fwd[0], bwd[0]
            kr, ki, kn = fc + bc, bs - fs, fn + bn
        else:
            parts, im_sign = (fwd, -1.0) if d > 0 else (bwd, 1.0)
            c1, s1, n1, _, _ = parts[abs(d)]
            c0, s0, n0, x0, x0_seen = parts[abs(d) - 1]
            kr, ki, kn = c1 + sg * (c0 - x0_seen), im_sign * (s1 + sg * s0), n1 + n0 - x0
        slot = d + nb - 1
        kr_ref[slot * blk:(slot + 1) * blk, :] = (wgt * kr).astype(kr_ref.dtype)
        ki_ref[slot * blk:(slot + 1) * blk, :] = (wgt * ki).astype(ki_ref.dtype)
        kn_ref[slot:slot + 1, :] = kn * (1.0 / (2 * blk))


def _filter_spectrum(seq, blk, hf, hb, cmat, smat):
    nb = seq // blk
    half = hf.shape[1]
    tn = 256
    col = pl.BlockSpec((seq, tn), lambda j: (0, j))
    spec_rows = (2 * nb - 1) * blk
    out_col = pl.BlockSpec((spec_rows, tn), lambda j: (0, j))
    return pl.pallas_call(
        functools.partial(_filter_spectrum_kernel, blk, nb),
        grid=(half // tn,),
        in_specs=[col, col, _resident((blk, blk)), _resident((blk, blk))],
        out_specs=[out_col, out_col, pl.BlockSpec((NYQUIST_ROWS, tn), lambda j: (0, j))],
        out_shape=[jax.ShapeDtypeStruct((spec_rows, half), BF16)] * 2
        + [jax.ShapeDtypeStruct((NYQUIST_ROWS, half), F32)],
        compiler_params=_params("parallel"),
        name="hyena_filter_spectrum",
    )(hf, hb, cmat, smat)


def _hyena_core_kernel(seq, blk, *refs):
    projections, shared, per_seq = refs[:3], refs[3:19], refs[19:]
    for s in range(projections[0].shape[0]):
        _hyena_core_one_sequence(seq, blk, *[r.at[s] for r in projections], *shared,
                                 *[r.at[s] for r in per_seq])


def _hyena_core_one_sequence(seq, blk, pv_ref, p1_ref, p2_ref, cwv_ref, cw1_ref, cw2_ref, cbv_ref,
                             cb1_ref, cb2_ref, kr0_ref, ki0_ref, kn0_ref, kr1_ref, ki1_ref,
                             kn1_ref, bias0_ref, bias1_ref, c_ref, s_ref, z_ref,
                             u_ref, ub_ref, gate_ref, a_ref, b_ref):
    row = lax.broadcasted_iota(jnp.int32, (blk, 1), 0)
    sign = (1 - 2 * (row & 1)).astype(F32)
    nb = seq // blk
    blocks = [slice(j * blk, (j + 1) * blk) for j in range(nb)]

    def short_conv(dst_ref, x_ref, w_ref, b_ref):
        x = x_ref[...].astype(F32)
        w = w_ref[...]
        time = lax.broadcasted_iota(jnp.int32, (seq, 1), 0)
        prev = jnp.where(time == 0, 0.0, pltpu.roll(x, 1, 0))
        nxt = jnp.where(time == seq - 1, 0.0, pltpu.roll(x, seq - 1, 0))
        dst_ref[...] = prev * w[0:1] + x * w[1:2] + nxt * w[2:3] + b_ref[...]

    def gated_long_conv(kr_ref, ki_ref, kn_ref, bias_ref, write):
        ub_ref[...] = u_ref[...].astype(BF16)
        nyq_in = []
        for rows in blocks:
            a_ref[rows, :] = _dot(c_ref[...], ub_ref[rows, :]).astype(BF16)
            b_ref[rows, :] = _dot(s_ref[...], ub_ref[rows, :]).astype(BF16)
            nyq_in.append(jnp.sum(u_ref[rows, :] * sign, axis=0, keepdims=True))
        for i, rows in enumerate(blocks):
            p = q = nyq = None
            for j, src in enumerate(blocks):
                slot = i - j + nb - 1
                kr = kr_ref[slot * blk:(slot + 1) * blk, :]
                ki = ki_ref[slot * blk:(slot + 1) * blk, :]
                a, b = a_ref[src, :], b_ref[src, :]
                pj, qj = a * kr + b * ki, b * kr - a * ki
                nj = nyq_in[j] * kn_ref[slot:slot + 1, :]
                p, q, nyq = (pj, qj, nj) if j == 0 else (p + pj, q + qj, nyq + nj)
            y = _dot(c_ref[...], p) + _dot(s_ref[...], q)
            y = y + sign * nyq + u_ref[rows, :] * bias_ref[...]
            write(rows, gate_ref[rows, :] * y)

    def to_u(rows, val):
        u_ref[rows, :] = val

    def to_z(rows, val):
        z_ref[rows, :] = val.astype(z_ref.dtype)

    short_conv(u_ref, pv_ref, cwv_ref, cbv_ref)
    short_conv(gate_ref, p1_ref, cw1_ref, cb1_ref)
    gated_long_conv(kr0_ref, ki0_ref, kn0_ref, bias0_ref, to_u)
    short_conv(gate_ref, p2_ref, cw2_ref, cb2_ref)
    gated_long_conv(kr1_ref, ki1_ref, kn1_ref, bias1_ref, to_z)


def _hyena_core(st, blk, proj, conv_w, conv_b, kr, ki, kn, bias, cmat, smat):
    seq, d = st.seq, D_MODEL
    tn = 256
    nj = d // tn
    spec_rows = kr.shape[0]
    proj3 = proj.reshape(st.batch, seq, 3 * d)
    group = max(1, min(HYENA_GROUP_STEPS // seq, st.batch))
    assert st.batch % group == 0

    def part(k):
        return pl.BlockSpec((group, seq, tn), lambda j, b: (b, 0, k * nj + j))

    def cols(rows, k, buffers=2):
        return pl.BlockSpec((rows, tn), lambda j, b: (0, k * nj + j),
                            pipeline_mode=pl.Buffered(buffers))

    in_specs = ([part(0), part(1), part(2)]
                + [cols(3, k) for k in range(3)] + [cols(1, k) for k in range(3)]
                + [cols(spec_rows, 0, 1), cols(spec_rows, 0, 1), cols(NYQUIST_ROWS, 0),
                   cols(spec_rows, 1, 1), cols(spec_rows, 1, 1), cols(NYQUIST_ROWS, 1)]
                + [cols(1, 0), cols(1, 0)]
                + [_resident((blk, blk)), _resident((blk, blk))])
    z = pl.pallas_call(
        functools.partial(_hyena_core_kernel, seq, blk),
        grid=(nj, st.batch // group),
        in_specs=in_specs,
        out_specs=pl.BlockSpec((group, seq, tn), lambda j, b: (b, 0, j)),
        out_shape=jax.ShapeDtypeStruct((st.batch, seq, d), BF16),
        scratch_shapes=[pltpu.VMEM((group, seq, tn), F32), pltpu.VMEM((group, seq, tn), BF16),
                        pltpu.VMEM((group, seq, tn), F32), pltpu.VMEM((group, seq, tn), BF16),
                        pltpu.VMEM((group, seq, tn), BF16)],
        compiler_params=_params("parallel", "parallel"),
        name="hyena_core",
    )(proj3, proj3, proj3, conv_w, conv_w, conv_w, conv_b, conv_b, conv_b,
      kr, ki, kn, kr, ki, kn, bias[0:1], bias[1:2], cmat, smat)
    return z.reshape(st.rows, d)


def _route_kernel(first_tiles, *refs):
    n = len(first_tiles)
    z_refs, x_refs, shared = refs[:n], refs[n:2 * n], refs[2 * n:2 * n + 8]
    h2_ref, x1_refs, route_refs = refs[2 * n + 8], refs[2 * n + 9:3 * n + 9], refs[3 * n + 9:]
    i = pl.program_id(0)
    for s in range(n):
        lo = first_tiles[s]
        hi = first_tiles[s + 1] if s + 1 < n else pl.num_programs(0)

        @pl.when((i >= lo) & (i < hi))
        def _(s=s):
            _route_tile(z_refs[s], x_refs[s], *shared, x1_refs[s], h2_ref, route_refs[s])


def _route_tile(z_ref, x_ref, w_ref, g1_ref, ng_ref, sh_ref, sc_ref, rw_hi_ref, rw_lo_ref, rb_ref,
                x1_ref, h2_ref, route_ref):
    x1 = x_ref[...] + g1_ref[...] * _dot(z_ref[...], w_ref[...])
    x1_ref[...] = x1
    h2 = _modulate(x1, ng_ref[...], sh_ref[...], sc_ref[...])
    _store_token_tiles(h2_ref, h2)
    h_hi = h2.astype(BF16)
    h_lo = (h2 - h_hi.astype(F32)).astype(BF16)
    logits = (_dot(h_hi, rw_hi_ref[...])
              + (_dot(h_lo, rw_hi_ref[...]) + _dot(h_hi, rw_lo_ref[...])) + rb_ref[...])
    lane = lax.broadcasted_iota(jnp.int32, logits.shape, 1)
    neg = -jnp.inf
    logits = jnp.where(lane < N_EXPERTS, logits, neg)
    m1 = jnp.max(logits, axis=-1, keepdims=True)
    i1 = jnp.min(jnp.where(logits == m1, lane, V7X_LANES), axis=-1, keepdims=True)
    rest = jnp.where(lane == i1, neg, logits)
    m2 = jnp.max(rest, axis=-1, keepdims=True)
    i2 = jnp.min(jnp.where(rest == m2, lane, V7X_LANES), axis=-1, keepdims=True)
    e2 = jnp.exp(m2 - m1)
    den = 1.0 + e2
    route = jnp.where(lane == 0, i1.astype(F32), jnp.where(lane == 1, i2.astype(F32), 0.0))
    route_ref[...] = route + jnp.where(lane == 2, 1.0 / den, 0.0) + jnp.where(lane == 3, e2 / den, 0.0)


def _store_token_tiles(ref, val):
    rows = val.shape[0]
    for c in range(LANE_CHUNKS):
        ref[pl.ds(c, rows, stride=LANE_CHUNKS), :] = val[:, c * V7X_LANES:(c + 1) * V7X_LANES]


def _out_proj_route(streams, w_bf16, mod, layer, norm_g, rw_pad, rb_pad):
    d = D_MODEL
    streams = [(st.retiled(WIDE_OUT_TILE), z, x) for st, z, x in streams]
    tm = streams[0][0].tm
    assert all(st.tm == tm and st.row0 % tm == 0 for st, _, _ in streams)
    first = [st.row0 // tm for st, _, _ in streams]
    tiles = sum(st.tiles for st, _, _ in streams)
    rw_hi = rw_pad.astype(BF16)
    rw_lo = (rw_pad - rw_hi.astype(F32)).astype(BF16)

    def local(s, i):
        return jnp.clip(i - first[s], 0, streams[s][0].tiles - 1)

    def cond_row(i):
        row = streams[0][0].cond_row(local(0, i))
        for s in range(1, len(streams)):
            row = jnp.where(i >= first[s], streams[s][0].cond_row(local(s, i)), row)
        return row

    def rows_of(s, width):
        return pl.BlockSpec((tm, width), lambda i: (local(s, i), 0))

    def mod_chunk(chunk):
        return pl.BlockSpec((None, None, 1, d), lambda i: (layer, cond_row(i), 0, chunk))

    n = len(streams)
    in_specs = ([rows_of(s, streams[s][1].shape[1]) for s in range(n)]
                + [rows_of(s, d) for s in range(n)]
                + [_resident(w_bf16.shape), mod_chunk(2), _resident((1, d)), mod_chunk(3),
                   mod_chunk(4), _resident(rw_pad.shape), _resident(rw_pad.shape),
                   _resident(rb_pad.shape)])
    out_specs = ([pl.BlockSpec((tm * LANE_CHUNKS, V7X_LANES), lambda i: (i, 0))]
                 + [rows_of(s, d) for s in range(n)] + [rows_of(s, V7X_LANES) for s in range(n)])
    out_shape = ([jax.ShapeDtypeStruct((tiles * tm * LANE_CHUNKS, V7X_LANES), F32)]
                 + [jax.ShapeDtypeStruct((st.rows, d), F32) for st, _, _ in streams]
                 + [jax.ShapeDtypeStruct((st.rows, V7X_LANES), F32) for st, _, _ in streams])
    outs = pl.pallas_call(
        functools.partial(_route_kernel, first),
        grid=(tiles,),
        in_specs=in_specs,
        out_specs=out_specs,
        out_shape=out_shape,
        compiler_params=_params("arbitrary"),
        name="out_proj_route",
    )(*[z for _, z, _ in streams], *[x for _, _, x in streams], w_bf16, mod,
      norm_g.reshape(1, d), mod, mod, rw_hi, rw_lo, rb_pad)
    return outs[0], outs[1:1 + n], outs[1 + n:]


def _swiglu_part(x_bf16, wg_ref, wu_ref, wd_ref):
    g = _dot(x_bf16, wg_ref[...].astype(BF16))
    u = _dot(x_bf16, wu_ref[...].astype(BF16))
    return _dot((g * jax.nn.sigmoid(g) * u).astype(BF16), wd_ref[...].astype(BF16))


def _out_proj_swiglu_kernel(z_ref, w_ref, b_ref, x_ref, g1_ref, ng_ref, sh_ref, sc_ref, wg_ref,
                            wu_ref, wd_ref, g2_ref, o_ref, x1_ref, h2_ref, acc_ref):
    j = pl.program_id(1)

    @pl.when(j == 0)
    def _():
        x1 = x_ref[...] + g1_ref[...] * (_dot(z_ref[...], w_ref[...]) + b_ref[...])
        x1_ref[...] = x1
        h2_ref[...] = _modulate(x1, ng_ref[...], sh_ref[...], sc_ref[...]).astype(BF16)

    part = _swiglu_part(h2_ref[...], wg_ref, wu_ref, wd_ref)

    @pl.when(j == 0)
    def _():
        acc_ref[...] = part

    @pl.when(j > 0)
    def _():
        acc_ref[...] += part

    @pl.when(j == pl.num_programs(1) - 1)
    def _():
        o_ref[...] = x1_ref[...] + g2_ref[...] * acc_ref[...]


def _out_proj_swiglu(st, z_bf16, w_bf16, bias, x, mod, layer, norm_g, w_gu, w_down):
    st = st.retiled(WIDE_OUT_TILE)
    d, two_f = w_gu.shape
    f = two_f // 2
    tf = DENSE_FF_CHUNK
    nf = f // tf
    rows = pl.BlockSpec((st.tm, d), lambda i, j: (i, 0))
    return pl.pallas_call(
        _out_proj_swiglu_kernel,
        grid=(st.tiles, nf),
        in_specs=[
            rows, _resident(w_bf16.shape), _resident((1, d)), rows, st.mod_spec(layer, 2),
            _resident((1, d)), st.mod_spec(layer, 3), st.mod_spec(layer, 4),
            pl.BlockSpec((d, tf), lambda i, j: (0, j)),
            pl.BlockSpec((d, tf), lambda i, j: (0, nf + j)),
            pl.BlockSpec((tf, d), lambda i, j: (j, 0)),
            st.mod_spec(layer, 5),
        ],
        out_specs=rows,
        out_shape=jax.ShapeDtypeStruct((st.rows, d), F32),
        scratch_shapes=[pltpu.VMEM((st.tm, d), F32), pltpu.VMEM((st.tm, d), BF16),
                        pltpu.VMEM((st.tm, d), F32)],
        compiler_params=_params("parallel", "arbitrary"),
        name="out_proj_swiglu",
    )(z_bf16, w_bf16, bias.reshape(1, d), x, mod, norm_g.reshape(1, d), mod, mod,
      w_gu, w_gu, w_down, mod)


def _moe_plan(route, tm):
    tokens = route.shape[0]
    max_tiles = (2 * tokens) // tm + N_EXPERTS
    expert = route[:, :2].astype(jnp.int32).reshape(-1)
    experts = jnp.arange(N_EXPERTS, dtype=jnp.int32)
    onehot = (expert[:, None] == experts[None, :]).astype(jnp.int32)
    csum = jnp.cumsum(onehot, axis=0)
    rank = jnp.sum(csum * onehot, axis=1) - 1
    counts = csum[-1]
    tiles_per_expert = (counts + tm - 1) // tm
    tiles_end = jnp.cumsum(tiles_per_expert)
    first_tile = tiles_end - tiles_per_expert
    first_sorted = jnp.cumsum(counts) - counts
    pos = jnp.sum((first_tile * tm)[None, :] * onehot, axis=1) + rank
    token = jnp.arange(2 * tokens, dtype=jnp.int32) // 2
    _, order = lax.sort_key_val(pos, token)
    n_tiles = tiles_end[-1:]
    tile = jnp.minimum(jnp.arange(max_tiles, dtype=jnp.int32), n_tiles - 1)
    tile_expert = jnp.sum((tile[:, None] >= tiles_end[None, :]).astype(jnp.int32), axis=1)
    mine = (tile_expert[:, None] == experts[None, :]).astype(jnp.int32)
    done = (tile - jnp.sum(first_tile[None, :] * mine, axis=1)) * tm
    tile_base = jnp.sum(first_sorted[None, :] * mine, axis=1) + done
    tile_valid = jnp.clip(jnp.sum(counts[None, :] * mine, axis=1) - done, 1, tm)
    as_i32 = lambda a: a.astype(jnp.int32)
    return (as_i32(order), as_i32(pos), as_i32(tile_expert), as_i32(tile_base),
            as_i32(tile_valid), as_i32(n_tiles))


def _token_tile_copy(src_hbm, row, dst, slot_row, sem):
    return pltpu.make_async_copy(
        src_hbm.at[pl.ds(pl.multiple_of(row * LANE_CHUNKS, LANE_CHUNKS), LANE_CHUNKS)],
        dst.at[pl.ds(pl.multiple_of(slot_row * LANE_CHUNKS, LANE_CHUNKS), LANE_CHUNKS)],
        sem)


def _moe_ffn_kernel(tm, order_ref, te_ref, base_ref, valid_ref, nt_ref, x_hbm, wg_ref, wu_ref,
                    wd_ref, y_ref, xbuf, xd_ref, acc_ref, sem):
    t, j = pl.program_id(0), pl.program_id(1)
    last_j = pl.num_programs(1) - 1
    n_tiles = nt_ref[0]
    slot = t % 2
    rows = tm * LANE_CHUNKS

    def start_gather(tile, into):
        base, last = base_ref[tile], valid_ref[tile] - 1

        def body(r, carry):
            token = order_ref[base + jnp.minimum(r, last)]
            _token_tile_copy(x_hbm, token, xbuf.at[into], r, sem.at[into]).start()
            return carry
        lax.fori_loop(0, tm, body, 0, unroll=8)

    @pl.when((t == 0) & (j == 0))
    def _():
        start_gather(0, 0)

    @pl.when((j == 0) & (t < n_tiles))
    def _():
        pltpu.make_async_copy(x_hbm.at[pl.ds(0, rows)], xbuf.at[slot], sem.at[slot]).wait()
        for c in range(LANE_CHUNKS):
            xd_ref[:, c * V7X_LANES:(c + 1) * V7X_LANES] = (
                xbuf[slot, pl.ds(c, tm, stride=LANE_CHUNKS), :].astype(BF16))

        @pl.when(t + 1 < n_tiles)
        def _():
            start_gather(t + 1, 1 - slot)

    @pl.when(t < n_tiles)
    def _():
        part = _swiglu_part(xd_ref[...], wg_ref, wu_ref, wd_ref)

        @pl.when(j == 0)
        def _():
            acc_ref[...] = part

        @pl.when(j > 0)
        def _():
            acc_ref[...] += part

        @pl.when(j == last_j)
        def _():
            _store_token_tiles(y_ref, acc_ref[...])

    @pl.when((t >= n_tiles) & (j == last_j))
    def _():
        y_ref[...] = jnp.zeros_like(y_ref)


def _moe_ffn(pool, order, tile_expert, tile_base, tile_valid, n_tiles, w_gu, w_down, tm):
    n_e, d, two_f = w_gu.shape
    f = two_f // 2
    tf = MOE_FF_CHUNK
    nf = f // tf
    max_tiles = tile_expert.shape[0]
    rows = tm * LANE_CHUNKS
    grid_spec = pltpu.PrefetchScalarGridSpec(
        num_scalar_prefetch=5,
        grid=(max_tiles, nf),
        in_specs=[
            pl.BlockSpec(memory_space=pl.ANY),
            pl.BlockSpec((None, d, tf), lambda t, j, order, te, *_: (te[t], 0, j)),
            pl.BlockSpec((None, d, tf), lambda t, j, order, te, *_: (te[t], 0, nf + j)),
            pl.BlockSpec((None, tf, d), lambda t, j, order, te, *_: (te[t], j, 0)),
        ],
        out_specs=pl.BlockSpec((rows, V7X_LANES), lambda t, j, *_: (t, 0)),
        scratch_shapes=[pltpu.VMEM((2, rows, V7X_LANES), F32), pltpu.VMEM((tm, d), BF16),
                        pltpu.VMEM((tm, d), F32), pltpu.SemaphoreType.DMA((2,))],
    )
    return pl.pallas_call(
        functools.partial(_moe_ffn_kernel, tm),
        grid_spec=grid_spec,
        out_shape=jax.ShapeDtypeStruct((max_tiles * rows, V7X_LANES), F32),
        compiler_params=_params("arbitrary", "arbitrary"),
        name="moe_ffn",
    )(order, tile_expert, tile_base, tile_valid, n_tiles, pool, w_gu, w_gu, w_down)


def _moe_combine_kernel(tm, token0, pos_ref, y_hbm, route_ref, x_ref, g2_ref, o_ref, ybuf, sem):
    i = pl.program_id(0)
    slot = i % 2
    rows = tm * LANE_CHUNKS

    def start_gather(tile, into):
        def body(r, carry):
            a = 2 * (token0 + tile * tm + r)
            for k in range(2):
                _token_tile_copy(y_hbm, pos_ref[a + k], ybuf.at[into], k * tm + r,
                                 sem.at[into]).start(priority=k)
            return carry
        lax.fori_loop(0, tm, body, 0, unroll=4)

    @pl.when(i == 0)
    def _():
        start_gather(0, 0)

    pltpu.make_async_copy(y_hbm.at[pl.ds(0, 2 * rows)], ybuf.at[slot], sem.at[slot]).wait()

    @pl.when(i + 1 < pl.num_programs(0))
    def _():
        start_gather(i + 1, 1 - slot)

    route = route_ref[...]
    lane = lax.broadcasted_iota(jnp.int32, route.shape, 1)
    gate0 = jnp.sum(jnp.where(lane == 2, route, 0.0), axis=-1, keepdims=True)
    gate1 = jnp.sum(jnp.where(lane == 3, route, 0.0), axis=-1, keepdims=True)
    for c in range(LANE_CHUNKS):
        cols = slice(c * V7X_LANES, (c + 1) * V7X_LANES)
        y0 = ybuf[slot, pl.ds(c, tm, stride=LANE_CHUNKS), :]
        y1 = ybuf[slot, pl.ds(rows + c, tm, stride=LANE_CHUNKS), :]
        o_ref[:, cols] = x_ref[:, cols] + g2_ref[:, cols] * (gate0 * y0 + gate1 * y1)


def _moe_combine(st, y_slots, pos, route, x1, mod, layer):
    d = D_MODEL
    tm = st.tm
    rows = tm * LANE_CHUNKS
    grid_spec = pltpu.PrefetchScalarGridSpec(
        num_scalar_prefetch=1,
        grid=(st.tiles,),
        in_specs=[
            pl.BlockSpec(memory_space=pl.ANY),
            st.row_spec(V7X_LANES), st.row_spec(d), st.mod_spec(layer, 5),
        ],
        out_specs=st.row_spec(d),
        scratch_shapes=[pltpu.VMEM((2, 2 * rows, V7X_LANES), F32), pltpu.SemaphoreType.DMA((2,))],
    )
    return pl.pallas_call(
        functools.partial(_moe_combine_kernel, tm, st.row0),
        grid_spec=grid_spec,
        out_shape=jax.ShapeDtypeStruct((st.rows, d), F32),
        compiler_params=_params("arbitrary"),
        name="moe_combine",
    )(pos, y_slots, route, x1, mod)


def _head_rmsnorm(x, g2):
    lane = lax.broadcasted_iota(jnp.int32, x.shape, 1)
    lo = lane < HEAD_DIM
    sq = x * x
    s_lo = jnp.sum(jnp.where(lo, sq, 0.0), axis=-1, keepdims=True)
    s_hi = jnp.sum(jnp.where(lo, 0.0, sq), axis=-1, keepdims=True)
    ms = jnp.where(lo, s_lo, s_hi) * (1.0 / HEAD_DIM)
    return x * lax.rsqrt(ms + EPS) * g2


def _rope(x, cos, sin_signed):
    q4 = HEAD_DIM // 4
    lane = lax.broadcasted_iota(jnp.int32, x.shape, 1)
    first = (lane & q4) == 0
    width = x.shape[1]
    partner = jnp.where(first, pltpu.roll(x, width - q4, 1), pltpu.roll(x, q4, 1))
    return x * cos + partner * sin_signed


def _attention_kernel(past, use_rope, lam_init, *refs):
    refs = list(refs)
    n_seq = refs[0].shape[0]
    n_in = 3 + (2 if past else 0)
    seq_refs, refs = refs[:n_in], refs[n_in:]
    if use_rope:
        rope_refs, refs = refs[:4], refs[4:]
    else:
        rope_refs = []
    param_refs, refs = refs[:4], refs[4:]
    n_out = 1 if past else 2
    out_refs, (kall_ref, vall_ref) = refs[:n_out], refs[n_out:]
    for s in range(n_seq):
        _attention_one_sequence(past, use_rope, lam_init, *[r.at[s] for r in seq_refs],
                                *rope_refs, *param_refs, *[r.at[s] for r in out_refs],
                                kall_ref.at[s], vall_ref.at[s])


def _attention_one_sequence(past, use_rope, lam_init, *refs):
    refs = list(refs)
    q_ref, k_ref, v_ref = refs.pop(0), refs.pop(0), refs.pop(0)
    if past:
        ck_ref, cv_ref = refs.pop(0), refs.pop(0)
    if use_rope:
        cosq_ref, sinq_ref, cosk_ref, sink_ref = (refs.pop(0) for _ in range(4))
    qg_ref, kg_ref, lam_ref, sg_ref = (refs.pop(0) for _ in range(4))
    o_ref = refs.pop(0)
    nk_ref = None if past else refs.pop(0)
    kall_ref, vall_ref = refs

    qi = pl.program_id(2)

    @pl.when(qi == 0)
    def _():
        k = _head_rmsnorm(k_ref[...], kg_ref[...])
        if nk_ref is not None:
            nk_ref[...] = k
        if use_rope:
            k = _rope(k, cosk_ref[...], sink_ref[...])
        if past:
            kall_ref[0:past, :] = ck_ref[...].astype(BF16)
            vall_ref[0:past, :] = cv_ref[...].astype(BF16)
        kall_ref[past:, :] = k.astype(BF16)
        vall_ref[past:, :] = v_ref[...].astype(BF16)

    q = _head_rmsnorm(q_ref[...], qg_ref[...])
    if use_rope:
        q = _rope(q, cosq_ref[...], sinq_ref[...])
    q = q * (HEAD_DIM ** -0.5 * math.log2(math.e))
    tq = q.shape[0]
    nt = (((1,), (1,)), ((), ()))
    lv = lam_ref[...]
    lam = (jnp.exp(jnp.sum(lv[0:1] * lv[1:2], axis=-1, keepdims=True))
           - jnp.exp(jnp.sum(lv[2:3] * lv[3:4], axis=-1, keepdims=True)) + lam_init)

    def attend(qm):
        s = lax.dot_general(qm.astype(BF16), kall_ref[...], nt, preferred_element_type=F32)
        p = jnp.exp2(s - jnp.max(s, axis=-1, keepdims=True))
        norm = 1.0 / jnp.sum(p, axis=-1, keepdims=True)
        return _dot(p.astype(BF16), vall_ref[...]) * norm

    row_chunks = [slice(r, r + ATTENTION_ROWS) for r in range(0, tq, ATTENTION_ROWS)]
    lo = lax.broadcasted_iota(jnp.int32, q.shape, 1) < HEAD_DIM
    first = jnp.where(lo, q, 0.0)
    second = jnp.where(lo, 0.0, q)
    a0 = jnp.concatenate([attend(first[rows]) for rows in row_chunks], axis=0)
    a1 = jnp.concatenate([attend(second[rows]) for rows in row_chunks], axis=0)
    o = a0 - lam * a1
    ms = jnp.mean(o * o, axis=-1, keepdims=True)
    o = o * lax.rsqrt(ms + EPS) * sg_ref[...] * (1.0 - lam_init)
    o_ref[...] = o.astype(o_ref.dtype)


def _attention(st, q, k, v, q_g, k_g, lam_vecs, subln_g, lam_init, cache_k=None, cache_v=None,
               rope=None, cache_layer=0):
    seq, hd = st.seq, N_HEADS * V_DIM
    tq = min(seq, ATTENTION_QUERIES)
    nq = seq // tq
    group = max(1, min(ATTENTION_GROUP_QUERIES // seq, st.batch))
    assert st.batch % group == 0
    past = 0 if cache_k is None else cache_k.shape[2]
    lk = past + seq
    q3, k3, v3 = (a.reshape(st.batch, seq, hd) for a in (q, k, v))
    qblk = pl.BlockSpec((group, tq, V_DIM), lambda b, h, i: (b, i, h))
    kblk = pl.BlockSpec((group, seq, V_DIM), lambda b, h, i: (b, 0, h))
    in_specs = [qblk, kblk, kblk]
    args = [q3, k3, v3]
    if past:
        layers = cache_k.shape[1]
        cblk = pl.BlockSpec((group, None, past, V_DIM), lambda b, h, i: (b, cache_layer, 0, h))
        in_specs += [cblk, cblk]
        args += [cache_k.reshape(st.batch, layers, past, hd),
                 cache_v.reshape(st.batch, layers, past, hd)]
    if rope is not None:
        cos2, sin2 = rope
        tq_tab = pl.BlockSpec((tq, V_DIM), lambda b, h, i: (i, 0))
        k_tab = pl.BlockSpec((seq, V_DIM), lambda b, h, i: (0, 0))
        in_specs += [tq_tab, tq_tab, k_tab, k_tab]
        args += [cos2, sin2, cos2, sin2]
    small = lambda shape: pl.BlockSpec(shape, lambda b, h, i: (0,) * len(shape))
    in_specs += [small((1, V_DIM)), small((1, V_DIM)), small((4, HEAD_DIM)), small((1, V_DIM))]
    args += [jnp.tile(q_g, 2).reshape(1, V_DIM), jnp.tile(k_g, 2).reshape(1, V_DIM), lam_vecs,
             subln_g.reshape(1, V_DIM)]
    out_specs = [qblk]
    out_shape = [jax.ShapeDtypeStruct((st.batch, seq, hd), BF16)]
    if not past:
        out_specs.append(kblk)
        out_shape.append(jax.ShapeDtypeStruct((st.batch, seq, hd), F32))
    outs = pl.pallas_call(
        functools.partial(_attention_kernel, past, rope is not None, lam_init),
        grid=(st.batch // group, N_HEADS, nq),
        in_specs=in_specs,
        out_specs=out_specs,
        out_shape=out_shape,
        scratch_shapes=[pltpu.VMEM((group, lk, V_DIM), BF16),
                        pltpu.VMEM((group, lk, V_DIM), BF16)],
        compiler_params=_params("parallel", "parallel", "arbitrary"),
        name="diff_attention",
    )(*args)
    return [o.reshape(st.rows, hd) for o in outs]


def _dft_matrices(seq):
    idx = np.arange(seq, dtype=np.int64)
    ang = (np.outer(idx, idx) % (2 * seq)).astype(np.float64) * (math.pi / seq)
    return jnp.asarray(np.cos(ang), dtype=BF16), jnp.asarray(np.sin(ang), dtype=BF16)


def _filter_features(seq):
    t = jnp.linspace(0.0, 1.0, seq, dtype=F32)[:, None]
    bands = (HY_EMB - 1) // 2
    w_ang = 2.0 * math.pi * jnp.arange(seq, dtype=F32)[:, None] / seq
    f = jnp.linspace(1e-4, bands - 1, bands, dtype=F32)[None, :]
    ang = f * w_ang
    feats = jnp.concatenate([t, jnp.cos(ang), -jnp.sin(ang)], axis=-1)
    return jnp.pad(feats, ((0, 0), (0, V7X_LANES - HY_EMB)))


def _decay_rates():
    min_decay = math.log(HY_DECAY_TARGET) / HY_SLOW_PCT
    max_decay = math.log(HY_DECAY_TARGET) / HY_FAST_PCT
    return jnp.linspace(min_decay, max_decay, D_MODEL, dtype=F32)[None, :]


def _rope_tables(seq):
    rows = seq // GRID_W
    row = jnp.repeat(jnp.arange(rows, dtype=F32), GRID_W)
    col = jnp.tile(jnp.arange(GRID_W, dtype=F32), rows)
    quarter = HEAD_DIM // 4
    inv = ROPE_BASE ** (-jnp.arange(quarter, dtype=F32) / quarter)

    def axis_angles(pos):
        a = pos[:, None] * inv[None, :]
        return jnp.concatenate([a, a], axis=-1)

    ang = jnp.concatenate([axis_angles(row), axis_angles(col)], axis=-1)
    sign = jnp.where((jnp.arange(HEAD_DIM) & quarter) == 0, -1.0, 1.0).astype(F32)
    return jnp.tile(jnp.cos(ang), (1, 2)), jnp.tile(jnp.sin(ang) * sign[None, :], (1, 2))


def kernel(x_prompt, x_sample, cache_k, cache_v, c, c_ctx, ada_w, ada_b, norm_g, hy_in_w, hy_in_b, hy_conv_w, hy_conv_b, hy_f_w1, hy_f_b1, hy_f_w2, hy_f_b2, hy_f_freq, hy_f_w3, hy_bias, hy_out_w, hy_out_b, at_qkv_w, at_q_g, at_k_g, at_lam, at_subln_g, at_out_w, dn_w_gu, dn_w_down, mo_router_w, mo_router_b, mo_w_gu, mo_w_down):
    d = D_MODEL
    batch, seq = x_prompt.shape[:2]
    dec_batch, dec_seq = x_sample.shape[:2]
    streams = [
        (_Stream(batch, seq, 0, False), x_prompt.reshape(batch * seq, d), None),
        (_Stream(dec_batch, dec_seq, 1, True, row0=batch * seq),
         x_sample.reshape(dec_batch * dec_seq, d), (cache_k, cache_v)),
    ]

    cond = jnp.concatenate(
        [c_ctx[None, :], c, jnp.zeros((COND_ROWS - 1 - dec_batch, d), F32)], axis=0)
    mod = _adaln(cond, ada_w, ada_b)
    mod = mod.reshape(mod.shape[0], COND_ROWS, 1, 6 * d)

    in_w, out_w = hy_in_w[0].astype(BF16), hy_out_w[0].astype(BF16)
    qkv_w, at_out = at_qkv_w[0].astype(BF16), at_out_w[0].astype(BF16)
    dn_gu, dn_down = dn_w_gu[0].astype(BF16), dn_w_down[0].astype(BF16)
    mo_gu, mo_down = mo_w_gu[0], mo_w_down[0]
    w1_pad = jnp.pad(hy_f_w1[0], ((0, V7X_LANES - HY_EMB), (0, 0)))
    rw_pad = jnp.pad(mo_router_w[0], ((0, 0), (0, V7X_LANES - N_EXPERTS)))
    rb_pad = jnp.pad(mo_router_b[0], (0, V7X_LANES - N_EXPERTS)).reshape(1, V7X_LANES)
    deltas = _decay_rates()
    lam_init = 0.8 - 0.6 * math.exp(-0.3 * 1)

    attended = []
    for st, x, cache in streams:
        blk = min(st.seq, HYENA_BLOCK)
        cmat, smat = _dft_matrices(blk)
        hf, hb = _filter_time(st.seq, _filter_features(st.seq), w1_pad, hy_f_b1[0], hy_f_w2[0],
                              hy_f_b2[0], hy_f_freq[0], hy_f_w3[0], deltas)
        kr, ki, kn = _filter_spectrum(st.seq, blk, hf, hb, cmat, smat)
        (proj,) = _mod_matmul(st, x, norm_g[0, 0], mod, 0, in_w, hy_in_b[0], 1, BF16)
        z = _hyena_core(st, blk, proj, hy_conv_w[0], hy_conv_b[0].reshape(1, 3 * d), kr, ki, kn,
                        hy_bias[0], cmat, smat)
        x = _out_proj_swiglu(st, z, out_w, hy_out_b[0], x, mod, 0, norm_g[0, 1], dn_gu, dn_down)

        q, k, v = _mod_matmul(st, x, norm_g[1, 0], mod, 1, qkv_w, None, 3, F32)
        if cache is None:
            o, new_k = _attention(st, q, k, v, at_q_g[0], at_k_g[0], at_lam[0], at_subln_g[0],
                                  lam_init)
            new_kv = (new_k, v)
        else:
            (o,) = _attention(st, q, k, v, at_q_g[0], at_k_g[0], at_lam[0], at_subln_g[0],
                              lam_init, cache[0], cache[1], _rope_tables(st.seq))
        attended.append((st, o, x))

    pool, x1s, routes = _out_proj_route(attended, at_out, mod, 1, norm_g[1, 1], rw_pad, rb_pad)
    order, pos, *tile_table = _moe_plan(
        jnp.concatenate([route[:, :4] for route in routes], axis=0), MOE_TILE)
    y_slots = _moe_ffn(pool, order, *tile_table, mo_gu, mo_down, MOE_TILE)
    results = [_moe_combine(st, y_slots, pos, route, x1, mod, 1).reshape(st.batch, st.seq, d)
               for (st, _, _), x1, route in zip(attended, x1s, routes)]

    new_k, new_v = new_kv
    return (results[0], results[1],
            new_k.reshape(batch, 1, seq, N_HEADS, 2, HEAD_DIM),
            new_v.reshape(batch, 1, seq, N_HEADS, V_DIM))
```

```python
import functools
import math

import numpy as np
import jax
import jax.numpy as jnp
from jax import lax
from jax.experimental import pallas as pl
from jax.experimental.pallas import tpu as pltpu

F32 = jnp.float32
BF16 = jnp.bfloat16

D_MODEL = 1024
GRID_W = 64
HY_ORDER = 2
HY_EMB = 33
HY_FW = 64
HY_DECAY_TARGET = 1e-2
HY_FAST_PCT = 0.3
HY_SLOW_PCT = 1.5
N_HEADS = 8
HEAD_DIM = 64
V_DIM = 2 * HEAD_DIM
ROPE_BASE = 10000.0
D_FF = 2816
N_EXPERTS = 8
D_FF_EXPERT = 3584
EPS = 1e-6

V7X_LANES = 128
V7X_VMEM_LIMIT_BYTES = 56 * 1024 * 1024
LANE_CHUNKS = D_MODEL // V7X_LANES
COND_ROWS = 16
TOKEN_TILE = 1024
WIDE_OUT_TILE = 512
MOE_TILE = 1024
MOE_FF_CHUNK = 512
DENSE_FF_CHUNK = D_FF // 2
HYENA_BLOCK = 512
HYENA_GROUP_STEPS = 1024
NYQUIST_ROWS = 8
ATTENTION_QUERIES = 1024
ATTENTION_GROUP_QUERIES = 1024
ATTENTION_ROWS = 256


def _params(*semantics):
    return pltpu.CompilerParams(dimension_semantics=semantics,
                                vmem_limit_bytes=V7X_VMEM_LIMIT_BYTES)


def _resident(shape):
    zeros = (0,) * len(shape)
    return pl.BlockSpec(shape, lambda *_: zeros, pipeline_mode=pl.Buffered(1))


def _dot(a, b):
    return jnp.dot(a, b, preferred_element_type=F32)


def _dot_f32(a, b):
    a_hi, b_hi = a.astype(BF16), b.astype(BF16)
    a_lo = (a - a_hi.astype(F32)).astype(BF16)
    b_lo = (b - b_hi.astype(F32)).astype(BF16)
    return _dot(a_hi, b_hi) + (_dot(a_lo, b_hi) + _dot(a_hi, b_lo))


def _modulate(x, g, shift, scale):
    ms = jnp.mean(x * x, axis=-1, keepdims=True)
    return (x * lax.rsqrt(ms + EPS) * g) * (1.0 + scale) + shift


def _adaln_kernel(cond_ref, w_ref, b_ref, o_ref):
    c = cond_ref[...]
    o_ref[...] = _dot_f32(c * jax.nn.sigmoid(c), w_ref[...]) + b_ref[...]


def _adaln(cond, ada_w, ada_b):
    depth, d, n = ada_w.shape
    tn = 1536
    return pl.pallas_call(
        _adaln_kernel,
        grid=(depth, n // tn),
        in_specs=[
            pl.BlockSpec((COND_ROWS, d), lambda i, j: (0, 0)),
            pl.BlockSpec((None, d, tn), lambda i, j: (i, 0, j)),
            pl.BlockSpec((None, 1, tn), lambda i, j: (i, 0, j)),
        ],
        out_specs=pl.BlockSpec((None, COND_ROWS, tn), lambda i, j: (i, 0, j)),
        out_shape=jax.ShapeDtypeStruct((depth, COND_ROWS, n), F32),
        compiler_params=_params("parallel", "parallel"),
        name="adaln",
    )(cond, ada_w, ada_b.reshape(depth, 1, n))


class _Stream:
    def __init__(self, batch, seq, cond_row0, per_seq_cond, row0=0, tile=TOKEN_TILE):
        self.batch, self.seq = batch, seq
        self.rows = batch * seq
        self.row0 = row0
        self._cond = (cond_row0, per_seq_cond)
        if per_seq_cond:
            self.tm = min(tile, seq)
            tiles_per_seq = seq // self.tm
            self.cond_row = lambda i: cond_row0 + i // tiles_per_seq
        else:
            self.tm = min(tile, self.rows)
            self.cond_row = lambda i: cond_row0
        self.tiles = self.rows // self.tm

    def retiled(self, tile):
        return _Stream(self.batch, self.seq, *self._cond, row0=self.row0, tile=tile)

    def mod_spec(self, layer, chunk):
        return pl.BlockSpec((None, None, 1, D_MODEL),
                            lambda i, *_: (layer, self.cond_row(i), 0, chunk))

    def row_spec(self, width):
        return pl.BlockSpec((self.tm, width), lambda i, *_: (i, 0))


def _mod_matmul_kernel(n_out, has_bias, x_ref, g_ref, sh_ref, sc_ref, w_ref, *rest):
    if has_bias:
        b_ref, out_refs = rest[0], rest[1:]
    else:
        b_ref, out_refs = None, rest
    h = _modulate(x_ref[...], g_ref[...], sh_ref[...], sc_ref[...])
    y = _dot(h.astype(BF16), w_ref[...])
    if has_bias:
        y = y + b_ref[...]
    width = y.shape[1] // n_out
    for k, o_ref in enumerate(out_refs):
        o_ref[...] = y[:, k * width:(k + 1) * width].astype(o_ref.dtype)


def _mod_matmul(st, x, norm_g, mod, layer, w_bf16, bias, n_out, out_dtype):
    st = st.retiled(WIDE_OUT_TILE)
    d, n = w_bf16.shape
    in_specs = [st.row_spec(d), _resident((1, d)), st.mod_spec(layer, 0), st.mod_spec(layer, 1),
                _resident((d, n))]
    args = [x, norm_g.reshape(1, d), mod, mod, w_bf16]
    if bias is not None:
        in_specs.append(_resident((1, n)))
        args.append(bias.reshape(1, n))
    width = n // n_out
    outs = pl.pallas_call(
        functools.partial(_mod_matmul_kernel, n_out, bias is not None),
        grid=(st.tiles,),
        in_specs=in_specs,
        out_specs=[st.row_spec(width)] * n_out,
        out_shape=[jax.ShapeDtypeStruct((st.rows, width), out_dtype)] * n_out,
        compiler_params=_params("parallel"),
        name="mod_matmul",
    )(*args)
    return outs


def _filter_time_kernel(feats_ref, w1_ref, b1_ref, w2_ref, b2_ref, fr_ref, w3_ref, dl_ref,
                        hf_ref, hb_ref):
    feats = feats_ref[...]
    fr = fr_ref[...]
    h = jnp.sin(fr[0:1] * (_dot_f32(feats, w1_ref[...]) + b1_ref[...]))
    h = jnp.sin(fr[1:2] * (_dot_f32(h, w2_ref[...]) + b2_ref[...]))
    h = _dot_f32(h, w3_ref[...])
    t = feats[:, 0:1]
    decay = jnp.exp(-t * jnp.abs(dl_ref[...]))
    half = HY_ORDER * D_MODEL
    decay2 = jnp.concatenate([decay] * HY_ORDER, axis=1)
    hf_ref[...] = h[:, :half] * decay2
    hb_ref[...] = jnp.where(t == 0.0, 0.0, h[:, half:] * decay2)


def _filter_time(seq, feats_pad, w1_pad, b1, w2, b2, freq, w3, deltas):
    tl = min(seq, 256)
    half = HY_ORDER * D_MODEL
    out = jax.ShapeDtypeStruct((seq, half), F32)
    return pl.pallas_call(
        _filter_time_kernel,
        grid=(seq // tl,),
        in_specs=[
            pl.BlockSpec((tl, V7X_LANES), lambda i: (i, 0)),
            _resident(w1_pad.shape), _resident((1, HY_FW)), _resident((HY_FW, HY_FW)),
            _resident((1, HY_FW)), _resident((2, HY_FW)), _resident(w3.shape),
            _resident((1, D_MODEL)),
        ],
        out_specs=[pl.BlockSpec((tl, half), lambda i: (i, 0))] * 2,
        out_shape=[out, out],
        compiler_params=_params("parallel"),
        name="hyena_filter_time",
    )(feats_pad, w1_pad, b1.reshape(1, HY_FW), w2, b2.reshape(1, HY_FW), freq, w3, deltas)


def _filter_spectrum_kernel(blk, nb, hf_ref, hb_ref, c_ref, s_ref, kr_ref, ki_ref, kn_ref):
    row = lax.broadcasted_iota(jnp.int32, (blk, 1), 0)
    sg = (1 - 2 * (row & 1)).astype(F32)
    wgt = jnp.where(row == 0, 1.0, 2.0) * (1.0 / (2 * blk))
    fwd, bwd = [], []
    for j in range(nb):
        rows = slice(j * blk, (j + 1) * blk)
        for ref, out in ((hf_ref, fwd), (hb_ref, bwd)):
            x = ref[rows, :]
            xb = x.astype(BF16)
            out.append((_dot(c_ref[...], xb), _dot(s_ref[...], xb),
                        jnp.sum(x * sg, axis=0, keepdims=True), x[0:1, :],
                        xb[0:1, :].astype(F32)))
    kn_ref[...] = jnp.zeros_like(kn_ref)
    for d in range(-(nb - 1), nb):
        if d == 0:
            (fc, fs, fn, _, _), (bc, bs, bn, _, _) = fwd[0], bwd[0]
            kr, ki, kn = fc + bc, bs - fs, fn + bn
        else:
            parts, im_sign = (fwd, -1.0) if d > 0 else (bwd, 1.0)
            c1, s1, n1, _, _ = parts[abs(d)]
            c0, s0, n0, x0, x0_seen = parts[abs(d) - 1]
            kr, ki, kn = c1 + sg * (c0 - x0_seen), im_sign * (s1 + sg * s0), n1 + n0 - x0
        slot = d + nb - 1
        kr_ref[slot * blk:(slot + 1) * blk, :] = (wgt * kr).astype(kr_ref.dtype)
        ki_ref[slot * blk:(slot + 1) * blk, :] = (wgt * ki).astype(ki_ref.dtype)
        kn_ref[slot:slot + 1, :] = kn * (1.0 / (2 * blk))


def _filter_spectrum(seq, blk, hf, hb, cmat, smat):
    nb = seq // blk
    half = hf.shape[1]
    tn = 256
    col = pl.BlockSpec((seq, tn), lambda j: (0, j))
    spec_rows = (2 * nb - 1) * blk
    out_col = pl.BlockSpec((spec_rows, tn), lambda j: (0, j))
    return pl.pallas_call(
        functools.partial(_filter_spectrum_kernel, blk, nb),
        grid=(half // tn,),
        in_specs=[col, col, _resident((blk, blk)), _resident((blk, blk))],
        out_specs=[out_col, out_col, pl.BlockSpec((NYQUIST_ROWS, tn), lambda j: (0, j))],
        out_shape=[jax.ShapeDtypeStruct((spec_rows, half), BF16)] * 2
        + [jax.ShapeDtypeStruct((NYQUIST_ROWS, half), F32)],
        compiler_params=_params("parallel"),
        name="hyena_filter_spectrum",
    )(hf, hb, cmat, smat)


def _hyena_core_kernel(seq, blk, *refs):
    projections, shared, per_seq = refs[:3], refs[3:19], refs[19:]
    for s in range(projections[0].shape[0]):
        _hyena_core_one_sequence(seq, blk, *[r.at[s] for r in projections], *shared,
                                 *[r.at[s] for r in per_seq])


def _hyena_core_one_sequence(seq, blk, pv_ref, p1_ref, p2_ref, cwv_ref, cw1_ref, cw2_ref, cbv_ref,
                             cb1_ref, cb2_ref, kr0_ref, ki0_ref, kn0_ref, kr1_ref, ki1_ref,
                             kn1_ref, bias0_ref, bias1_ref, c_ref, s_ref, z_ref,
                             u_ref, ub_ref, gate_ref, a_ref, b_ref):
    row = lax.broadcasted_iota(jnp.int32, (blk, 1), 0)
    sign = (1 - 2 * (row & 1)).astype(F32)
    nb = seq // blk
    blocks = [slice(j * blk, (j + 1) * blk) for j in range(nb)]

    def short_conv(dst_ref, x_ref, w_ref, b_ref):
        x = x_ref[...].astype(F32)
        w = w_ref[...]
        time = lax.broadcasted_iota(jnp.int32, (seq, 1), 0)
        prev = jnp.where(time == 0, 0.0, pltpu.roll(x, 1, 0))
        nxt = jnp.where(time == seq - 1, 0.0, pltpu.roll(x, seq - 1, 0))
        dst_ref[...] = prev * w[0:1] + x * w[1:2] + nxt * w[2:3] + b_ref[...]

    def gated_long_conv(kr_ref, ki_ref, kn_ref, bias_ref, write):
        ub_ref[...] = u_ref[...].astype(BF16)
        nyq_in = []
        for rows in blocks:
            a_ref[rows, :] = _dot(c_ref[...], ub_ref[rows, :]).astype(BF16)
            b_ref[rows, :] = _dot(s_ref[...], ub_ref[rows, :]).astype(BF16)
            nyq_in.append(jnp.sum(u_ref[rows, :] * sign, axis=0, keepdims=True))
        for i, rows in enumerate(blocks):
            p = q = nyq = None
            for j, src in enumerate(blocks):
                slot = i - j + nb - 1
                kr = kr_ref[slot * blk:(slot + 1) * blk, :]
                ki = ki_ref[slot * blk:(slot + 1) * blk, :]
                a, b = a_ref[src, :], b_ref[src, :]
                pj, qj = a * kr + b * ki, b * kr - a * ki
                nj = nyq_in[j] * kn_ref[slot:slot + 1, :]
                p, q, nyq = (pj, qj, nj) if j == 0 else (p + pj, q + qj, nyq + nj)
            y = _dot(c_ref[...], p) + _dot(s_ref[...], q)
            y = y + sign * nyq + u_ref[rows, :] * bias_ref[...]
            write(rows, gate_ref[rows, :] * y)

    def to_u(rows, val):
        u_ref[rows, :] = val

    def to_z(rows, val):
        z_ref[rows, :] = val.astype(z_ref.dtype)

    short_conv(u_ref, pv_ref, cwv_ref, cbv_ref)
    short_conv(gate_ref, p1_ref, cw1_ref, cb1_ref)
    gated_long_conv(kr0_ref, ki0_ref, kn0_ref, bias0_ref, to_u)
    short_conv(gate_ref, p2_ref, cw2_ref, cb2_ref)
    gated_long_conv(kr1_ref, ki1_ref, kn1_ref, bias1_ref, to_z)


def _hyena_core(st, blk, proj, conv_w, conv_b, kr, ki, kn, bias, cmat, smat):
    seq, d = st.seq, D_MODEL
    tn = 256
    nj = d // tn
    spec_rows = kr.shape[0]
    proj3 = proj.reshape(st.batch, seq, 3 * d)
    group = max(1, min(HYENA_GROUP_STEPS // seq, st.batch))
    assert st.batch % group == 0

    def part(k):
        return pl.BlockSpec((group, seq, tn), lambda j, b: (b, 0, k * nj + j))

    def cols(rows, k, buffers=2):
        return pl.BlockSpec((rows, tn), lambda j, b: (0, k * nj + j),
                            pipeline_mode=pl.Buffered(buffers))

    in_specs = ([part(0), part(1), part(2)]
                + [cols(3, k) for k in range(3)] + [cols(1, k) for k in range(3)]
                + [cols(spec_rows, 0, 1), cols(spec_rows, 0, 1), cols(NYQUIST_ROWS, 0),
                   cols(spec_rows, 1, 1), cols(spec_rows, 1, 1), cols(NYQUIST_ROWS, 1)]
                + [cols(1, 0), cols(1, 0)]
                + [_resident((blk, blk)), _resident((blk, blk))])
    z = pl.pallas_call(
        functools.partial(_hyena_core_kernel, seq, blk),
        grid=(nj, st.batch // group),
        in_specs=in_specs,
        out_specs=pl.BlockSpec((group, seq, tn), lambda j, b: (b, 0, j)),
        out_shape=jax.ShapeDtypeStruct((st.batch, seq, d), BF16),
        scratch_shapes=[pltpu.VMEM((group, seq, tn), F32), pltpu.VMEM((group, seq, tn), BF16),
                        pltpu.VMEM((group, seq, tn), F32), pltpu.VMEM((group, seq, tn), BF16),
                        pltpu.VMEM((group, seq, tn), BF16)],
        compiler_params=_params("parallel", "parallel"),
        name="hyena_core",
    )(proj3, proj3, proj3, conv_w, conv_w, conv_w, conv_b, conv_b, conv_b,
      kr, ki, kn, kr, ki, kn, bias[0:1], bias[1:2], cmat, smat)
    return z.reshape(st.rows, d)


def _route_kernel(first_tiles, *refs):
    n = len(first_tiles)
    z_refs, x_refs, shared = refs[:n], refs[n:2 * n], refs[2 * n:2 * n + 8]
    h2_ref, x1_refs, route_refs = refs[2 * n + 8], refs[2 * n + 9:3 * n + 9], refs[3 * n + 9:]
    i = pl.program_id(0)
    for s in range(n):
        lo = first_tiles[s]
        hi = first_tiles[s + 1] if s + 1 < n else pl.num_programs(0)

        @pl.when((i >= lo) & (i < hi))
        def _(s=s):
            _route_tile(z_refs[s], x_refs[s], *shared, x1_refs[s], h2_ref, route_refs[s])


def _route_tile(z_ref, x_ref, w_ref, g1_ref, ng_ref, sh_ref, sc_ref, rw_hi_ref, rw_lo_ref, rb_ref,
                x1_ref, h2_ref, route_ref):
    x1 = x_ref[...] + g1_ref[...] * _dot(z_ref[...], w_ref[...])
    x1_ref[...] = x1
    h2 = _modulate(x1, ng_ref[...], sh_ref[...], sc_ref[...])
    _store_token_tiles(h2_ref, h2)
    h_hi = h2.astype(BF16)
    h_lo = (h2 - h_hi.astype(F32)).astype(BF16)
    logits = (_dot(h_hi, rw_hi_ref[...])
              + (_dot(h_lo, rw_hi_ref[...]) + _dot(h_hi, rw_lo_ref[...])) + rb_ref[...])
    lane = lax.broadcasted_iota(jnp.int32, logits.shape, 1)
    neg = -jnp.inf
    logits = jnp.where(lane < N_EXPERTS, logits, neg)
    m1 = jnp.max(logits, axis=-1, keepdims=True)
    i1 = jnp.min(jnp.where(logits == m1, lane, V7X_LANES), axis=-1, keepdims=True)
    rest = jnp.where(lane == i1, neg, logits)
    m2 = jnp.max(rest, axis=-1, keepdims=True)
    i2 = jnp.min(jnp.where(rest == m2, lane, V7X_LANES), axis=-1, keepdims=True)
    e2 = jnp.exp(m2 - m1)
    den = 1.0 + e2
    route = jnp.where(lane == 0, i1.astype(F32), jnp.where(lane == 1, i2.astype(F32), 0.0))
    route_ref[...] = route + jnp.where(lane == 2, 1.0 / den, 0.0) + jnp.where(lane == 3, e2 / den, 0.0)


def _store_token_tiles(ref, val):
    rows = val.shape[0]
    for c in range(LANE_CHUNKS):
        ref[pl.ds(c, rows, stride=LANE_CHUNKS), :] = val[:, c * V7X_LANES:(c + 1) * V7X_LANES]


def _out_proj_route(streams, w_bf16, mod, layer, norm_g, rw_pad, rb_pad):
    d = D_MODEL
    streams = [(st.retiled(WIDE_OUT_TILE), z, x) for st, z, x in streams]
    tm = streams[0][0].tm
    assert all(st.tm == tm and st.row0 % tm == 0 for st, _, _ in streams)
    first = [st.row0 // tm for st, _, _ in streams]
    tiles = sum(st.tiles for st, _, _ in streams)
    rw_hi = rw_pad.astype(BF16)
    rw_lo = (rw_pad - rw_hi.astype(F32)).astype(BF16)

    def local(s, i):
        return jnp.clip(i - first[s], 0, streams[s][0].tiles - 1)

    def cond_row(i):
        row = streams[0][0].cond_row(local(0, i))
        for s in range(1, len(streams)):
            row = jnp.where(i >= first[s], streams[s][0].cond_row(local(s, i)), row)
        return row

    def rows_of(s, width):
        return pl.BlockSpec((tm, width), lambda i: (local(s, i), 0))

    def mod_chunk(chunk):
        return pl.BlockSpec((None, None, 1, d), lambda i: (layer, cond_row(i), 0, chunk))

    n = len(streams)
    in_specs = ([rows_of(s, streams[s][1].shape[1]) for s in range(n)]
                + [rows_of(s, d) for s in range(n)]
                + [_resident(w_bf16.shape), mod_chunk(2), _resident((1, d)), mod_chunk(3),
                   mod_chunk(4), _resident(rw_pad.shape), _resident(rw_pad.shape),
                   _resident(rb_pad.shape)])
    out_specs = ([pl.BlockSpec((tm * LANE_CHUNKS, V7X_LANES), lambda i: (i, 0))]
                 + [rows_of(s, d) for s in range(n)] + [rows_of(s, V7X_LANES) for s in range(n)])
    out_shape = ([jax.ShapeDtypeStruct((tiles * tm * LANE_CHUNKS, V7X_LANES), F32)]
                 + [jax.ShapeDtypeStruct((st.rows, d), F32) for st, _, _ in streams]
                 + [jax.ShapeDtypeStruct((st.rows, V7X_LANES), F32) for st, _, _ in streams])
    outs = pl.pallas_call(
        functools.partial(_route_kernel, first),
        grid=(tiles,),
        in_specs=in_specs,
        out_specs=out_specs,
        out_shape=out_shape,
        compiler_params=_params("arbitrary"),
        name="out_proj_route",
    )(*[z for _, z, _ in streams], *[x for _, _, x in streams], w_bf16, mod,
      norm_g.reshape(1, d), mod, mod, rw_hi, rw_lo, rb_pad)
    return outs[0], outs[1:1 + n], outs[1 + n:]


def _swiglu_part(x_bf16, wg_ref, wu_ref, wd_ref):
    g = _dot(x_bf16, wg_ref[...].astype(BF16))
    u = _dot(x_bf16, wu_ref[...].astype(BF16))
    return _dot((g * jax.nn.sigmoid(g) * u).astype(BF16), wd_ref[...].astype(BF16))


def _out_proj_swiglu_kernel(z_ref, w_ref, b_ref, x_ref, g1_ref, ng_ref, sh_ref, sc_ref, wg_ref,
                            wu_ref, wd_ref, g2_ref, o_ref, x1_ref, h2_ref, acc_ref):
    j = pl.program_id(1)

    @pl.when(j == 0)
    def _():
        x1 = x_ref[...] + g1_ref[...] * (_dot(z_ref[...], w_ref[...]) + b_ref[...])
        x1_ref[...] = x1
        h2_ref[...] = _modulate(x1, ng_ref[...], sh_ref[...], sc_ref[...]).astype(BF16)

    part = _swiglu_part(h2_ref[...], wg_ref, wu_ref, wd_ref)

    @pl.when(j == 0)
    def _():
        acc_ref[...] = part

    @pl.when(j > 0)
    def _():
        acc_ref[...] += part

    @pl.when(j == pl.num_programs(1) - 1)
    def _():
        o_ref[...] = x1_ref[...] + g2_ref[...] * acc_ref[...]


def _out_proj_swiglu(st, z_bf16, w_bf16, bias, x, mod, layer, norm_g, w_gu, w_down):
    st = st.retiled(WIDE_OUT_TILE)
    d, two_f = w_gu.shape
    f = two_f // 2
    tf = DENSE_FF_CHUNK
    nf = f // tf
    rows = pl.BlockSpec((st.tm, d), lambda i, j: (i, 0))
    return pl.pallas_call(
        _out_proj_swiglu_kernel,
        grid=(st.tiles, nf),
        in_specs=[
            rows, _resident(w_bf16.shape), _resident((1, d)), rows, st.mod_spec(layer, 2),
            _resident((1, d)), st.mod_spec(layer, 3), st.mod_spec(layer, 4),
            pl.BlockSpec((d, tf), lambda i, j: (0, j)),
            pl.BlockSpec((d, tf), lambda i, j: (0, nf + j)),
            pl.BlockSpec((tf, d), lambda i, j: (j, 0)),
            st.mod_spec(layer, 5),
        ],
        out_specs=rows,
        out_shape=jax.ShapeDtypeStruct((st.rows, d), F32),
        scratch_shapes=[pltpu.VMEM((st.tm, d), F32), pltpu.VMEM((st.tm, d), BF16),
                        pltpu.VMEM((st.tm, d), F32)],
        compiler_params=_params("parallel", "arbitrary"),
        name="out_proj_swiglu",
    )(z_bf16, w_bf16, bias.reshape(1, d), x, mod, norm_g.reshape(1, d), mod, mod,
      w_gu, w_gu, w_down, mod)


def _moe_plan(route, tm):
    tokens = route.shape[0]
    max_tiles = (2 * tokens) // tm + N_EXPERTS
    expert = route[:, :2].astype(jnp.int32).reshape(-1)
    experts = jnp.arange(N_EXPERTS, dtype=jnp.int32)
    onehot = (expert[:, None] == experts[None, :]).astype(jnp.int32)
    csum = jnp.cumsum(onehot, axis=0)
    rank = jnp.sum(csum * onehot, axis=1) - 1
    counts = csum[-1]
    tiles_per_expert = (counts + tm - 1) // tm
    tiles_end = jnp.cumsum(tiles_per_expert)
    first_tile = tiles_end - tiles_per_expert
    first_sorted = jnp.cumsum(counts) - counts
    pos = jnp.sum((first_tile * tm)[None, :] * onehot, axis=1) + rank
    token = jnp.arange(2 * tokens, dtype=jnp.int32) // 2
    _, order = lax.sort_key_val(pos, token)
    n_tiles = tiles_end[-1:]
    tile = jnp.minimum(jnp.arange(max_tiles, dtype=jnp.int32), n_tiles - 1)
    tile_expert = jnp.sum((tile[:, None] >= tiles_end[None, :]).astype(jnp.int32), axis=1)
    mine = (tile_expert[:, None] == experts[None, :]).astype(jnp.int32)
    done = (tile - jnp.sum(first_tile[None, :] * mine, axis=1)) * tm
    tile_base = jnp.sum(first_sorted[None, :] * mine, axis=1) + done
    tile_valid = jnp.clip(jnp.sum(counts[None, :] * mine, axis=1) - done, 1, tm)
    as_i32 = lambda a: a.astype(jnp.int32)
    return (as_i32(order), as_i32(pos), as_i32(tile_expert), as_i32(tile_base),
            as_i32(tile_valid), as_i32(n_tiles))


def _token_tile_copy(src_hbm, row, dst, slot_row, sem):
    return pltpu.make_async_copy(
        src_hbm.at[pl.ds(pl.multiple_of(row * LANE_CHUNKS, LANE_CHUNKS), LANE_CHUNKS)],
        dst.at[pl.ds(pl.multiple_of(slot_row * LANE_CHUNKS, LANE_CHUNKS), LANE_CHUNKS)],
        sem)


def _moe_ffn_kernel(tm, order_ref, te_ref, base_ref, valid_ref, nt_ref, x_hbm, wg_ref, wu_ref,
                    wd_ref, y_ref, xbuf, xd_ref, acc_ref, sem):
    t, j = pl.program_id(0), pl.program_id(1)
    last_j = pl.num_programs(1) - 1
    n_tiles = nt_ref[0]
    slot = t % 2
    rows = tm * LANE_CHUNKS

    def start_gather(tile, into):
        base, last = base_ref[tile], valid_ref[tile] - 1

        def body(r, carry):
            token = order_ref[base + jnp.minimum(r, last)]
            _token_tile_copy(x_hbm, token, xbuf.at[into], r, sem.at[into]).start()
            return carry
        lax.fori_loop(0, tm, body, 0, unroll=8)

    @pl.when((t == 0) & (j == 0))
    def _():
        start_gather(0, 0)

    @pl.when((j == 0) & (t < n_tiles))
    def _():
        pltpu.make_async_copy(x_hbm.at[pl.ds(0, rows)], xbuf.at[slot], sem.at[slot]).wait()
        for c in range(LANE_CHUNKS):
            xd_ref[:, c * V7X_LANES:(c + 1) * V7X_LANES] = (
                xbuf[slot, pl.ds(c, tm, stride=LANE_CHUNKS), :].astype(BF16))

        @pl.when(t + 1 < n_tiles)
        def _():
            start_gather(t + 1, 1 - slot)

    @pl.when(t < n_tiles)
    def _():
        part = _swiglu_part(xd_ref[...], wg_ref, wu_ref, wd_ref)

        @pl.when(j == 0)
        def _():
            acc_ref[...] = part

        @pl.when(j > 0)
        def _():
            acc_ref[...] += part

        @pl.when(j == last_j)
        def _():
            _store_token_tiles(y_ref, acc_ref[...])

    @pl.when((t >= n_tiles) & (j == last_j))
    def _():
        y_ref[...] = jnp.zeros_like(y_ref)


def _moe_ffn(pool, order, tile_expert, tile_base, tile_valid, n_tiles, w_gu, w_down, tm):
    n_e, d, two_f = w_gu.shape
    f = two_f // 2
    tf = MOE_FF_CHUNK
    nf = f // tf
    max_tiles = tile_expert.shape[0]
    rows = tm * LANE_CHUNKS
    grid_spec = pltpu.PrefetchScalarGridSpec(
        num_scalar_prefetch=5,
        grid=(max_tiles, nf),
        in_specs=[
            pl.BlockSpec(memory_space=pl.ANY),
            pl.BlockSpec((None, d, tf), lambda t, j, order, te, *_: (te[t], 0, j)),
            pl.BlockSpec((None, d, tf), lambda t, j, order, te, *_: (te[t], 0, nf + j)),
            pl.BlockSpec((None, tf, d), lambda t, j, order, te, *_: (te[t], j, 0)),
        ],
        out_specs=pl.BlockSpec((rows, V7X_LANES), lambda t, j, *_: (t, 0)),
        scratch_shapes=[pltpu.VMEM((2, rows, V7X_LANES), F32), pltpu.VMEM((tm, d), BF16),
                        pltpu.VMEM((tm, d), F32), pltpu.SemaphoreType.DMA((2,))],
    )
    return pl.pallas_call(
        functools.partial(_moe_ffn_kernel, tm),
        grid_spec=grid_spec,
        out_shape=jax.ShapeDtypeStruct((max_tiles * rows, V7X_LANES), F32),
        compiler_params=_params("arbitrary", "arbitrary"),
        name="moe_ffn",
    )(order, tile_expert, tile_base, tile_valid, n_tiles, pool, w_gu, w_gu, w_down)


def _moe_combine_kernel(tm, token0, pos_ref, y_hbm, route_ref, x_ref, g2_ref, o_ref, ybuf, sem):
    i = pl.program_id(0)
    slot = i % 2
    rows = tm * LANE_CHUNKS

    def start_gather(tile, into):
        def body(r, carry):
            a = 2 * (token0 + tile * tm + r)
            for k in range(2):
                _token_tile_copy(y_hbm, pos_ref[a + k], ybuf.at[into], k * tm + r,
                                 sem.at[into]).start(priority=k)
            return carry
        lax.fori_loop(0, tm, body, 0, unroll=4)

    @pl.when(i == 0)
    def _():
        start_gather(0, 0)

    pltpu.make_async_copy(y_hbm.at[pl.ds(0, 2 * rows)], ybuf.at[slot], sem.at[slot]).wait()

    @pl.when(i + 1 < pl.num_programs(0))
    def _():
        start_gather(i + 1, 1 - slot)

    route = route_ref[...]
    lane = lax.broadcasted_iota(jnp.int32, route.shape, 1)
    gate0 = jnp.sum(jnp.where(lane == 2, route, 0.0), axis=-1, keepdims=True)
    gate1 = jnp.sum(jnp.where(lane == 3, route, 0.0), axis=-1, keepdims=True)
    for c in range(LANE_CHUNKS):
        cols = slice(c * V7X_LANES, (c + 1) * V7X_LANES)
        y0 = ybuf[slot, pl.ds(c, tm, stride=LANE_CHUNKS), :]
        y1 = ybuf[slot, pl.ds(rows + c, tm, stride=LANE_CHUNKS), :]
        o_ref[:, cols] = x_ref[:, cols] + g2_ref[:, cols] * (gate0 * y0 + gate1 * y1)


def _moe_combine(st, y_slots, pos, route, x1, mod, layer):
    d = D_MODEL
    tm = st.tm
    rows = tm * LANE_CHUNKS
    grid_spec = pltpu.PrefetchScalarGridSpec(
        num_scalar_prefetch=1,
        grid=(st.tiles,),
        in_specs=[
            pl.BlockSpec(memory_space=pl.ANY),
            st.row_spec(V7X_LANES), st.row_spec(d), st.mod_spec(layer, 5),
        ],
        out_specs=st.row_spec(d),
        scratch_shapes=[pltpu.VMEM((2, 2 * rows, V7X_LANES), F32), pltpu.SemaphoreType.DMA((2,))],
    )
    return pl.pallas_call(
        functools.partial(_moe_combine_kernel, tm, st.row0),
        grid_spec=grid_spec,
        out_shape=jax.ShapeDtypeStruct((st.rows, d), F32),
        compiler_params=_params("arbitrary"),
        name="moe_combine",
    )(pos, y_slots, route, x1, mod)


def _head_rmsnorm(x, g2):
    lane = lax.broadcasted_iota(jnp.int32, x.shape, 1)
    lo = lane < HEAD_DIM
    sq = x * x
    s_lo = jnp.sum(jnp.where(lo, sq, 0.0), axis=-1, keepdims=True)
    s_hi = jnp.sum(jnp.where(lo, 0.0, sq), axis=-1, keepdims=True)
    ms = jnp.where(lo, s_lo, s_hi) * (1.0 / HEAD_DIM)
    return x * lax.rsqrt(ms + EPS) * g2


def _rope(x, cos, sin_signed):
    q4 = HEAD_DIM // 4
    lane = lax.broadcasted_iota(jnp.int32, x.shape, 1)
    first = (lane & q4) == 0
    width = x.shape[1]
    partner = jnp.where(first, pltpu.roll(x, width - q4, 1), pltpu.roll(x, q4, 1))
    return x * cos + partner * sin_signed


def _attention_kernel(past, use_rope, lam_init, *refs):
    refs = list(refs)
    n_seq = refs[0].shape[0]
    n_in = 3 + (2 if past else 0)
    seq_refs, refs = refs[:n_in], refs[n_in:]
    if use_rope:
        rope_refs, refs = refs[:4], refs[4:]
    else:
        rope_refs = []
    param_refs, refs = refs[:4], refs[4:]
    n_out = 1 if past else 2
    out_refs, (kall_ref, vall_ref) = refs[:n_out], refs[n_out:]
    for s in range(n_seq):
        _attention_one_sequence(past, use_rope, lam_init, *[r.at[s] for r in seq_refs],
                                *rope_refs, *param_refs, *[r.at[s] for r in out_refs],
                                kall_ref.at[s], vall_ref.at[s])


def _attention_one_sequence(past, use_rope, lam_init, *refs):
    refs = list(refs)
    q_ref, k_ref, v_ref = refs.pop(0), refs.pop(0), refs.pop(0)
    if past:
        ck_ref, cv_ref = refs.pop(0), refs.pop(0)
    if use_rope:
        cosq_ref, sinq_ref, cosk_ref, sink_ref = (refs.pop(0) for _ in range(4))
    qg_ref, kg_ref, lam_ref, sg_ref = (refs.pop(0) for _ in range(4))
    o_ref = refs.pop(0)
    nk_ref = None if past else refs.pop(0)
    kall_ref, vall_ref = refs

    qi = pl.program_id(2)

    @pl.when(qi == 0)
    def _():
        k = _head_rmsnorm(k_ref[...], kg_ref[...])
        if nk_ref is not None:
            nk_ref[...] = k
        if use_rope:
            k = _rope(k, cosk_ref[...], sink_ref[...])
        if past:
            kall_ref[0:past, :] = ck_ref[...].astype(BF16)
            vall_ref[0:past, :] = cv_ref[...].astype(BF16)
        kall_ref[past:, :] = k.astype(BF16)
        vall_ref[past:, :] = v_ref[...].astype(BF16)

    q = _head_rmsnorm(q_ref[...], qg_ref[...])
    if use_rope:
        q = _rope(q, cosq_ref[...], sinq_ref[...])
    q = q * (HEAD_DIM ** -0.5 * math.log2(math.e))
    tq = q.shape[0]
    nt = (((1,), (1,)), ((), ()))
    lv = lam_ref[...]
    lam = (jnp.exp(jnp.sum(lv[0:1] * lv[1:2], axis=-1, keepdims=True))
           - jnp.exp(jnp.sum(lv[2:3] * lv[3:4], axis=-1, keepdims=True)) + lam_init)

    def attend(qm):
        s = lax.dot_general(qm.astype(BF16), kall_ref[...], nt, preferred_element_type=F32)
        p = jnp.exp2(s - jnp.max(s, axis=-1, keepdims=True))
        norm = 1.0 / jnp.sum(p, axis=-1, keepdims=True)
        return _dot(p.astype(BF16), vall_ref[...]) * norm

    row_chunks = [slice(r, r + ATTENTION_ROWS) for r in range(0, tq, ATTENTION_ROWS)]
    lo = lax.broadcasted_iota(jnp.int32, q.shape, 1) < HEAD_DIM
    first = jnp.where(lo, q, 0.0)
    second = jnp.where(lo, 0.0, q)
    a0 = jnp.concatenate([attend(first[rows]) for rows in row_chunks], axis=0)
    a1 = jnp.concatenate([attend(second[rows]) for rows in row_chunks], axis=0)
    o = a0 - lam * a1
    ms = jnp.mean(o * o, axis=-1, keepdims=True)
    o = o * lax.rsqrt(ms + EPS) * sg_ref[...] * (1.0 - lam_init)
    o_ref[...] = o.astype(o_ref.dtype)


def _attention(st, q, k, v, q_g, k_g, lam_vecs, subln_g, lam_init, cache_k=None, cache_v=None,
               rope=None, cache_layer=0):
    seq, hd = st.seq, N_HEADS * V_DIM
    tq = min(seq, ATTENTION_QUERIES)
    nq = seq // tq
    group = max(1, min(ATTENTION_GROUP_QUERIES // seq, st.batch))
    assert st.batch % group == 0
    past = 0 if cache_k is None else cache_k.shape[2]
    lk = past + seq
    q3, k3, v3 = (a.reshape(st.batch, seq, hd) for a in (q, k, v))
    qblk = pl.BlockSpec((group, tq, V_DIM), lambda b, h, i: (b, i, h))
    kblk = pl.BlockSpec((group, seq, V_DIM), lambda b, h, i: (b, 0, h))
    in_specs = [qblk, kblk, kblk]
    args = [q3, k3, v3]
    if past:
        layers = cache_k.shape[1]
        cblk = pl.BlockSpec((group, None, past, V_DIM), lambda b, h, i: (b, cache_layer, 0, h))
        in_specs += [cblk, cblk]
        args += [cache_k.reshape(st.batch, layers, past, hd),
                 cache_v.reshape(st.batch, layers, past, hd)]
    if rope is not None:
        cos2, sin2 = rope
        tq_tab = pl.BlockSpec((tq, V_DIM), lambda b, h, i: (i, 0))
        k_tab = pl.BlockSpec((seq, V_DIM), lambda b, h, i: (0, 0))
        in_specs += [tq_tab, tq_tab, k_tab, k_tab]
        args += [cos2, sin2, cos2, sin2]
    small = lambda shape: pl.BlockSpec(shape, lambda b, h, i: (0,) * len(shape))
    in_specs += [small((1, V_DIM)), small((1, V_DIM)), small((4, HEAD_DIM)), small((1, V_DIM))]
    args += [jnp.tile(q_g, 2).reshape(1, V_DIM), jnp.tile(k_g, 2).reshape(1, V_DIM), lam_vecs,
             subln_g.reshape(1, V_DIM)]
    out_specs = [qblk]
    out_shape = [jax.ShapeDtypeStruct((st.batch, seq, hd), BF16)]
    if not past:
        out_specs.append(kblk)
        out_shape.append(jax.ShapeDtypeStruct((st.batch, seq, hd), F32))
    outs = pl.pallas_call(
        functools.partial(_attention_kernel, past, rope is not None, lam_init),
        grid=(st.batch // group, N_HEADS, nq),
        in_specs=in_specs,
        out_specs=out_specs,
        out_shape=out_shape,
        scratch_shapes=[pltpu.VMEM((group, lk, V_DIM), BF16),
                        pltpu.VMEM((group, lk, V_DIM), BF16)],
        compiler_params=_params("parallel", "parallel", "arbitrary"),
        name="diff_attention",
    )(*args)
    return [o.reshape(st.rows, hd) for o in outs]


def _dft_matrices(seq):
    idx = np.arange(seq, dtype=np.int64)
    ang = (np.outer(idx, idx) % (2 * seq)).astype(np.float64) * (math.pi / seq)
    return jnp.asarray(np.cos(ang), dtype=BF16), jnp.asarray(np.sin(ang), dtype=BF16)


def _filter_features(seq):
    t = jnp.linspace(0.0, 1.0, seq, dtype=F32)[:, None]
    bands = (HY_EMB - 1) // 2
    w_ang = 2.0 * math.pi * jnp.arange(seq, dtype=F32)[:, None] / seq
    f = jnp.linspace(1e-4, bands - 1, bands, dtype=F32)[None, :]
    ang = f * w_ang
    feats = jnp.concatenate([t, jnp.cos(ang), -jnp.sin(ang)], axis=-1)
    return jnp.pad(feats, ((0, 0), (0, V7X_LANES - HY_EMB)))


def _decay_rates():
    min_decay = math.log(HY_DECAY_TARGET) / HY_SLOW_PCT
    max_decay = math.log(HY_DECAY_TARGET) / HY_FAST_PCT
    return jnp.linspace(min_decay, max_decay, D_MODEL, dtype=F32)[None, :]


def _rope_tables(seq):
    rows = seq // GRID_W
    row = jnp.repeat(jnp.arange(rows, dtype=F32), GRID_W)
    col = jnp.tile(jnp.arange(GRID_W, dtype=F32), rows)
    quarter = HEAD_DIM // 4
    inv = ROPE_BASE ** (-jnp.arange(quarter, dtype=F32) / quarter)

    def axis_angles(pos):
        a = pos[:, None] * inv[None, :]
        return jnp.concatenate([a, a], axis=-1)

    ang = jnp.concatenate([axis_angles(row), axis_angles(col)], axis=-1)
    sign = jnp.where((jnp.arange(HEAD_DIM) & quarter) == 0, -1.0, 1.0).astype(F32)
    return jnp.tile(jnp.cos(ang), (1, 2)), jnp.tile(jnp.sin(ang) * sign[None, :], (1, 2))


def kernel(x_prompt, x_sample, cache_k, cache_v, c, c_ctx, ada_w, ada_b, norm_g, hy_in_w, hy_in_b, hy_conv_w, hy_conv_b, hy_f_w1, hy_f_b1, hy_f_w2, hy_f_b2, hy_f_freq, hy_f_w3, hy_bias, hy_out_w, hy_out_b, at_qkv_w, at_q_g, at_k_g, at_lam, at_subln_g, at_out_w, dn_w_gu, dn_w_down, mo_router_w, mo_router_b, mo_w_gu, mo_w_down):
    d = D_MODEL
    batch, seq = x_prompt.shape[:2]
    dec_batch, dec_seq = x_sample.shape[:2]
    streams = [
        (_Stream(batch, seq, 0, False), x_prompt.reshape(batch * seq, d), None),
        (_Stream(dec_batch, dec_seq, 1, True, row0=batch * seq),
         x_sample.reshape(dec_batch * dec_seq, d), (cache_k, cache_v)),
    ]

    cond = jnp.concatenate(
        [c_ctx[None, :], c, jnp.zeros((COND_ROWS - 1 - dec_batch, d), F32)], axis=0)
    mod = _adaln(cond, ada_w, ada_b)
    mod = mod.reshape(mod.shape[0], COND_ROWS, 1, 6 * d)

    in_w, out_w = hy_in_w[0].astype(BF16), hy_out_w[0].astype(BF16)
    qkv_w, at_out = at_qkv_w[0].astype(BF16), at_out_w[0].astype(BF16)
    dn_gu, dn_down = dn_w_gu[0].astype(BF16), dn_w_down[0].astype(BF16)
    mo_gu, mo_down = mo_w_gu[0], mo_w_down[0]
    w1_pad = jnp.pad(hy_f_w1[0], ((0, V7X_LANES - HY_EMB), (0, 0)))
    rw_pad = jnp.pad(mo_router_w[0], ((0, 0), (0, V7X_LANES - N_EXPERTS)))
    rb_pad = jnp.pad(mo_router_b[0], (0, V7X_LANES - N_EXPERTS)).reshape(1, V7X_LANES)
    deltas = _decay_rates()
    lam_init = 0.8 - 0.6 * math.exp(-0.3 * 1)

    attended = []
    for st, x, cache in streams:
        blk = min(st.seq, HYENA_BLOCK)
        cmat, smat = _dft_matrices(blk)
        hf, hb = _filter_time(st.seq, _filter_features(st.seq), w1_pad, hy_f_b1[0], hy_f_w2[0],
                              hy_f_b2[0], hy_f_freq[0], hy_f_w3[0], deltas)
        kr, ki, kn = _filter_spectrum(st.seq, blk, hf, hb, cmat, smat)
        (proj,) = _mod_matmul(st, x, norm_g[0, 0], mod, 0, in_w, hy_in_b[0], 1, BF16)
        z = _hyena_core(st, blk, proj, hy_conv_w[0], hy_conv_b[0].reshape(1, 3 * d), kr, ki, kn,
                        hy_bias[0], cmat, smat)
        x = _out_proj_swiglu(st, z, out_w, hy_out_b[0], x, mod, 0, norm_g[0, 1], dn_gu, dn_down)

        q, k, v = _mod_matmul(st, x, norm_g[1, 0], mod, 1, qkv_w, None, 3, F32)
        if cache is None:
            o, new_k = _attention(st, q, k, v, at_q_g[0], at_k_g[0], at_lam[0], at_subln_g[0],
                                  lam_init)
            new_kv = (new_k, v)
        else:
            (o,) = _attention(st, q, k, v, at_q_g[0], at_k_g[0], at_lam[0], at_subln_g[0],
                              lam_init, cache[0], cache[1], _rope_tables(st.seq))
        attended.append((st, o, x))

    pool, x1s, routes = _out_proj_route(attended, at_out, mod, 1, norm_g[1, 1], rw_pad, rb_pad)
    order, pos, *tile_table = _moe_plan(
        jnp.concatenate([route[:, :4] for route in routes], axis=0), MOE_TILE)
    y_slots = _moe_ffn(pool, order, *tile_table, mo_gu, mo_down, MOE_TILE)
    results = [_moe_combine(st, y_slots, pos, route, x1, mod, 1).reshape(st.batch, st.seq, d)
               for (st, _, _), x1, route in zip(attended, x1s, routes)]

    new_k, new_v = new_kv
    return (results[0], results[1],
            new_k.reshape(batch, 1, seq, N_HEADS, 2, HEAD_DIM),
            new_v.reshape(batch, 1, seq, N_HEADS, V_DIM))
```

```python
import functools
import math

import numpy as np
import jax
import jax.numpy as jnp
from jax import lax
from jax.experimental import pallas as pl
from jax.experimental.pallas import tpu as pltpu

F32 = jnp.float32
BF16 = jnp.bfloat16

D_MODEL = 1024
GRID_W = 64
HY_ORDER = 2
HY_EMB = 33
HY_FW = 64
HY_DECAY_TARGET = 1e-2
HY_FAST_PCT = 0.3
HY_SLOW_PCT = 1.5
N_HEADS = 8
HEAD_DIM = 64
V_DIM = 2 * HEAD_DIM
ROPE_BASE = 10000.0
D_FF = 2816
N_EXPERTS = 8
D_FF_EXPERT = 3584
EPS = 1e-6

V7X_LANES = 128
V7X_VMEM_LIMIT_BYTES = 56 * 1024 * 1024
LANE_CHUNKS = D_MODEL // V7X_LANES
COND_ROWS = 16
TOKEN_TILE = 1024
WIDE_OUT_TILE = 512
MOE_TILE = 1024
MOE_FF_CHUNK = 512
DENSE_FF_CHUNK = D_FF // 2
HYENA_BLOCK = 512
HYENA_GROUP_STEPS = 1024
NYQUIST_ROWS = 8
ATTENTION_QUERIES = 1024
ATTENTION_GROUP_QUERIES = 1024
ATTENTION_ROWS = 256


def _params(*semantics):
    return pltpu.CompilerParams(dimension_semantics=semantics,
                                vmem_limit_bytes=V7X_VMEM_LIMIT_BYTES)


def _resident(shape):
    zeros = (0,) * len(shape)
    return pl.BlockSpec(shape, lambda *_: zeros, pipeline_mode=pl.Buffered(1))


def _dot(a, b):
    return jnp.dot(a, b, preferred_element_type=F32)


def _dot_f32(a, b):
    a_hi, b_hi = a.astype(BF16), b.astype(BF16)
    a_lo = (a - a_hi.astype(F32)).astype(BF16)
    b_lo = (b - b_hi.astype(F32)).astype(BF16)
    return _dot(a_hi, b_hi) + (_dot(a_lo, b_hi) + _dot(a_hi, b_lo))


def _modulate(x, g, shift, scale):
    ms = jnp.mean(x * x, axis=-1, keepdims=True)
    return (x * lax.rsqrt(ms + EPS) * g) * (1.0 + scale) + shift


def _adaln_kernel(cond_ref, w_ref, b_ref, o_ref):
    c = cond_ref[...]
    o_ref[...] = _dot_f32(c * jax.nn.sigmoid(c), w_ref[...]) + b_ref[...]


def _adaln(cond, ada_w, ada_b):
    depth, d, n = ada_w.shape
    tn = 1536
    return pl.pallas_call(
        _adaln_kernel,
        grid=(depth, n // tn),
        in_specs=[
            pl.BlockSpec((COND_ROWS, d), lambda i, j: (0, 0)),
            pl.BlockSpec((None, d, tn), lambda i, j: (i, 0, j)),
            pl.BlockSpec((None, 1, tn), lambda i, j: (i, 0, j)),
        ],
        out_specs=pl.BlockSpec((None, COND_ROWS, tn), lambda i, j: (i, 0, j)),
        out_shape=jax.ShapeDtypeStruct((depth, COND_ROWS, n), F32),
        compiler_params=_params("parallel", "parallel"),
        name="adaln",
    )(cond, ada_w, ada_b.reshape(depth, 1, n))


class _Stream:
    def __init__(self, batch, seq, cond_row0, per_seq_cond, row0=0, tile=TOKEN_TILE):
        self.batch, self.seq = batch, seq
        self.rows = batch * seq
        self.row0 = row0
        self._cond = (cond_row0, per_seq_cond)
        if per_seq_cond:
            self.tm = min(tile, seq)
            tiles_per_seq = seq // self.tm
            self.cond_row = lambda i: cond_row0 + i // tiles_per_seq
        else:
            self.tm = min(tile, self.rows)
            self.cond_row = lambda i: cond_row0
        self.tiles = self.rows // self.tm

    def retiled(self, tile):
        return _Stream(self.batch, self.seq, *self._cond, row0=self.row0, tile=tile)

    def mod_spec(self, layer, chunk):
        return pl.BlockSpec((None, None, 1, D_MODEL),
                            lambda i, *_: (layer, self.cond_row(i), 0, chunk))

    def row_spec(self, width):
        return pl.BlockSpec((self.tm, width), lambda i, *_: (i, 0))


def _mod_matmul_kernel(n_out, has_bias, x_ref, g_ref, sh_ref, sc_ref, w_ref, *rest):
    if has_bias:
        b_ref, out_refs = rest[0], rest[1:]
    else:
        b_ref, out_refs = None, rest
    h = _modulate(x_ref[...], g_ref[...], sh_ref[...], sc_ref[...])
    y = _dot(h.astype(BF16), w_ref[...])
    if has_bias:
        y = y + b_ref[...]
    width = y.shape[1] // n_out
    for k, o_ref in enumerate(out_refs):
        o_ref[...] = y[:, k * width:(k + 1) * width].astype(o_ref.dtype)


def _mod_matmul(st, x, norm_g, mod, layer, w_bf16, bias, n_out, out_dtype):
    st = st.retiled(WIDE_OUT_TILE)
    d, n = w_bf16.shape
    in_specs = [st.row_spec(d), _resident((1, d)), st.mod_spec(layer, 0), st.mod_spec(layer, 1),
                _resident((d, n))]
    args = [x, norm_g.reshape(1, d), mod, mod, w_bf16]
    if bias is not None:
        in_specs.append(_resident((1, n)))
        args.append(bias.reshape(1, n))
    width = n // n_out
    outs = pl.pallas_call(
        functools.partial(_mod_matmul_kernel, n_out, bias is not None),
        grid=(st.tiles,),
        in_specs=in_specs,
        out_specs=[st.row_spec(width)] * n_out,
        out_shape=[jax.ShapeDtypeStruct((st.rows, width), out_dtype)] * n_out,
        compiler_params=_params("parallel"),
        name="mod_matmul",
    )(*args)
    return outs


def _filter_time_kernel(feats_ref, w1_ref, b1_ref, w2_ref, b2_ref, fr_ref, w3_ref, dl_ref,
                        hf_ref, hb_ref):
    feats = feats_ref[...]
    fr = fr_ref[...]
    h = jnp.sin(fr[0:1] * (_dot_f32(feats, w1_ref[...]) + b1_ref[...]))
    h = jnp.sin(fr[1:2] * (_dot_f32(h, w2_ref[...]) + b2_ref[...]))
    h = _dot_f32(h, w3_ref[...])
    t = feats[:, 0:1]
    decay = jnp.exp(-t * jnp.abs(dl_ref[...]))
    half = HY_ORDER * D_MODEL
    decay2 = jnp.concatenate([decay] * HY_ORDER, axis=1)
    hf_ref[...] = h[:, :half] * decay2
    hb_ref[...] = jnp.where(t == 0.0, 0.0, h[:, half:] * decay2)


def _filter_time(seq, feats_pad, w1_pad, b1, w2, b2, freq, w3, deltas):
    tl = min(seq, 256)
    half = HY_ORDER * D_MODEL
    out = jax.ShapeDtypeStruct((seq, half), F32)
    return pl.pallas_call(
        _filter_time_kernel,
        grid=(seq // tl,),
        in_specs=[
            pl.BlockSpec((tl, V7X_LANES), lambda i: (i, 0)),
            _resident(w1_pad.shape), _resident((1, HY_FW)), _resident((HY_FW, HY_FW)),
            _resident((1, HY_FW)), _resident((2, HY_FW)), _resident(w3.shape),
            _resident((1, D_MODEL)),
        ],
        out_specs=[pl.BlockSpec((tl, half), lambda i: (i, 0))] * 2,
        out_shape=[out, out],
        compiler_params=_params("parallel"),
        name="hyena_filter_time",
    )(feats_pad, w1_pad, b1.reshape(1, HY_FW), w2, b2.reshape(1, HY_FW), freq, w3, deltas)


def _filter_spectrum_kernel(blk, nb, hf_ref, hb_ref, c_ref, s_ref, kr_ref, ki_ref, kn_ref):
    row = lax.broadcasted_iota(jnp.int32, (blk, 1), 0)
    sg = (1 - 2 * (row & 1)).astype(F32)
    wgt = jnp.where(row == 0, 1.0, 2.0) * (1.0 / (2 * blk))
    fwd, bwd = [], []
    for j in range(nb):
        rows = slice(j * blk, (j + 1) * blk)
        for ref, out in ((hf_ref, fwd), (hb_ref, bwd)):
            x = ref[rows, :]
            xb = x.astype(BF16)
            out.append((_dot(c_ref[...], xb), _dot(s_ref[...], xb),
                        jnp.sum(x * sg, axis=0, keepdims=True), x[0:1, :],
                        xb[0:1, :].astype(F32)))
    kn_ref[...] = jnp.zeros_like(kn_ref)
    for d in range(-(nb - 1), nb):
        if d == 0:
            (fc, fs, fn, _, _), (bc, bs, bn, _, _) = fwd[0], bwd[0]
            kr, ki, kn = fc + bc, bs - fs, fn + bn
        else:
            parts, im_sign = (fwd, -1.0) if d > 0 else (bwd, 1.0)
            c1, s1, n1, _, _ = parts[abs(d)]
            c0, s0, n0, x0, x0_seen = parts[abs(d) - 1]
            kr, ki, kn = c1 + sg * (c0 - x0_seen), im_sign * (s1 + sg * s0), n1 + n0 - x0
        slot = d + nb - 1
        kr_ref[slot * blk:(slot + 1) * blk, :] = (wgt * kr).astype(kr_ref.dtype)
        ki_ref[slot * blk:(slot + 1) * blk, :] = (wgt * ki).astype(ki_ref.dtype)
        kn_ref[slot:slot + 1, :] = kn * (1.0 / (2 * blk))


def _filter_spectrum(seq, blk, hf, hb, cmat, smat):
    nb = seq // blk
    half = hf.shape[1]
    tn = 256
    col = pl.BlockSpec((seq, tn), lambda j: (0, j))
    spec_rows = (2 * nb - 1) * blk
    out_col = pl.BlockSpec((spec_rows, tn), lambda j: (0, j))
    return pl.pallas_call(
        functools.partial(_filter_spectrum_kernel, blk, nb),
        grid=(half // tn,),
        in_specs=[col, col, _resident((blk, blk)), _resident((blk, blk))],
        out_specs=[out_col, out_col, pl.BlockSpec((NYQUIST_ROWS, tn), lambda j: (0, j))],
        out_shape=[jax.ShapeDtypeStruct((spec_rows, half), BF16)] * 2
        + [jax.ShapeDtypeStruct((NYQUIST_ROWS, half), F32)],
        compiler_params=_params("parallel"),
        name="hyena_filter_spectrum",
    )(hf, hb, cmat, smat)


def _hyena_core_kernel(seq, blk, *refs):
    projections, shared, per_seq = refs[:3], refs[3:19], refs[19:]
    for s in range(projections[0].shape[0]):
        _hyena_core_one_sequence(seq, blk, *[r.at[s] for r in projections], *shared,
                                 *[r.at[s] for r in per_seq])


def _hyena_core_one_sequence(seq, blk, pv_ref, p1_ref, p2_ref, cwv_ref, cw1_ref, cw2_ref, cbv_ref,
                             cb1_ref, cb2_ref, kr0_ref, ki0_ref, kn0_ref, kr1_ref, ki1_ref,
                             kn1_ref, bias0_ref, bias1_ref, c_ref, s_ref, z_ref,
                             u_ref, ub_ref, gate_ref, a_ref, b_ref):
    row = lax.broadcasted_iota(jnp.int32, (blk, 1), 0)
    sign = (1 - 2 * (row & 1)).astype(F32)
    nb = seq // blk
    blocks = [slice(j * blk, (j + 1) * blk) for j in range(nb)]

    def short_conv(dst_ref, x_ref, w_ref, b_ref):
        x = x_ref[...].astype(F32)
        w = w_ref[...]
        time = lax.broadcasted_iota(jnp.int32, (seq, 1), 0)
        prev = jnp.where(time == 0, 0.0, pltpu.roll(x, 1, 0))
        nxt = jnp.where(time == seq - 1, 0.0, pltpu.roll(x, seq - 1, 0))
        dst_ref[...] = prev * w[0:1] + x * w[1:2] + nxt * w[2:3] + b_ref[...]

    def gated_long_conv(kr_ref, ki_ref, kn_ref, bias_ref, write):
        ub_ref[...] = u_ref[...].astype(BF16)
        nyq_in = []
        for rows in blocks:
            a_ref[rows, :] = _dot(c_ref[...], ub_ref[rows, :]).astype(BF16)
            b_ref[rows, :] = _dot(s_ref[...], ub_ref[rows, :]).astype(BF16)
            nyq_in.append(jnp.sum(u_ref[rows, :] * sign, axis=0, keepdims=True))
        for i, rows in enumerate(blocks):
            p = q = nyq = None
            for j, src in enumerate(blocks):
                slot = i - j + nb - 1
                kr = kr_ref[slot * blk:(slot + 1) * blk, :]
                ki = ki_ref[slot * blk:(slot + 1) * blk, :]
                a, b = a_ref[src, :], b_ref[src, :]
                pj, qj = a * kr + b * ki, b * kr - a * ki
                nj = nyq_in[j] * kn_ref[slot:slot + 1, :]
                p, q, nyq = (pj, qj, nj) if j == 0 else (p + pj, q + qj, nyq + nj)
            y = _dot(c_ref[...], p) + _dot(s_ref[...], q)
            y = y + sign * nyq + u_ref[rows, :] * bias_ref[...]
            write(rows, gate_ref[rows, :] * y)

    def to_u(rows, val):
        u_ref[rows, :] = val

    def to_z(rows, val):
        z_ref[rows, :] = val.astype(z_ref.dtype)

    short_conv(u_ref, pv_ref, cwv_ref, cbv_ref)
    short_conv(gate_ref, p1_ref, cw1_ref, cb1_ref)
    gated_long_conv(kr0_ref, ki0_ref, kn0_ref, bias0_ref, to_u)
    short_conv(gate_ref, p2_ref, cw2_ref, cb2_ref)
    gated_long_conv(kr1_ref, ki1_ref, kn1_ref, bias1_ref, to_z)


def _hyena_core(st, blk, proj, conv_w, conv_b, kr, ki, kn, bias, cmat, smat):
    seq, d = st.seq, D_MODEL
    tn = 256
    nj = d // tn
    spec_rows = kr.shape[0]
    proj3 = proj.reshape(st.batch, seq, 3 * d)
    group = max(1, min(HYENA_GROUP_STEPS // seq, st.batch))
    assert st.batch % group == 0

    def part(k):
        return pl.BlockSpec((group, seq, tn), lambda j, b: (b, 0, k * nj + j))

    def cols(rows, k, buffers=2):
        return pl.BlockSpec((rows, tn), lambda j, b: (0, k * nj + j),
                            pipeline_mode=pl.Buffered(buffers))

    in_specs = ([part(0), part(1), part(2)]
                + [cols(3, k) for k in range(3)] + [cols(1, k) for k in range(3)]
                + [cols(spec_rows, 0, 1), cols(spec_rows, 0, 1), cols(NYQUIST_ROWS, 0),
                   cols(spec_rows, 1, 1), cols(spec_rows, 1, 1), cols(NYQUIST_ROWS, 1)]
                + [cols(1, 0), cols(1, 0)]
                + [_resident((blk, blk)), _resident((blk, blk))])
    z = pl.pallas_call(
        functools.partial(_hyena_core_kernel, seq, blk),
        grid=(nj, st.batch // group),
        in_specs=in_specs,
        out_specs=pl.BlockSpec((group, seq, tn), lambda j, b: (b, 0, j)),
        out_shape=jax.ShapeDtypeStruct((st.batch, seq, d), BF16),
        scratch_shapes=[pltpu.VMEM((group, seq, tn), F32), pltpu.VMEM((group, seq, tn), BF16),
                        pltpu.VMEM((group, seq, tn), F32), pltpu.VMEM((group, seq, tn), BF16),
                        pltpu.VMEM((group, seq, tn), BF16)],
        compiler_params=_params("parallel", "parallel"),
        name="hyena_core",
    )(proj3, proj3, proj3, conv_w, conv_w, conv_w, conv_b, conv_b, conv_b,
      kr, ki, kn, kr, ki, kn, bias[0:1], bias[1:2], cmat, smat)
    return z.reshape(st.rows, d)


def _route_kernel(first_tiles, *refs):
    n = len(first_tiles)
    z_refs, x_refs, shared = refs[:n], refs[n:2 * n], refs[2 * n:2 * n + 8]
    h2_ref, x1_refs, route_refs = refs[2 * n + 8], refs[2 * n + 9:3 * n + 9], refs[3 * n + 9:]
    i = pl.program_id(0)
    for s in range(n):
        lo = first_tiles[s]
        hi = first_tiles[s + 1] if s + 1 < n else pl.num_programs(0)

        @pl.when((i >= lo) & (i < hi))
        def _(s=s):
            _route_tile(z_refs[s], x_refs[s], *shared, x1_refs[s], h2_ref, route_refs[s])


def _route_tile(z_ref, x_ref, w_ref, g1_ref, ng_ref, sh_ref, sc_ref, rw_hi_ref, rw_lo_ref, rb_ref,
                x1_ref, h2_ref, route_ref):
    x1 = x_ref[...] + g1_ref[...] * _dot(z_ref[...], w_ref[...])
    x1_ref[...] = x1
    h2 = _modulate(x1, ng_ref[...], sh_ref[...], sc_ref[...])
    _store_token_tiles(h2_ref, h2)
    h_hi = h2.astype(BF16)
    h_lo = (h2 - h_hi.astype(F32)).astype(BF16)
    logits = (_dot(h_hi, rw_hi_ref[...])
              + (_dot(h_lo, rw_hi_ref[...]) + _dot(h_hi, rw_lo_ref[...])) + rb_ref[...])
    lane = lax.broadcasted_iota(jnp.int32, logits.shape, 1)
    neg = -jnp.inf
    logits = jnp.where(lane < N_EXPERTS, logits, neg)
    m1 = jnp.max(logits, axis=-1, keepdims=True)
    i1 = jnp.min(jnp.where(logits == m1, lane, V7X_LANES), axis=-1, keepdims=True)
    rest = jnp.where(lane == i1, neg, logits)
    m2 = jnp.max(rest, axis=-1, keepdims=True)
    i2 = jnp.min(jnp.where(rest == m2, lane, V7X_LANES), axis=-1, keepdims=True)
    e2 = jnp.exp(m2 - m1)
    den = 1.0 + e2
    route = jnp.where(lane == 0, i1.astype(F32), jnp.where(lane == 1, i2.astype(F32), 0.0))
    route_ref[...] = route + jnp.where(lane == 2, 1.0 / den, 0.0) + jnp.where(lane == 3, e2 / den, 0.0)


def _store_token_tiles(ref, val):
    rows = val.shape[0]
    for c in range(LANE_CHUNKS):
        ref[pl.ds(c, rows, stride=LANE_CHUNKS), :] = val[:, c * V7X_LANES:(c + 1) * V7X_LANES]


def _out_proj_route(streams, w_bf16, mod, layer, norm_g, rw_pad, rb_pad):
    d = D_MODEL
    streams = [(st.retiled(WIDE_OUT_TILE), z, x) for st, z, x in streams]
    tm = streams[0][0].tm
    assert all(st.tm == tm and st.row0 % tm == 0 for st, _, _ in streams)
    first = [st.row0 // tm for st, _, _ in streams]
    tiles = sum(st.tiles for st, _, _ in streams)
    rw_hi = rw_pad.astype(BF16)
    rw_lo = (rw_pad - rw_hi.astype(F32)).astype(BF16)

    def local(s, i):
        return jnp.clip(i - first[s], 0, streams[s][0].tiles - 1)

    def cond_row(i):
        row = streams[0][0].cond_row(local(0, i))
        for s in range(1, len(streams)):
            row = jnp.where(i >= first[s], streams[s][0].cond_row(local(s, i)), row)
        return row

    def rows_of(s, width):
        return pl.BlockSpec((tm, width), lambda i: (local(s, i), 0))

    def mod_chunk(chunk):
        return pl.BlockSpec((None, None, 1, d), lambda i: (layer, cond_row(i), 0, chunk))

    n = len(streams)
    in_specs = ([rows_of(s, streams[s][1].shape[1]) for s in range(n)]
                + [rows_of(s, d) for s in range(n)]
                + [_resident(w_bf16.shape), mod_chunk(2), _resident((1, d)), mod_chunk(3),
                   mod_chunk(4), _resident(rw_pad.shape), _resident(rw_pad.shape),
                   _resident(rb_pad.shape)])
    out_specs = ([pl.BlockSpec((tm * LANE_CHUNKS, V7X_LANES), lambda i: (i, 0))]
                 + [rows_of(s, d) for s in range(n)] + [rows_of(s, V7X_LANES) for s in range(n)])
    out_shape = ([jax.ShapeDtypeStruct((tiles * tm * LANE_CHUNKS, V7X_LANES), F32)]
                 + [jax.ShapeDtypeStruct((st.rows, d), F32) for st, _, _ in streams]
                 + [jax.ShapeDtypeStruct((st.rows, V7X_LANES), F32) for st, _, _ in streams])
    outs = pl.pallas_call(
        functools.partial(_route_kernel, first),
        grid=(tiles,),
        in_specs=in_specs,
        out_specs=out_specs,
        out_shape=out_shape,
        compiler_params=_params("arbitrary"),
        name="out_proj_route",
    )(*[z for _, z, _ in streams], *[x for _, _, x in streams], w_bf16, mod,
      norm_g.reshape(1, d), mod, mod, rw_hi, rw_lo, rb_pad)
    return outs[0], outs[1:1 + n], outs[1 + n:]


def _swiglu_part(x_bf16, wg_ref, wu_ref, wd_ref):
    g = _dot(x_bf16, wg_ref[...].astype(BF16))
    u = _dot(x_bf16, wu_ref[...].astype(BF16))
    return _dot((g * jax.nn.sigmoid(g) * u).astype(BF16), wd_ref[...].astype(BF16))


def _out_proj_swiglu_kernel(z_ref, w_ref, b_ref, x_ref, g1_ref, ng_ref, sh_ref, sc_ref, wg_ref,
                            wu_ref, wd_ref, g2_ref, o_ref, x1_ref, h2_ref, acc_ref):
    j = pl.program_id(1)

    @pl.when(j == 0)
    def _():
        x1 = x_ref[...] + g1_ref[...] * (_dot(z_ref[...], w_ref[...]) + b_ref[...])
        x1_ref[...] = x1
        h2_ref[...] = _modulate(x1, ng_ref[...], sh_ref[...], sc_ref[...]).astype(BF16)

    part = _swiglu_part(h2_ref[...], wg_ref, wu_ref, wd_ref)

    @pl.when(j == 0)
    def _():
        acc_ref[...] = part

    @pl.when(j > 0)
    def _():
        acc_ref[...] += part

    @pl.when(j == pl.num_programs(1) - 1)
    def _():
        o_ref[...] = x1_ref[...] + g2_ref[...] * acc_ref[...]


def _out_proj_swiglu(st, z_bf16, w_bf16, bias, x, mod, layer, norm_g, w_gu, w_down):
    st = st.retiled(WIDE_OUT_TILE)
    d, two_f = w_gu.shape
    f = two_f // 2
    tf = DENSE_FF_CHUNK
    nf = f // tf
    rows = pl.BlockSpec((st.tm, d), lambda i, j: (i, 0))
    return pl.pallas_call(
        _out_proj_swiglu_kernel,
        grid=(st.tiles, nf),
        in_specs=[
            rows, _resident(w_bf16.shape), _resident((1, d)), rows, st.mod_spec(layer, 2),
            _resident((1, d)), st.mod_spec(layer, 3), st.mod_spec(layer, 4),
            pl.BlockSpec((d, tf), lambda i, j: (0, j)),
            pl.BlockSpec((d, tf), lambda i, j: (0, nf + j)),
            pl.BlockSpec((tf, d), lambda i, j: (j, 0)),
            st.mod_spec(layer, 5),
        ],
        out_specs=rows,
        out_shape=jax.ShapeDtypeStruct((st.rows, d), F32),
        scratch_shapes=[pltpu.VMEM((st.tm, d), F32), pltpu.VMEM((st.tm, d), BF16),
                        pltpu.VMEM((st.tm, d), F32)],
        compiler_params=_params("parallel", "arbitrary"),
        name="out_proj_swiglu",
    )(z_bf16, w_bf16, bias.reshape(1, d), x, mod, norm_g.reshape(1, d), mod, mod,
      w_gu, w_gu, w_down, mod)


def _moe_plan(route, tm):
    tokens = route.shape[0]
    max_tiles = (2 * tokens) // tm + N_EXPERTS + 1
    expert = route[:, :2].astype(jnp.int32).reshape(-1)
    experts = jnp.arange(N_EXPERTS, dtype=jnp.int32)
    onehot = (expert[:, None] == experts[None, :]).astype(jnp.int32)
    csum = jnp.cumsum(onehot, axis=0)
    rank = jnp.sum(csum * onehot, axis=1) - 1
    counts = csum[-1]
    tiles_per_expert = (counts + tm - 1) // tm
    tiles_end = jnp.cumsum(tiles_per_expert)
    first_tile = tiles_end - tiles_per_expert
    first_sorted = jnp.cumsum(counts) - counts
    pos = jnp.sum((first_tile * tm)[None, :] * onehot, axis=1) + rank
    token = jnp.arange(2 * tokens, dtype=jnp.int32) // 2
    _, order = lax.sort_key_val(pos, token)
    n_tiles = tiles_end[-1:]
    tile = jnp.minimum(jnp.arange(max_tiles, dtype=jnp.int32), n_tiles - 1)
    tile_expert = jnp.sum((tile[:, None] >= tiles_end[None, :]).astype(jnp.int32), axis=1)
    mine = (tile_expert[:, None] == experts[None, :]).astype(jnp.int32)
    done = (tile - jnp.sum(first_tile[None, :] * mine, axis=1)) * tm
    tile_base = jnp.sum(first_sorted[None, :] * mine, axis=1) + done
    tile_valid = jnp.clip(jnp.sum(counts[None, :] * mine, axis=1) - done, 1, tm)
    as_i32 = lambda a: a.astype(jnp.int32)
    return (as_i32(order), as_i32(pos), as_i32(tile_expert), as_i32(tile_base),
            as_i32(tile_valid), as_i32(n_tiles))


def _token_tile_copy(src_hbm, row, dst, slot_row, sem):
    return pltpu.make_async_copy(
        src_hbm.at[pl.ds(pl.multiple_of(row * LANE_CHUNKS, LANE_CHUNKS), LANE_CHUNKS)],
        dst.at[pl.ds(pl.multiple_of(slot_row * LANE_CHUNKS, LANE_CHUNKS), LANE_CHUNKS)],
        sem)


def _moe_ffn_kernel(tm, per_step, n_steps, order_ref, te_ref, base_ref, valid_ref, nt_ref, x_hbm,
                    wg_ref, wu_ref, wd_ref, y_ref, xbuf, xd_ref, acc_ref, sem):
    t, j = pl.program_id(0), pl.program_id(1)
    last_j = n_steps - 1
    n_tiles = nt_ref[0]
    slot = t % 2
    total = per_step * n_steps

    def copy_row(tile_base, tile_last, r, into):
        token = order_ref[tile_base + jnp.minimum(r, tile_last)]
        return _token_tile_copy(x_hbm, token, xbuf.at[into], r, sem.at[into])

    @pl.when((t == 0) & (j == 0))
    def _():
        base, last = base_ref[0], valid_ref[0] - 1

        def body(r, carry):
            copy_row(base, last, r, 0).start()
            return carry
        lax.fori_loop(0, total, body, 0)

    @pl.when((j == 0) & (t <= n_tiles))
    def _():
        pltpu.make_async_copy(x_hbm.at[pl.ds(0, total * LANE_CHUNKS)],
                              xbuf.at[slot, pl.ds(0, total * LANE_CHUNKS)], sem.at[slot]).wait()

    @pl.when((j == 0) & (t < n_tiles))
    def _():
        for c in range(LANE_CHUNKS):
            xd_ref[:, c * V7X_LANES:(c + 1) * V7X_LANES] = (
                xbuf[slot, pl.ds(c, tm, stride=LANE_CHUNKS), :].astype(BF16))

    @pl.when(t < n_tiles)
    def _():
        nxt = jnp.minimum(t + 1, n_tiles - 1)
        base, last = base_ref[nxt], valid_ref[nxt] - 1
        for k in range(per_step):
            copy_row(base, last, j * per_step + k, 1 - slot).start()
        part = _swiglu_part(xd_ref[...], wg_ref, wu_ref, wd_ref)

        @pl.when(j == 0)
        def _():
            acc_ref[...] = part

        @pl.when(j > 0)
        def _():
            acc_ref[...] += part

        @pl.when(j == last_j)
        def _():
            _store_token_tiles(y_ref, acc_ref[...])

    @pl.when((t >= n_tiles) & (j == last_j))
    def _():
        y_ref[...] = jnp.zeros_like(y_ref)


def _moe_ffn(pool, order, tile_expert, tile_base, tile_valid, n_tiles, w_gu, w_down, tm):
    n_e, d, two_f = w_gu.shape
    f = two_f // 2
    tf = MOE_FF_CHUNK
    nf = f // tf
    max_tiles = tile_expert.shape[0]
    rows = tm * LANE_CHUNKS
    per_step = -(-tm // nf)
    buf_rows = per_step * nf * LANE_CHUNKS
    grid_spec = pltpu.PrefetchScalarGridSpec(
        num_scalar_prefetch=5,
        grid=(max_tiles, nf),
        in_specs=[
            pl.BlockSpec(memory_space=pl.ANY),
            pl.BlockSpec((None, d, tf), lambda t, j, order, te, *_: (te[t], 0, j)),
            pl.BlockSpec((None, d, tf), lambda t, j, order, te, *_: (te[t], 0, nf + j)),
            pl.BlockSpec((None, tf, d), lambda t, j, order, te, *_: (te[t], j, 0)),
        ],
        out_specs=pl.BlockSpec((rows, V7X_LANES), lambda t, j, *_: (t, 0)),
        scratch_shapes=[pltpu.VMEM((2, buf_rows, V7X_LANES), F32), pltpu.VMEM((tm, d), BF16),
                        pltpu.VMEM((tm, d), F32), pltpu.SemaphoreType.DMA((2,))],
    )
    return pl.pallas_call(
        functools.partial(_moe_ffn_kernel, tm, per_step, nf),
        grid_spec=grid_spec,
        out_shape=jax.ShapeDtypeStruct((max_tiles * rows, V7X_LANES), F32),
        compiler_params=_params("arbitrary", "arbitrary"),
        name="moe_ffn",
    )(order, tile_expert, tile_base, tile_valid, n_tiles, pool, w_gu, w_gu, w_down)


def _moe_combine_kernel(tm, token0, pos_ref, y_hbm, route_ref, x_ref, g2_ref, o_ref, ybuf, sem):
    i = pl.program_id(0)
    slot = i % 2
    rows = tm * LANE_CHUNKS

    def start_gather(tile, into):
        def body(r, carry):
            a = 2 * (token0 + tile * tm + r)
            for k in range(2):
                _token_tile_copy(y_hbm, pos_ref[a + k], ybuf.at[into], k * tm + r,
                                 sem.at[into]).start(priority=k)
            return carry
        lax.fori_loop(0, tm, body, 0, unroll=4)

    @pl.when(i == 0)
    def _():
        start_gather(0, 0)

    pltpu.make_async_copy(y_hbm.at[pl.ds(0, 2 * rows)], ybuf.at[slot], sem.at[slot]).wait()

    @pl.when(i + 1 < pl.num_programs(0))
    def _():
        start_gather(i + 1, 1 - slot)

    route = route_ref[...]
    lane = lax.broadcasted_iota(jnp.int32, route.shape, 1)
    gate0 = jnp.sum(jnp.where(lane == 2, route, 0.0), axis=-1, keepdims=True)
    gate1 = jnp.sum(jnp.where(lane == 3, route, 0.0), axis=-1, keepdims=True)
    for c in range(LANE_CHUNKS):
        cols = slice(c * V7X_LANES, (c + 1) * V7X_LANES)
        y0 = ybuf[slot, pl.ds(c, tm, stride=LANE_CHUNKS), :]
        y1 = ybuf[slot, pl.ds(rows + c, tm, stride=LANE_CHUNKS), :]
        o_ref[:, cols] = x_ref[:, cols] + g2_ref[:, cols] * (gate0 * y0 + gate1 * y1)


def _moe_combine(st, y_slots, pos, route, x1, mod, layer):
    d = D_MODEL
    tm = st.tm
    rows = tm * LANE_CHUNKS
    grid_spec = pltpu.PrefetchScalarGridSpec(
        num_scalar_prefetch=1,
        grid=(st.tiles,),
        in_specs=[
            pl.BlockSpec(memory_space=pl.ANY),
            st.row_spec(V7X_LANES), st.row_spec(d), st.mod_spec(layer, 5),
        ],
        out_specs=st.row_spec(d),
        scratch_shapes=[pltpu.VMEM((2, 2 * rows, V7X_LANES), F32), pltpu.SemaphoreType.DMA((2,))],
    )
    return pl.pallas_call(
        functools.partial(_moe_combine_kernel, tm, st.row0),
        grid_spec=grid_spec,
        out_shape=jax.ShapeDtypeStruct((st.rows, d), F32),
        compiler_params=_params("arbitrary"),
        name="moe_combine",
    )(pos, y_slots, route, x1, mod)


def _head_rmsnorm(x, g2):
    lane = lax.broadcasted_iota(jnp.int32, x.shape, 1)
    lo = lane < HEAD_DIM
    sq = x * x
    s_lo = jnp.sum(jnp.where(lo, sq, 0.0), axis=-1, keepdims=True)
    s_hi = jnp.sum(jnp.where(lo, 0.0, sq), axis=-1, keepdims=True)
    ms = jnp.where(lo, s_lo, s_hi) * (1.0 / HEAD_DIM)
    return x * lax.rsqrt(ms + EPS) * g2


def _rope(x, cos, sin_signed):
    q4 = HEAD_DIM // 4
    lane = lax.broadcasted_iota(jnp.int32, x.shape, 1)
    first = (lane & q4) == 0
    width = x.shape[1]
    partner = jnp.where(first, pltpu.roll(x, width - q4, 1), pltpu.roll(x, q4, 1))
    return x * cos + partner * sin_signed


def _attention_kernel(past, use_rope, lam_init, *refs):
    refs = list(refs)
    n_seq = refs[0].shape[0]
    n_in = 3 + (2 if past else 0)
    seq_refs, refs = refs[:n_in], refs[n_in:]
    if use_rope:
        rope_refs, refs = refs[:4], refs[4:]
    else:
        rope_refs = []
    param_refs, refs = refs[:4], refs[4:]
    n_out = 1 if past else 2
    out_refs, (kall_ref, vall_ref) = refs[:n_out], refs[n_out:]
    for s in range(n_seq):
        _attention_one_sequence(past, use_rope, lam_init, *[r.at[s] for r in seq_refs],
                                *rope_refs, *param_refs, *[r.at[s] for r in out_refs],
                                kall_ref.at[s], vall_ref.at[s])


def _attention_one_sequence(past, use_rope, lam_init, *refs):
    refs = list(refs)
    q_ref, k_ref, v_ref = refs.pop(0), refs.pop(0), refs.pop(0)
    if past:
        ck_ref, cv_ref = refs.pop(0), refs.pop(0)
    if use_rope:
        cosq_ref, sinq_ref, cosk_ref, sink_ref = (refs.pop(0) for _ in range(4))
    qg_ref, kg_ref, lam_ref, sg_ref = (refs.pop(0) for _ in range(4))
    o_ref = refs.pop(0)
    nk_ref = None if past else refs.pop(0)
    kall_ref, vall_ref = refs

    qi = pl.program_id(2)

    @pl.when(qi == 0)
    def _():
        k = _head_rmsnorm(k_ref[...], kg_ref[...])
        if nk_ref is not None:
            nk_ref[...] = k
        if use_rope:
            k = _rope(k, cosk_ref[...], sink_ref[...])
        if past:
            kall_ref[0:past, :] = ck_ref[...].astype(BF16)
            vall_ref[0:past, :] = cv_ref[...].astype(BF16)
        kall_ref[past:, :] = k.astype(BF16)
        vall_ref[past:, :] = v_ref[...].astype(BF16)

    q = _head_rmsnorm(q_ref[...], qg_ref[...])
    if use_rope:
        q = _rope(q, cosq_ref[...], sinq_ref[...])
    q = q * (HEAD_DIM ** -0.5 * math.log2(math.e))
    tq = q.shape[0]
    nt = (((1,), (1,)), ((), ()))
    lv = lam_ref[...]
    lam = (jnp.exp(jnp.sum(lv[0:1] * lv[1:2], axis=-1, keepdims=True))
           - jnp.exp(jnp.sum(lv[2:3] * lv[3:4], axis=-1, keepdims=True)) + lam_init)

    def attend(qm):
        s = lax.dot_general(qm.astype(BF16), kall_ref[...], nt, preferred_element_type=F32)
        p = jnp.exp2(s - jnp.max(s, axis=-1, keepdims=True))
        norm = 1.0 / jnp.sum(p, axis=-1, keepdims=True)
        return _dot(p.astype(BF16), vall_ref[...]) * norm

    row_chunks = [slice(r, r + ATTENTION_ROWS) for r in range(0, tq, ATTENTION_ROWS)]
    lo = lax.broadcasted_iota(jnp.int32, q.shape, 1) < HEAD_DIM
    first = jnp.where(lo, q, 0.0)
    second = jnp.where(lo, 0.0, q)
    a0 = jnp.concatenate([attend(first[rows]) for rows in row_chunks], axis=0)
    a1 = jnp.concatenate([attend(second[rows]) for rows in row_chunks], axis=0)
    o = a0 - lam * a1
    ms = jnp.mean(o * o, axis=-1, keepdims=True)
    o = o * lax.rsqrt(ms + EPS) * sg_ref[...] * (1.0 - lam_init)
    o_ref[...] = o.astype(o_ref.dtype)


def _attention(st, q, k, v, q_g, k_g, lam_vecs, subln_g, lam_init, cache_k=None, cache_v=None,
               rope=None, cache_layer=0):
    seq, hd = st.seq, N_HEADS * V_DIM
    tq = min(seq, ATTENTION_QUERIES)
    nq = seq // tq
    group = max(1, min(ATTENTION_GROUP_QUERIES // seq, st.batch))
    assert st.batch % group == 0
    past = 0 if cache_k is None else cache_k.shape[2]
    lk = past + seq
    q3, k3, v3 = (a.reshape(st.batch, seq, hd) for a in (q, k, v))
    qblk = pl.BlockSpec((group, tq, V_DIM), lambda b, h, i: (b, i, h))
    kblk = pl.BlockSpec((group, seq, V_DIM), lambda b, h, i: (b, 0, h))
    in_specs = [qblk, kblk, kblk]
    args = [q3, k3, v3]
    if past:
        layers = cache_k.shape[1]
        cblk = pl.BlockSpec((group, None, past, V_DIM), lambda b, h, i: (b, cache_layer, 0, h))
        in_specs += [cblk, cblk]
        args += [cache_k.reshape(st.batch, layers, past, hd),
                 cache_v.reshape(st.batch, layers, past, hd)]
    if rope is not None:
        cos2, sin2 = rope
        tq_tab = pl.BlockSpec((tq, V_DIM), lambda b, h, i: (i, 0))
        k_tab = pl.BlockSpec((seq, V_DIM), lambda b, h, i: (0, 0))
        in_specs += [tq_tab, tq_tab, k_tab, k_tab]
        args += [cos2, sin2, cos2, sin2]
    small = lambda shape: pl.BlockSpec(shape, lambda b, h, i: (0,) * len(shape))
    in_specs += [small((1, V_DIM)), small((1, V_DIM)), small((4, HEAD_DIM)), small((1, V_DIM))]
    args += [jnp.tile(q_g, 2).reshape(1, V_DIM), jnp.tile(k_g, 2).reshape(1, V_DIM), lam_vecs,
             subln_g.reshape(1, V_DIM)]
    out_specs = [qblk]
    out_shape = [jax.ShapeDtypeStruct((st.batch, seq, hd), BF16)]
    if not past:
        out_specs.append(kblk)
        out_shape.append(jax.ShapeDtypeStruct((st.batch, seq, hd), F32))
    outs = pl.pallas_call(
        functools.partial(_attention_kernel, past, rope is not None, lam_init),
        grid=(st.batch // group, N_HEADS, nq),
        in_specs=in_specs,
        out_specs=out_specs,
        out_shape=out_shape,
        scratch_shapes=[pltpu.VMEM((group, lk, V_DIM), BF16),
                        pltpu.VMEM((group, lk, V_DIM), BF16)],
        compiler_params=_params("parallel", "parallel", "arbitrary"),
        name="diff_attention",
    )(*args)
    return [o.reshape(st.rows, hd) for o in outs]


def _dft_matrices(seq):
    idx = np.arange(seq, dtype=np.int64)
    ang = (np.outer(idx, idx) % (2 * seq)).astype(np.float64) * (math.pi / seq)
    return jnp.asarray(np.cos(ang), dtype=BF16), jnp.asarray(np.sin(ang), dtype=BF16)


def _filter_features(seq):
    t = jnp.linspace(0.0, 1.0, seq, dtype=F32)[:, None]
    bands = (HY_EMB - 1) // 2
    w_ang = 2.0 * math.pi * jnp.arange(seq, dtype=F32)[:, None] / seq
    f = jnp.linspace(1e-4, bands - 1, bands, dtype=F32)[None, :]
    ang = f * w_ang
    feats = jnp.concatenate([t, jnp.cos(ang), -jnp.sin(ang)], axis=-1)
    return jnp.pad(feats, ((0, 0), (0, V7X_LANES - HY_EMB)))


def _decay_rates():
    min_decay = math.log(HY_DECAY_TARGET) / HY_SLOW_PCT
    max_decay = math.log(HY_DECAY_TARGET) / HY_FAST_PCT
    return jnp.linspace(min_decay, max_decay, D_MODEL, dtype=F32)[None, :]


def _rope_tables(seq):
    rows = seq // GRID_W
    row = jnp.repeat(jnp.arange(rows, dtype=F32), GRID_W)
    col = jnp.tile(jnp.arange(GRID_W, dtype=F32), rows)
    quarter = HEAD_DIM // 4
    inv = ROPE_BASE ** (-jnp.arange(quarter, dtype=F32) / quarter)

    def axis_angles(pos):
        a = pos[:, None] * inv[None, :]
        return jnp.concatenate([a, a], axis=-1)

    ang = jnp.concatenate([axis_angles(row), axis_angles(col)], axis=-1)
    sign = jnp.where((jnp.arange(HEAD_DIM) & quarter) == 0, -1.0, 1.0).astype(F32)
    return jnp.tile(jnp.cos(ang), (1, 2)), jnp.tile(jnp.sin(ang) * sign[None, :], (1, 2))


def kernel(x_prompt, x_sample, cache_k, cache_v, c, c_ctx, ada_w, ada_b, norm_g, hy_in_w, hy_in_b, hy_conv_w, hy_conv_b, hy_f_w1, hy_f_b1, hy_f_w2, hy_f_b2, hy_f_freq, hy_f_w3, hy_bias, hy_out_w, hy_out_b, at_qkv_w, at_q_g, at_k_g, at_lam, at_subln_g, at_out_w, dn_w_gu, dn_w_down, mo_router_w, mo_router_b, mo_w_gu, mo_w_down):
    d = D_MODEL
    batch, seq = x_prompt.shape[:2]
    dec_batch, dec_seq = x_sample.shape[:2]
    streams = [
        (_Stream(batch, seq, 0, False), x_prompt.reshape(batch * seq, d), None),
        (_Stream(dec_batch, dec_seq, 1, True, row0=batch * seq),
         x_sample.reshape(dec_batch * dec_seq, d), (cache_k, cache_v)),
    ]

    cond = jnp.concatenate(
        [c_ctx[None, :], c, jnp.zeros((COND_ROWS - 1 - dec_batch, d), F32)], axis=0)
    mod = _adaln(cond, ada_w, ada_b)
    mod = mod.reshape(mod.shape[0], COND_ROWS, 1, 6 * d)

    in_w, out_w = hy_in_w[0].astype(BF16), hy_out_w[0].astype(BF16)
    qkv_w, at_out = at_qkv_w[0].astype(BF16), at_out_w[0].astype(BF16)
    dn_gu, dn_down = dn_w_gu[0].astype(BF16), dn_w_down[0].astype(BF16)
    mo_gu, mo_down = mo_w_gu[0], mo_w_down[0]
    w1_pad = jnp.pad(hy_f_w1[0], ((0, V7X_LANES - HY_EMB), (0, 0)))
    rw_pad = jnp.pad(mo_router_w[0], ((0, 0), (0, V7X_LANES - N_EXPERTS)))
    rb_pad = jnp.pad(mo_router_b[0], (0, V7X_LANES - N_EXPERTS)).reshape(1, V7X_LANES)
    deltas = _decay_rates()
    lam_init = 0.8 - 0.6 * math.exp(-0.3 * 1)

    attended = []
    for st, x, cache in streams:
        blk = min(st.seq, HYENA_BLOCK)
        cmat, smat = _dft_matrices(blk)
        hf, hb = _filter_time(st.seq, _filter_features(st.seq), w1_pad, hy_f_b1[0], hy_f_w2[0],
                              hy_f_b2[0], hy_f_freq[0], hy_f_w3[0], deltas)
        kr, ki, kn = _filter_spectrum(st.seq, blk, hf, hb, cmat, smat)
        (proj,) = _mod_matmul(st, x, norm_g[0, 0], mod, 0, in_w, hy_in_b[0], 1, BF16)
        z = _hyena_core(st, blk, proj, hy_conv_w[0], hy_conv_b[0].reshape(1, 3 * d), kr, ki, kn,
                        hy_bias[0], cmat, smat)
        x = _out_proj_swiglu(st, z, out_w, hy_out_b[0], x, mod, 0, norm_g[0, 1], dn_gu, dn_down)

        q, k, v = _mod_matmul(st, x, norm_g[1, 0], mod, 1, qkv_w, None, 3, F32)
        if cache is None:
            o, new_k = _attention(st, q, k, v, at_q_g[0], at_k_g[0], at_lam[0], at_subln_g[0],
                                  lam_init)
            new_kv = (new_k, v)
        else:
            (o,) = _attention(st, q, k, v, at_q_g[0], at_k_g[0], at_lam[0], at_subln_g[0],
                              lam_init, cache[0], cache[1], _rope_tables(st.seq))
        attended.append((st, o, x))

    pool, x1s, routes = _out_proj_route(attended, at_out, mod, 1, norm_g[1, 1], rw_pad, rb_pad)
    order, pos, *tile_table = _moe_plan(
        jnp.concatenate([route[:, :4] for route in routes], axis=0), MOE_TILE)
    y_slots = _moe_ffn(pool, order, *tile_table, mo_gu, mo_down, MOE_TILE)
    results = [_moe_combine(st, y_slots, pos, route, x1, mod, 1).reshape(st.batch, st.seq, d)
               for (st, _, _), x1, route in zip(attended, x1s, routes)]

    new_k, new_v = new_kv
    return (results[0], results[1],
            new_k.reshape(batch, 1, seq, N_HEADS, 2, HEAD_DIM),
            new_v.reshape(batch, 1, seq, N_HEADS, V_DIM))
```

```python
import functools
import math

import numpy as np
import jax
import jax.numpy as jnp
from jax import lax
from jax.experimental import pallas as pl
from jax.experimental.pallas import tpu as pltpu

F32 = jnp.float32
BF16 = jnp.bfloat16

D_MODEL = 1024
GRID_W = 64
HY_ORDER = 2
HY_EMB = 33
HY_FW = 64
HY_DECAY_TARGET = 1e-2
HY_FAST_PCT = 0.3
HY_SLOW_PCT = 1.5
N_HEADS = 8
HEAD_DIM = 64
V_DIM = 2 * HEAD_DIM
ROPE_BASE = 10000.0
D_FF = 2816
N_EXPERTS = 8
D_FF_EXPERT = 3584
EPS = 1e-6

V7X_LANES = 128
V7X_VMEM_LIMIT_BYTES = 56 * 1024 * 1024
LANE_CHUNKS = D_MODEL // V7X_LANES
COND_ROWS = 16
TOKEN_TILE = 1024
WIDE_OUT_TILE = 512
MOE_TILE = 1024
MOE_FF_CHUNK = 512
DENSE_FF_CHUNK = D_FF // 2
HYENA_BLOCK = 512
HYENA_GROUP_STEPS = 1024
NYQUIST_ROWS = 8
ATTENTION_QUERIES = 1024
ATTENTION_GROUP_QUERIES = 1024
ATTENTION_ROWS = 256


def _params(*semantics):
    return pltpu.CompilerParams(dimension_semantics=semantics,
                                vmem_limit_bytes=V7X_VMEM_LIMIT_BYTES)


def _resident(shape):
    zeros = (0,) * len(shape)
    return pl.BlockSpec(shape, lambda *_: zeros, pipeline_mode=pl.Buffered(1))


def _dot(a, b):
    return jnp.dot(a, b, preferred_element_type=F32)


def _dot_f32(a, b):
    a_hi, b_hi = a.astype(BF16), b.astype(BF16)
    a_lo = (a - a_hi.astype(F32)).astype(BF16)
    b_lo = (b - b_hi.astype(F32)).astype(BF16)
    return _dot(a_hi, b_hi) + (_dot(a_lo, b_hi) + _dot(a_hi, b_lo))


def _modulate(x, g, shift, scale):
    ms = jnp.mean(x * x, axis=-1, keepdims=True)
    return (x * lax.rsqrt(ms + EPS) * g) * (1.0 + scale) + shift


def _adaln_kernel(cond_ref, w_ref, b_ref, o_ref):
    c = cond_ref[...]
    o_ref[...] = _dot_f32(c * jax.nn.sigmoid(c), w_ref[...]) + b_ref[...]


def _adaln(cond, ada_w, ada_b):
    depth, d, n = ada_w.shape
    tn = 1536
    return pl.pallas_call(
        _adaln_kernel,
        grid=(depth, n // tn),
        in_specs=[
            pl.BlockSpec((COND_ROWS, d), lambda i, j: (0, 0)),
            pl.BlockSpec((None, d, tn), lambda i, j: (i, 0, j)),
            pl.BlockSpec((None, 1, tn), lambda i, j: (i, 0, j)),
        ],
        out_specs=pl.BlockSpec((None, COND_ROWS, tn), lambda i, j: (i, 0, j)),
        out_shape=jax.ShapeDtypeStruct((depth, COND_ROWS, n), F32),
        compiler_params=_params("parallel", "parallel"),
        name="adaln",
    )(cond, ada_w, ada_b.reshape(depth, 1, n))


class _Stream:
    def __init__(self, batch, seq, cond_row0, per_seq_cond, row0=0, tile=TOKEN_TILE):
        self.batch, self.seq = batch, seq
        self.rows = batch * seq
        self.row0 = row0
        self._cond = (cond_row0, per_seq_cond)
        if per_seq_cond:
            self.tm = min(tile, seq)
            tiles_per_seq = seq // self.tm
            self.cond_row = lambda i: cond_row0 + i // tiles_per_seq
        else:
            self.tm = min(tile, self.rows)
            self.cond_row = lambda i: cond_row0
        self.tiles = self.rows // self.tm

    def retiled(self, tile):
        return _Stream(self.batch, self.seq, *self._cond, row0=self.row0, tile=tile)

    def mod_spec(self, layer, chunk):
        return pl.BlockSpec((None, None, 1, D_MODEL),
                            lambda i, *_: (layer, self.cond_row(i), 0, chunk))

    def row_spec(self, width):
        return pl.BlockSpec((self.tm, width), lambda i, *_: (i, 0))


def _mod_matmul_kernel(n_out, has_bias, x_ref, g_ref, sh_ref, sc_ref, w_ref, *rest):
    if has_bias:
        b_ref, out_refs = rest[0], rest[1:]
    else:
        b_ref, out_refs = None, rest
    h = _modulate(x_ref[...], g_ref[...], sh_ref[...], sc_ref[...])
    y = _dot(h.astype(BF16), w_ref[...])
    if has_bias:
        y = y + b_ref[...]
    width = y.shape[1] // n_out
    for k, o_ref in enumerate(out_refs):
        o_ref[...] = y[:, k * width:(k + 1) * width].astype(o_ref.dtype)


def _mod_matmul(st, x, norm_g, mod, layer, w_bf16, bias, n_out, out_dtype):
    st = st.retiled(WIDE_OUT_TILE)
    d, n = w_bf16.shape
    in_specs = [st.row_spec(d), _resident((1, d)), st.mod_spec(layer, 0), st.mod_spec(layer, 1),
                _resident((d, n))]
    args = [x, norm_g.reshape(1, d), mod, mod, w_bf16]
    if bias is not None:
        in_specs.append(_resident((1, n)))
        args.append(bias.reshape(1, n))
    width = n // n_out
    outs = pl.pallas_call(
        functools.partial(_mod_matmul_kernel, n_out, bias is not None),
        grid=(st.tiles,),
        in_specs=in_specs,
        out_specs=[st.row_spec(width)] * n_out,
        out_shape=[jax.ShapeDtypeStruct((st.rows, width), out_dtype)] * n_out,
        compiler_params=_params("parallel"),
        name="mod_matmul",
    )(*args)
    return outs


def _filter_time_kernel(feats_ref, w1_ref, b1_ref, w2_ref, b2_ref, fr_ref, w3_ref, dl_ref,
                        hf_ref, hb_ref):
    feats = feats_ref[...]
    fr = fr_ref[...]
    h = jnp.sin(fr[0:1] * (_dot_f32(feats, w1_ref[...]) + b1_ref[...]))
    h = jnp.sin(fr[1:2] * (_dot_f32(h, w2_ref[...]) + b2_ref[...]))
    h = _dot_f32(h, w3_ref[...])
    t = feats[:, 0:1]
    decay = jnp.exp(-t * jnp.abs(dl_ref[...]))
    half = HY_ORDER * D_MODEL
    decay2 = jnp.concatenate([decay] * HY_ORDER, axis=1)
    hf_ref[...] = h[:, :half] * decay2
    hb_ref[...] = jnp.where(t == 0.0, 0.0, h[:, half:] * decay2)


def _filter_time(seq, feats_pad, w1_pad, b1, w2, b2, freq, w3, deltas):
    tl = min(seq, 256)
    half = HY_ORDER * D_MODEL
    out = jax.ShapeDtypeStruct((seq, half), F32)
    return pl.pallas_call(
        _filter_time_kernel,
        grid=(seq // tl,),
        in_specs=[
            pl.BlockSpec((tl, V7X_LANES), lambda i: (i, 0)),
            _resident(w1_pad.shape), _resident((1, HY_FW)), _resident((HY_FW, HY_FW)),
            _resident((1, HY_FW)), _resident((2, HY_FW)), _resident(w3.shape),
            _resident((1, D_MODEL)),
        ],
        out_specs=[pl.BlockSpec((tl, half), lambda i: (i, 0))] * 2,
        out_shape=[out, out],
        compiler_params=_params("parallel"),
        name="hyena_filter_time",
    )(feats_pad, w1_pad, b1.reshape(1, HY_FW), w2, b2.reshape(1, HY_FW), freq, w3, deltas)


def _filter_spectrum_kernel(blk, nb, hf_ref, hb_ref, c_ref, s_ref, kr_ref, ki_ref, kn_ref):
    row = lax.broadcasted_iota(jnp.int32, (blk, 1), 0)
    sg = (1 - 2 * (row & 1)).astype(F32)
    wgt = jnp.where(row == 0, 1.0, 2.0) * (1.0 / (2 * blk))
    fwd, bwd = [], []
    for j in range(nb):
        rows = slice(j * blk, (j + 1) * blk)
        for ref, out in ((hf_ref, fwd), (hb_ref, bwd)):
            x = ref[rows, :]
            xb = x.astype(BF16)
            out.append((_dot(c_ref[...], xb), _dot(s_ref[...], xb),
                        jnp.sum(x * sg, axis=0, keepdims=True), x[0:1, :],
                        xb[0:1, :].astype(F32)))
    kn_ref[...] = jnp.zeros_like(kn_ref)
    for d in range(-(nb - 1), nb):
        if d == 0:
            (fc, fs, fn, _, _), (bc, bs, bn, _, _) = fwd[0], bwd[0]
            kr, ki, kn = fc + bc, bs - fs, fn + bn
        else:
            parts, im_sign = (fwd, -1.0) if d > 0 else (bwd, 1.0)
            c1, s1, n1, _, _ = parts[abs(d)]
            c0, s0, n0, x0, x0_seen = parts[abs(d) - 1]
            kr, ki, kn = c1 + sg * (c0 - x0_seen), im_sign * (s1 + sg * s0), n1 + n0 - x0
        slot = d + nb - 1
        kr_ref[slot * blk:(slot + 1) * blk, :] = (wgt * kr).astype(kr_ref.dtype)
        ki_ref[slot * blk:(slot + 1) * blk, :] = (wgt * ki).astype(ki_ref.dtype)
        kn_ref[slot:slot + 1, :] = kn * (1.0 / (2 * blk))


def _filter_spectrum(seq, blk, hf, hb, cmat, smat):
    nb = seq // blk
    half = hf.shape[1]
    tn = 256
    col = pl.BlockSpec((seq, tn), lambda j: (0, j))
    spec_rows = (2 * nb - 1) * blk
    out_col = pl.BlockSpec((spec_rows, tn), lambda j: (0, j))
    return pl.pallas_call(
        functools.partial(_filter_spectrum_kernel, blk, nb),
        grid=(half // tn,),
        in_specs=[col, col, _resident((blk, blk)), _resident((blk, blk))],
        out_specs=[out_col, out_col, pl.BlockSpec((NYQUIST_ROWS, tn), lambda j: (0, j))],
        out_shape=[jax.ShapeDtypeStruct((spec_rows, half), BF16)] * 2
        + [jax.ShapeDtypeStruct((NYQUIST_ROWS, half), F32)],
        compiler_params=_params("parallel"),
        name="hyena_filter_spectrum",
    )(hf, hb, cmat, smat)


def _hyena_core_kernel(seq, blk, *refs):
    projections, shared, per_seq = refs[:3], refs[3:19], refs[19:]
    for s in range(projections[0].shape[0]):
        _hyena_core_one_sequence(seq, blk, *[r.at[s] for r in projections], *shared,
                                 *[r.at[s] for r in per_seq])


def _hyena_core_one_sequence(seq, blk, pv_ref, p1_ref, p2_ref, cwv_ref, cw1_ref, cw2_ref, cbv_ref,
                             cb1_ref, cb2_ref, kr0_ref, ki0_ref, kn0_ref, kr1_ref, ki1_ref,
                             kn1_ref, bias0_ref, bias1_ref, c_ref, s_ref, z_ref,
                             u_ref, ub_ref, gate_ref, a_ref, b_ref):
    row = lax.broadcasted_iota(jnp.int32, (blk, 1), 0)
    sign = (1 - 2 * (row & 1)).astype(F32)
    nb = seq // blk
    blocks = [slice(j * blk, (j + 1) * blk) for j in range(nb)]

    def short_conv(dst_ref, x_ref, w_ref, b_ref):
        x = x_ref[...].astype(F32)
        w = w_ref[...]
        time = lax.broadcasted_iota(jnp.int32, (seq, 1), 0)
        prev = jnp.where(time == 0, 0.0, pltpu.roll(x, 1, 0))
        nxt = jnp.where(time == seq - 1, 0.0, pltpu.roll(x, seq - 1, 0))
        dst_ref[...] = prev * w[0:1] + x * w[1:2] + nxt * w[2:3] + b_ref[...]

    def gated_long_conv(kr_ref, ki_ref, kn_ref, bias_ref, write):
        ub_ref[...] = u_ref[...].astype(BF16)
        nyq_in = []
        for rows in blocks:
            a_ref[rows, :] = _dot(c_ref[...], ub_ref[rows, :]).astype(BF16)
            b_ref[rows, :] = _dot(s_ref[...], ub_ref[rows, :]).astype(BF16)
            nyq_in.append(jnp.sum(u_ref[rows, :] * sign, axis=0, keepdims=True))
        for i, rows in enumerate(blocks):
            p = q = nyq = None
            for j, src in enumerate(blocks):
                slot = i - j + nb - 1
                kr = kr_ref[slot * blk:(slot + 1) * blk, :]
                ki = ki_ref[slot * blk:(slot + 1) * blk, :]
                a, b = a_ref[src, :], b_ref[src, :]
                pj, qj = a * kr + b * ki, b * kr - a * ki
                nj = nyq_in[j] * kn_ref[slot:slot + 1, :]
                p, q, nyq = (pj, qj, nj) if j == 0 else (p + pj, q + qj, nyq + nj)
            y = _dot(c_ref[...], p) + _dot(s_ref[...], q)
            y = y + sign * nyq + u_ref[rows, :] * bias_ref[...]
            write(rows, gate_ref[rows, :] * y)

    def to_u(rows, val):
        u_ref[rows, :] = val

    def to_z(rows, val):
        z_ref[rows, :] = val.astype(z_ref.dtype)

    short_conv(u_ref, pv_ref, cwv_ref, cbv_ref)
    short_conv(gate_ref, p1_ref, cw1_ref, cb1_ref)
    gated_long_conv(kr0_ref, ki0_ref, kn0_ref, bias0_ref, to_u)
    short_conv(gate_ref, p2_ref, cw2_ref, cb2_ref)
    gated_long_conv(kr1_ref, ki1_ref, kn1_ref, bias1_ref, to_z)


def _hyena_core(st, blk, proj, conv_w, conv_b, kr, ki, kn, bias, cmat, smat):
    seq, d = st.seq, D_MODEL
    tn = 256
    nj = d // tn
    spec_rows = kr.shape[0]
    proj3 = proj.reshape(st.batch, seq, 3 * d)
    group = max(1, min(HYENA_GROUP_STEPS // seq, st.batch))
    assert st.batch % group == 0

    def part(k):
        return pl.BlockSpec((group, seq, tn), lambda j, b: (b, 0, k * nj + j))

    def cols(rows, k, buffers=2):
        return pl.BlockSpec((rows, tn), lambda j, b: (0, k * nj + j),
                            pipeline_mode=pl.Buffered(buffers))

    in_specs = ([part(0), part(1), part(2)]
                + [cols(3, k) for k in range(3)] + [cols(1, k) for k in range(3)]
                + [cols(spec_rows, 0, 1), cols(spec_rows, 0, 1), cols(NYQUIST_ROWS, 0),
                   cols(spec_rows, 1, 1), cols(spec_rows, 1, 1), cols(NYQUIST_ROWS, 1)]
                + [cols(1, 0), cols(1, 0)]
                + [_resident((blk, blk)), _resident((blk, blk))])
    z = pl.pallas_call(
        functools.partial(_hyena_core_kernel, seq, blk),
        grid=(nj, st.batch // group),
        in_specs=in_specs,
        out_specs=pl.BlockSpec((group, seq, tn), lambda j, b: (b, 0, j)),
        out_shape=jax.ShapeDtypeStruct((st.batch, seq, d), BF16),
        scratch_shapes=[pltpu.VMEM((group, seq, tn), F32), pltpu.VMEM((group, seq, tn), BF16),
                        pltpu.VMEM((group, seq, tn), F32), pltpu.VMEM((group, seq, tn), BF16),
                        pltpu.VMEM((group, seq, tn), BF16)],
        compiler_params=_params("parallel", "parallel"),
        name="hyena_core",
    )(proj3, proj3, proj3, conv_w, conv_w, conv_w, conv_b, conv_b, conv_b,
      kr, ki, kn, kr, ki, kn, bias[0:1], bias[1:2], cmat, smat)
    return z.reshape(st.rows, d)


def _route_kernel(first_tiles, *refs):
    n = len(first_tiles)
    z_refs, x_refs, shared = refs[:n], refs[n:2 * n], refs[2 * n:2 * n + 8]
    h2_ref, x1_refs, route_refs = refs[2 * n + 8], refs[2 * n + 9:3 * n + 9], refs[3 * n + 9:]
    i = pl.program_id(0)
    for s in range(n):
        lo = first_tiles[s]
        hi = first_tiles[s + 1] if s + 1 < n else pl.num_programs(0)

        @pl.when((i >= lo) & (i < hi))
        def _(s=s):
            _route_tile(z_refs[s], x_refs[s], *shared, x1_refs[s], h2_ref, route_refs[s])


def _route_tile(z_ref, x_ref, w_ref, g1_ref, ng_ref, sh_ref, sc_ref, rw_hi_ref, rw_lo_ref, rb_ref,
                x1_ref, h2_ref, route_ref):
    x1 = x_ref[...] + g1_ref[...] * _dot(z_ref[...], w_ref[...])
    x1_ref[...] = x1
    h2 = _modulate(x1, ng_ref[...], sh_ref[...], sc_ref[...])
    _store_token_tiles(h2_ref, h2)
    h_hi = h2.astype(BF16)
    h_lo = (h2 - h_hi.astype(F32)).astype(BF16)
    logits = (_dot(h_hi, rw_hi_ref[...])
              + (_dot(h_lo, rw_hi_ref[...]) + _dot(h_hi, rw_lo_ref[...])) + rb_ref[...])
    lane = lax.broadcasted_iota(jnp.int32, logits.shape, 1)
    neg = -jnp.inf
    logits = jnp.where(lane < N_EXPERTS, logits, neg)
    m1 = jnp.max(logits, axis=-1, keepdims=True)
    i1 = jnp.min(jnp.where(logits == m1, lane, V7X_LANES), axis=-1, keepdims=True)
    rest = jnp.where(lane == i1, neg, logits)
    m2 = jnp.max(rest, axis=-1, keepdims=True)
    i2 = jnp.min(jnp.where(rest == m2, lane, V7X_LANES), axis=-1, keepdims=True)
    e2 = jnp.exp(m2 - m1)
    den = 1.0 + e2
    route = jnp.where(lane == 0, i1.astype(F32), jnp.where(lane == 1, i2.astype(F32), 0.0))
    route_ref[...] = route + jnp.where(lane == 2, 1.0 / den, 0.0) + jnp.where(lane == 3, e2 / den, 0.0)


def _store_token_tiles(ref, val):
    rows = val.shape[0]
    for c in range(LANE_CHUNKS):
        ref[pl.ds(c, rows, stride=LANE_CHUNKS), :] = val[:, c * V7X_LANES:(c + 1) * V7X_LANES]


def _out_proj_route(streams, w_bf16, mod, layer, norm_g, rw_pad, rb_pad):
    d = D_MODEL
    streams = [(st.retiled(WIDE_OUT_TILE), z, x) for st, z, x in streams]
    tm = streams[0][0].tm
    assert all(st.tm == tm and st.row0 % tm == 0 for st, _, _ in streams)
    first = [st.row0 // tm for st, _, _ in streams]
    tiles = sum(st.tiles for st, _, _ in streams)
    rw_hi = rw_pad.astype(BF16)
    rw_lo = (rw_pad - rw_hi.astype(F32)).astype(BF16)

    def local(s, i):
        return jnp.clip(i - first[s], 0, streams[s][0].tiles - 1)

    def cond_row(i):
        row = streams[0][0].cond_row(local(0, i))
        for s in range(1, len(streams)):
            row = jnp.where(i >= first[s], streams[s][0].cond_row(local(s, i)), row)
        return row

    def rows_of(s, width):
        return pl.BlockSpec((tm, width), lambda i: (local(s, i), 0))

    def mod_chunk(chunk):
        return pl.BlockSpec((None, None, 1, d), lambda i: (layer, cond_row(i), 0, chunk))

    n = len(streams)
    in_specs = ([rows_of(s, streams[s][1].shape[1]) for s in range(n)]
                + [rows_of(s, d) for s in range(n)]
                + [_resident(w_bf16.shape), mod_chunk(2), _resident((1, d)), mod_chunk(3),
                   mod_chunk(4), _resident(rw_pad.shape), _resident(rw_pad.shape),
                   _resident(rb_pad.shape)])
    out_specs = ([pl.BlockSpec((tm * LANE_CHUNKS, V7X_LANES), lambda i: (i, 0))]
                 + [rows_of(s, d) for s in range(n)] + [rows_of(s, V7X_LANES) for s in range(n)])
    out_shape = ([jax.ShapeDtypeStruct((tiles * tm * LANE_CHUNKS, V7X_LANES), F32)]
                 + [jax.ShapeDtypeStruct((st.rows, d), F32) for st, _, _ in streams]
                 + [jax.ShapeDtypeStruct((st.rows, V7X_LANES), F32) for st, _, _ in streams])
    outs = pl.pallas_call(
        functools.partial(_route_kernel, first),
        grid=(tiles,),
        in_specs=in_specs,
        out_specs=out_specs,
        out_shape=out_shape,
        compiler_params=_params("arbitrary"),
        name="out_proj_route",
    )(*[z for _, z, _ in streams], *[x for _, _, x in streams], w_bf16, mod,
      norm_g.reshape(1, d), mod, mod, rw_hi, rw_lo, rb_pad)
    return outs[0], outs[1:1 + n], outs[1 + n:]


def _swiglu_part(x_bf16, wg_ref, wu_ref, wd_ref):
    g = _dot(x_bf16, wg_ref[...].astype(BF16))
    u = _dot(x_bf16, wu_ref[...].astype(BF16))
    return _dot((g * jax.nn.sigmoid(g) * u).astype(BF16), wd_ref[...].astype(BF16))


def _out_proj_swiglu_kernel(z_ref, w_ref, b_ref, x_ref, g1_ref, ng_ref, sh_ref, sc_ref, wg_ref,
                            wu_ref, wd_ref, g2_ref, o_ref, x1_ref, h2_ref, acc_ref):
    j = pl.program_id(1)

    @pl.when(j == 0)
    def _():
        x1 = x_ref[...] + g1_ref[...] * (_dot(z_ref[...], w_ref[...]) + b_ref[...])
        x1_ref[...] = x1
        h2_ref[...] = _modulate(x1, ng_ref[...], sh_ref[...], sc_ref[...]).astype(BF16)

    part = _swiglu_part(h2_ref[...], wg_ref, wu_ref, wd_ref)

    @pl.when(j == 0)
    def _():
        acc_ref[...] = part

    @pl.when(j > 0)
    def _():
        acc_ref[...] += part

    @pl.when(j == pl.num_programs(1) - 1)
    def _():
        o_ref[...] = x1_ref[...] + g2_ref[...] * acc_ref[...]


def _out_proj_swiglu(st, z_bf16, w_bf16, bias, x, mod, layer, norm_g, w_gu, w_down):
    st = st.retiled(WIDE_OUT_TILE)
    d, two_f = w_gu.shape
    f = two_f // 2
    tf = DENSE_FF_CHUNK
    nf = f // tf
    rows = pl.BlockSpec((st.tm, d), lambda i, j: (i, 0))
    return pl.pallas_call(
        _out_proj_swiglu_kernel,
        grid=(st.tiles, nf),
        in_specs=[
            rows, _resident(w_bf16.shape), _resident((1, d)), rows, st.mod_spec(layer, 2),
            _resident((1, d)), st.mod_spec(layer, 3), st.mod_spec(layer, 4),
            pl.BlockSpec((d, tf), lambda i, j: (0, j)),
            pl.BlockSpec((d, tf), lambda i, j: (0, nf + j)),
            pl.BlockSpec((tf, d), lambda i, j: (j, 0)),
            st.mod_spec(layer, 5),
        ],
        out_specs=rows,
        out_shape=jax.ShapeDtypeStruct((st.rows, d), F32),
        scratch_shapes=[pltpu.VMEM((st.tm, d), F32), pltpu.VMEM((st.tm, d), BF16),
                        pltpu.VMEM((st.tm, d), F32)],
        compiler_params=_params("parallel", "arbitrary"),
        name="out_proj_swiglu",
    )(z_bf16, w_bf16, bias.reshape(1, d), x, mod, norm_g.reshape(1, d), mod, mod,
      w_gu, w_gu, w_down, mod)


def _moe_plan(route, tm):
    tokens = route.shape[0]
    max_tiles = (2 * tokens) // tm + N_EXPERTS + 1
    expert = route[:, :2].astype(jnp.int32).reshape(-1)
    experts = jnp.arange(N_EXPERTS, dtype=jnp.int32)
    onehot = (expert[:, None] == experts[None, :]).astype(jnp.int32)
    csum = jnp.cumsum(onehot, axis=0)
    rank = jnp.sum(csum * onehot, axis=1) - 1
    counts = csum[-1]
    tiles_per_expert = (counts + tm - 1) // tm
    tiles_end = jnp.cumsum(tiles_per_expert)
    first_tile = tiles_end - tiles_per_expert
    first_sorted = jnp.cumsum(counts) - counts
    pos = jnp.sum((first_tile * tm)[None, :] * onehot, axis=1) + rank
    token = jnp.arange(2 * tokens, dtype=jnp.int32) // 2
    _, order = lax.sort_key_val(pos, token)
    n_tiles = tiles_end[-1:]
    tile = jnp.minimum(jnp.arange(max_tiles, dtype=jnp.int32), n_tiles - 1)
    tile_expert = jnp.sum((tile[:, None] >= tiles_end[None, :]).astype(jnp.int32), axis=1)
    mine = (tile_expert[:, None] == experts[None, :]).astype(jnp.int32)
    done = (tile - jnp.sum(first_tile[None, :] * mine, axis=1)) * tm
    tile_base = jnp.sum(first_sorted[None, :] * mine, axis=1) + done
    tile_valid = jnp.clip(jnp.sum(counts[None, :] * mine, axis=1) - done, 1, tm)
    as_i32 = lambda a: a.astype(jnp.int32)
    return (as_i32(order), as_i32(pos), as_i32(tile_expert), as_i32(tile_base),
            as_i32(tile_valid), as_i32(n_tiles))


def _token_tile_copy(src_hbm, row, dst, slot_row, sem):
    return pltpu.make_async_copy(
        src_hbm.at[pl.ds(pl.multiple_of(row * LANE_CHUNKS, LANE_CHUNKS), LANE_CHUNKS)],
        dst.at[pl.ds(pl.multiple_of(slot_row * LANE_CHUNKS, LANE_CHUNKS), LANE_CHUNKS)],
        sem)


def _moe_ffn_kernel(tm, per_step, n_steps, order_ref, te_ref, base_ref, valid_ref, nt_ref, x_hbm,
                    wg_ref, wu_ref, wd_ref, y_ref, xbuf, xd_ref, acc_ref, sem):
    t, j = pl.program_id(0), pl.program_id(1)
    last_j = n_steps - 1
    n_tiles = nt_ref[0]
    slot = t % 2
    total = per_step * n_steps

    def copy_row(tile_base, tile_last, r, into):
        token = order_ref[tile_base + jnp.minimum(r, tile_last)]
        return _token_tile_copy(x_hbm, token, xbuf.at[into], r, sem.at[into])

    @pl.when((t == 0) & (j == 0))
    def _():
        base, last = base_ref[0], valid_ref[0] - 1

        def body(r, carry):
            copy_row(base, last, r, 0).start()
            return carry
        lax.fori_loop(0, total, body, 0)

    @pl.when((j == 0) & (t <= n_tiles))
    def _():
        pltpu.make_async_copy(x_hbm.at[pl.ds(0, total * LANE_CHUNKS)],
                              xbuf.at[slot, pl.ds(0, total * LANE_CHUNKS)], sem.at[slot]).wait()

    def start_next_tile_copies():
        nxt = jnp.minimum(t + 1, n_tiles - 1)
        base, last = base_ref[nxt], valid_ref[nxt] - 1
        for k in range(per_step):
            copy_row(base, last, j * per_step + k, 1 - slot).start()

    @pl.when((j == 0) & (t < n_tiles))
    def _():
        start_next_tile_copies()
        x = jnp.concatenate([xbuf[slot, pl.ds(c, tm, stride=LANE_CHUNKS), :].astype(BF16)
                             for c in range(LANE_CHUNKS)], axis=1)
        xd_ref[...] = x
        acc_ref[...] = _swiglu_part(x, wg_ref, wu_ref, wd_ref)

    @pl.when((j > 0) & (t < n_tiles))
    def _():
        start_next_tile_copies()
        acc_ref[...] += _swiglu_part(xd_ref[...], wg_ref, wu_ref, wd_ref)

        @pl.when(j == last_j)
        def _():
            _store_token_tiles(y_ref, acc_ref[...])

    @pl.when((t >= n_tiles) & (j == last_j))
    def _():
        y_ref[...] = jnp.zeros_like(y_ref)


def _moe_ffn(pool, order, tile_expert, tile_base, tile_valid, n_tiles, w_gu, w_down, tm):
    n_e, d, two_f = w_gu.shape
    f = two_f // 2
    tf = MOE_FF_CHUNK
    nf = f // tf
    max_tiles = tile_expert.shape[0]
    rows = tm * LANE_CHUNKS
    per_step = -(-tm // nf)
    buf_rows = per_step * nf * LANE_CHUNKS
    grid_spec = pltpu.PrefetchScalarGridSpec(
        num_scalar_prefetch=5,
        grid=(max_tiles, nf),
        in_specs=[
            pl.BlockSpec(memory_space=pl.ANY),
            pl.BlockSpec((None, d, tf), lambda t, j, order, te, *_: (te[t], 0, j)),
            pl.BlockSpec((None, d, tf), lambda t, j, order, te, *_: (te[t], 0, nf + j)),
            pl.BlockSpec((None, tf, d), lambda t, j, order, te, *_: (te[t], j, 0)),
        ],
        out_specs=pl.BlockSpec((rows, V7X_LANES), lambda t, j, *_: (t, 0)),
        scratch_shapes=[pltpu.VMEM((2, buf_rows, V7X_LANES), F32), pltpu.VMEM((tm, d), BF16),
                        pltpu.VMEM((tm, d), F32), pltpu.SemaphoreType.DMA((2,))],
    )
    return pl.pallas_call(
        functools.partial(_moe_ffn_kernel, tm, per_step, nf),
        grid_spec=grid_spec,
        out_shape=jax.ShapeDtypeStruct((max_tiles * rows, V7X_LANES), F32),
        compiler_params=_params("arbitrary", "arbitrary"),
        name="moe_ffn",
    )(order, tile_expert, tile_base, tile_valid, n_tiles, pool, w_gu, w_gu, w_down)


def _moe_combine_kernel(tm, token0, pos_ref, y_hbm, route_ref, x_ref, g2_ref, o_ref, ybuf, sem):
    i = pl.program_id(0)
    slot = i % 2
    rows = tm * LANE_CHUNKS

    def start_gather(tile, into):
        def body(r, carry):
            a = 2 * (token0 + tile * tm + r)
            for k in range(2):
                _token_tile_copy(y_hbm, pos_ref[a + k], ybuf.at[into], k * tm + r,
                                 sem.at[into]).start(priority=k)
            return carry
        lax.fori_loop(0, tm, body, 0, unroll=4)

    @pl.when(i == 0)
    def _():
        start_gather(0, 0)

    pltpu.make_async_copy(y_hbm.at[pl.ds(0, 2 * rows)], ybuf.at[slot], sem.at[slot]).wait()

    @pl.when(i + 1 < pl.num_programs(0))
    def _():
        start_gather(i + 1, 1 - slot)

    route = route_ref[...]
    lane = lax.broadcasted_iota(jnp.int32, route.shape, 1)
    gate0 = jnp.sum(jnp.where(lane == 2, route, 0.0), axis=-1, keepdims=True)
    gate1 = jnp.sum(jnp.where(lane == 3, route, 0.0), axis=-1, keepdims=True)
    for c in range(LANE_CHUNKS):
        cols = slice(c * V7X_LANES, (c + 1) * V7X_LANES)
        y0 = ybuf[slot, pl.ds(c, tm, stride=LANE_CHUNKS), :]
        y1 = ybuf[slot, pl.ds(rows + c, tm, stride=LANE_CHUNKS), :]
        o_ref[:, cols] = x_ref[:, cols] + g2_ref[:, cols] * (gate0 * y0 + gate1 * y1)


def _moe_combine(st, y_slots, pos, route, x1, mod, layer):
    d = D_MODEL
    tm = st.tm
    rows = tm * LANE_CHUNKS
    grid_spec = pltpu.PrefetchScalarGridSpec(
        num_scalar_prefetch=1,
        grid=(st.tiles,),
        in_specs=[
            pl.BlockSpec(memory_space=pl.ANY),
            st.row_spec(V7X_LANES), st.row_spec(d), st.mod_spec(layer, 5),
        ],
        out_specs=st.row_spec(d),
        scratch_shapes=[pltpu.VMEM((2, 2 * rows, V7X_LANES), F32), pltpu.SemaphoreType.DMA((2,))],
    )
    return pl.pallas_call(
        functools.partial(_moe_combine_kernel, tm, st.row0),
        grid_spec=grid_spec,
        out_shape=jax.ShapeDtypeStruct((st.rows, d), F32),
        compiler_params=_params("arbitrary"),
        name="moe_combine",
    )(pos, y_slots, route, x1, mod)


def _head_rmsnorm(x, g2):
    lane = lax.broadcasted_iota(jnp.int32, x.shape, 1)
    lo = lane < HEAD_DIM
    sq = x * x
    s_lo = jnp.sum(jnp.where(lo, sq, 0.0), axis=-1, keepdims=True)
    s_hi = jnp.sum(jnp.where(lo, 0.0, sq), axis=-1, keepdims=True)
    ms = jnp.where(lo, s_lo, s_hi) * (1.0 / HEAD_DIM)
    return x * lax.rsqrt(ms + EPS) * g2


def _rope(x, cos, sin_signed):
    q4 = HEAD_DIM // 4
    lane = lax.broadcasted_iota(jnp.int32, x.shape, 1)
    first = (lane & q4) == 0
    width = x.shape[1]
    partner = jnp.where(first, pltpu.roll(x, width - q4, 1), pltpu.roll(x, q4, 1))
    return x * cos + partner * sin_signed


def _attention_kernel(past, use_rope, lam_init, *refs):
    refs = list(refs)
    n_seq = refs[0].shape[0]
    n_in = 3 + (2 if past else 0)
    seq_refs, refs = refs[:n_in], refs[n_in:]
    if use_rope:
        rope_refs, refs = refs[:4], refs[4:]
    else:
        rope_refs = []
    param_refs, refs = refs[:4], refs[4:]
    n_out = 1 if past else 2
    out_refs, (kall_ref, vall_ref) = refs[:n_out], refs[n_out:]
    for s in range(n_seq):
        _attention_one_sequence(past, use_rope, lam_init, *[r.at[s] for r in seq_refs],
                                *rope_refs, *param_refs, *[r.at[s] for r in out_refs],
                                kall_ref.at[s], vall_ref.at[s])


def _attention_one_sequence(past, use_rope, lam_init, *refs):
    refs = list(refs)
    q_ref, k_ref, v_ref = refs.pop(0), refs.pop(0), refs.pop(0)
    if past:
        ck_ref, cv_ref = refs.pop(0), refs.pop(0)
    if use_rope:
        cosq_ref, sinq_ref, cosk_ref, sink_ref = (refs.pop(0) for _ in range(4))
    qg_ref, kg_ref, lam_ref, sg_ref = (refs.pop(0) for _ in range(4))
    o_ref = refs.pop(0)
    nk_ref = None if past else refs.pop(0)
    kall_ref, vall_ref = refs

    qi = pl.program_id(2)

    @pl.when(qi == 0)
    def _():
        k = _head_rmsnorm(k_ref[...], kg_ref[...])
        if nk_ref is not None:
            nk_ref[...] = k
        if use_rope:
            k = _rope(k, cosk_ref[...], sink_ref[...])
        if past:
            kall_ref[0:past, :] = ck_ref[...].astype(BF16)
            vall_ref[0:past, :] = cv_ref[...].astype(BF16)
        kall_ref[past:, :] = k.astype(BF16)
        vall_ref[past:, :] = v_ref[...].astype(BF16)

    q = _head_rmsnorm(q_ref[...], qg_ref[...])
    if use_rope:
        q = _rope(q, cosq_ref[...], sinq_ref[...])
    q = q * (HEAD_DIM ** -0.5 * math.log2(math.e))
    tq = q.shape[0]
    nt = (((1,), (1,)), ((), ()))
    lv = lam_ref[...]
    lam = (jnp.exp(jnp.sum(lv[0:1] * lv[1:2], axis=-1, keepdims=True))
           - jnp.exp(jnp.sum(lv[2:3] * lv[3:4], axis=-1, keepdims=True)) + lam_init)

    def attend(qm):
        s = lax.dot_general(qm.astype(BF16), kall_ref[...], nt, preferred_element_type=F32)
        p = jnp.exp2(s - jnp.max(s, axis=-1, keepdims=True))
        norm = 1.0 / jnp.sum(p, axis=-1, keepdims=True)
        return _dot(p.astype(BF16), vall_ref[...]) * norm

    row_chunks = [slice(r, r + ATTENTION_ROWS) for r in range(0, tq, ATTENTION_ROWS)]
    lo = lax.broadcasted_iota(jnp.int32, q.shape, 1) < HEAD_DIM
    first = jnp.where(lo, q, 0.0)
    second = jnp.where(lo, 0.0, q)
    a0 = jnp.concatenate([attend(first[rows]) for rows in row_chunks], axis=0)
    a1 = jnp.concatenate([attend(second[rows]) for rows in row_chunks], axis=0)
    o = a0 - lam * a1
    ms = jnp.mean(o * o, axis=-1, keepdims=True)
    o = o * lax.rsqrt(ms + EPS) * sg_ref[...] * (1.0 - lam_init)
    o_ref[...] = o.astype(o_ref.dtype)


def _attention(st, q, k, v, q_g, k_g, lam_vecs, subln_g, lam_init, cache_k=None, cache_v=None,
               rope=None, cache_layer=0):
    seq, hd = st.seq, N_HEADS * V_DIM
    tq = min(seq, ATTENTION_QUERIES)
    nq = seq // tq
    group = max(1, min(ATTENTION_GROUP_QUERIES // seq, st.batch))
    assert st.batch % group == 0
    past = 0 if cache_k is None else cache_k.shape[2]
    lk = past + seq
    q3, k3, v3 = (a.reshape(st.batch, seq, hd) for a in (q, k, v))
    qblk = pl.BlockSpec((group, tq, V_DIM), lambda b, h, i: (b, i, h))
    kblk = pl.BlockSpec((group, seq, V_DIM), lambda b, h, i: (b, 0, h))
    in_specs = [qblk, kblk, kblk]
    args = [q3, k3, v3]
    if past:
        layers = cache_k.shape[1]
        cblk = pl.BlockSpec((group, None, past, V_DIM), lambda b, h, i: (b, cache_layer, 0, h))
        in_specs += [cblk, cblk]
        args += [cache_k.reshape(st.batch, layers, past, hd),
                 cache_v.reshape(st.batch, layers, past, hd)]
    if rope is not None:
        cos2, sin2 = rope
        tq_tab = pl.BlockSpec((tq, V_DIM), lambda b, h, i: (i, 0))
        k_tab = pl.BlockSpec((seq, V_DIM), lambda b, h, i: (0, 0))
        in_specs += [tq_tab, tq_tab, k_tab, k_tab]
        args += [cos2, sin2, cos2, sin2]
    small = lambda shape: pl.BlockSpec(shape, lambda b, h, i: (0,) * len(shape))
    in_specs += [small((1, V_DIM)), small((1, V_DIM)), small((4, HEAD_DIM)), small((1, V_DIM))]
    args += [jnp.tile(q_g, 2).reshape(1, V_DIM), jnp.tile(k_g, 2).reshape(1, V_DIM), lam_vecs,
             subln_g.reshape(1, V_DIM)]
    out_specs = [qblk]
    out_shape = [jax.ShapeDtypeStruct((st.batch, seq, hd), BF16)]
    if not past:
        out_specs.append(kblk)
        out_shape.append(jax.ShapeDtypeStruct((st.batch, seq, hd), F32))
    outs = pl.pallas_call(
        functools.partial(_attention_kernel, past, rope is not None, lam_init),
        grid=(st.batch // group, N_HEADS, nq),
        in_specs=in_specs,
        out_specs=out_specs,
        out_shape=out_shape,
        scratch_shapes=[pltpu.VMEM((group, lk, V_DIM), BF16),
                        pltpu.VMEM((group, lk, V_DIM), BF16)],
        compiler_params=_params("parallel", "parallel", "arbitrary"),
        name="diff_attention",
    )(*args)
    return [o.reshape(st.rows, hd) for o in outs]


def _dft_matrices(seq):
    idx = np.arange(seq, dtype=np.int64)
    ang = (np.outer(idx, idx) % (2 * seq)).astype(np.float64) * (math.pi / seq)
    return jnp.asarray(np.cos(ang), dtype=BF16), jnp.asarray(np.sin(ang), dtype=BF16)


def _filter_features(seq):
    t = jnp.linspace(0.0, 1.0, seq, dtype=F32)[:, None]
    bands = (HY_EMB - 1) // 2
    w_ang = 2.0 * math.pi * jnp.arange(seq, dtype=F32)[:, None] / seq
    f = jnp.linspace(1e-4, bands - 1, bands, dtype=F32)[None, :]
    ang = f * w_ang
    feats = jnp.concatenate([t, jnp.cos(ang), -jnp.sin(ang)], axis=-1)
    return jnp.pad(feats, ((0, 0), (0, V7X_LANES - HY_EMB)))


def _decay_rates():
    min_decay = math.log(HY_DECAY_TARGET) / HY_SLOW_PCT
    max_decay = math.log(HY_DECAY_TARGET) / HY_FAST_PCT
    return jnp.linspace(min_decay, max_decay, D_MODEL, dtype=F32)[None, :]


def _rope_tables(seq):
    rows = seq // GRID_W
    row = jnp.repeat(jnp.arange(rows, dtype=F32), GRID_W)
    col = jnp.tile(jnp.arange(GRID_W, dtype=F32), rows)
    quarter = HEAD_DIM // 4
    inv = ROPE_BASE ** (-jnp.arange(quarter, dtype=F32) / quarter)

    def axis_angles(pos):
        a = pos[:, None] * inv[None, :]
        return jnp.concatenate([a, a], axis=-1)

    ang = jnp.concatenate([axis_angles(row), axis_angles(col)], axis=-1)
    sign = jnp.where((jnp.arange(HEAD_DIM) & quarter) == 0, -1.0, 1.0).astype(F32)
    return jnp.tile(jnp.cos(ang), (1, 2)), jnp.tile(jnp.sin(ang) * sign[None, :], (1, 2))


def kernel(x_prompt, x_sample, cache_k, cache_v, c, c_ctx, ada_w, ada_b, norm_g, hy_in_w, hy_in_b, hy_conv_w, hy_conv_b, hy_f_w1, hy_f_b1, hy_f_w2, hy_f_b2, hy_f_freq, hy_f_w3, hy_bias, hy_out_w, hy_out_b, at_qkv_w, at_q_g, at_k_g, at_lam, at_subln_g, at_out_w, dn_w_gu, dn_w_down, mo_router_w, mo_router_b, mo_w_gu, mo_w_down):
    d = D_MODEL
    batch, seq = x_prompt.shape[:2]
    dec_batch, dec_seq = x_sample.shape[:2]
    streams = [
        (_Stream(batch, seq, 0, False), x_prompt.reshape(batch * seq, d), None),
        (_Stream(dec_batch, dec_seq, 1, True, row0=batch * seq),
         x_sample.reshape(dec_batch * dec_seq, d), (cache_k, cache_v)),
    ]

    cond = jnp.concatenate(
        [c_ctx[None, :], c, jnp.zeros((COND_ROWS - 1 - dec_batch, d), F32)], axis=0)
    mod = _adaln(cond, ada_w, ada_b)
    mod = mod.reshape(mod.shape[0], COND_ROWS, 1, 6 * d)

    in_w, out_w = hy_in_w[0].astype(BF16), hy_out_w[0].astype(BF16)
    qkv_w, at_out = at_qkv_w[0].astype(BF16), at_out_w[0].astype(BF16)
    dn_gu, dn_down = dn_w_gu[0].astype(BF16), dn_w_down[0].astype(BF16)
    mo_gu, mo_down = mo_w_gu[0], mo_w_down[0]
    w1_pad = jnp.pad(hy_f_w1[0], ((0, V7X_LANES - HY_EMB), (0, 0)))
    rw_pad = jnp.pad(mo_router_w[0], ((0, 0), (0, V7X_LANES - N_EXPERTS)))
    rb_pad = jnp.pad(mo_router_b[0], (0, V7X_LANES - N_EXPERTS)).reshape(1, V7X_LANES)
    deltas = _decay_rates()
    lam_init = 0.8 - 0.6 * math.exp(-0.3 * 1)

    attended = []
    for st, x, cache in streams:
        blk = min(st.seq, HYENA_BLOCK)
        cmat, smat = _dft_matrices(blk)
        hf, hb = _filter_time(st.seq, _filter_features(st.seq), w1_pad, hy_f_b1[0], hy_f_w2[0],
                              hy_f_b2[0], hy_f_freq[0], hy_f_w3[0], deltas)
        kr, ki, kn = _filter_spectrum(st.seq, blk, hf, hb, cmat, smat)
        (proj,) = _mod_matmul(st, x, norm_g[0, 0], mod, 0, in_w, hy_in_b[0], 1, BF16)
        z = _hyena_core(st, blk, proj, hy_conv_w[0], hy_conv_b[0].reshape(1, 3 * d), kr, ki, kn,
                        hy_bias[0], cmat, smat)
        x = _out_proj_swiglu(st, z, out_w, hy_out_b[0], x, mod, 0, norm_g[0, 1], dn_gu, dn_down)

        q, k, v = _mod_matmul(st, x, norm_g[1, 0], mod, 1, qkv_w, None, 3, F32)
        if cache is None:
            o, new_k = _attention(st, q, k, v, at_q_g[0], at_k_g[0], at_lam[0], at_subln_g[0],
                                  lam_init)
            new_kv = (new_k, v)
        else:
            (o,) = _attention(st, q, k, v, at_q_g[0], at_k_g[0], at_lam[0], at_subln_g[0],
                              lam_init, cache[0], cache[1], _rope_tables(st.seq))
        attended.append((st, o, x))

    pool, x1s, routes = _out_proj_route(attended, at_out, mod, 1, norm_g[1, 1], rw_pad, rb_pad)
    order, pos, *tile_table = _moe_plan(
        jnp.concatenate([route[:, :4] for route in routes], axis=0), MOE_TILE)
    y_slots = _moe_ffn(pool, order, *tile_table, mo_gu, mo_down, MOE_TILE)
    results = [_moe_combine(st, y_slots, pos, route, x1, mod, 1).reshape(st.batch, st.seq, d)
               for (st, _, _), x1, route in zip(attended, x1s, routes)]

    new_k, new_v = new_kv
    return (results[0], results[1],
            new_k.reshape(batch, 1, seq, N_HEADS, 2, HEAD_DIM),
            new_v.reshape(batch, 1, seq, N_HEADS, V_DIM))
```

```python
import functools
import math

import numpy as np
import jax
import jax.numpy as jnp
from jax import lax
from jax.experimental import pallas as pl
from jax.experimental.pallas import tpu as pltpu

F32 = jnp.float32
BF16 = jnp.bfloat16

D_MODEL = 1024
GRID_W = 64
HY_ORDER = 2
HY_EMB = 33
HY_FW = 64
HY_DECAY_TARGET = 1e-2
HY_FAST_PCT = 0.3
HY_SLOW_PCT = 1.5
N_HEADS = 8
HEAD_DIM = 64
V_DIM = 2 * HEAD_DIM
ROPE_BASE = 10000.0
D_FF = 2816
N_EXPERTS = 8
D_FF_EXPERT = 3584
EPS = 1e-6

V7X_LANES = 128
V7X_VMEM_LIMIT_BYTES = 56 * 1024 * 1024
LANE_CHUNKS = D_MODEL // V7X_LANES
COND_ROWS = 16
TOKEN_TILE = 1024
WIDE_OUT_TILE = 512
MOE_TILE = 1024
MOE_FF_CHUNK = 512
DENSE_FF_CHUNK = D_FF // 2
HYENA_BLOCK = 512
HYENA_GROUP_STEPS = 1024
NYQUIST_ROWS = 8
ATTENTION_QUERIES = 1024
ATTENTION_GROUP_QUERIES = 1024
ATTENTION_ROWS = 256


def _params(*semantics):
    return pltpu.CompilerParams(dimension_semantics=semantics,
                                vmem_limit_bytes=V7X_VMEM_LIMIT_BYTES)


def _resident(shape):
    zeros = (0,) * len(shape)
    return pl.BlockSpec(shape, lambda *_: zeros, pipeline_mode=pl.Buffered(1))


def _dot(a, b):
    return jnp.dot(a, b, preferred_element_type=F32)


def _dot_f32(a, b):
    a_hi, b_hi = a.astype(BF16), b.astype(BF16)
    a_lo = (a - a_hi.astype(F32)).astype(BF16)
    b_lo = (b - b_hi.astype(F32)).astype(BF16)
    return _dot(a_hi, b_hi) + (_dot(a_lo, b_hi) + _dot(a_hi, b_lo))


def _modulate(x, g, shift, scale):
    ms = jnp.mean(x * x, axis=-1, keepdims=True)
    return (x * lax.rsqrt(ms + EPS) * g) * (1.0 + scale) + shift


def _adaln_kernel(cond_ref, w_ref, b_ref, o_ref):
    c = cond_ref[...]
    o_ref[...] = _dot_f32(c * jax.nn.sigmoid(c), w_ref[...]) + b_ref[...]


def _adaln(cond, ada_w, ada_b):
    depth, d, n = ada_w.shape
    tn = 1536
    return pl.pallas_call(
        _adaln_kernel,
        grid=(depth, n // tn),
        in_specs=[
            pl.BlockSpec((COND_ROWS, d), lambda i, j: (0, 0)),
            pl.BlockSpec((None, d, tn), lambda i, j: (i, 0, j)),
            pl.BlockSpec((None, 1, tn), lambda i, j: (i, 0, j)),
        ],
        out_specs=pl.BlockSpec((None, COND_ROWS, tn), lambda i, j: (i, 0, j)),
        out_shape=jax.ShapeDtypeStruct((depth, COND_ROWS, n), F32),
        compiler_params=_params("parallel", "parallel"),
        name="adaln",
    )(cond, ada_w, ada_b.reshape(depth, 1, n))


class _Stream:
    def __init__(self, batch, seq, cond_row0, per_seq_cond, row0=0, tile=TOKEN_TILE):
        self.batch, self.seq = batch, seq
        self.rows = batch * seq
        self.row0 = row0
        self._cond = (cond_row0, per_seq_cond)
        if per_seq_cond:
            self.tm = min(tile, seq)
            tiles_per_seq = seq // self.tm
            self.cond_row = lambda i: cond_row0 + i // tiles_per_seq
        else:
            self.tm = min(tile, self.rows)
            self.cond_row = lambda i: cond_row0
        self.tiles = self.rows // self.tm

    def retiled(self, tile):
        return _Stream(self.batch, self.seq, *self._cond, row0=self.row0, tile=tile)

    def mod_spec(self, layer, chunk):
        return pl.BlockSpec((None, None, 1, D_MODEL),
                            lambda i, *_: (layer, self.cond_row(i), 0, chunk))

    def row_spec(self, width):
        return pl.BlockSpec((self.tm, width), lambda i, *_: (i, 0))


def _mod_matmul_kernel(n_out, has_bias, x_ref, g_ref, sh_ref, sc_ref, w_ref, *rest):
    if has_bias:
        b_ref, out_refs = rest[0], rest[1:]
    else:
        b_ref, out_refs = None, rest
    h = _modulate(x_ref[...], g_ref[...], sh_ref[...], sc_ref[...])
    y = _dot(h.astype(BF16), w_ref[...])
    if has_bias:
        y = y + b_ref[...]
    width = y.shape[1] // n_out
    for k, o_ref in enumerate(out_refs):
        o_ref[...] = y[:, k * width:(k + 1) * width].astype(o_ref.dtype)


def _mod_matmul(st, x, norm_g, mod, layer, w_bf16, bias, n_out, out_dtype):
    st = st.retiled(WIDE_OUT_TILE)
    d, n = w_bf16.shape
    in_specs = [st.row_spec(d), _resident((1, d)), st.mod_spec(layer, 0), st.mod_spec(layer, 1),
                _resident((d, n))]
    args = [x, norm_g.reshape(1, d), mod, mod, w_bf16]
    if bias is not None:
        in_specs.append(_resident((1, n)))
        args.append(bias.reshape(1, n))
    width = n // n_out
    outs = pl.pallas_call(
        functools.partial(_mod_matmul_kernel, n_out, bias is not None),
        grid=(st.tiles,),
        in_specs=in_specs,
        out_specs=[st.row_spec(width)] * n_out,
        out_shape=[jax.ShapeDtypeStruct((st.rows, width), out_dtype)] * n_out,
        compiler_params=_params("parallel"),
        name="mod_matmul",
    )(*args)
    return outs


def _filter_time_kernel(feats_ref, w1_ref, b1_ref, w2_ref, b2_ref, fr_ref, w3_ref, dl_ref,
                        hf_ref, hb_ref):
    feats = feats_ref[...]
    fr = fr_ref[...]
    h = jnp.sin(fr[0:1] * (_dot_f32(feats, w1_ref[...]) + b1_ref[...]))
    h = jnp.sin(fr[1:2] * (_dot_f32(h, w2_ref[...]) + b2_ref[...]))
    h = _dot_f32(h, w3_ref[...])
    t = feats[:, 0:1]
    decay = jnp.exp(-t * jnp.abs(dl_ref[...]))
    half = HY_ORDER * D_MODEL
    decay2 = jnp.concatenate([decay] * HY_ORDER, axis=1)
    hf_ref[...] = h[:, :half] * decay2
    hb_ref[...] = jnp.where(t == 0.0, 0.0, h[:, half:] * decay2)


def _filter_time(seq, feats_pad, w1_pad, b1, w2, b2, freq, w3, deltas):
    tl = min(seq, 256)
    half = HY_ORDER * D_MODEL
    out = jax.ShapeDtypeStruct((seq, half), F32)
    return pl.pallas_call(
        _filter_time_kernel,
        grid=(seq // tl,),
        in_specs=[
            pl.BlockSpec((tl, V7X_LANES), lambda i: (i, 0)),
            _resident(w1_pad.shape), _resident((1, HY_FW)), _resident((HY_FW, HY_FW)),
            _resident((1, HY_FW)), _resident((2, HY_FW)), _resident(w3.shape),
            _resident((1, D_MODEL)),
        ],
        out_specs=[pl.BlockSpec((tl, half), lambda i: (i, 0))] * 2,
        out_shape=[out, out],
        compiler_params=_params("parallel"),
        name="hyena_filter_time",
    )(feats_pad, w1_pad, b1.reshape(1, HY_FW), w2, b2.reshape(1, HY_FW), freq, w3, deltas)


def _filter_spectrum_kernel(blk, nb, hf_ref, hb_ref, c_ref, s_ref, kr_ref, ki_ref, kn_ref):
    row = lax.broadcasted_iota(jnp.int32, (blk, 1), 0)
    sg = (1 - 2 * (row & 1)).astype(F32)
    wgt = jnp.where(row == 0, 1.0, 2.0) * (1.0 / (2 * blk))
    fwd, bwd = [], []
    for j in range(nb):
        rows = slice(j * blk, (j + 1) * blk)
        for ref, out in ((hf_ref, fwd), (hb_ref, bwd)):
            x = ref[rows, :]
            xb = x.astype(BF16)
            out.append((_dot(c_ref[...], xb), _dot(s_ref[...], xb),
                        jnp.sum(x * sg, axis=0, keepdims=True), x[0:1, :],
                        xb[0:1, :].astype(F32)))
    kn_ref[...] = jnp.zeros_like(kn_ref)
    for d in range(-(nb - 1), nb):
        if d == 0:
            (fc, fs, fn, _, _), (bc, bs, bn, _, _) = fwd[0], bwd[0]
            kr, ki, kn = fc + bc, bs - fs, fn + bn
        else:
            parts, im_sign = (fwd, -1.0) if d > 0 else (bwd, 1.0)
            c1, s1, n1, _, _ = parts[abs(d)]
            c0, s0, n0, x0, x0_seen = parts[abs(d) - 1]
            kr, ki, kn = c1 + sg * (c0 - x0_seen), im_sign * (s1 + sg * s0), n1 + n0 - x0
        slot = d + nb - 1
        kr_ref[slot * blk:(slot + 1) * blk, :] = (wgt * kr).astype(kr_ref.dtype)
        ki_ref[slot * blk:(slot + 1) * blk, :] = (wgt * ki).astype(ki_ref.dtype)
        kn_ref[slot:slot + 1, :] = kn * (1.0 / (2 * blk))


def _filter_spectrum(seq, blk, hf, hb, cmat, smat):
    nb = seq // blk
    half = hf.shape[1]
    tn = 256
    col = pl.BlockSpec((seq, tn), lambda j: (0, j))
    spec_rows = (2 * nb - 1) * blk
    out_col = pl.BlockSpec((spec_rows, tn), lambda j: (0, j))
    return pl.pallas_call(
        functools.partial(_filter_spectrum_kernel, blk, nb),
        grid=(half // tn,),
        in_specs=[col, col, _resident((blk, blk)), _resident((blk, blk))],
        out_specs=[out_col, out_col, pl.BlockSpec((NYQUIST_ROWS, tn), lambda j: (0, j))],
        out_shape=[jax.ShapeDtypeStruct((spec_rows, half), BF16)] * 2
        + [jax.ShapeDtypeStruct((NYQUIST_ROWS, half), F32)],
        compiler_params=_params("parallel"),
        name="hyena_filter_spectrum",
    )(hf, hb, cmat, smat)


def _hyena_core_kernel(seq, blk, *refs):
    projections, shared, per_seq = refs[:3], refs[3:19], refs[19:]
    for s in range(projections[0].shape[0]):
        _hyena_core_one_sequence(seq, blk, *[r.at[s] for r in projections], *shared,
                                 *[r.at[s] for r in per_seq])


def _hyena_core_one_sequence(seq, blk, pv_ref, p1_ref, p2_ref, cwv_ref, cw1_ref, cw2_ref, cbv_ref,
                             cb1_ref, cb2_ref, kr0_ref, ki0_ref, kn0_ref, kr1_ref, ki1_ref,
                             kn1_ref, bias0_ref, bias1_ref, c_ref, s_ref, z_ref,
                             u_ref, ub_ref, gate_ref, a_ref, b_ref):
    row = lax.broadcasted_iota(jnp.int32, (blk, 1), 0)
    sign = (1 - 2 * (row & 1)).astype(F32)
    nb = seq // blk
    blocks = [slice(j * blk, (j + 1) * blk) for j in range(nb)]

    def short_conv(dst_ref, x_ref, w_ref, b_ref):
        x = x_ref[...].astype(F32)
        w = w_ref[...]
        time = lax.broadcasted_iota(jnp.int32, (seq, 1), 0)
        prev = jnp.where(time == 0, 0.0, pltpu.roll(x, 1, 0))
        nxt = jnp.where(time == seq - 1, 0.0, pltpu.roll(x, seq - 1, 0))
        dst_ref[...] = prev * w[0:1] + x * w[1:2] + nxt * w[2:3] + b_ref[...]

    def gated_long_conv(kr_ref, ki_ref, kn_ref, bias_ref, write):
        ub_ref[...] = u_ref[...].astype(BF16)
        nyq_in = []
        for rows in blocks:
            a_ref[rows, :] = _dot(c_ref[...], ub_ref[rows, :]).astype(BF16)
            b_ref[rows, :] = _dot(s_ref[...], ub_ref[rows, :]).astype(BF16)
            nyq_in.append(jnp.sum(u_ref[rows, :] * sign, axis=0, keepdims=True))
        for i, rows in enumerate(blocks):
            p = q = nyq = None
            for j, src in enumerate(blocks):
                slot = i - j + nb - 1
                kr = kr_ref[slot * blk:(slot + 1) * blk, :]
                ki = ki_ref[slot * blk:(slot + 1) * blk, :]
                a, b = a_ref[src, :], b_ref[src, :]
                pj, qj = a * kr + b * ki, b * kr - a * ki
                nj = nyq_in[j] * kn_ref[slot:slot + 1, :]
                p, q, nyq = (pj, qj, nj) if j == 0 else (p + pj, q + qj, nyq + nj)
            y = _dot(c_ref[...], p) + _dot(s_ref[...], q)
            y = y + sign * nyq + u_ref[rows, :] * bias_ref[...]
            write(rows, gate_ref[rows, :] * y)

    def to_u(rows, val):
        u_ref[rows, :] = val

    def to_z(rows, val):
        z_ref[rows, :] = val.astype(z_ref.dtype)

    short_conv(u_ref, pv_ref, cwv_ref, cbv_ref)
    short_conv(gate_ref, p1_ref, cw1_ref, cb1_ref)
    gated_long_conv(kr0_ref, ki0_ref, kn0_ref, bias0_ref, to_u)
    short_conv(gate_ref, p2_ref, cw2_ref, cb2_ref)
    gated_long_conv(kr1_ref, ki1_ref, kn1_ref, bias1_ref, to_z)


def _hyena_core(st, blk, proj, conv_w, conv_b, kr, ki, kn, bias, cmat, smat):
    seq, d = st.seq, D_MODEL
    tn = 256
    nj = d // tn
    spec_rows = kr.shape[0]
    proj3 = proj.reshape(st.batch, seq, 3 * d)
    group = max(1, min(HYENA_GROUP_STEPS // seq, st.batch))
    assert st.batch % group == 0

    def part(k):
        return pl.BlockSpec((group, seq, tn), lambda j, b: (b, 0, k * nj + j))

    def cols(rows, k, buffers=2):
        return pl.BlockSpec((rows, tn), lambda j, b: (0, k * nj + j),
                            pipeline_mode=pl.Buffered(buffers))

    in_specs = ([part(0), part(1), part(2)]
                + [cols(3, k) for k in range(3)] + [cols(1, k) for k in range(3)]
                + [cols(spec_rows, 0, 1), cols(spec_rows, 0, 1), cols(NYQUIST_ROWS, 0),
                   cols(spec_rows, 1, 1), cols(spec_rows, 1, 1), cols(NYQUIST_ROWS, 1)]
                + [cols(1, 0), cols(1, 0)]
                + [_resident((blk, blk)), _resident((blk, blk))])
    z = pl.pallas_call(
        functools.partial(_hyena_core_kernel, seq, blk),
        grid=(nj, st.batch // group),
        in_specs=in_specs,
        out_specs=pl.BlockSpec((group, seq, tn), lambda j, b: (b, 0, j)),
        out_shape=jax.ShapeDtypeStruct((st.batch, seq, d), BF16),
        scratch_shapes=[pltpu.VMEM((group, seq, tn), F32), pltpu.VMEM((group, seq, tn), BF16),
                        pltpu.VMEM((group, seq, tn), F32), pltpu.VMEM((group, seq, tn), BF16),
                        pltpu.VMEM((group, seq, tn), BF16)],
        compiler_params=_params("parallel", "parallel"),
        name="hyena_core",
    )(proj3, proj3, proj3, conv_w, conv_w, conv_w, conv_b, conv_b, conv_b,
      kr, ki, kn, kr, ki, kn, bias[0:1], bias[1:2], cmat, smat)
    return z.reshape(st.rows, d)


def _route_kernel(first_tiles, *refs):
    n = len(first_tiles)
    z_refs, x_refs, shared = refs[:n], refs[n:2 * n], refs[2 * n:2 * n + 8]
    h2_ref, x1_refs, route_refs = refs[2 * n + 8], refs[2 * n + 9:3 * n + 9], refs[3 * n + 9:]
    i = pl.program_id(0)
    for s in range(n):
        lo = first_tiles[s]
        hi = first_tiles[s + 1] if s + 1 < n else pl.num_programs(0)

        @pl.when((i >= lo) & (i < hi))
        def _(s=s):
            _route_tile(z_refs[s], x_refs[s], *shared, x1_refs[s], h2_ref, route_refs[s])


def _route_tile(z_ref, x_ref, w_ref, g1_ref, ng_ref, sh_ref, sc_ref, rw_hi_ref, rw_lo_ref, rb_ref,
                x1_ref, h2_ref, route_ref):
    x1 = x_ref[...] + g1_ref[...] * _dot(z_ref[...], w_ref[...])
    x1_ref[...] = x1
    h2 = _modulate(x1, ng_ref[...], sh_ref[...], sc_ref[...])
    _store_token_tiles(h2_ref, h2)
    h_hi = h2.astype(BF16)
    h_lo = (h2 - h_hi.astype(F32)).astype(BF16)
    logits = (_dot(h_hi, rw_hi_ref[...])
              + (_dot(h_lo, rw_hi_ref[...]) + _dot(h_hi, rw_lo_ref[...])) + rb_ref[...])
    lane = lax.broadcasted_iota(jnp.int32, logits.shape, 1)
    neg = -jnp.inf
    logits = jnp.where(lane < N_EXPERTS, logits, neg)
    m1 = jnp.max(logits, axis=-1, keepdims=True)
    i1 = jnp.min(jnp.where(logits == m1, lane, V7X_LANES), axis=-1, keepdims=True)
    rest = jnp.where(lane == i1, neg, logits)
    m2 = jnp.max(rest, axis=-1, keepdims=True)
    i2 = jnp.min(jnp.where(rest == m2, lane, V7X_LANES), axis=-1, keepdims=True)
    e2 = jnp.exp(m2 - m1)
    den = 1.0 + e2
    route = jnp.where(lane == 0, i1.astype(F32), jnp.where(lane == 1, i2.astype(F32), 0.0))
    route_ref[...] = route + jnp.where(lane == 2, 1.0 / den, 0.0) + jnp.where(lane == 3, e2 / den, 0.0)


def _store_token_tiles(ref, val):
    rows = val.shape[0]
    for c in range(LANE_CHUNKS):
        ref[pl.ds(c, rows, stride=LANE_CHUNKS), :] = val[:, c * V7X_LANES:(c + 1) * V7X_LANES]


def _out_proj_route(streams, w_bf16, mod, layer, norm_g, rw_pad, rb_pad):
    d = D_MODEL
    streams = [(st.retiled(WIDE_OUT_TILE), z, x) for st, z, x in streams]
    tm = streams[0][0].tm
    assert all(st.tm == tm and st.row0 % tm == 0 for st, _, _ in streams)
    first = [st.row0 // tm for st, _, _ in streams]
    tiles = sum(st.tiles for st, _, _ in streams)
    rw_hi = rw_pad.astype(BF16)
    rw_lo = (rw_pad - rw_hi.astype(F32)).astype(BF16)

    def local(s, i):
        return jnp.clip(i - first[s], 0, streams[s][0].tiles - 1)

    def cond_row(i):
        row = streams[0][0].cond_row(local(0, i))
        for s in range(1, len(streams)):
            row = jnp.where(i >= first[s], streams[s][0].cond_row(local(s, i)), row)
        return row

    def rows_of(s, width):
        return pl.BlockSpec((tm, width), lambda i: (local(s, i), 0))

    def mod_chunk(chunk):
        return pl.BlockSpec((None, None, 1, d), lambda i: (layer, cond_row(i), 0, chunk))

    n = len(streams)
    in_specs = ([rows_of(s, streams[s][1].shape[1]) for s in range(n)]
                + [rows_of(s, d) for s in range(n)]
                + [_resident(w_bf16.shape), mod_chunk(2), _resident((1, d)), mod_chunk(3),
                   mod_chunk(4), _resident(rw_pad.shape), _resident(rw_pad.shape),
                   _resident(rb_pad.shape)])
    out_specs = ([pl.BlockSpec((tm * LANE_CHUNKS, V7X_LANES), lambda i: (i, 0))]
                 + [rows_of(s, d) for s in range(n)] + [rows_of(s, V7X_LANES) for s in range(n)])
    out_shape = ([jax.ShapeDtypeStruct((tiles * tm * LANE_CHUNKS, V7X_LANES), F32)]
                 + [jax.ShapeDtypeStruct((st.rows, d), F32) for st, _, _ in streams]
                 + [jax.ShapeDtypeStruct((st.rows, V7X_LANES), F32) for st, _, _ in streams])
    outs = pl.pallas_call(
        functools.partial(_route_kernel, first),
        grid=(tiles,),
        in_specs=in_specs,
        out_specs=out_specs,
        out_shape=out_shape,
        compiler_params=_params("arbitrary"),
        name="out_proj_route",
    )(*[z for _, z, _ in streams], *[x for _, _, x in streams], w_bf16, mod,
      norm_g.reshape(1, d), mod, mod, rw_hi, rw_lo, rb_pad)
    return outs[0], outs[1:1 + n], outs[1 + n:]


def _swiglu_part(x_bf16, wg_ref, wu_ref, wd_ref):
    g = _dot(x_bf16, wg_ref[...].astype(BF16))
    u = _dot(x_bf16, wu_ref[...].astype(BF16))
    return _dot((g * jax.nn.sigmoid(g) * u).astype(BF16), wd_ref[...].astype(BF16))


def _out_proj_swiglu_kernel(z_ref, w_ref, b_ref, x_ref, g1_ref, ng_ref, sh_ref, sc_ref, wg_ref,
                            wu_ref, wd_ref, g2_ref, o_ref, x1_ref, h2_ref, acc_ref):
    j = pl.program_id(1)
    last_j = pl.num_programs(1) - 1

    @pl.when(j == 0)
    def _():
        x1 = x_ref[...] + g1_ref[...] * (_dot(z_ref[...], w_ref[...]) + b_ref[...])
        x1_ref[...] = x1
        h2 = _modulate(x1, ng_ref[...], sh_ref[...], sc_ref[...]).astype(BF16)
        h2_ref[...] = h2
        acc_ref[...] = _swiglu_part(h2, wg_ref, wu_ref, wd_ref)

    @pl.when((j > 0) & (j < last_j))
    def _():
        acc_ref[...] += _swiglu_part(h2_ref[...], wg_ref, wu_ref, wd_ref)

    @pl.when(j == last_j)
    def _():
        f = acc_ref[...] + _swiglu_part(h2_ref[...], wg_ref, wu_ref, wd_ref)
        o_ref[...] = x1_ref[...] + g2_ref[...] * f


def _out_proj_swiglu(st, z_bf16, w_bf16, bias, x, mod, layer, norm_g, w_gu, w_down):
    st = st.retiled(WIDE_OUT_TILE)
    d, two_f = w_gu.shape
    f = two_f // 2
    tf = DENSE_FF_CHUNK
    nf = f // tf
    rows = pl.BlockSpec((st.tm, d), lambda i, j: (i, 0))
    return pl.pallas_call(
        _out_proj_swiglu_kernel,
        grid=(st.tiles, nf),
        in_specs=[
            rows, _resident(w_bf16.shape), _resident((1, d)), rows, st.mod_spec(layer, 2),
            _resident((1, d)), st.mod_spec(layer, 3), st.mod_spec(layer, 4),
            pl.BlockSpec((d, tf), lambda i, j: (0, j)),
            pl.BlockSpec((d, tf), lambda i, j: (0, nf + j)),
            pl.BlockSpec((tf, d), lambda i, j: (j, 0)),
            st.mod_spec(layer, 5),
        ],
        out_specs=rows,
        out_shape=jax.ShapeDtypeStruct((st.rows, d), F32),
        scratch_shapes=[pltpu.VMEM((st.tm, d), F32), pltpu.VMEM((st.tm, d), BF16),
                        pltpu.VMEM((st.tm, d), F32)],
        compiler_params=_params("parallel", "arbitrary"),
        name="out_proj_swiglu",
    )(z_bf16, w_bf16, bias.reshape(1, d), x, mod, norm_g.reshape(1, d), mod, mod,
      w_gu, w_gu, w_down, mod)


def _moe_plan(route, tm):
    tokens = route.shape[0]
    max_tiles = (2 * tokens) // tm + N_EXPERTS + 1
    expert = route[:, :2].astype(jnp.int32).reshape(-1)
    experts = jnp.arange(N_EXPERTS, dtype=jnp.int32)
    onehot = (expert[:, None] == experts[None, :]).astype(jnp.int32)
    csum = jnp.cumsum(onehot, axis=0)
    rank = jnp.sum(csum * onehot, axis=1) - 1
    counts = csum[-1]
    tiles_per_expert = (counts + tm - 1) // tm
    tiles_end = jnp.cumsum(tiles_per_expert)
    first_tile = tiles_end - tiles_per_expert
    first_sorted = jnp.cumsum(counts) - counts
    pos = jnp.sum((first_tile * tm)[None, :] * onehot, axis=1) + rank
    token = jnp.arange(2 * tokens, dtype=jnp.int32) // 2
    _, order = lax.sort_key_val(pos, token)
    n_tiles = tiles_end[-1:]
    tile = jnp.minimum(jnp.arange(max_tiles, dtype=jnp.int32), n_tiles - 1)
    tile_expert = jnp.sum((tile[:, None] >= tiles_end[None, :]).astype(jnp.int32), axis=1)
    mine = (tile_expert[:, None] == experts[None, :]).astype(jnp.int32)
    done = (tile - jnp.sum(first_tile[None, :] * mine, axis=1)) * tm
    tile_base = jnp.sum(first_sorted[None, :] * mine, axis=1) + done
    tile_valid = jnp.clip(jnp.sum(counts[None, :] * mine, axis=1) - done, 1, tm)
    as_i32 = lambda a: a.astype(jnp.int32)
    return (as_i32(order), as_i32(pos), as_i32(tile_expert), as_i32(tile_base),
            as_i32(tile_valid), as_i32(n_tiles))


def _token_tile_copy(src_hbm, row, dst, slot_row, sem):
    return pltpu.make_async_copy(
        src_hbm.at[pl.ds(pl.multiple_of(row * LANE_CHUNKS, LANE_CHUNKS), LANE_CHUNKS)],
        dst.at[pl.ds(pl.multiple_of(slot_row * LANE_CHUNKS, LANE_CHUNKS), LANE_CHUNKS)],
        sem)


def _moe_ffn_kernel(tm, per_step, n_steps, order_ref, te_ref, base_ref, valid_ref, nt_ref, x_hbm,
                    wg_ref, wu_ref, wd_ref, y_ref, xbuf, xd_ref, acc_ref, sem):
    t, j = pl.program_id(0), pl.program_id(1)
    last_j = n_steps - 1
    n_tiles = nt_ref[0]
    slot = t % 2
    total = per_step * n_steps

    def copy_row(tile_base, tile_last, r, into):
        token = order_ref[tile_base + jnp.minimum(r, tile_last)]
        return _token_tile_copy(x_hbm, token, xbuf.at[into], r, sem.at[into])

    @pl.when((t == 0) & (j == 0))
    def _():
        base, last = base_ref[0], valid_ref[0] - 1

        def body(r, carry):
            copy_row(base, last, r, 0).start()
            return carry
        lax.fori_loop(0, total, body, 0)

    @pl.when((j == 0) & (t <= n_tiles))
    def _():
        pltpu.make_async_copy(x_hbm.at[pl.ds(0, total * LANE_CHUNKS)],
                              xbuf.at[slot, pl.ds(0, total * LANE_CHUNKS)], sem.at[slot]).wait()

    def start_next_tile_copies():
        nxt = jnp.minimum(t + 1, n_tiles - 1)
        base, last = base_ref[nxt], valid_ref[nxt] - 1
        for k in range(per_step):
            copy_row(base, last, j * per_step + k, 1 - slot).start()

    @pl.when((j == 0) & (t < n_tiles))
    def _():
        start_next_tile_copies()
        x = jnp.concatenate([xbuf[slot, pl.ds(c, tm, stride=LANE_CHUNKS), :].astype(BF16)
                             for c in range(LANE_CHUNKS)], axis=1)
        xd_ref[...] = x
        acc_ref[...] = _swiglu_part(x, wg_ref, wu_ref, wd_ref)

    @pl.when((j > 0) & (j < last_j) & (t < n_tiles))
    def _():
        start_next_tile_copies()
        acc_ref[...] += _swiglu_part(xd_ref[...], wg_ref, wu_ref, wd_ref)

    @pl.when((j == last_j) & (t < n_tiles))
    def _():
        start_next_tile_copies()
        _store_token_tiles(y_ref, acc_ref[...] + _swiglu_part(xd_ref[...], wg_ref, wu_ref, wd_ref))

    @pl.when((t >= n_tiles) & (j == last_j))
    def _():
        y_ref[...] = jnp.zeros_like(y_ref)


def _moe_ffn(pool, order, tile_expert, tile_base, tile_valid, n_tiles, w_gu, w_down, tm):
    n_e, d, two_f = w_gu.shape
    f = two_f // 2
    tf = MOE_FF_CHUNK
    nf = f // tf
    max_tiles = tile_expert.shape[0]
    rows = tm * LANE_CHUNKS
    per_step = -(-tm // nf)
    buf_rows = per_step * nf * LANE_CHUNKS
    grid_spec = pltpu.PrefetchScalarGridSpec(
        num_scalar_prefetch=5,
        grid=(max_tiles, nf),
        in_specs=[
            pl.BlockSpec(memory_space=pl.ANY),
            pl.BlockSpec((None, d, tf), lambda t, j, order, te, *_: (te[t], 0, j)),
            pl.BlockSpec((None, d, tf), lambda t, j, order, te, *_: (te[t], 0, nf + j)),
            pl.BlockSpec((None, tf, d), lambda t, j, order, te, *_: (te[t], j, 0)),
        ],
        out_specs=pl.BlockSpec((rows, V7X_LANES), lambda t, j, *_: (t, 0)),
        scratch_shapes=[pltpu.VMEM((2, buf_rows, V7X_LANES), F32), pltpu.VMEM((tm, d), BF16),
                        pltpu.VMEM((tm, d), F32), pltpu.SemaphoreType.DMA((2,))],
    )
    return pl.pallas_call(
        functools.partial(_moe_ffn_kernel, tm, per_step, nf),
        grid_spec=grid_spec,
        out_shape=jax.ShapeDtypeStruct((max_tiles * rows, V7X_LANES), F32),
        compiler_params=_params("arbitrary", "arbitrary"),
        name="moe_ffn",
    )(order, tile_expert, tile_base, tile_valid, n_tiles, pool, w_gu, w_gu, w_down)


def _moe_combine_kernel(tm, token0, pos_ref, y_hbm, route_ref, x_ref, g2_ref, o_ref, ybuf, sem):
    i = pl.program_id(0)
    slot = i % 2
    rows = tm * LANE_CHUNKS

    def start_gather(tile, into):
        def body(r, carry):
            a = 2 * (token0 + tile * tm + r)
            for k in range(2):
                _token_tile_copy(y_hbm, pos_ref[a + k], ybuf.at[into], k * tm + r,
                                 sem.at[into]).start(priority=k)
            return carry
        lax.fori_loop(0, tm, body, 0, unroll=4)

    @pl.when(i == 0)
    def _():
        start_gather(0, 0)

    pltpu.make_async_copy(y_hbm.at[pl.ds(0, 2 * rows)], ybuf.at[slot], sem.at[slot]).wait()

    @pl.when(i + 1 < pl.num_programs(0))
    def _():
        start_gather(i + 1, 1 - slot)

    route = route_ref[...]
    lane = lax.broadcasted_iota(jnp.int32, route.shape, 1)
    gate0 = jnp.sum(jnp.where(lane == 2, route, 0.0), axis=-1, keepdims=True)
    gate1 = jnp.sum(jnp.where(lane == 3, route, 0.0), axis=-1, keepdims=True)
    for c in range(LANE_CHUNKS):
        cols = slice(c * V7X_LANES, (c + 1) * V7X_LANES)
        y0 = ybuf[slot, pl.ds(c, tm, stride=LANE_CHUNKS), :]
        y1 = ybuf[slot, pl.ds(rows + c, tm, stride=LANE_CHUNKS), :]
        o_ref[:, cols] = x_ref[:, cols] + g2_ref[:, cols] * (gate0 * y0 + gate1 * y1)


def _moe_combine(st, y_slots, pos, route, x1, mod, layer):
    d = D_MODEL
    tm = st.tm
    rows = tm * LANE_CHUNKS
    grid_spec = pltpu.PrefetchScalarGridSpec(
        num_scalar_prefetch=1,
        grid=(st.tiles,),
        in_specs=[
            pl.BlockSpec(memory_space=pl.ANY),
            st.row_spec(V7X_LANES), st.row_spec(d), st.mod_spec(layer, 5),
        ],
        out_specs=st.row_spec(d),
        scratch_shapes=[pltpu.VMEM((2, 2 * rows, V7X_LANES), F32), pltpu.SemaphoreType.DMA((2,))],
    )
    return pl.pallas_call(
        functools.partial(_moe_combine_kernel, tm, st.row0),
        grid_spec=grid_spec,
        out_shape=jax.ShapeDtypeStruct((st.rows, d), F32),
        compiler_params=_params("arbitrary"),
        name="moe_combine",
    )(pos, y_slots, route, x1, mod)


def _head_rmsnorm(x, g2):
    lane = lax.broadcasted_iota(jnp.int32, x.shape, 1)
    lo = lane < HEAD_DIM
    sq = x * x
    s_lo = jnp.sum(jnp.where(lo, sq, 0.0), axis=-1, keepdims=True)
    s_hi = jnp.sum(jnp.where(lo, 0.0, sq), axis=-1, keepdims=True)
    ms = jnp.where(lo, s_lo, s_hi) * (1.0 / HEAD_DIM)
    return x * lax.rsqrt(ms + EPS) * g2


def _rope(x, cos, sin_signed):
    q4 = HEAD_DIM // 4
    lane = lax.broadcasted_iota(jnp.int32, x.shape, 1)
    first = (lane & q4) == 0
    width = x.shape[1]
    partner = jnp.where(first, pltpu.roll(x, width - q4, 1), pltpu.roll(x, q4, 1))
    return x * cos + partner * sin_signed


def _attention_kernel(past, use_rope, lam_init, *refs):
    refs = list(refs)
    n_seq = refs[0].shape[0]
    n_in = 3 + (2 if past else 0)
    seq_refs, refs = refs[:n_in], refs[n_in:]
    if use_rope:
        rope_refs, refs = refs[:4], refs[4:]
    else:
        rope_refs = []
    param_refs, refs = refs[:4], refs[4:]
    n_out = 1 if past else 2
    out_refs, (kall_ref, vall_ref) = refs[:n_out], refs[n_out:]
    for s in range(n_seq):
        _attention_one_sequence(past, use_rope, lam_init, *[r.at[s] for r in seq_refs],
                                *rope_refs, *param_refs, *[r.at[s] for r in out_refs],
                                kall_ref.at[s], vall_ref.at[s])


def _attention_one_sequence(past, use_rope, lam_init, *refs):
    refs = list(refs)
    q_ref, k_ref, v_ref = refs.pop(0), refs.pop(0), refs.pop(0)
    if past:
        ck_ref, cv_ref = refs.pop(0), refs.pop(0)
    if use_rope:
        cosq_ref, sinq_ref, cosk_ref, sink_ref = (refs.pop(0) for _ in range(4))
    qg_ref, kg_ref, lam_ref, sg_ref = (refs.pop(0) for _ in range(4))
    o_ref = refs.pop(0)
    nk_ref = None if past else refs.pop(0)
    kall_ref, vall_ref = refs

    def prepare_keys():
        k = _head_rmsnorm(k_ref[...], kg_ref[...])
        if nk_ref is not None:
            nk_ref[...] = k
        if use_rope:
            k = _rope(k, cosk_ref[...], sink_ref[...])
        if past:
            kall_ref[0:past, :] = ck_ref[...].astype(BF16)
            vall_ref[0:past, :] = cv_ref[...].astype(BF16)
        kall_ref[past:, :] = k.astype(BF16)
        vall_ref[past:, :] = v_ref[...].astype(BF16)

    if q_ref.shape[0] == k_ref.shape[0]:
        prepare_keys()
    else:
        pl.when(pl.program_id(2) == 0)(prepare_keys)

    q = _head_rmsnorm(q_ref[...], qg_ref[...])
    if use_rope:
        q = _rope(q, cosq_ref[...], sinq_ref[...])
    q = q * (HEAD_DIM ** -0.5 * math.log2(math.e))
    tq = q.shape[0]
    nt = (((1,), (1,)), ((), ()))
    lv = lam_ref[...]
    lam = (jnp.exp(jnp.sum(lv[0:1] * lv[1:2], axis=-1, keepdims=True))
           - jnp.exp(jnp.sum(lv[2:3] * lv[3:4], axis=-1, keepdims=True)) + lam_init)

    def attend(qm):
        s = lax.dot_general(qm.astype(BF16), kall_ref[...], nt, preferred_element_type=F32)
        p = jnp.exp2(s - jnp.max(s, axis=-1, keepdims=True))
        norm = 1.0 / jnp.sum(p, axis=-1, keepdims=True)
        return _dot(p.astype(BF16), vall_ref[...]) * norm

    row_chunks = [slice(r, r + ATTENTION_ROWS) for r in range(0, tq, ATTENTION_ROWS)]
    lo = lax.broadcasted_iota(jnp.int32, q.shape, 1) < HEAD_DIM
    first = jnp.where(lo, q, 0.0)
    second = jnp.where(lo, 0.0, q)
    a0 = jnp.concatenate([attend(first[rows]) for rows in row_chunks], axis=0)
    a1 = jnp.concatenate([attend(second[rows]) for rows in row_chunks], axis=0)
    o = a0 - lam * a1
    ms = jnp.mean(o * o, axis=-1, keepdims=True)
    o = o * lax.rsqrt(ms + EPS) * sg_ref[...] * (1.0 - lam_init)
    o_ref[...] = o.astype(o_ref.dtype)


def _attention(st, q, k, v, q_g, k_g, lam_vecs, subln_g, lam_init, cache_k=None, cache_v=None,
               rope=None, cache_layer=0):
    seq, hd = st.seq, N_HEADS * V_DIM
    tq = min(seq, ATTENTION_QUERIES)
    nq = seq // tq
    group = max(1, min(ATTENTION_GROUP_QUERIES // seq, st.batch))
    assert st.batch % group == 0
    past = 0 if cache_k is None else cache_k.shape[2]
    lk = past + seq
    q3, k3, v3 = (a.reshape(st.batch, seq, hd) for a in (q, k, v))
    qblk = pl.BlockSpec((group, tq, V_DIM), lambda b, h, i: (b, i, h))
    kblk = pl.BlockSpec((group, seq, V_DIM), lambda b, h, i: (b, 0, h))
    in_specs = [qblk, kblk, kblk]
    args = [q3, k3, v3]
    if past:
        layers = cache_k.shape[1]
        cblk = pl.BlockSpec((group, None, past, V_DIM), lambda b, h, i: (b, cache_layer, 0, h))
        in_specs += [cblk, cblk]
        args += [cache_k.reshape(st.batch, layers, past, hd),
                 cache_v.reshape(st.batch, layers, past, hd)]
    if rope is not None:
        cos2, sin2 = rope
        tq_tab = pl.BlockSpec((tq, V_DIM), lambda b, h, i: (i, 0))
        k_tab = pl.BlockSpec((seq, V_DIM), lambda b, h, i: (0, 0))
        in_specs += [tq_tab, tq_tab, k_tab, k_tab]
        args += [cos2, sin2, cos2, sin2]
    small = lambda shape: pl.BlockSpec(shape, lambda b, h, i: (0,) * len(shape))
    in_specs += [small((1, V_DIM)), small((1, V_DIM)), small((4, HEAD_DIM)), small((1, V_DIM))]
    args += [jnp.tile(q_g, 2).reshape(1, V_DIM), jnp.tile(k_g, 2).reshape(1, V_DIM), lam_vecs,
             subln_g.reshape(1, V_DIM)]
    out_specs = [qblk]
    out_shape = [jax.ShapeDtypeStruct((st.batch, seq, hd), BF16)]
    if not past:
        out_specs.append(kblk)
        out_shape.append(jax.ShapeDtypeStruct((st.batch, seq, hd), F32))
    outs = pl.pallas_call(
        functools.partial(_attention_kernel, past, rope is not None, lam_init),
        grid=(st.batch // group, N_HEADS, nq),
        in_specs=in_specs,
        out_specs=out_specs,
        out_shape=out_shape,
        scratch_shapes=[pltpu.VMEM((group, lk, V_DIM), BF16),
                        pltpu.VMEM((group, lk, V_DIM), BF16)],
        compiler_params=_params("parallel", "parallel", "arbitrary"),
        name="diff_attention",
    )(*args)
    return [o.reshape(st.rows, hd) for o in outs]


def _dft_matrices(seq):
    idx = np.arange(seq, dtype=np.int64)
    ang = (np.outer(idx, idx) % (2 * seq)).astype(np.float64) * (math.pi / seq)
    return jnp.asarray(np.cos(ang), dtype=BF16), jnp.asarray(np.sin(ang), dtype=BF16)


def _filter_features(seq):
    t = jnp.linspace(0.0, 1.0, seq, dtype=F32)[:, None]
    bands = (HY_EMB - 1) // 2
    w_ang = 2.0 * math.pi * jnp.arange(seq, dtype=F32)[:, None] / seq
    f = jnp.linspace(1e-4, bands - 1, bands, dtype=F32)[None, :]
    ang = f * w_ang
    feats = jnp.concatenate([t, jnp.cos(ang), -jnp.sin(ang)], axis=-1)
    return jnp.pad(feats, ((0, 0), (0, V7X_LANES - HY_EMB)))


def _decay_rates():
    min_decay = math.log(HY_DECAY_TARGET) / HY_SLOW_PCT
    max_decay = math.log(HY_DECAY_TARGET) / HY_FAST_PCT
    return jnp.linspace(min_decay, max_decay, D_MODEL, dtype=F32)[None, :]


def _rope_tables(seq):
    rows = seq // GRID_W
    row = jnp.repeat(jnp.arange(rows, dtype=F32), GRID_W)
    col = jnp.tile(jnp.arange(GRID_W, dtype=F32), rows)
    quarter = HEAD_DIM // 4
    inv = ROPE_BASE ** (-jnp.arange(quarter, dtype=F32) / quarter)

    def axis_angles(pos):
        a = pos[:, None] * inv[None, :]
        return jnp.concatenate([a, a], axis=-1)

    ang = jnp.concatenate([axis_angles(row), axis_angles(col)], axis=-1)
    sign = jnp.where((jnp.arange(HEAD_DIM) & quarter) == 0, -1.0, 1.0).astype(F32)
    return jnp.tile(jnp.cos(ang), (1, 2)), jnp.tile(jnp.sin(ang) * sign[None, :], (1, 2))


def kernel(x_prompt, x_sample, cache_k, cache_v, c, c_ctx, ada_w, ada_b, norm_g, hy_in_w, hy_in_b, hy_conv_w, hy_conv_b, hy_f_w1, hy_f_b1, hy_f_w2, hy_f_b2, hy_f_freq, hy_f_w3, hy_bias, hy_out_w, hy_out_b, at_qkv_w, at_q_g, at_k_g, at_lam, at_subln_g, at_out_w, dn_w_gu, dn_w_down, mo_router_w, mo_router_b, mo_w_gu, mo_w_down):
    d = D_MODEL
    batch, seq = x_prompt.shape[:2]
    dec_batch, dec_seq = x_sample.shape[:2]
    streams = [
        (_Stream(batch, seq, 0, False), x_prompt.reshape(batch * seq, d), None),
        (_Stream(dec_batch, dec_seq, 1, True, row0=batch * seq),
         x_sample.reshape(dec_batch * dec_seq, d), (cache_k, cache_v)),
    ]

    cond = jnp.concatenate(
        [c_ctx[None, :], c, jnp.zeros((COND_ROWS - 1 - dec_batch, d), F32)], axis=0)
    mod = _adaln(cond, ada_w, ada_b)
    mod = mod.reshape(mod.shape[0], COND_ROWS, 1, 6 * d)

    in_w, out_w = hy_in_w[0].astype(BF16), hy_out_w[0].astype(BF16)
    qkv_w, at_out = at_qkv_w[0].astype(BF16), at_out_w[0].astype(BF16)
    dn_gu, dn_down = dn_w_gu[0].astype(BF16), dn_w_down[0].astype(BF16)
    mo_gu, mo_down = mo_w_gu[0], mo_w_down[0]
    w1_pad = jnp.pad(hy_f_w1[0], ((0, V7X_LANES - HY_EMB), (0, 0)))
    rw_pad = jnp.pad(mo_router_w[0], ((0, 0), (0, V7X_LANES - N_EXPERTS)))
    rb_pad = jnp.pad(mo_router_b[0], (0, V7X_LANES - N_EXPERTS)).reshape(1, V7X_LANES)
    deltas = _decay_rates()
    lam_init = 0.8 - 0.6 * math.exp(-0.3 * 1)

    attended = []
    for st, x, cache in streams:
        blk = min(st.seq, HYENA_BLOCK)
        cmat, smat = _dft_matrices(blk)
        hf, hb = _filter_time(st.seq, _filter_features(st.seq), w1_pad, hy_f_b1[0], hy_f_w2[0],
                              hy_f_b2[0], hy_f_freq[0], hy_f_w3[0], deltas)
        kr, ki, kn = _filter_spectrum(st.seq, blk, hf, hb, cmat, smat)
        (proj,) = _mod_matmul(st, x, norm_g[0, 0], mod, 0, in_w, hy_in_b[0], 1, BF16)
        z = _hyena_core(st, blk, proj, hy_conv_w[0], hy_conv_b[0].reshape(1, 3 * d), kr, ki, kn,
                        hy_bias[0], cmat, smat)
        x = _out_proj_swiglu(st, z, out_w, hy_out_b[0], x, mod, 0, norm_g[0, 1], dn_gu, dn_down)

        q, k, v = _mod_matmul(st, x, norm_g[1, 0], mod, 1, qkv_w, None, 3, F32)
        if cache is None:
            o, new_k = _attention(st, q, k, v, at_q_g[0], at_k_g[0], at_lam[0], at_subln_g[0],
                                  lam_init)
            new_kv = (new_k, v)
        else:
            (o,) = _attention(st, q, k, v, at_q_g[0], at_k_g[0], at_lam[0], at_subln_g[0],
                              lam_init, cache[0], cache[1], _rope_tables(st.seq))
        attended.append((st, o, x))

    pool, x1s, routes = _out_proj_route(attended, at_out, mod, 1, norm_g[1, 1], rw_pad, rb_pad)
    order, pos, *tile_table = _moe_plan(
        jnp.concatenate([route[:, :4] for route in routes], axis=0), MOE_TILE)
    y_slots = _moe_ffn(pool, order, *tile_table, mo_gu, mo_down, MOE_TILE)
    results = [_moe_combine(st, y_slots, pos, route, x1, mod, 1).reshape(st.batch, st.seq, d)
               for (st, _, _), x1, route in zip(attended, x1s, routes)]

    new_k, new_v = new_kv
    return (results[0], results[1],
            new_k.reshape(batch, 1, seq, N_HEADS, 2, HEAD_DIM),
            new_v.reshape(batch, 1, seq, N_HEADS, V_DIM))
```

```python
import functools
import math

import numpy as np
import jax
import jax.numpy as jnp
from jax import lax
from jax.experimental import pallas as pl
from jax.experimental.pallas import tpu as pltpu

F32 = jnp.float32
BF16 = jnp.bfloat16

D_MODEL = 1024
GRID_W = 64
HY_ORDER = 2
HY_EMB = 33
HY_FW = 64
HY_DECAY_TARGET = 1e-2
HY_FAST_PCT = 0.3
HY_SLOW_PCT = 1.5
N_HEADS = 8
HEAD_DIM = 64
V_DIM = 2 * HEAD_DIM
ROPE_BASE = 10000.0
D_FF = 2816
N_EXPERTS = 8
D_FF_EXPERT = 3584
EPS = 1e-6

V7X_LANES = 128
V7X_VMEM_LIMIT_BYTES = 56 * 1024 * 1024
LANE_CHUNKS = D_MODEL // V7X_LANES
COND_ROWS = 16
TOKEN_TILE = 1024
WIDE_OUT_TILE = 512
MOE_TILE = 1024
MOE_FF_CHUNK = 512
DENSE_FF_CHUNK = D_FF // 2
HYENA_BLOCK = 512
HYENA_GROUP_STEPS = 1024
NYQUIST_ROWS = 8
ATTENTION_QUERIES = 1024
ATTENTION_GROUP_QUERIES = 1024
ATTENTION_ROWS = 256


def _params(*semantics):
    return pltpu.CompilerParams(dimension_semantics=semantics,
                                vmem_limit_bytes=V7X_VMEM_LIMIT_BYTES)


def _resident(shape):
    zeros = (0,) * len(shape)
    return pl.BlockSpec(shape, lambda *_: zeros, pipeline_mode=pl.Buffered(1))


def _dot(a, b):
    return jnp.dot(a, b, preferred_element_type=F32)


def _dot_f32(a, b):
    a_hi, b_hi = a.astype(BF16), b.astype(BF16)
    a_lo = (a - a_hi.astype(F32)).astype(BF16)
    b_lo = (b - b_hi.astype(F32)).astype(BF16)
    return _dot(a_hi, b_hi) + (_dot(a_lo, b_hi) + _dot(a_hi, b_lo))


def _modulate(x, g, shift, scale):
    ms = jnp.mean(x * x, axis=-1, keepdims=True)
    return (x * lax.rsqrt(ms + EPS) * g) * (1.0 + scale) + shift


def _adaln_kernel(cond_ref, w_ref, b_ref, o_ref):
    c = cond_ref[...]
    o_ref[...] = _dot_f32(c * jax.nn.sigmoid(c), w_ref[...]) + b_ref[...]


def _adaln(cond, ada_w, ada_b):
    depth, d, n = ada_w.shape
    tn = 1536
    return pl.pallas_call(
        _adaln_kernel,
        grid=(depth, n // tn),
        in_specs=[
            pl.BlockSpec((COND_ROWS, d), lambda i, j: (0, 0)),
            pl.BlockSpec((None, d, tn), lambda i, j: (i, 0, j)),
            pl.BlockSpec((None, 1, tn), lambda i, j: (i, 0, j)),
        ],
        out_specs=pl.BlockSpec((None, COND_ROWS, tn), lambda i, j: (i, 0, j)),
        out_shape=jax.ShapeDtypeStruct((depth, COND_ROWS, n), F32),
        compiler_params=_params("parallel", "parallel"),
        name="adaln",
    )(cond, ada_w, ada_b.reshape(depth, 1, n))


class _Stream:
    def __init__(self, batch, seq, cond_row0, per_seq_cond, row0=0, tile=TOKEN_TILE):
        self.batch, self.seq = batch, seq
        self.rows = batch * seq
        self.row0 = row0
        self._cond = (cond_row0, per_seq_cond)
        if per_seq_cond:
            self.tm = min(tile, seq)
            tiles_per_seq = seq // self.tm
            self.cond_row = lambda i: cond_row0 + i // tiles_per_seq
        else:
            self.tm = min(tile, self.rows)
            self.cond_row = lambda i: cond_row0
        self.tiles = self.rows // self.tm

    def retiled(self, tile):
        return _Stream(self.batch, self.seq, *self._cond, row0=self.row0, tile=tile)

    def mod_spec(self, layer, chunk):
        return pl.BlockSpec((None, None, 1, D_MODEL),
                            lambda i, *_: (layer, self.cond_row(i), 0, chunk))

    def row_spec(self, width):
        return pl.BlockSpec((self.tm, width), lambda i, *_: (i, 0))


def _mod_matmul_kernel(n_out, has_bias, x_ref, g_ref, sh_ref, sc_ref, w_ref, *rest):
    if has_bias:
        b_ref, out_refs = rest[0], rest[1:]
    else:
        b_ref, out_refs = None, rest
    h = _modulate(x_ref[...], g_ref[...], sh_ref[...], sc_ref[...])
    y = _dot(h.astype(BF16), w_ref[...])
    if has_bias:
        y = y + b_ref[...]
    width = y.shape[1] // n_out
    for k, o_ref in enumerate(out_refs):
        o_ref[...] = y[:, k * width:(k + 1) * width].astype(o_ref.dtype)


def _mod_matmul(st, x, norm_g, mod, layer, w_bf16, bias, n_out, out_dtype):
    st = st.retiled(WIDE_OUT_TILE)
    d, n = w_bf16.shape
    in_specs = [st.row_spec(d), _resident((1, d)), st.mod_spec(layer, 0), st.mod_spec(layer, 1),
                _resident((d, n))]
    args = [x, norm_g.reshape(1, d), mod, mod, w_bf16]
    if bias is not None:
        in_specs.append(_resident((1, n)))
        args.append(bias.reshape(1, n))
    width = n // n_out
    outs = pl.pallas_call(
        functools.partial(_mod_matmul_kernel, n_out, bias is not None),
        grid=(st.tiles,),
        in_specs=in_specs,
        out_specs=[st.row_spec(width)] * n_out,
        out_shape=[jax.ShapeDtypeStruct((st.rows, width), out_dtype)] * n_out,
        compiler_params=_params("parallel"),
        name="mod_matmul",
    )(*args)
    return outs


def _filter_time_kernel(feats_ref, w1_ref, b1_ref, w2_ref, b2_ref, fr_ref, w3_ref, dl_ref,
                        hf_ref, hb_ref):
    feats = feats_ref[...]
    fr = fr_ref[...]
    h = jnp.sin(fr[0:1] * (_dot_f32(feats, w1_ref[...]) + b1_ref[...]))
    h = jnp.sin(fr[1:2] * (_dot_f32(h, w2_ref[...]) + b2_ref[...]))
    h = _dot_f32(h, w3_ref[...])
    t = feats[:, 0:1]
    decay = jnp.exp(-t * jnp.abs(dl_ref[...]))
    half = HY_ORDER * D_MODEL
    decay2 = jnp.concatenate([decay] * HY_ORDER, axis=1)
    hf_ref[...] = h[:, :half] * decay2
    hb_ref[...] = jnp.where(t == 0.0, 0.0, h[:, half:] * decay2)


def _filter_time(seq, feats_pad, w1_pad, b1, w2, b2, freq, w3, deltas):
    tl = min(seq, 256)
    half = HY_ORDER * D_MODEL
    out = jax.ShapeDtypeStruct((seq, half), F32)
    return pl.pallas_call(
        _filter_time_kernel,
        grid=(seq // tl,),
        in_specs=[
            pl.BlockSpec((tl, V7X_LANES), lambda i: (i, 0)),
            _resident(w1_pad.shape), _resident((1, HY_FW)), _resident((HY_FW, HY_FW)),
            _resident((1, HY_FW)), _resident((2, HY_FW)), _resident(w3.shape),
            _resident((1, D_MODEL)),
        ],
        out_specs=[pl.BlockSpec((tl, half), lambda i: (i, 0))] * 2,
        out_shape=[out, out],
        compiler_params=_params("parallel"),
        name="hyena_filter_time",
    )(feats_pad, w1_pad, b1.reshape(1, HY_FW), w2, b2.reshape(1, HY_FW), freq, w3, deltas)


def _filter_spectrum_kernel(blk, nb, hf_ref, hb_ref, c_ref, s_ref, kr_ref, ki_ref, kn_ref):
    row = lax.broadcasted_iota(jnp.int32, (blk, 1), 0)
    sg = (1 - 2 * (row & 1)).astype(F32)
    wgt = jnp.where(row == 0, 1.0, 2.0) * (1.0 / (2 * blk))
    fwd, bwd = [], []
    for j in range(nb):
        rows = slice(j * blk, (j + 1) * blk)
        for ref, out in ((hf_ref, fwd), (hb_ref, bwd)):
            x = ref[rows, :]
            xb = x.astype(BF16)
            out.append((_dot(c_ref[...], xb), _dot(s_ref[...], xb),
                        jnp.sum(x * sg, axis=0, keepdims=True), x[0:1, :],
                        xb[0:1, :].astype(F32)))
    kn_ref[...] = jnp.zeros_like(kn_ref)
    for d in range(-(nb - 1), nb):
        if d == 0:
            (fc, fs, fn, _, _), (bc, bs, bn, _, _) = fwd[0], bwd[0]
            kr, ki, kn = fc + bc, bs - fs, fn + bn
        else:
            parts, im_sign = (fwd, -1.0) if d > 0 else (bwd, 1.0)
            c1, s1, n1, _, _ = parts[abs(d)]
            c0, s0, n0, x0, x0_seen = parts[abs(d) - 1]
            kr, ki, kn = c1 + sg * (c0 - x0_seen), im_sign * (s1 + sg * s0), n1 + n0 - x0
        slot = d + nb - 1
        kr_ref[slot * blk:(slot + 1) * blk, :] = (wgt * kr).astype(kr_ref.dtype)
        ki_ref[slot * blk:(slot + 1) * blk, :] = (wgt * ki).astype(ki_ref.dtype)
        kn_ref[slot:slot + 1, :] = kn * (1.0 / (2 * blk))


def _filter_spectrum(seq, blk, hf, hb, cmat, smat):
    nb = seq // blk
    half = hf.shape[1]
    tn = 256
    col = pl.BlockSpec((seq, tn), lambda j: (0, j))
    spec_rows = (2 * nb - 1) * blk
    out_col = pl.BlockSpec((spec_rows, tn), lambda j: (0, j))
    return pl.pallas_call(
        functools.partial(_filter_spectrum_kernel, blk, nb),
        grid=(half // tn,),
        in_specs=[col, col, _resident((blk, blk)), _resident((blk, blk))],
        out_specs=[out_col, out_col, pl.BlockSpec((NYQUIST_ROWS, tn), lambda j: (0, j))],
        out_shape=[jax.ShapeDtypeStruct((spec_rows, half), BF16)] * 2
        + [jax.ShapeDtypeStruct((NYQUIST_ROWS, half), F32)],
        compiler_params=_params("parallel"),
        name="hyena_filter_spectrum",
    )(hf, hb, cmat, smat)


def _hyena_core_kernel(seq, blk, *refs):
    projections, shared, per_seq = refs[:3], refs[3:19], refs[19:]
    for s in range(projections[0].shape[0]):
        _hyena_core_one_sequence(seq, blk, *[r.at[s] for r in projections], *shared,
                                 *[r.at[s] for r in per_seq])


def _hyena_core_one_sequence(seq, blk, pv_ref, p1_ref, p2_ref, cwv_ref, cw1_ref, cw2_ref, cbv_ref,
                             cb1_ref, cb2_ref, kr0_ref, ki0_ref, kn0_ref, kr1_ref, ki1_ref,
                             kn1_ref, bias0_ref, bias1_ref, c_ref, s_ref, z_ref,
                             u_ref, ub_ref, gate_ref, a_ref, b_ref):
    row = lax.broadcasted_iota(jnp.int32, (blk, 1), 0)
    sign = (1 - 2 * (row & 1)).astype(F32)
    nb = seq // blk
    blocks = [slice(j * blk, (j + 1) * blk) for j in range(nb)]

    def short_conv(dst_ref, x_ref, w_ref, b_ref):
        x = x_ref[...].astype(F32)
        w = w_ref[...]
        time = lax.broadcasted_iota(jnp.int32, (seq, 1), 0)
        prev = jnp.where(time == 0, 0.0, pltpu.roll(x, 1, 0))
        nxt = jnp.where(time == seq - 1, 0.0, pltpu.roll(x, seq - 1, 0))
        dst_ref[...] = prev * w[0:1] + x * w[1:2] + nxt * w[2:3] + b_ref[...]

    def gated_long_conv(kr_ref, ki_ref, kn_ref, bias_ref, write):
        ub_ref[...] = u_ref[...].astype(BF16)
        nyq_in = []
        for rows in blocks:
            a_ref[rows, :] = _dot(c_ref[...], ub_ref[rows, :]).astype(BF16)
            b_ref[rows, :] = _dot(s_ref[...], ub_ref[rows, :]).astype(BF16)
            nyq_in.append(jnp.sum(u_ref[rows, :] * sign, axis=0, keepdims=True))
        for i, rows in enumerate(blocks):
            p = q = nyq = None
            for j, src in enumerate(blocks):
                slot = i - j + nb - 1
                kr = kr_ref[slot * blk:(slot + 1) * blk, :]
                ki = ki_ref[slot * blk:(slot + 1) * blk, :]
                a, b = a_ref[src, :], b_ref[src, :]
                pj, qj = a * kr + b * ki, b * kr - a * ki
                nj = nyq_in[j] * kn_ref[slot:slot + 1, :]
                p, q, nyq = (pj, qj, nj) if j == 0 else (p + pj, q + qj, nyq + nj)
            y = _dot(c_ref[...], p) + _dot(s_ref[...], q)
            y = y + sign * nyq + u_ref[rows, :] * bias_ref[...]
            write(rows, gate_ref[rows, :] * y)

    def to_u(rows, val):
        u_ref[rows, :] = val

    def to_z(rows, val):
        z_ref[rows, :] = val.astype(z_ref.dtype)

    short_conv(u_ref, pv_ref, cwv_ref, cbv_ref)
    short_conv(gate_ref, p1_ref, cw1_ref, cb1_ref)
    gated_long_conv(kr0_ref, ki0_ref, kn0_ref, bias0_ref, to_u)
    short_conv(gate_ref, p2_ref, cw2_ref, cb2_ref)
    gated_long_conv(kr1_ref, ki1_ref, kn1_ref, bias1_ref, to_z)


def _hyena_core(st, blk, proj, conv_w, conv_b, kr, ki, kn, bias, cmat, smat):
    seq, d = st.seq, D_MODEL
    tn = 256
    nj = d // tn
    spec_rows = kr.shape[0]
    proj3 = proj.reshape(st.batch, seq, 3 * d)
    group = max(1, min(HYENA_GROUP_STEPS // seq, st.batch))
    assert st.batch % group == 0

    def part(k):
        return pl.BlockSpec((group, seq, tn), lambda j, b: (b, 0, k * nj + j))

    def cols(rows, k, buffers=2):
        return pl.BlockSpec((rows, tn), lambda j, b: (0, k * nj + j),
                            pipeline_mode=pl.Buffered(buffers))

    in_specs = ([part(0), part(1), part(2)]
                + [cols(3, k) for k in range(3)] + [cols(1, k) for k in range(3)]
                + [cols(spec_rows, 0, 1), cols(spec_rows, 0, 1), cols(NYQUIST_ROWS, 0),
                   cols(spec_rows, 1, 1), cols(spec_rows, 1, 1), cols(NYQUIST_ROWS, 1)]
                + [cols(1, 0), cols(1, 0)]
                + [_resident((blk, blk)), _resident((blk, blk))])
    z = pl.pallas_call(
        functools.partial(_hyena_core_kernel, seq, blk),
        grid=(nj, st.batch // group),
        in_specs=in_specs,
        out_specs=pl.BlockSpec((group, seq, tn), lambda j, b: (b, 0, j)),
        out_shape=jax.ShapeDtypeStruct((st.batch, seq, d), BF16),
        scratch_shapes=[pltpu.VMEM((group, seq, tn), F32), pltpu.VMEM((group, seq, tn), BF16),
                        pltpu.VMEM((group, seq, tn), F32), pltpu.VMEM((group, seq, tn), BF16),
                        pltpu.VMEM((group, seq, tn), BF16)],
        compiler_params=_params("parallel", "parallel"),
        name="hyena_core",
    )(proj3, proj3, proj3, conv_w, conv_w, conv_w, conv_b, conv_b, conv_b,
      kr, ki, kn, kr, ki, kn, bias[0:1], bias[1:2], cmat, smat)
    return z.reshape(st.rows, d)


def _route_kernel(first_tiles, *refs):
    n = len(first_tiles)
    z_refs, x_refs, shared = refs[:n], refs[n:2 * n], refs[2 * n:2 * n + 8]
    h2_ref, x1_refs, route_refs = refs[2 * n + 8], refs[2 * n + 9:3 * n + 9], refs[3 * n + 9:]
    i = pl.program_id(0)
    for s in range(n):
        lo = first_tiles[s]
        hi = first_tiles[s + 1] if s + 1 < n else pl.num_programs(0)

        @pl.when((i >= lo) & (i < hi))
        def _(s=s):
            _route_tile(z_refs[s], x_refs[s], *shared, x1_refs[s], h2_ref, route_refs[s])


def _route_tile(z_ref, x_ref, w_ref, g1_ref, ng_ref, sh_ref, sc_ref, rw_hi_ref, rw_lo_ref, rb_ref,
                x1_ref, h2_ref, route_ref):
    x1 = x_ref[...] + g1_ref[...] * _dot(z_ref[...], w_ref[...])
    x1_ref[...] = x1
    h2 = _modulate(x1, ng_ref[...], sh_ref[...], sc_ref[...])
    _store_token_tiles(h2_ref, h2)
    h_hi = h2.astype(BF16)
    h_lo = (h2 - h_hi.astype(F32)).astype(BF16)
    logits = (_dot(h_hi, rw_hi_ref[...])
              + (_dot(h_lo, rw_hi_ref[...]) + _dot(h_hi, rw_lo_ref[...])) + rb_ref[...])
    lane = lax.broadcasted_iota(jnp.int32, logits.shape, 1)
    neg = -jnp.inf
    logits = jnp.where(lane < N_EXPERTS, logits, neg)
    m1 = jnp.max(logits, axis=-1, keepdims=True)
    i1 = jnp.min(jnp.where(logits == m1, lane, V7X_LANES), axis=-1, keepdims=True)
    rest = jnp.where(lane == i1, neg, logits)
    m2 = jnp.max(rest, axis=-1, keepdims=True)
    i2 = jnp.min(jnp.where(rest == m2, lane, V7X_LANES), axis=-1, keepdims=True)
    e2 = jnp.exp(m2 - m1)
    den = 1.0 + e2
    route = jnp.where(lane == 0, i1.astype(F32), jnp.where(lane == 1, i2.astype(F32), 0.0))
    route_ref[...] = route + jnp.where(lane == 2, 1.0 / den, 0.0) + jnp.where(lane == 3, e2 / den, 0.0)


def _store_token_tiles(ref, val):
    rows = val.shape[0]
    for c in range(LANE_CHUNKS):
        ref[pl.ds(c, rows, stride=LANE_CHUNKS), :] = val[:, c * V7X_LANES:(c + 1) * V7X_LANES]


def _out_proj_route(streams, w_bf16, mod, layer, norm_g, rw_pad, rb_pad):
    d = D_MODEL
    streams = [(st.retiled(WIDE_OUT_TILE), z, x) for st, z, x in streams]
    tm = streams[0][0].tm
    assert all(st.tm == tm and st.row0 % tm == 0 for st, _, _ in streams)
    first = [st.row0 // tm for st, _, _ in streams]
    tiles = sum(st.tiles for st, _, _ in streams)
    rw_hi = rw_pad.astype(BF16)
    rw_lo = (rw_pad - rw_hi.astype(F32)).astype(BF16)

    def local(s, i):
        return jnp.clip(i - first[s], 0, streams[s][0].tiles - 1)

    def cond_row(i):
        row = streams[0][0].cond_row(local(0, i))
        for s in range(1, len(streams)):
            row = jnp.where(i >= first[s], streams[s][0].cond_row(local(s, i)), row)
        return row

    def rows_of(s, width):
        return pl.BlockSpec((tm, width), lambda i: (local(s, i), 0))

    def mod_chunk(chunk):
        return pl.BlockSpec((None, None, 1, d), lambda i: (layer, cond_row(i), 0, chunk))

    n = len(streams)
    in_specs = ([rows_of(s, streams[s][1].shape[1]) for s in range(n)]
                + [rows_of(s, d) for s in range(n)]
                + [_resident(w_bf16.shape), mod_chunk(2), _resident((1, d)), mod_chunk(3),
                   mod_chunk(4), _resident(rw_pad.shape), _resident(rw_pad.shape),
                   _resident(rb_pad.shape)])
    out_specs = ([pl.BlockSpec((tm * LANE_CHUNKS, V7X_LANES), lambda i: (i, 0))]
                 + [rows_of(s, d) for s in range(n)] + [rows_of(s, V7X_LANES) for s in range(n)])
    out_shape = ([jax.ShapeDtypeStruct((tiles * tm * LANE_CHUNKS, V7X_LANES), F32)]
                 + [jax.ShapeDtypeStruct((st.rows, d), F32) for st, _, _ in streams]
                 + [jax.ShapeDtypeStruct((st.rows, V7X_LANES), F32) for st, _, _ in streams])
    outs = pl.pallas_call(
        functools.partial(_route_kernel, first),
        grid=(tiles,),
        in_specs=in_specs,
        out_specs=out_specs,
        out_shape=out_shape,
        compiler_params=_params("arbitrary"),
        name="out_proj_route",
    )(*[z for _, z, _ in streams], *[x for _, _, x in streams], w_bf16, mod,
      norm_g.reshape(1, d), mod, mod, rw_hi, rw_lo, rb_pad)
    return outs[0], outs[1:1 + n], outs[1 + n:]


def _swiglu_part(x_bf16, wg_ref, wu_ref, wd_ref):
    g = _dot(x_bf16, wg_ref[...].astype(BF16))
    u = _dot(x_bf16, wu_ref[...].astype(BF16))
    return _dot((g * jax.nn.sigmoid(g) * u).astype(BF16), wd_ref[...].astype(BF16))


def _out_proj_swiglu_kernel(z_ref, w_ref, b_ref, x_ref, g1_ref, ng_ref, sh_ref, sc_ref, wg_ref,
                            wu_ref, wd_ref, g2_ref, o_ref, x1_ref, h2_ref, acc_ref):
    j = pl.program_id(1)
    last_j = pl.num_programs(1) - 1

    @pl.when(j == 0)
    def _():
        x1 = x_ref[...] + g1_ref[...] * (_dot(z_ref[...], w_ref[...]) + b_ref[...])
        x1_ref[...] = x1
        h2 = _modulate(x1, ng_ref[...], sh_ref[...], sc_ref[...]).astype(BF16)
        h2_ref[...] = h2
        acc_ref[...] = _swiglu_part(h2, wg_ref, wu_ref, wd_ref)

    @pl.when((j > 0) & (j < last_j))
    def _():
        acc_ref[...] += _swiglu_part(h2_ref[...], wg_ref, wu_ref, wd_ref)

    @pl.when(j == last_j)
    def _():
        f = acc_ref[...] + _swiglu_part(h2_ref[...], wg_ref, wu_ref, wd_ref)
        o_ref[...] = x1_ref[...] + g2_ref[...] * f


def _out_proj_swiglu(st, z_bf16, w_bf16, bias, x, mod, layer, norm_g, w_gu, w_down):
    st = st.retiled(WIDE_OUT_TILE)
    d, two_f = w_gu.shape
    f = two_f // 2
    tf = DENSE_FF_CHUNK
    nf = f // tf
    rows = pl.BlockSpec((st.tm, d), lambda i, j: (i, 0))
    return pl.pallas_call(
        _out_proj_swiglu_kernel,
        grid=(st.tiles, nf),
        in_specs=[
            rows, _resident(w_bf16.shape), _resident((1, d)), rows, st.mod_spec(layer, 2),
            _resident((1, d)), st.mod_spec(layer, 3), st.mod_spec(layer, 4),
            pl.BlockSpec((d, tf), lambda i, j: (0, j)),
            pl.BlockSpec((d, tf), lambda i, j: (0, nf + j)),
            pl.BlockSpec((tf, d), lambda i, j: (j, 0)),
            st.mod_spec(layer, 5),
        ],
        out_specs=rows,
        out_shape=jax.ShapeDtypeStruct((st.rows, d), F32),
        scratch_shapes=[pltpu.VMEM((st.tm, d), F32), pltpu.VMEM((st.tm, d), BF16),
                        pltpu.VMEM((st.tm, d), F32)],
        compiler_params=_params("parallel", "arbitrary"),
        name="out_proj_swiglu",
    )(z_bf16, w_bf16, bias.reshape(1, d), x, mod, norm_g.reshape(1, d), mod, mod,
      w_gu, w_gu, w_down, mod)


def _moe_plan(route, tm):
    tokens = route.shape[0]
    max_tiles = (2 * tokens) // tm + N_EXPERTS + 1
    expert = route[:, :2].astype(jnp.int32).reshape(-1)
    experts = jnp.arange(N_EXPERTS, dtype=jnp.int32)
    onehot = (expert[:, None] == experts[None, :]).astype(jnp.int32)
    csum = jnp.cumsum(onehot, axis=0)
    rank = jnp.sum(csum * onehot, axis=1) - 1
    counts = csum[-1]
    tiles_per_expert = (counts + tm - 1) // tm
    tiles_end = jnp.cumsum(tiles_per_expert)
    first_tile = tiles_end - tiles_per_expert
    first_sorted = jnp.cumsum(counts) - counts
    pos = jnp.sum((first_tile * tm)[None, :] * onehot, axis=1) + rank
    token = jnp.arange(2 * tokens, dtype=jnp.int32) // 2
    _, order = lax.sort_key_val(pos, token)
    n_tiles = tiles_end[-1:]
    tile = jnp.minimum(jnp.arange(max_tiles, dtype=jnp.int32), n_tiles - 1)
    tile_expert = jnp.sum((tile[:, None] >= tiles_end[None, :]).astype(jnp.int32), axis=1)
    mine = (tile_expert[:, None] == experts[None, :]).astype(jnp.int32)
    done = (tile - jnp.sum(first_tile[None, :] * mine, axis=1)) * tm
    tile_base = jnp.sum(first_sorted[None, :] * mine, axis=1) + done
    tile_valid = jnp.clip(jnp.sum(counts[None, :] * mine, axis=1) - done, 1, tm)
    as_i32 = lambda a: a.astype(jnp.int32)
    return (as_i32(order), as_i32(pos), as_i32(tile_expert), as_i32(tile_base),
            as_i32(tile_valid), as_i32(n_tiles))


def _token_tile_copy(src_hbm, row, dst, slot_row, sem):
    return pltpu.make_async_copy(
        src_hbm.at[pl.ds(pl.multiple_of(row * LANE_CHUNKS, LANE_CHUNKS), LANE_CHUNKS)],
        dst.at[pl.ds(pl.multiple_of(slot_row * LANE_CHUNKS, LANE_CHUNKS), LANE_CHUNKS)],
        sem)


def _moe_ffn_kernel(tm, per_step, n_steps, order_ref, te_ref, base_ref, valid_ref, nt_ref, x_hbm,
                    wg_ref, wu_ref, wd_ref, y_ref, xbuf, xd_ref, acc_ref, sem):
    t, j = pl.program_id(0), pl.program_id(1)
    last_j = n_steps - 1
    n_tiles = nt_ref[0]
    slot = t % 2
    total = per_step * n_steps

    def copy_row(tile_base, tile_last, r, into):
        token = order_ref[tile_base + jnp.minimum(r, tile_last)]
        return _token_tile_copy(x_hbm, token, xbuf.at[into], r, sem.at[into])

    @pl.when((t == 0) & (j == 0))
    def _():
        base, last = base_ref[0], valid_ref[0] - 1

        def body(r, carry):
            copy_row(base, last, r, 0).start()
            return carry
        lax.fori_loop(0, total, body, 0)

    @pl.when((j == 0) & (t <= n_tiles))
    def _():
        pltpu.make_async_copy(x_hbm.at[pl.ds(0, total * LANE_CHUNKS)],
                              xbuf.at[slot, pl.ds(0, total * LANE_CHUNKS)], sem.at[slot]).wait()

    def start_next_tile_copies():
        nxt = jnp.minimum(t + 1, n_tiles - 1)
        base, last = base_ref[nxt], valid_ref[nxt] - 1
        for k in range(per_step):
            copy_row(base, last, j * per_step + k, 1 - slot).start()

    @pl.when((j == 0) & (t < n_tiles))
    def _():
        start_next_tile_copies()
        x = jnp.concatenate([xbuf[slot, pl.ds(c, tm, stride=LANE_CHUNKS), :].astype(BF16)
                             for c in range(LANE_CHUNKS)], axis=1)
        xd_ref[...] = x
        acc_ref[...] = _swiglu_part(x, wg_ref, wu_ref, wd_ref)

    @pl.when((j > 0) & (t < n_tiles))
    def _():
        start_next_tile_copies()
        acc_ref[...] += _swiglu_part(xd_ref[...], wg_ref, wu_ref, wd_ref)

        @pl.when(j == last_j)
        def _():
            _store_token_tiles(y_ref, acc_ref[...])

    @pl.when((t >= n_tiles) & (j == last_j))
    def _():
        y_ref[...] = jnp.zeros_like(y_ref)


def _moe_ffn(pool, order, tile_expert, tile_base, tile_valid, n_tiles, w_gu, w_down, tm):
    n_e, d, two_f = w_gu.shape
    f = two_f // 2
    tf = MOE_FF_CHUNK
    nf = f // tf
    max_tiles = tile_expert.shape[0]
    rows = tm * LANE_CHUNKS
    per_step = -(-tm // nf)
    buf_rows = per_step * nf * LANE_CHUNKS
    grid_spec = pltpu.PrefetchScalarGridSpec(
        num_scalar_prefetch=5,
        grid=(max_tiles, nf),
        in_specs=[
            pl.BlockSpec(memory_space=pl.ANY),
            pl.BlockSpec((None, d, tf), lambda t, j, order, te, *_: (te[t], 0, j)),
            pl.BlockSpec((None, d, tf), lambda t, j, order, te, *_: (te[t], 0, nf + j)),
            pl.BlockSpec((None, tf, d), lambda t, j, order, te, *_: (te[t], j, 0)),
        ],
        out_specs=pl.BlockSpec((rows, V7X_LANES), lambda t, j, *_: (t, 0)),
        scratch_shapes=[pltpu.VMEM((2, buf_rows, V7X_LANES), F32), pltpu.VMEM((tm, d), BF16),
                        pltpu.VMEM((tm, d), F32), pltpu.SemaphoreType.DMA((2,))],
    )
    return pl.pallas_call(
        functools.partial(_moe_ffn_kernel, tm, per_step, nf),
        grid_spec=grid_spec,
        out_shape=jax.ShapeDtypeStruct((max_tiles * rows, V7X_LANES), F32),
        compiler_params=_params("arbitrary", "arbitrary"),
        name="moe_ffn",
    )(order, tile_expert, tile_base, tile_valid, n_tiles, pool, w_gu, w_gu, w_down)


def _moe_combine_kernel(tm, token0, pos_ref, y_hbm, route_ref, x_ref, g2_ref, o_ref, ybuf, sem):
    i = pl.program_id(0)
    slot = i % 2
    rows = tm * LANE_CHUNKS

    def start_gather(tile, into):
        def body(r, carry):
            a = 2 * (token0 + tile * tm + r)
            for k in range(2):
                _token_tile_copy(y_hbm, pos_ref[a + k], ybuf.at[into], k * tm + r,
                                 sem.at[into]).start(priority=k)
            return carry
        lax.fori_loop(0, tm, body, 0, unroll=4)

    @pl.when(i == 0)
    def _():
        start_gather(0, 0)

    pltpu.make_async_copy(y_hbm.at[pl.ds(0, 2 * rows)], ybuf.at[slot], sem.at[slot]).wait()

    @pl.when(i + 1 < pl.num_programs(0))
    def _():
        start_gather(i + 1, 1 - slot)

    route = route_ref[...]
    lane = lax.broadcasted_iota(jnp.int32, route.shape, 1)
    gate0 = jnp.sum(jnp.where(lane == 2, route, 0.0), axis=-1, keepdims=True)
    gate1 = jnp.sum(jnp.where(lane == 3, route, 0.0), axis=-1, keepdims=True)
    for c in range(LANE_CHUNKS):
        cols = slice(c * V7X_LANES, (c + 1) * V7X_LANES)
        y0 = ybuf[slot, pl.ds(c, tm, stride=LANE_CHUNKS), :]
        y1 = ybuf[slot, pl.ds(rows + c, tm, stride=LANE_CHUNKS), :]
        o_ref[:, cols] = x_ref[:, cols] + g2_ref[:, cols] * (gate0 * y0 + gate1 * y1)


def _moe_combine(st, y_slots, pos, route, x1, mod, layer):
    d = D_MODEL
    tm = st.tm
    rows = tm * LANE_CHUNKS
    grid_spec = pltpu.PrefetchScalarGridSpec(
        num_scalar_prefetch=1,
        grid=(st.tiles,),
        in_specs=[
            pl.BlockSpec(memory_space=pl.ANY),
            st.row_spec(V7X_LANES), st.row_spec(d), st.mod_spec(layer, 5),
        ],
        out_specs=st.row_spec(d),
        scratch_shapes=[pltpu.VMEM((2, 2 * rows, V7X_LANES), F32), pltpu.SemaphoreType.DMA((2,))],
    )
    return pl.pallas_call(
        functools.partial(_moe_combine_kernel, tm, st.row0),
        grid_spec=grid_spec,
        out_shape=jax.ShapeDtypeStruct((st.rows, d), F32),
        compiler_params=_params("arbitrary"),
        name="moe_combine",
    )(pos, y_slots, route, x1, mod)


def _head_rmsnorm(x, g2):
    lane = lax.broadcasted_iota(jnp.int32, x.shape, 1)
    lo = lane < HEAD_DIM
    sq = x * x
    s_lo = jnp.sum(jnp.where(lo, sq, 0.0), axis=-1, keepdims=True)
    s_hi = jnp.sum(jnp.where(lo, 0.0, sq), axis=-1, keepdims=True)
    ms = jnp.where(lo, s_lo, s_hi) * (1.0 / HEAD_DIM)
    return x * lax.rsqrt(ms + EPS) * g2


def _rope(x, cos, sin_signed):
    q4 = HEAD_DIM // 4
    lane = lax.broadcasted_iota(jnp.int32, x.shape, 1)
    first = (lane & q4) == 0
    width = x.shape[1]
    partner = jnp.where(first, pltpu.roll(x, width - q4, 1), pltpu.roll(x, q4, 1))
    return x * cos + partner * sin_signed


def _attention_kernel(past, use_rope, lam_init, *refs):
    refs = list(refs)
    n_seq = refs[0].shape[0]
    n_in = 3 + (2 if past else 0)
    seq_refs, refs = refs[:n_in], refs[n_in:]
    if use_rope:
        rope_refs, refs = refs[:4], refs[4:]
    else:
        rope_refs = []
    param_refs, refs = refs[:4], refs[4:]
    n_out = 1 if past else 2
    out_refs, (kall_ref, vall_ref) = refs[:n_out], refs[n_out:]
    for s in range(n_seq):
        _attention_one_sequence(past, use_rope, lam_init, *[r.at[s] for r in seq_refs],
                                *rope_refs, *param_refs, *[r.at[s] for r in out_refs],
                                kall_ref.at[s], vall_ref.at[s])


def _attention_one_sequence(past, use_rope, lam_init, *refs):
    refs = list(refs)
    q_ref, k_ref, v_ref = refs.pop(0), refs.pop(0), refs.pop(0)
    if past:
        ck_ref, cv_ref = refs.pop(0), refs.pop(0)
    if use_rope:
        cosq_ref, sinq_ref, cosk_ref, sink_ref = (refs.pop(0) for _ in range(4))
    qg_ref, kg_ref, lam_ref, sg_ref = (refs.pop(0) for _ in range(4))
    o_ref = refs.pop(0)
    nk_ref = None if past else refs.pop(0)
    kall_ref, vall_ref = refs

    def prepare_keys():
        k = _head_rmsnorm(k_ref[...], kg_ref[...])
        if nk_ref is not None:
            nk_ref[...] = k
        if use_rope:
            k = _rope(k, cosk_ref[...], sink_ref[...])
        if past:
            kall_ref[0:past, :] = ck_ref[...].astype(BF16)
            vall_ref[0:past, :] = cv_ref[...].astype(BF16)
        kall_ref[past:, :] = k.astype(BF16)
        vall_ref[past:, :] = v_ref[...].astype(BF16)

    def attend(qm):
        nt = (((1,), (1,)), ((), ()))
        s = lax.dot_general(qm.astype(BF16), kall_ref[...], nt, preferred_element_type=F32)
        p = jnp.exp2(s - jnp.max(s, axis=-1, keepdims=True))
        norm = 1.0 / jnp.sum(p, axis=-1, keepdims=True)
        return _dot(p.astype(BF16), vall_ref[...]) * norm

    def attend_queries():
        q = _head_rmsnorm(q_ref[...], qg_ref[...])
        if use_rope:
            q = _rope(q, cosq_ref[...], sinq_ref[...])
        q = q * (HEAD_DIM ** -0.5 * math.log2(math.e))
        lv = lam_ref[...]
        lam = (jnp.exp(jnp.sum(lv[0:1] * lv[1:2], axis=-1, keepdims=True))
               - jnp.exp(jnp.sum(lv[2:3] * lv[3:4], axis=-1, keepdims=True)) + lam_init)
        row_chunks = [slice(r, r + ATTENTION_ROWS) for r in range(0, q.shape[0], ATTENTION_ROWS)]
        lo = lax.broadcasted_iota(jnp.int32, q.shape, 1) < HEAD_DIM
        first = jnp.where(lo, q, 0.0)
        second = jnp.where(lo, 0.0, q)
        a0 = jnp.concatenate([attend(first[rows]) for rows in row_chunks], axis=0)
        a1 = jnp.concatenate([attend(second[rows]) for rows in row_chunks], axis=0)
        o = a0 - lam * a1
        ms = jnp.mean(o * o, axis=-1, keepdims=True)
        o = o * lax.rsqrt(ms + EPS) * sg_ref[...] * (1.0 - lam_init)
        o_ref[...] = o.astype(o_ref.dtype)

    if q_ref.shape[0] == k_ref.shape[0]:
        prepare_keys()
        attend_queries()
    else:
        qi = pl.program_id(2)

        @pl.when(qi == 0)
        def _():
            prepare_keys()
            attend_queries()

        pl.when(qi > 0)(attend_queries)


def _attention(st, q, k, v, q_g, k_g, lam_vecs, subln_g, lam_init, cache_k=None, cache_v=None,
               rope=None, cache_layer=0):
    seq, hd = st.seq, N_HEADS * V_DIM
    tq = min(seq, ATTENTION_QUERIES)
    nq = seq // tq
    group = max(1, min(ATTENTION_GROUP_QUERIES // seq, st.batch))
    assert st.batch % group == 0
    past = 0 if cache_k is None else cache_k.shape[2]
    lk = past + seq
    q3, k3, v3 = (a.reshape(st.batch, seq, hd) for a in (q, k, v))
    qblk = pl.BlockSpec((group, tq, V_DIM), lambda b, h, i: (b, i, h))
    kblk = pl.BlockSpec((group, seq, V_DIM), lambda b, h, i: (b, 0, h))
    in_specs = [qblk, kblk, kblk]
    args = [q3, k3, v3]
    if past:
        layers = cache_k.shape[1]
        cblk = pl.BlockSpec((group, None, past, V_DIM), lambda b, h, i: (b, cache_layer, 0, h))
        in_specs += [cblk, cblk]
        args += [cache_k.reshape(st.batch, layers, past, hd),
                 cache_v.reshape(st.batch, layers, past, hd)]
    if rope is not None:
        cos2, sin2 = rope
        tq_tab = pl.BlockSpec((tq, V_DIM), lambda b, h, i: (i, 0))
        k_tab = pl.BlockSpec((seq, V_DIM), lambda b, h, i: (0, 0))
        in_specs += [tq_tab, tq_tab, k_tab, k_tab]
        args += [cos2, sin2, cos2, sin2]
    small = lambda shape: pl.BlockSpec(shape, lambda b, h, i: (0,) * len(shape))
    in_specs += [small((1, V_DIM)), small((1, V_DIM)), small((4, HEAD_DIM)), small((1, V_DIM))]
    args += [jnp.tile(q_g, 2).reshape(1, V_DIM), jnp.tile(k_g, 2).reshape(1, V_DIM), lam_vecs,
             subln_g.reshape(1, V_DIM)]
    out_specs = [qblk]
    out_shape = [jax.ShapeDtypeStruct((st.batch, seq, hd), BF16)]
    if not past:
        out_specs.append(kblk)
        out_shape.append(jax.ShapeDtypeStruct((st.batch, seq, hd), F32))
    outs = pl.pallas_call(
        functools.partial(_attention_kernel, past, rope is not None, lam_init),
        grid=(st.batch // group, N_HEADS, nq),
        in_specs=in_specs,
        out_specs=out_specs,
        out_shape=out_shape,
        scratch_shapes=[pltpu.VMEM((group, lk, V_DIM), BF16),
                        pltpu.VMEM((group, lk, V_DIM), BF16)],
        compiler_params=_params("parallel", "parallel", "arbitrary"),
        name="diff_attention",
    )(*args)
    return [o.reshape(st.rows, hd) for o in outs]


def _dft_matrices(seq):
    idx = np.arange(seq, dtype=np.int64)
    ang = (np.outer(idx, idx) % (2 * seq)).astype(np.float64) * (math.pi / seq)
    return jnp.asarray(np.cos(ang), dtype=BF16), jnp.asarray(np.sin(ang), dtype=BF16)


def _filter_features(seq):
    t = jnp.linspace(0.0, 1.0, seq, dtype=F32)[:, None]
    bands = (HY_EMB - 1) // 2
    w_ang = 2.0 * math.pi * jnp.arange(seq, dtype=F32)[:, None] / seq
    f = jnp.linspace(1e-4, bands - 1, bands, dtype=F32)[None, :]
    ang = f * w_ang
    feats = jnp.concatenate([t, jnp.cos(ang), -jnp.sin(ang)], axis=-1)
    return jnp.pad(feats, ((0, 0), (0, V7X_LANES - HY_EMB)))


def _decay_rates():
    min_decay = math.log(HY_DECAY_TARGET) / HY_SLOW_PCT
    max_decay = math.log(HY_DECAY_TARGET) / HY_FAST_PCT
    return jnp.linspace(min_decay, max_decay, D_MODEL, dtype=F32)[None, :]


def _rope_tables(seq):
    rows = seq // GRID_W
    row = jnp.repeat(jnp.arange(rows, dtype=F32), GRID_W)
    col = jnp.tile(jnp.arange(GRID_W, dtype=F32), rows)
    quarter = HEAD_DIM // 4
    inv = ROPE_BASE ** (-jnp.arange(quarter, dtype=F32) / quarter)

    def axis_angles(pos):
        a = pos[:, None] * inv[None, :]
        return jnp.concatenate([a, a], axis=-1)

    ang = jnp.concatenate([axis_angles(row), axis_angles(col)], axis=-1)
    sign = jnp.where((jnp.arange(HEAD_DIM) & quarter) == 0, -1.0, 1.0).astype(F32)
    return jnp.tile(jnp.cos(ang), (1, 2)), jnp.tile(jnp.sin(ang) * sign[None, :], (1, 2))


def kernel(x_prompt, x_sample, cache_k, cache_v, c, c_ctx, ada_w, ada_b, norm_g, hy_in_w, hy_in_b, hy_conv_w, hy_conv_b, hy_f_w1, hy_f_b1, hy_f_w2, hy_f_b2, hy_f_freq, hy_f_w3, hy_bias, hy_out_w, hy_out_b, at_qkv_w, at_q_g, at_k_g, at_lam, at_subln_g, at_out_w, dn_w_gu, dn_w_down, mo_router_w, mo_router_b, mo_w_gu, mo_w_down):
    d = D_MODEL
    batch, seq = x_prompt.shape[:2]
    dec_batch, dec_seq = x_sample.shape[:2]
    streams = [
        (_Stream(batch, seq, 0, False), x_prompt.reshape(batch * seq, d), None),
        (_Stream(dec_batch, dec_seq, 1, True, row0=batch * seq),
         x_sample.reshape(dec_batch * dec_seq, d), (cache_k, cache_v)),
    ]

    cond = jnp.concatenate(
        [c_ctx[None, :], c, jnp.zeros((COND_ROWS - 1 - dec_batch, d), F32)], axis=0)
    mod = _adaln(cond, ada_w, ada_b)
    mod = mod.reshape(mod.shape[0], COND_ROWS, 1, 6 * d)

    in_w, out_w = hy_in_w[0].astype(BF16), hy_out_w[0].astype(BF16)
    qkv_w, at_out = at_qkv_w[0].astype(BF16), at_out_w[0].astype(BF16)
    dn_gu, dn_down = dn_w_gu[0].astype(BF16), dn_w_down[0].astype(BF16)
    mo_gu, mo_down = mo_w_gu[0], mo_w_down[0]
    w1_pad = jnp.pad(hy_f_w1[0], ((0, V7X_LANES - HY_EMB), (0, 0)))
    rw_pad = jnp.pad(mo_router_w[0], ((0, 0), (0, V7X_LANES - N_EXPERTS)))
    rb_pad = jnp.pad(mo_router_b[0], (0, V7X_LANES - N_EXPERTS)).reshape(1, V7X_LANES)
    deltas = _decay_rates()
    lam_init = 0.8 - 0.6 * math.exp(-0.3 * 1)

    attended = []
    for st, x, cache in streams:
        blk = min(st.seq, HYENA_BLOCK)
        cmat, smat = _dft_matrices(blk)
        hf, hb = _filter_time(st.seq, _filter_features(st.seq), w1_pad, hy_f_b1[0], hy_f_w2[0],
                              hy_f_b2[0], hy_f_freq[0], hy_f_w3[0], deltas)
        kr, ki, kn = _filter_spectrum(st.seq, blk, hf, hb, cmat, smat)
        (proj,) = _mod_matmul(st, x, norm_g[0, 0], mod, 0, in_w, hy_in_b[0], 1, BF16)
        z = _hyena_core(st, blk, proj, hy_conv_w[0], hy_conv_b[0].reshape(1, 3 * d), kr, ki, kn,
                        hy_bias[0], cmat, smat)
        x = _out_proj_swiglu(st, z, out_w, hy_out_b[0], x, mod, 0, norm_g[0, 1], dn_gu, dn_down)

        q, k, v = _mod_matmul(st, x, norm_g[1, 0], mod, 1, qkv_w, None, 3, F32)
        if cache is None:
            o, new_k = _attention(st, q, k, v, at_q_g[0], at_k_g[0], at_lam[0], at_subln_g[0],
                                  lam_init)
            new_kv = (new_k, v)
        else:
            (o,) = _attention(st, q, k, v, at_q_g[0], at_k_g[0], at_lam[0], at_subln_g[0],
                              lam_init, cache[0], cache[1], _rope_tables(st.seq))
        attended.append((st, o, x))

    pool, x1s, routes = _out_proj_route(attended, at_out, mod, 1, norm_g[1, 1], rw_pad, rb_pad)
    order, pos, *tile_table = _moe_plan(
        jnp.concatenate([route[:, :4] for route in routes], axis=0), MOE_TILE)
    y_slots = _moe_ffn(pool, order, *tile_table, mo_gu, mo_down, MOE_TILE)
    results = [_moe_combine(st, y_slots, pos, route, x1, mod, 1).reshape(st.batch, st.seq, d)
               for (st, _, _), x1, route in zip(attended, x1s, routes)]

    new_k, new_v = new_kv
    return (results[0], results[1],
            new_k.reshape(batch, 1, seq, N_HEADS, 2, HEAD_DIM),
            new_v.reshape(batch, 1, seq, N_HEADS, V_DIM))
```

```python
import functools
import math

import numpy as np
import jax
import jax.numpy as jnp
from jax import lax
from jax.experimental import pallas as pl
from jax.experimental.pallas import tpu as pltpu

F32 = jnp.float32
BF16 = jnp.bfloat16

D_MODEL = 1024
GRID_W = 64
HY_ORDER = 2
HY_EMB = 33
HY_FW = 64
HY_DECAY_TARGET = 1e-2
HY_FAST_PCT = 0.3
HY_SLOW_PCT = 1.5
N_HEADS = 8
HEAD_DIM = 64
V_DIM = 2 * HEAD_DIM
ROPE_BASE = 10000.0
D_FF = 2816
N_EXPERTS = 8
D_FF_EXPERT = 3584
EPS = 1e-6

V7X_LANES = 128
V7X_VMEM_LIMIT_BYTES = 56 * 1024 * 1024
LANE_CHUNKS = D_MODEL // V7X_LANES
COND_ROWS = 16
TOKEN_TILE = 1024
WIDE_OUT_TILE = 512
MOE_TILE = 1024
MOE_FF_CHUNK = 512
DENSE_FF_CHUNK = D_FF // 2
HYENA_BLOCK = 512
HYENA_GROUP_STEPS = 1024
NYQUIST_ROWS = 8
ATTENTION_QUERIES = 1024
ATTENTION_GROUP_QUERIES = 1024
ATTENTION_ROWS = 256


def _params(*semantics):
    return pltpu.CompilerParams(dimension_semantics=semantics,
                                vmem_limit_bytes=V7X_VMEM_LIMIT_BYTES)


def _resident(shape):
    zeros = (0,) * len(shape)
    return pl.BlockSpec(shape, lambda *_: zeros, pipeline_mode=pl.Buffered(1))


def _dot(a, b):
    return jnp.dot(a, b, preferred_element_type=F32)


def _dot_f32(a, b):
    a_hi, b_hi = a.astype(BF16), b.astype(BF16)
    a_lo = (a - a_hi.astype(F32)).astype(BF16)
    b_lo = (b - b_hi.astype(F32)).astype(BF16)
    return _dot(a_hi, b_hi) + (_dot(a_lo, b_hi) + _dot(a_hi, b_lo))


def _modulate(x, g, shift, scale):
    ms = jnp.mean(x * x, axis=-1, keepdims=True)
    return (x * lax.rsqrt(ms + EPS) * g) * (1.0 + scale) + shift


def _adaln_kernel(cond_ref, w_ref, b_ref, o_ref):
    c = cond_ref[...]
    o_ref[...] = _dot_f32(c * jax.nn.sigmoid(c), w_ref[...]) + b_ref[...]


def _adaln(cond, ada_w, ada_b):
    depth, d, n = ada_w.shape
    tn = 1536
    return pl.pallas_call(
        _adaln_kernel,
        grid=(depth, n // tn),
        in_specs=[
            pl.BlockSpec((COND_ROWS, d), lambda i, j: (0, 0)),
            pl.BlockSpec((None, d, tn), lambda i, j: (i, 0, j)),
            pl.BlockSpec((None, 1, tn), lambda i, j: (i, 0, j)),
        ],
        out_specs=pl.BlockSpec((None, COND_ROWS, tn), lambda i, j: (i, 0, j)),
        out_shape=jax.ShapeDtypeStruct((depth, COND_ROWS, n), F32),
        compiler_params=_params("parallel", "parallel"),
        name="adaln",
    )(cond, ada_w, ada_b.reshape(depth, 1, n))


class _Stream:
    def __init__(self, batch, seq, cond_row0, per_seq_cond, row0=0, tile=TOKEN_TILE):
        self.batch, self.seq = batch, seq
        self.rows = batch * seq
        self.row0 = row0
        self._cond = (cond_row0, per_seq_cond)
        if per_seq_cond:
            self.tm = min(tile, seq)
            tiles_per_seq = seq // self.tm
            self.cond_row = lambda i: cond_row0 + i // tiles_per_seq
        else:
            self.tm = min(tile, self.rows)
            self.cond_row = lambda i: cond_row0
        self.tiles = self.rows // self.tm

    def retiled(self, tile):
        return _Stream(self.batch, self.seq, *self._cond, row0=self.row0, tile=tile)

    def mod_spec(self, layer, chunk):
        return pl.BlockSpec((None, None, 1, D_MODEL),
                            lambda i, *_: (layer, self.cond_row(i), 0, chunk))

    def row_spec(self, width):
        return pl.BlockSpec((self.tm, width), lambda i, *_: (i, 0))


def _mod_matmul_kernel(n_out, has_bias, x_ref, g_ref, sh_ref, sc_ref, w_ref, *rest):
    if has_bias:
        b_ref, out_refs = rest[0], rest[1:]
    else:
        b_ref, out_refs = None, rest
    h = _modulate(x_ref[...], g_ref[...], sh_ref[...], sc_ref[...])
    y = _dot(h.astype(BF16), w_ref[...])
    if has_bias:
        y = y + b_ref[...]
    width = y.shape[1] // n_out
    for k, o_ref in enumerate(out_refs):
        o_ref[...] = y[:, k * width:(k + 1) * width].astype(o_ref.dtype)


def _mod_matmul(st, x, norm_g, mod, layer, w_bf16, bias, n_out, out_dtype):
    st = st.retiled(WIDE_OUT_TILE)
    d, n = w_bf16.shape
    in_specs = [st.row_spec(d), _resident((1, d)), st.mod_spec(layer, 0), st.mod_spec(layer, 1),
                _resident((d, n))]
    args = [x, norm_g.reshape(1, d), mod, mod, w_bf16]
    if bias is not None:
        in_specs.append(_resident((1, n)))
        args.append(bias.reshape(1, n))
    width = n // n_out
    outs = pl.pallas_call(
        functools.partial(_mod_matmul_kernel, n_out, bias is not None),
        grid=(st.tiles,),
        in_specs=in_specs,
        out_specs=[st.row_spec(width)] * n_out,
        out_shape=[jax.ShapeDtypeStruct((st.rows, width), out_dtype)] * n_out,
        compiler_params=_params("parallel"),
        name="mod_matmul",
    )(*args)
    return outs


def _filter_time_kernel(feats_ref, w1_ref, b1_ref, w2_ref, b2_ref, fr_ref, w3_ref, dl_ref,
                        hf_ref, hb_ref):
    feats = feats_ref[...]
    fr = fr_ref[...]
    h = jnp.sin(fr[0:1] * (_dot_f32(feats, w1_ref[...]) + b1_ref[...]))
    h = jnp.sin(fr[1:2] * (_dot_f32(h, w2_ref[...]) + b2_ref[...]))
    h = _dot_f32(h, w3_ref[...])
    t = feats[:, 0:1]
    decay = jnp.exp(-t * jnp.abs(dl_ref[...]))
    half = HY_ORDER * D_MODEL
    decay2 = jnp.concatenate([decay] * HY_ORDER, axis=1)
    hf_ref[...] = h[:, :half] * decay2
    hb_ref[...] = jnp.where(t == 0.0, 0.0, h[:, half:] * decay2)


def _filter_time(seq, feats_pad, w1_pad, b1, w2, b2, freq, w3, deltas):
    tl = min(seq, 256)
    half = HY_ORDER * D_MODEL
    out = jax.ShapeDtypeStruct((seq, half), F32)
    return pl.pallas_call(
        _filter_time_kernel,
        grid=(seq // tl,),
        in_specs=[
            pl.BlockSpec((tl, V7X_LANES), lambda i: (i, 0)),
            _resident(w1_pad.shape), _resident((1, HY_FW)), _resident((HY_FW, HY_FW)),
            _resident((1, HY_FW)), _resident((2, HY_FW)), _resident(w3.shape),
            _resident((1, D_MODEL)),
        ],
        out_specs=[pl.BlockSpec((tl, half), lambda i: (i, 0))] * 2,
        out_shape=[out, out],
        compiler_params=_params("parallel"),
        name="hyena_filter_time",
    )(feats_pad, w1_pad, b1.reshape(1, HY_FW), w2, b2.reshape(1, HY_FW), freq, w3, deltas)


def _filter_spectrum_kernel(blk, nb, hf_ref, hb_ref, c_ref, s_ref, kr_ref, ki_ref, kn_ref):
    row = lax.broadcasted_iota(jnp.int32, (blk, 1), 0)
    sg = (1 - 2 * (row & 1)).astype(F32)
    wgt = jnp.where(row == 0, 1.0, 2.0) * (1.0 / (2 * blk))
    fwd, bwd = [], []
    for j in range(nb):
        rows = slice(j * blk, (j + 1) * blk)
        for ref, out in ((hf_ref, fwd), (hb_ref, bwd)):
            x = ref[rows, :]
            xb = x.astype(BF16)
            out.append((_dot(c_ref[...], xb), _dot(s_ref[...], xb),
                        jnp.sum(x * sg, axis=0, keepdims=True), x[0:1, :],
                        xb[0:1, :].astype(F32)))
    kn_ref[...] = jnp.zeros_like(kn_ref)
    for d in range(-(nb - 1), nb):
        if d == 0:
            (fc, fs, fn, _, _), (bc, bs, bn, _, _) = fwd[0], bwd[0]
            kr, ki, kn = fc + bc, bs - fs, fn + bn
        else:
            parts, im_sign = (fwd, -1.0) if d > 0 else (bwd, 1.0)
            c1, s1, n1, _, _ = parts[abs(d)]
            c0, s0, n0, x0, x0_seen = parts[abs(d) - 1]
            kr, ki, kn = c1 + sg * (c0 - x0_seen), im_sign * (s1 + sg * s0), n1 + n0 - x0
        slot = d + nb - 1
        kr_ref[slot * blk:(slot + 1) * blk, :] = (wgt * kr).astype(kr_ref.dtype)
        ki_ref[slot * blk:(slot + 1) * blk, :] = (wgt * ki).astype(ki_ref.dtype)
        kn_ref[slot:slot + 1, :] = kn * (1.0 / (2 * blk))


def _filter_spectrum(seq, blk, hf, hb, cmat, smat):
    nb = seq // blk
    half = hf.shape[1]
    tn = 256
    col = pl.BlockSpec((seq, tn), lambda j: (0, j))
    spec_rows = (2 * nb - 1) * blk
    out_col = pl.BlockSpec((spec_rows, tn), lambda j: (0, j))
    return pl.pallas_call(
        functools.partial(_filter_spectrum_kernel, blk, nb),
        grid=(half // tn,),
        in_specs=[col, col, _resident((blk, blk)), _resident((blk, blk))],
        out_specs=[out_col, out_col, pl.BlockSpec((NYQUIST_ROWS, tn), lambda j: (0, j))],
        out_shape=[jax.ShapeDtypeStruct((spec_rows, half), BF16)] * 2
        + [jax.ShapeDtypeStruct((NYQUIST_ROWS, half), F32)],
        compiler_params=_params("parallel"),
        name="hyena_filter_spectrum",
    )(hf, hb, cmat, smat)


def _hyena_core_kernel(seq, blk, *refs):
    projections, shared, per_seq = refs[:3], refs[3:19], refs[19:]
    for s in range(projections[0].shape[0]):
        _hyena_core_one_sequence(seq, blk, *[r.at[s] for r in projections], *shared,
                                 *[r.at[s] for r in per_seq])


def _hyena_core_one_sequence(seq, blk, pv_ref, p1_ref, p2_ref, cwv_ref, cw1_ref, cw2_ref, cbv_ref,
                             cb1_ref, cb2_ref, kr0_ref, ki0_ref, kn0_ref, kr1_ref, ki1_ref,
                             kn1_ref, bias0_ref, bias1_ref, c_ref, s_ref, z_ref,
                             u_ref, ub_ref, gate_ref, a_ref, b_ref):
    row = lax.broadcasted_iota(jnp.int32, (blk, 1), 0)
    sign = (1 - 2 * (row & 1)).astype(F32)
    nb = seq // blk
    blocks = [slice(j * blk, (j + 1) * blk) for j in range(nb)]

    def short_conv(dst_ref, x_ref, w_ref, b_ref):
        x = x_ref[...].astype(F32)
        w = w_ref[...]
        time = lax.broadcasted_iota(jnp.int32, (seq, 1), 0)
        prev = jnp.where(time == 0, 0.0, pltpu.roll(x, 1, 0))
        nxt = jnp.where(time == seq - 1, 0.0, pltpu.roll(x, seq - 1, 0))
        dst_ref[...] = prev * w[0:1] + x * w[1:2] + nxt * w[2:3] + b_ref[...]

    def gated_long_conv(kr_ref, ki_ref, kn_ref, bias_ref, write):
        ub_ref[...] = u_ref[...].astype(BF16)
        nyq_in = []
        for rows in blocks:
            a_ref[rows, :] = _dot(c_ref[...], ub_ref[rows, :]).astype(BF16)
            b_ref[rows, :] = _dot(s_ref[...], ub_ref[rows, :]).astype(BF16)
            nyq_in.append(jnp.sum(u_ref[rows, :] * sign, axis=0, keepdims=True))
        for i, rows in enumerate(blocks):
            p = q = nyq = None
            for j, src in enumerate(blocks):
                slot = i - j + nb - 1
                kr = kr_ref[slot * blk:(slot + 1) * blk, :]
                ki = ki_ref[slot * blk:(slot + 1) * blk, :]
                a, b = a_ref[src, :], b_ref[src, :]
                pj, qj = a * kr + b * ki, b * kr - a * ki
                nj = nyq_in[j] * kn_ref[slot:slot + 1, :]
                p, q, nyq = (pj, qj, nj) if j == 0 else (p + pj, q + qj, nyq + nj)
            y = _dot(c_ref[...], p) + _dot(s_ref[...], q)
            y = y + sign * nyq + u_ref[rows, :] * bias_ref[...]
            write(rows, gate_ref[rows, :] * y)

    def to_u(rows, val):
        u_ref[rows, :] = val

    def to_z(rows, val):
        z_ref[rows, :] = val.astype(z_ref.dtype)

    short_conv(u_ref, pv_ref, cwv_ref, cbv_ref)
    short_conv(gate_ref, p1_ref, cw1_ref, cb1_ref)
    gated_long_conv(kr0_ref, ki0_ref, kn0_ref, bias0_ref, to_u)
    short_conv(gate_ref, p2_ref, cw2_ref, cb2_ref)
    gated_long_conv(kr1_ref, ki1_ref, kn1_ref, bias1_ref, to_z)


def _hyena_core(st, blk, proj, conv_w, conv_b, kr, ki, kn, bias, cmat, smat):
    seq, d = st.seq, D_MODEL
    tn = 256
    nj = d // tn
    spec_rows = kr.shape[0]
    proj3 = proj.reshape(st.batch, seq, 3 * d)
    group = max(1, min(HYENA_GROUP_STEPS // seq, st.batch))
    assert st.batch % group == 0

    def part(k):
        return pl.BlockSpec((group, seq, tn), lambda j, b: (b, 0, k * nj + j))

    def cols(rows, k, buffers=2):
        return pl.BlockSpec((rows, tn), lambda j, b: (0, k * nj + j),
                            pipeline_mode=pl.Buffered(buffers))

    in_specs = ([part(0), part(1), part(2)]
                + [cols(3, k) for k in range(3)] + [cols(1, k) for k in range(3)]
                + [cols(spec_rows, 0, 1), cols(spec_rows, 0, 1), cols(NYQUIST_ROWS, 0),
                   cols(spec_rows, 1, 1), cols(spec_rows, 1, 1), cols(NYQUIST_ROWS, 1)]
                + [cols(1, 0), cols(1, 0)]
                + [_resident((blk, blk)), _resident((blk, blk))])
    z = pl.pallas_call(
        functools.partial(_hyena_core_kernel, seq, blk),
        grid=(nj, st.batch // group),
        in_specs=in_specs,
        out_specs=pl.BlockSpec((group, seq, tn), lambda j, b: (b, 0, j)),
        out_shape=jax.ShapeDtypeStruct((st.batch, seq, d), BF16),
        scratch_shapes=[pltpu.VMEM((group, seq, tn), F32), pltpu.VMEM((group, seq, tn), BF16),
                        pltpu.VMEM((group, seq, tn), F32), pltpu.VMEM((group, seq, tn), BF16),
                        pltpu.VMEM((group, seq, tn), BF16)],
        compiler_params=_params("parallel", "parallel"),
        name="hyena_core",
    )(proj3, proj3, proj3, conv_w, conv_w, conv_w, conv_b, conv_b, conv_b,
      kr, ki, kn, kr, ki, kn, bias[0:1], bias[1:2], cmat, smat)
    return z.reshape(st.rows, d)


def _route_kernel(first_tiles, *refs):
    n = len(first_tiles)
    z_refs, x_refs, shared = refs[:n], refs[n:2 * n], refs[2 * n:2 * n + 8]
    h2_ref, x1_refs, route_refs = refs[2 * n + 8], refs[2 * n + 9:3 * n + 9], refs[3 * n + 9:]
    i = pl.program_id(0)
    for s in range(n):
        lo = first_tiles[s]
        hi = first_tiles[s + 1] if s + 1 < n else pl.num_programs(0)

        @pl.when((i >= lo) & (i < hi))
        def _(s=s):
            _route_tile(z_refs[s], x_refs[s], *shared, x1_refs[s], h2_ref, route_refs[s])


def _route_tile(z_ref, x_ref, w_ref, g1_ref, ng_ref, sh_ref, sc_ref, rw_hi_ref, rw_lo_ref, rb_ref,
                x1_ref, h2_ref, route_ref):
    x1 = x_ref[...] + g1_ref[...] * _dot(z_ref[...], w_ref[...])
    x1_ref[...] = x1
    h2 = _modulate(x1, ng_ref[...], sh_ref[...], sc_ref[...])
    _store_token_tiles(h2_ref, h2)
    h_hi = h2.astype(BF16)
    h_lo = (h2 - h_hi.astype(F32)).astype(BF16)
    logits = (_dot(h_hi, rw_hi_ref[...])
              + (_dot(h_lo, rw_hi_ref[...]) + _dot(h_hi, rw_lo_ref[...])) + rb_ref[...])
    lane = lax.broadcasted_iota(jnp.int32, logits.shape, 1)
    neg = -jnp.inf
    logits = jnp.where(lane < N_EXPERTS, logits, neg)
    m1 = jnp.max(logits, axis=-1, keepdims=True)
    i1 = jnp.min(jnp.where(logits == m1, lane, V7X_LANES), axis=-1, keepdims=True)
    rest = jnp.where(lane == i1, neg, logits)
    m2 = jnp.max(rest, axis=-1, keepdims=True)
    i2 = jnp.min(jnp.where(rest == m2, lane, V7X_LANES), axis=-1, keepdims=True)
    e2 = jnp.exp(m2 - m1)
    den = 1.0 + e2
    route = jnp.where(lane == 0, i1.astype(F32), jnp.where(lane == 1, i2.astype(F32), 0.0))
    route_ref[...] = route + jnp.where(lane == 2, 1.0 / den, 0.0) + jnp.where(lane == 3, e2 / den, 0.0)


def _store_token_tiles(ref, val):
    rows = val.shape[0]
    for c in range(LANE_CHUNKS):
        ref[pl.ds(c, rows, stride=LANE_CHUNKS), :] = val[:, c * V7X_LANES:(c + 1) * V7X_LANES]


def _out_proj_route(streams, w_bf16, mod, layer, norm_g, rw_pad, rb_pad):
    d = D_MODEL
    streams = [(st.retiled(WIDE_OUT_TILE), z, x) for st, z, x in streams]
    tm = streams[0][0].tm
    assert all(st.tm == tm and st.row0 % tm == 0 for st, _, _ in streams)
    first = [st.row0 // tm for st, _, _ in streams]
    tiles = sum(st.tiles for st, _, _ in streams)
    rw_hi = rw_pad.astype(BF16)
    rw_lo = (rw_pad - rw_hi.astype(F32)).astype(BF16)

    def local(s, i):
        return jnp.clip(i - first[s], 0, streams[s][0].tiles - 1)

    def cond_row(i):
        row = streams[0][0].cond_row(local(0, i))
        for s in range(1, len(streams)):
            row = jnp.where(i >= first[s], streams[s][0].cond_row(local(s, i)), row)
        return row

    def rows_of(s, width):
        return pl.BlockSpec((tm, width), lambda i: (local(s, i), 0))

    def mod_chunk(chunk):
        return pl.BlockSpec((None, None, 1, d), lambda i: (layer, cond_row(i), 0, chunk))

    n = len(streams)
    in_specs = ([rows_of(s, streams[s][1].shape[1]) for s in range(n)]
                + [rows_of(s, d) for s in range(n)]
                + [_resident(w_bf16.shape), mod_chunk(2), _resident((1, d)), mod_chunk(3),
                   mod_chunk(4), _resident(rw_pad.shape), _resident(rw_pad.shape),
                   _resident(rb_pad.shape)])
    out_specs = ([pl.BlockSpec((tm * LANE_CHUNKS, V7X_LANES), lambda i: (i, 0))]
                 + [rows_of(s, d) for s in range(n)] + [rows_of(s, V7X_LANES) for s in range(n)])
    out_shape = ([jax.ShapeDtypeStruct((tiles * tm * LANE_CHUNKS, V7X_LANES), F32)]
                 + [jax.ShapeDtypeStruct((st.rows, d), F32) for st, _, _ in streams]
                 + [jax.ShapeDtypeStruct((st.rows, V7X_LANES), F32) for st, _, _ in streams])
    outs = pl.pallas_call(
        functools.partial(_route_kernel, first),
        grid=(tiles,),
        in_specs=in_specs,
        out_specs=out_specs,
        out_shape=out_shape,
        compiler_params=_params("arbitrary"),
        name="out_proj_route",
    )(*[z for _, z, _ in streams], *[x for _, _, x in streams], w_bf16, mod,
      norm_g.reshape(1, d), mod, mod, rw_hi, rw_lo, rb_pad)
    return outs[0], outs[1:1 + n], outs[1 + n:]


def _swiglu_part(x_bf16, wg_ref, wu_ref, wd_ref):
    g = _dot(x_bf16, wg_ref[...].astype(BF16))
    u = _dot(x_bf16, wu_ref[...].astype(BF16))
    return _dot((g * jax.nn.sigmoid(g) * u).astype(BF16), wd_ref[...].astype(BF16))


def _out_proj_swiglu_kernel(z_ref, w_ref, b_ref, x_ref, g1_ref, ng_ref, sh_ref, sc_ref, wg_ref,
                            wu_ref, wd_ref, g2_ref, o_ref, x1_ref, h2_ref, acc_ref):
    j = pl.program_id(1)
    last_j = pl.num_programs(1) - 1

    @pl.when(j == 0)
    def _():
        x1 = x_ref[...] + g1_ref[...] * (_dot(z_ref[...], w_ref[...]) + b_ref[...])
        x1_ref[...] = x1
        h2 = _modulate(x1, ng_ref[...], sh_ref[...], sc_ref[...]).astype(BF16)
        h2_ref[...] = h2
        acc_ref[...] = _swiglu_part(h2, wg_ref, wu_ref, wd_ref)

    @pl.when((j > 0) & (j < last_j))
    def _():
        acc_ref[...] += _swiglu_part(h2_ref[...], wg_ref, wu_ref, wd_ref)

    @pl.when(j == last_j)
    def _():
        f = acc_ref[...] + _swiglu_part(h2_ref[...], wg_ref, wu_ref, wd_ref)
        o_ref[...] = x1_ref[...] + g2_ref[...] * f


def _out_proj_swiglu(st, z_bf16, w_bf16, bias, x, mod, layer, norm_g, w_gu, w_down):
    st = st.retiled(WIDE_OUT_TILE)
    d, two_f = w_gu.shape
    f = two_f // 2
    tf = DENSE_FF_CHUNK
    nf = f // tf
    rows = pl.BlockSpec((st.tm, d), lambda i, j: (i, 0))
    return pl.pallas_call(
        _out_proj_swiglu_kernel,
        grid=(st.tiles, nf),
        in_specs=[
            rows, _resident(w_bf16.shape), _resident((1, d)), rows, st.mod_spec(layer, 2),
            _resident((1, d)), st.mod_spec(layer, 3), st.mod_spec(layer, 4),
            pl.BlockSpec((d, tf), lambda i, j: (0, j)),
            pl.BlockSpec((d, tf), lambda i, j: (0, nf + j)),
            pl.BlockSpec((tf, d), lambda i, j: (j, 0)),
            st.mod_spec(layer, 5),
        ],
        out_specs=rows,
        out_shape=jax.ShapeDtypeStruct((st.rows, d), F32),
        scratch_shapes=[pltpu.VMEM((st.tm, d), F32), pltpu.VMEM((st.tm, d), BF16),
                        pltpu.VMEM((st.tm, d), F32)],
        compiler_params=_params("parallel", "arbitrary"),
        name="out_proj_swiglu",
    )(z_bf16, w_bf16, bias.reshape(1, d), x, mod, norm_g.reshape(1, d), mod, mod,
      w_gu, w_gu, w_down, mod)


def _moe_plan(route, tm):
    tokens = route.shape[0]
    max_tiles = (2 * tokens) // tm + N_EXPERTS + 1
    expert = route[:, :2].astype(jnp.int32).reshape(-1)
    experts = jnp.arange(N_EXPERTS, dtype=jnp.int32)
    onehot = (expert[:, None] == experts[None, :]).astype(jnp.int32)
    csum = jnp.cumsum(onehot, axis=0)
    rank = jnp.sum(csum * onehot, axis=1) - 1
    counts = csum[-1]
    tiles_per_expert = (counts + tm - 1) // tm
    tiles_end = jnp.cumsum(tiles_per_expert)
    first_tile = tiles_end - tiles_per_expert
    first_sorted = jnp.cumsum(counts) - counts
    pos = jnp.sum((first_tile * tm)[None, :] * onehot, axis=1) + rank
    token = jnp.arange(2 * tokens, dtype=jnp.int32) // 2
    _, order = lax.sort_key_val(pos, token)
    n_tiles = tiles_end[-1:]
    tile = jnp.minimum(jnp.arange(max_tiles, dtype=jnp.int32), n_tiles - 1)
    tile_expert = jnp.sum((tile[:, None] >= tiles_end[None, :]).astype(jnp.int32), axis=1)
    mine = (tile_expert[:, None] == experts[None, :]).astype(jnp.int32)
    done = (tile - jnp.sum(first_tile[None, :] * mine, axis=1)) * tm
    tile_base = jnp.sum(first_sorted[None, :] * mine, axis=1) + done
    tile_valid = jnp.clip(jnp.sum(counts[None, :] * mine, axis=1) - done, 1, tm)
    as_i32 = lambda a: a.astype(jnp.int32)
    return (as_i32(order), as_i32(pos), as_i32(tile_expert), as_i32(tile_base),
            as_i32(tile_valid), as_i32(n_tiles))


def _token_tile_copy(src_hbm, row, dst, slot_row, sem):
    return pltpu.make_async_copy(
        src_hbm.at[pl.ds(pl.multiple_of(row * LANE_CHUNKS, LANE_CHUNKS), LANE_CHUNKS)],
        dst.at[pl.ds(pl.multiple_of(slot_row * LANE_CHUNKS, LANE_CHUNKS), LANE_CHUNKS)],
        sem)


def _moe_ffn_kernel(tm, per_step, n_steps, order_ref, te_ref, base_ref, valid_ref, nt_ref, x_hbm,
                    wg_ref, wu_ref, wd_ref, y_ref, xbuf, xd_ref, acc_ref, sem):
    t, j = pl.program_id(0), pl.program_id(1)
    last_j = n_steps - 1
    n_tiles = nt_ref[0]
    slot = t % 2
    total = per_step * n_steps

    def copy_row(tile_base, tile_last, r, into):
        token = order_ref[tile_base + jnp.minimum(r, tile_last)]
        return _token_tile_copy(x_hbm, token, xbuf.at[into], r, sem.at[into])

    @pl.when((t == 0) & (j == 0))
    def _():
        base, last = base_ref[0], valid_ref[0] - 1

        def body(r, carry):
            copy_row(base, last, r, 0).start()
            return carry
        lax.fori_loop(0, total, body, 0)

    @pl.when((j == 0) & (t <= n_tiles))
    def _():
        pltpu.make_async_copy(x_hbm.at[pl.ds(0, total * LANE_CHUNKS)],
                              xbuf.at[slot, pl.ds(0, total * LANE_CHUNKS)], sem.at[slot]).wait()

    def start_next_tile_copies():
        nxt = jnp.minimum(t + 1, n_tiles - 1)
        base, last = base_ref[nxt], valid_ref[nxt] - 1
        for k in range(per_step):
            copy_row(base, last, j * per_step + k, 1 - slot).start()

    @pl.when((j == 0) & (t < n_tiles))
    def _():
        start_next_tile_copies()
        x = jnp.concatenate([xbuf[slot, pl.ds(c, tm, stride=LANE_CHUNKS), :].astype(BF16)
                             for c in range(LANE_CHUNKS)], axis=1)
        xd_ref[...] = x
        acc_ref[...] = _swiglu_part(x, wg_ref, wu_ref, wd_ref)

    @pl.when((j > 0) & (t < n_tiles))
    def _():
        start_next_tile_copies()
        acc_ref[...] += _swiglu_part(xd_ref[...], wg_ref, wu_ref, wd_ref)

        @pl.when(j == last_j)
        def _():
            _store_token_tiles(y_ref, acc_ref[...])

    @pl.when((t >= n_tiles) & (j == last_j))
    def _():
        y_ref[...] = jnp.zeros_like(y_ref)


def _moe_ffn(pool, order, tile_expert, tile_base, tile_valid, n_tiles, w_gu, w_down, tm):
    n_e, d, two_f = w_gu.shape
    f = two_f // 2
    tf = MOE_FF_CHUNK
    nf = f // tf
    max_tiles = tile_expert.shape[0]
    rows = tm * LANE_CHUNKS
    per_step = -(-tm // nf)
    buf_rows = per_step * nf * LANE_CHUNKS
    grid_spec = pltpu.PrefetchScalarGridSpec(
        num_scalar_prefetch=5,
        grid=(max_tiles, nf),
        in_specs=[
            pl.BlockSpec(memory_space=pl.ANY),
            pl.BlockSpec((None, d, tf), lambda t, j, order, te, *_: (te[t], 0, j)),
            pl.BlockSpec((None, d, tf), lambda t, j, order, te, *_: (te[t], 0, nf + j)),
            pl.BlockSpec((None, tf, d), lambda t, j, order, te, *_: (te[t], j, 0)),
        ],
        out_specs=pl.BlockSpec((rows, V7X_LANES), lambda t, j, *_: (t, 0)),
        scratch_shapes=[pltpu.VMEM((2, buf_rows, V7X_LANES), F32), pltpu.VMEM((tm, d), BF16),
                        pltpu.VMEM((tm, d), F32), pltpu.SemaphoreType.DMA((2,))],
    )
    return pl.pallas_call(
        functools.partial(_moe_ffn_kernel, tm, per_step, nf),
        grid_spec=grid_spec,
        out_shape=jax.ShapeDtypeStruct((max_tiles * rows, V7X_LANES), F32),
        compiler_params=_params("arbitrary", "arbitrary"),
        name="moe_ffn",
    )(order, tile_expert, tile_base, tile_valid, n_tiles, pool, w_gu, w_gu, w_down)


def _moe_combine_kernel(tm, token0, pos_ref, y_hbm, route_ref, x_ref, g2_ref, o_ref, ybuf, sem):
    i = pl.program_id(0)
    slot = i % 2
    rows = tm * LANE_CHUNKS

    def start_gather(tile, into):
        def body(r, carry):
            a = 2 * (token0 + tile * tm + r)
            for k in range(2):
                _token_tile_copy(y_hbm, pos_ref[a + k], ybuf.at[into], k * tm + r,
                                 sem.at[into]).start(priority=k)
            return carry
        lax.fori_loop(0, tm, body, 0, unroll=4)

    @pl.when(i == 0)
    def _():
        start_gather(0, 0)

    pltpu.make_async_copy(y_hbm.at[pl.ds(0, 2 * rows)], ybuf.at[slot], sem.at[slot]).wait()

    @pl.when(i + 1 < pl.num_programs(0))
    def _():
        start_gather(i + 1, 1 - slot)

    route = route_ref[...]
    lane = lax.broadcasted_iota(jnp.int32, route.shape, 1)
    gate0 = jnp.sum(jnp.where(lane == 2, route, 0.0), axis=-1, keepdims=True)
    gate1 = jnp.sum(jnp.where(lane == 3, route, 0.0), axis=-1, keepdims=True)
    for c in range(LANE_CHUNKS):
        cols = slice(c * V7X_LANES, (c + 1) * V7X_LANES)
        y0 = ybuf[slot, pl.ds(c, tm, stride=LANE_CHUNKS), :]
        y1 = ybuf[slot, pl.ds(rows + c, tm, stride=LANE_CHUNKS), :]
        o_ref[:, cols] = x_ref[:, cols] + g2_ref[:, cols] * (gate0 * y0 + gate1 * y1)


def _moe_combine(st, y_slots, pos, route, x1, mod, layer):
    d = D_MODEL
    tm = st.tm
    rows = tm * LANE_CHUNKS
    grid_spec = pltpu.PrefetchScalarGridSpec(
        num_scalar_prefetch=1,
        grid=(st.tiles,),
        in_specs=[
            pl.BlockSpec(memory_space=pl.ANY),
            st.row_spec(V7X_LANES), st.row_spec(d), st.mod_spec(layer, 5),
        ],
        out_specs=st.row_spec(d),
        scratch_shapes=[pltpu.VMEM((2, 2 * rows, V7X_LANES), F32), pltpu.SemaphoreType.DMA((2,))],
    )
    return pl.pallas_call(
        functools.partial(_moe_combine_kernel, tm, st.row0),
        grid_spec=grid_spec,
        out_shape=jax.ShapeDtypeStruct((st.rows, d), F32),
        compiler_params=_params("arbitrary"),
        name="moe_combine",
    )(pos, y_slots, route, x1, mod)


def _head_rmsnorm(x, g2):
    lane = lax.broadcasted_iota(jnp.int32, x.shape, 1)
    lo = lane < HEAD_DIM
    sq = x * x
    s_lo = jnp.sum(jnp.where(lo, sq, 0.0), axis=-1, keepdims=True)
    s_hi = jnp.sum(jnp.where(lo, 0.0, sq), axis=-1, keepdims=True)
    ms = jnp.where(lo, s_lo, s_hi) * (1.0 / HEAD_DIM)
    return x * lax.rsqrt(ms + EPS) * g2


def _rope(x, cos, sin_signed):
    q4 = HEAD_DIM // 4
    lane = lax.broadcasted_iota(jnp.int32, x.shape, 1)
    first = (lane & q4) == 0
    width = x.shape[1]
    partner = jnp.where(first, pltpu.roll(x, width - q4, 1), pltpu.roll(x, q4, 1))
    return x * cos + partner * sin_signed


def _attention_kernel(past, use_rope, lam_init, *refs):
    refs = list(refs)
    n_seq = refs[0].shape[0]
    n_in = 3 + (2 if past else 0)
    seq_refs, refs = refs[:n_in], refs[n_in:]
    if use_rope:
        rope_refs, refs = refs[:4], refs[4:]
    else:
        rope_refs = []
    param_refs, refs = refs[:4], refs[4:]
    n_out = 1 if past else 2
    out_refs, (kall_ref, vall_ref) = refs[:n_out], refs[n_out:]
    for s in range(n_seq):
        _attention_one_sequence(past, use_rope, lam_init, *[r.at[s] for r in seq_refs],
                                *rope_refs, *param_refs, *[r.at[s] for r in out_refs],
                                kall_ref.at[s], vall_ref.at[s])


def _attention_one_sequence(past, use_rope, lam_init, *refs):
    refs = list(refs)
    q_ref, k_ref, v_ref = refs.pop(0), refs.pop(0), refs.pop(0)
    if past:
        ck_ref, cv_ref = refs.pop(0), refs.pop(0)
    if use_rope:
        cosq_ref, sinq_ref, cosk_ref, sink_ref = (refs.pop(0) for _ in range(4))
    qg_ref, kg_ref, lam_ref, sg_ref = (refs.pop(0) for _ in range(4))
    o_ref = refs.pop(0)
    nk_ref = None if past else refs.pop(0)
    kall_ref, vall_ref = refs

    def prepare_keys():
        k = _head_rmsnorm(k_ref[...], kg_ref[...])
        if nk_ref is not None:
            nk_ref[...] = k
        if use_rope:
            k = _rope(k, cosk_ref[...], sink_ref[...])
        if past:
            kall_ref[0:past, :] = ck_ref[...].astype(BF16)
            vall_ref[0:past, :] = cv_ref[...].astype(BF16)
        kall_ref[past:, :] = k.astype(BF16)
        vall_ref[past:, :] = v_ref[...].astype(BF16)

    def attend(qm):
        nt = (((1,), (1,)), ((), ()))
        s = lax.dot_general(qm.astype(BF16), kall_ref[...], nt, preferred_element_type=F32)
        p = jnp.exp2(s - jnp.max(s, axis=-1, keepdims=True))
        norm = 1.0 / jnp.sum(p, axis=-1, keepdims=True)
        return _dot(p.astype(BF16), vall_ref[...]) * norm

    def attend_queries():
        q = _head_rmsnorm(q_ref[...], qg_ref[...])
        if use_rope:
            q = _rope(q, cosq_ref[...], sinq_ref[...])
        q = q * (HEAD_DIM ** -0.5 * math.log2(math.e))
        lv = lam_ref[...]
        lam = (jnp.exp(jnp.sum(lv[0:1] * lv[1:2], axis=-1, keepdims=True))
               - jnp.exp(jnp.sum(lv[2:3] * lv[3:4], axis=-1, keepdims=True)) + lam_init)
        row_chunks = [slice(r, r + ATTENTION_ROWS) for r in range(0, q.shape[0], ATTENTION_ROWS)]
        lo = lax.broadcasted_iota(jnp.int32, q.shape, 1) < HEAD_DIM
        first = jnp.where(lo, q, 0.0)
        second = jnp.where(lo, 0.0, q)
        a0 = jnp.concatenate([attend(first[rows]) for rows in row_chunks], axis=0)
        a1 = jnp.concatenate([attend(second[rows]) for rows in row_chunks], axis=0)
        o = a0 - lam * a1
        ms = jnp.mean(o * o, axis=-1, keepdims=True)
        o = o * lax.rsqrt(ms + EPS) * sg_ref[...] * (1.0 - lam_init)
        o_ref[...] = o.astype(o_ref.dtype)

    if q_ref.shape[0] == k_ref.shape[0]:
        prepare_keys()
    else:
        pl.when(pl.program_id(2) == 0)(prepare_keys)
    attend_queries()


def _attention(st, q, k, v, q_g, k_g, lam_vecs, subln_g, lam_init, cache_k=None, cache_v=None,
               rope=None, cache_layer=0):
    seq, hd = st.seq, N_HEADS * V_DIM
    tq = min(seq, ATTENTION_QUERIES)
    nq = seq // tq
    group = max(1, min(ATTENTION_GROUP_QUERIES // seq, st.batch))
    assert st.batch % group == 0
    past = 0 if cache_k is None else cache_k.shape[2]
    lk = past + seq
    q3, k3, v3 = (a.reshape(st.batch, seq, hd) for a in (q, k, v))
    qblk = pl.BlockSpec((group, tq, V_DIM), lambda b, h, i: (b, i, h))
    kblk = pl.BlockSpec((group, seq, V_DIM), lambda b, h, i: (b, 0, h))
    in_specs = [qblk, kblk, kblk]
    args = [q3, k3, v3]
    if past:
        layers = cache_k.shape[1]
        cblk = pl.BlockSpec((group, None, past, V_DIM), lambda b, h, i: (b, cache_layer, 0, h))
        in_specs += [cblk, cblk]
        args += [cache_k.reshape(st.batch, layers, past, hd),
                 cache_v.reshape(st.batch, layers, past, hd)]
    if rope is not None:
        cos2, sin2 = rope
        tq_tab = pl.BlockSpec((tq, V_DIM), lambda b, h, i: (i, 0))
        k_tab = pl.BlockSpec((seq, V_DIM), lambda b, h, i: (0, 0))
        in_specs += [tq_tab, tq_tab, k_tab, k_tab]
        args += [cos2, sin2, cos2, sin2]
    small = lambda shape: pl.BlockSpec(shape, lambda b, h, i: (0,) * len(shape))
    in_specs += [small((1, V_DIM)), small((1, V_DIM)), small((4, HEAD_DIM)), small((1, V_DIM))]
    args += [jnp.tile(q_g, 2).reshape(1, V_DIM), jnp.tile(k_g, 2).reshape(1, V_DIM), lam_vecs,
             subln_g.reshape(1, V_DIM)]
    out_specs = [qblk]
    out_shape = [jax.ShapeDtypeStruct((st.batch, seq, hd), BF16)]
    if not past:
        out_specs.append(kblk)
        out_shape.append(jax.ShapeDtypeStruct((st.batch, seq, hd), F32))
    outs = pl.pallas_call(
        functools.partial(_attention_kernel, past, rope is not None, lam_init),
        grid=(st.batch // group, N_HEADS, nq),
        in_specs=in_specs,
        out_specs=out_specs,
        out_shape=out_shape,
        scratch_shapes=[pltpu.VMEM((group, lk, V_DIM), BF16),
                        pltpu.VMEM((group, lk, V_DIM), BF16)],
        compiler_params=_params("parallel", "parallel", "arbitrary"),
        name="diff_attention",
    )(*args)
    return [o.reshape(st.rows, hd) for o in outs]


def _dft_matrices(seq):
    idx = np.arange(seq, dtype=np.int64)
    ang = (np.outer(idx, idx) % (2 * seq)).astype(np.float64) * (math.pi / seq)
    return jnp.asarray(np.cos(ang), dtype=BF16), jnp.asarray(np.sin(ang), dtype=BF16)


def _filter_features(seq):
    t = jnp.linspace(0.0, 1.0, seq, dtype=F32)[:, None]
    bands = (HY_EMB - 1) // 2
    w_ang = 2.0 * math.pi * jnp.arange(seq, dtype=F32)[:, None] / seq
    f = jnp.linspace(1e-4, bands - 1, bands, dtype=F32)[None, :]
    ang = f * w_ang
    feats = jnp.concatenate([t, jnp.cos(ang), -jnp.sin(ang)], axis=-1)
    return jnp.pad(feats, ((0, 0), (0, V7X_LANES - HY_EMB)))


def _decay_rates():
    min_decay = math.log(HY_DECAY_TARGET) / HY_SLOW_PCT
    max_decay = math.log(HY_DECAY_TARGET) / HY_FAST_PCT
    return jnp.linspace(min_decay, max_decay, D_MODEL, dtype=F32)[None, :]


def _rope_tables(seq):
    rows = seq // GRID_W
    row = jnp.repeat(jnp.arange(rows, dtype=F32), GRID_W)
    col = jnp.tile(jnp.arange(GRID_W, dtype=F32), rows)
    quarter = HEAD_DIM // 4
    inv = ROPE_BASE ** (-jnp.arange(quarter, dtype=F32) / quarter)

    def axis_angles(pos):
        a = pos[:, None] * inv[None, :]
        return jnp.concatenate([a, a], axis=-1)

    ang = jnp.concatenate([axis_angles(row), axis_angles(col)], axis=-1)
    sign = jnp.where((jnp.arange(HEAD_DIM) & quarter) == 0, -1.0, 1.0).astype(F32)
    return jnp.tile(jnp.cos(ang), (1, 2)), jnp.tile(jnp.sin(ang) * sign[None, :], (1, 2))


def kernel(x_prompt, x_sample, cache_k, cache_v, c, c_ctx, ada_w, ada_b, norm_g, hy_in_w, hy_in_b, hy_conv_w, hy_conv_b, hy_f_w1, hy_f_b1, hy_f_w2, hy_f_b2, hy_f_freq, hy_f_w3, hy_bias, hy_out_w, hy_out_b, at_qkv_w, at_q_g, at_k_g, at_lam, at_subln_g, at_out_w, dn_w_gu, dn_w_down, mo_router_w, mo_router_b, mo_w_gu, mo_w_down):
    d = D_MODEL
    batch, seq = x_prompt.shape[:2]
    dec_batch, dec_seq = x_sample.shape[:2]
    streams = [
        (_Stream(batch, seq, 0, False), x_prompt.reshape(batch * seq, d), None),
        (_Stream(dec_batch, dec_seq, 1, True, row0=batch * seq),
         x_sample.reshape(dec_batch * dec_seq, d), (cache_k, cache_v)),
    ]

    cond = jnp.concatenate(
        [c_ctx[None, :], c, jnp.zeros((COND_ROWS - 1 - dec_batch, d), F32)], axis=0)
    mod = _adaln(cond, ada_w, ada_b)
    mod = mod.reshape(mod.shape[0], COND_ROWS, 1, 6 * d)

    in_w, out_w = hy_in_w[0].astype(BF16), hy_out_w[0].astype(BF16)
    qkv_w, at_out = at_qkv_w[0].astype(BF16), at_out_w[0].astype(BF16)
    dn_gu, dn_down = dn_w_gu[0].astype(BF16), dn_w_down[0].astype(BF16)
    mo_gu, mo_down = mo_w_gu[0], mo_w_down[0]
    w1_pad = jnp.pad(hy_f_w1[0], ((0, V7X_LANES - HY_EMB), (0, 0)))
    rw_pad = jnp.pad(mo_router_w[0], ((0, 0), (0, V7X_LANES - N_EXPERTS)))
    rb_pad = jnp.pad(mo_router_b[0], (0, V7X_LANES - N_EXPERTS)).reshape(1, V7X_LANES)
    deltas = _decay_rates()
    lam_init = 0.8 - 0.6 * math.exp(-0.3 * 1)

    attended = []
    for st, x, cache in streams:
        blk = min(st.seq, HYENA_BLOCK)
        cmat, smat = _dft_matrices(blk)
        hf, hb = _filter_time(st.seq, _filter_features(st.seq), w1_pad, hy_f_b1[0], hy_f_w2[0],
                              hy_f_b2[0], hy_f_freq[0], hy_f_w3[0], deltas)
        kr, ki, kn = _filter_spectrum(st.seq, blk, hf, hb, cmat, smat)
        (proj,) = _mod_matmul(st, x, norm_g[0, 0], mod, 0, in_w, hy_in_b[0], 1, BF16)
        z = _hyena_core(st, blk, proj, hy_conv_w[0], hy_conv_b[0].reshape(1, 3 * d), kr, ki, kn,
                        hy_bias[0], cmat, smat)
        x = _out_proj_swiglu(st, z, out_w, hy_out_b[0], x, mod, 0, norm_g[0, 1], dn_gu, dn_down)

        q, k, v = _mod_matmul(st, x, norm_g[1, 0], mod, 1, qkv_w, None, 3, F32)
        if cache is None:
            o, new_k = _attention(st, q, k, v, at_q_g[0], at_k_g[0], at_lam[0], at_subln_g[0],
                                  lam_init)
            new_kv = (new_k, v)
        else:
            (o,) = _attention(st, q, k, v, at_q_g[0], at_k_g[0], at_lam[0], at_subln_g[0],
                              lam_init, cache[0], cache[1], _rope_tables(st.seq))
        attended.append((st, o, x))

    pool, x1s, routes = _out_proj_route(attended, at_out, mod, 1, norm_g[1, 1], rw_pad, rb_pad)
    order, pos, *tile_table = _moe_plan(
        jnp.concatenate([route[:, :4] for route in routes], axis=0), MOE_TILE)
    y_slots = _moe_ffn(pool, order, *tile_table, mo_gu, mo_down, MOE_TILE)
    results = [_moe_combine(st, y_slots, pos, route, x1, mod, 1).reshape(st.batch, st.seq, d)
               for (st, _, _), x1, route in zip(attended, x1s, routes)]

    new_k, new_v = new_kv
    return (results[0], results[1],
            new_k.reshape(batch, 1, seq, N_HEADS, 2, HEAD_DIM),
            new_v.reshape(batch, 1, seq, N_HEADS, V_DIM))
```
